```python
import jax, jax.numpy as jnp
from jax import lax
import numpy as np

D_MODEL = 1024
BATCH = 8
SEQ = 4096
DEPTH = 2

HEAD_DIM = 64
GRID_W = 64
NA_HEADS = 8
NA_WIN_ROWS = 8
NA_WIN_COLS = 16
DIL_GROUPS = ((128, 1), (512, 4), (2048, 16))
DIL_HEADS_PER_GROUP = 4
N_DIL_GROUPS = len(DIL_GROUPS)
NA_WIDTH = NA_HEADS * HEAD_DIM
DIL_WIDTH = N_DIL_GROUPS * DIL_HEADS_PER_GROUP * HEAD_DIM
DIL_OUT_WIDTH = DIL_HEADS_PER_GROUP * HEAD_DIM
N_BRANCHES = 2
IN_COLS = 3 * (NA_WIDTH + DIL_WIDTH) + N_BRANCHES * D_MODEL
N_EXPERTS = 64
TOP_K = 8
N_EXPERT_GROUPS = 8
TOP_GROUPS = 4
EXPERT_DIM = 256
ROUTED_SCALE = 2.5
EXPERT_BLOCK = 256
ALIBI_MAX = 8.0
EPS = 1e-6
NEG_INF = -1e30

kernel_name = 'hybrid_natten_dilated_moe_encoder'


def rmsnorm(x, g):
    xf = x.astype(jnp.float32)
    xf = xf * lax.rsqrt(jnp.mean(xf * xf, axis=-1, keepdims=True) + EPS)
    return (xf * g.astype(jnp.float32)).astype(x.dtype)


def swiglu(x, wg, wu, wd):
    return (jax.nn.silu(x @ wg) * (x @ wu)) @ wd


def alibi_slopes(n):
    return jnp.exp2(-ALIBI_MAX * jnp.arange(1, n + 1, dtype=jnp.float32) / n)


def neighbourhood_attention(q, k, v, rpb):
    B, H, S, hd = q.shape
    rows = S // GRID_W
    kr = min(NA_WIN_ROWS, rows)
    qg = q.reshape(B, H, rows, GRID_W, hd)
    kg = k.reshape(B, H, rows, GRID_W, hd)
    vg = v.reshape(B, H, rows, GRID_W, hd)
    cols = jnp.arange(GRID_W)
    col_start = jnp.clip(cols - NA_WIN_COLS // 2, 0, GRID_W - NA_WIN_COLS)
    col_mask = (cols[None, :] >= col_start[:, None]) & (cols[None, :] < col_start[:, None] + NA_WIN_COLS)
    dc = jnp.clip(cols[None, :] - cols[:, None], -(NA_WIN_COLS - 1), NA_WIN_COLS - 1) + NA_WIN_COLS - 1
    rpb_cols = rpb[:, :, dc].astype(jnp.float32)
    scale = hd ** -0.5

    def row_step(r):
        rs = jnp.clip(r - kr // 2, 0, rows - kr)
        q_r = lax.dynamic_index_in_dim(qg, r, axis=2, keepdims=False)
        k_w = lax.dynamic_slice_in_dim(kg, rs, kr, axis=2)
        v_w = lax.dynamic_slice_in_dim(vg, rs, kr, axis=2)
        dr = rs + jnp.arange(kr) - r + NA_WIN_ROWS - 1
        bias = jnp.take(rpb_cols, dr, axis=1).transpose(0, 2, 1, 3)
        s = jnp.einsum('bhcd,bhwkd->bhcwk', q_r, k_w).astype(jnp.float32) * scale + bias
        s = jnp.where(col_mask[:, None, :], s, NEG_INF)
        p = jax.nn.softmax(s.reshape(B, H, GRID_W, kr * GRID_W), axis=-1)
        p = p.reshape(B, H, GRID_W, kr, GRID_W).astype(v.dtype)
        return jnp.einsum('bhcwk,bhwkd->bhcd', p, v_w)

    out = lax.map(row_step, jnp.arange(rows))
    return out.transpose(1, 2, 0, 3, 4).reshape(B, H, S, hd)


def dilated_window_attention(q, k, v, window, dilation, slopes):
    B, H, S, hd = q.shape
    radius = window // (2 * dilation)
    blk = radius
    L = S // dilation
    nb = -(-L // blk)
    Lp = nb * blk

    def to_residue(a):
        return a.reshape(B, H, L, dilation, hd).transpose(0, 1, 3, 2, 4)

    qr = jnp.pad(to_residue(q), ((0, 0), (0, 0), (0, 0), (0, Lp - L), (0, 0)))
    kpad = jnp.pad(to_residue(k), ((0, 0), (0, 0), (0, 0), (blk, Lp - L + blk), (0, 0)))
    vpad = jnp.pad(to_residue(v), ((0, 0), (0, 0), (0, 0), (blk, Lp - L + blk), (0, 0)))
    qb = qr.reshape(B, H, dilation, nb, blk, hd)

    def band(a):
        ab = a.reshape(B, H, dilation, nb + 2, blk, hd)
        return jnp.concatenate([ab[:, :, :, :-2], ab[:, :, :, 1:-1], ab[:, :, :, 2:]], axis=4)

    kw, vw = band(kpad), band(vpad)
    qi = jnp.arange(blk)
    kj = jnp.arange(3 * blk)
    rel = kj[None, :] - blk - qi[:, None]
    key_pos = jnp.arange(nb)[:, None] * blk - blk + kj[None, :]
    valid = (jnp.abs(rel) <= radius)[None] & ((key_pos >= 0) & (key_pos < L))[:, None, :]
    dist = (dilation * jnp.abs(rel)).astype(jnp.float32)
    s = jnp.einsum('bhrnqd,bhrnkd->bhrnqk', qb, kw).astype(jnp.float32) * (hd ** -0.5)
    s = s - slopes.astype(jnp.float32)[None, :, None, None, None, None] * dist
    s = jnp.where(valid, s, NEG_INF)
    m = jnp.max(s, axis=-1, keepdims=True)
    p = jnp.exp(s - m)
    z = jnp.sum(p, axis=-1)
    o = jnp.einsum('bhrnqk,bhrnkd->bhrnqd', p.astype(v.dtype), vw).astype(jnp.float32) / z[..., None]
    lse = m[..., 0] + jnp.log(z)
    o = o.reshape(B, H, dilation, Lp, hd)[:, :, :, :L].transpose(0, 1, 3, 2, 4).reshape(B, H, S, hd)
    lse = lse.reshape(B, H, dilation, Lp)[..., :L].transpose(0, 1, 3, 2).reshape(B, H, S)
    return o.astype(q.dtype), lse


def token_mixer(h, w_in, b_gate, rpb, w_pa, w_pb, w_o, slopes):
    B, S, _ = h.shape
    proj = h @ w_in
    na_end = 3 * NA_WIDTH
    dil_end = na_end + 3 * DIL_WIDTH
    qkv_a = proj[..., :na_end].reshape(B, S, 3, NA_HEADS, HEAD_DIM).transpose(2, 0, 3, 1, 4)
    o_a = neighbourhood_attention(qkv_a[0], qkv_a[1], qkv_a[2], rpb)
    qkv_b = proj[..., na_end:dil_end].reshape(B, S, 3, N_DIL_GROUPS, DIL_HEADS_PER_GROUP, HEAD_DIM)
    qkv_b = qkv_b.transpose(2, 3, 0, 4, 1, 5)
    outs, lses = [], []
    for g, (window, dilation) in enumerate(DIL_GROUPS):
        o_g, lse_g = dilated_window_attention(qkv_b[0, g], qkv_b[1, g], qkv_b[2, g], window, dilation, slopes[g])
        outs.append(o_g)
        lses.append(lse_g)
    o_stack = jnp.stack(outs).astype(jnp.float32)
    w_den = jax.nn.softmax(jnp.stack(lses), axis=0)
    o_b = jnp.sum(w_den[..., None] * o_stack, axis=0).astype(h.dtype)
    gates = jax.nn.sigmoid((proj[..., dil_end:] + b_gate).astype(jnp.float32)).astype(h.dtype)
    g_a, g_b = jnp.split(gates, N_BRANCHES, axis=-1)
    y_a = o_a.transpose(0, 2, 1, 3).reshape(B, S, NA_WIDTH) @ w_pa
    y_b = o_b.transpose(0, 2, 1, 3).reshape(B, S, DIL_OUT_WIDTH) @ w_pb
    return (g_a * y_a + g_b * y_b) @ w_o


def route(h, w_router, e_bias):
    N = h.shape[0]
    scores = jax.nn.sigmoid((h @ w_router).astype(jnp.float32))
    biased = scores + e_bias.astype(jnp.float32)
    per_group = N_EXPERTS // N_EXPERT_GROUPS
    grp_score = lax.top_k(biased.reshape(N, N_EXPERT_GROUPS, per_group), 2)[0].sum(-1)
    _, top_grp = lax.top_k(grp_score, TOP_GROUPS)
    grp_mask = jnp.any(top_grp[:, :, None] == jnp.arange(N_EXPERT_GROUPS)[None, None, :], axis=1)
    expert_mask = jnp.repeat(grp_mask, per_group, axis=1)
    _, eidx = lax.top_k(jnp.where(expert_mask, biased, NEG_INF), TOP_K)
    w = jnp.take_along_axis(scores, eidx, axis=1)
    w = w / jnp.sum(w, axis=-1, keepdims=True) * ROUTED_SCALE
    return eidx, w


def routed_experts(h, eidx, ew, we_gate, we_up, we_down):
    N, D = h.shape
    NK = N * TOP_K
    flat_e = eidx.reshape(NK)
    flat_tok = jnp.arange(NK, dtype=jnp.int32) // TOP_K
    flat_w = ew.reshape(NK).astype(h.dtype)
    order = jnp.argsort(flat_e)
    e_sorted = flat_e[order]
    counts = jnp.bincount(flat_e, length=N_EXPERTS)
    starts = jnp.cumsum(counts) - counts
    padded = (counts + EXPERT_BLOCK - 1) // EXPERT_BLOCK * EXPERT_BLOCK
    pad_ends = jnp.cumsum(padded)
    pad_starts = pad_ends - padded
    dest = pad_starts[e_sorted] + jnp.arange(NK) - starts[e_sorted]
    n_blocks = -(-(NK + N_EXPERTS * (EXPERT_BLOCK - 1)) // EXPERT_BLOCK)
    P = n_blocks * EXPERT_BLOCK
    row_tok = jnp.zeros((P,), jnp.int32).at[dest].set(flat_tok[order])
    row_w = jnp.zeros((P,), h.dtype).at[dest].set(flat_w[order])
    block_e = jnp.minimum(jnp.searchsorted(pad_ends, jnp.arange(n_blocks) * EXPERT_BLOCK, side='right'),
                          N_EXPERTS - 1)

    def step(acc, blk):
        tok, wts, e = blk
        y = swiglu(h[tok], we_gate[e], we_up[e], we_down[e]) * wts[:, None]
        return acc.at[tok].add(y), None

    acc, _ = lax.scan(step, jnp.zeros_like(h),
                      (row_tok.reshape(n_blocks, EXPERT_BLOCK), row_w.reshape(n_blocks, EXPERT_BLOCK), block_e))
    return acc


def moe_ffn(h, w_router, e_bias, we_gate, we_up, we_down, ws_gate, ws_up, ws_down):
    eidx, ew = route(h, w_router, e_bias)
    return swiglu(h, ws_gate, ws_up, ws_down) + routed_experts(h, eidx, ew, we_gate, we_up, we_down)


def setup_inputs(seed: int = 0) -> dict:
    key = jax.random.key(seed)
    ks = jax.random.split(key, 24)
    D = D_MODEL
    F = EXPERT_DIM
    f32 = jnp.float32

    def nrm(k, shape, fan_in, gain=1.0):
        return jax.random.normal(k, shape, f32) * (gain * fan_in ** -0.5)

    def small(k, shape, s):
        return jax.random.normal(k, shape, f32) * s

    return {
        'x': jax.random.normal(ks[0], (BATCH, SEQ, D), f32),
        'c': jax.random.normal(ks[1], (BATCH, D), f32),
        'w_ada': nrm(ks[2], (DEPTH, D, 6 * D), D, 0.5),
        'b_ada': small(ks[3], (DEPTH, 6 * D), 0.02),
        'g_mix': 1.0 + small(ks[4], (DEPTH, D), 0.05),
        'w_in': nrm(ks[5], (DEPTH, D, IN_COLS), D),
        'b_gate': small(ks[6], (DEPTH, N_BRANCHES * D), 0.02),
        'rpb': small(ks[7], (DEPTH, NA_HEADS, 2 * NA_WIN_ROWS - 1, 2 * NA_WIN_COLS - 1), 0.05),
        'w_pa': nrm(ks[8], (DEPTH, NA_WIDTH, D), NA_WIDTH),
        'w_pb': nrm(ks[9], (DEPTH, DIL_OUT_WIDTH, D), DIL_OUT_WIDTH),
        'w_o': nrm(ks[10], (DEPTH, D, D), D),
        'g_ffn': 1.0 + small(ks[11], (DEPTH, D), 0.05),
        'w_router': nrm(ks[12], (DEPTH, D, N_EXPERTS), D),
        'e_bias': small(ks[13], (DEPTH, N_EXPERTS), 0.01),
        'we_gate': nrm(ks[14], (DEPTH, N_EXPERTS, D, F), D),
        'we_up': nrm(ks[15], (DEPTH, N_EXPERTS, D, F), D),
        'we_down': nrm(ks[16], (DEPTH, N_EXPERTS, F, D), F),
        'ws_gate': nrm(ks[17], (DEPTH, D, F), D),
        'ws_up': nrm(ks[18], (DEPTH, D, F), D),
        'ws_down': nrm(ks[19], (DEPTH, F, D), F),
        'g_final': 1.0 + small(ks[20], (D,), 0.05),
    }


def reference(x, c, w_ada, b_ada, g_mix, w_in, b_gate, rpb, w_pa, w_pb, w_o, g_ffn, w_router, e_bias,
              we_gate, we_up, we_down, ws_gate, ws_up, ws_down, g_final):
    B, S, D = x.shape
    slopes = alibi_slopes(N_DIL_GROUPS * DIL_HEADS_PER_GROUP).reshape(N_DIL_GROUPS, DIL_HEADS_PER_GROUP)
    c_act = jax.nn.silu(c)
    for l in range(DEPTH):
        mod = c_act @ w_ada[l] + b_ada[l]
        sh1, sc1, gt1, sh2, sc2, gt2 = [m[:, None, :] for m in jnp.split(mod, 6, axis=-1)]
        h = rmsnorm(x, g_mix[l]) * (1.0 + sc1) + sh1
        x = x + gt1 * token_mixer(h, w_in[l], b_gate[l], rpb[l], w_pa[l], w_pb[l], w_o[l], slopes)
        h = rmsnorm(x, g_ffn[l]) * (1.0 + sc2) + sh2
        y = moe_ffn(h.reshape(B * S, D), w_router[l], e_bias[l], we_gate[l], we_up[l], we_down[l],
                    ws_gate[l], ws_up[l], ws_down[l])
        x = x + gt2 * y.reshape(B, S, D)
    return rmsnorm(x, g_final)
```

```python
import functools

import numpy as np
import jax
import jax.numpy as jnp
from jax import lax
from jax.experimental import pallas as pl
from jax.experimental.pallas import tpu as pltpu

HEAD_DIM = 64
GRID_W = 64
NA_HEADS = 8
NA_WIN_ROWS = 8
NA_WIN_COLS = 16
DIL_GROUPS = ((128, 1), (512, 4), (2048, 16))
DIL_HEADS_PER_GROUP = 4
N_DIL_GROUPS = len(DIL_GROUPS)
NA_WIDTH = NA_HEADS * HEAD_DIM
DIL_WIDTH = N_DIL_GROUPS * DIL_HEADS_PER_GROUP * HEAD_DIM
DIL_OUT_WIDTH = DIL_HEADS_PER_GROUP * HEAD_DIM
QKV_COLS = 3 * (NA_WIDTH + DIL_WIDTH)
N_EXPERTS = 64
TOP_K = 8
N_EXPERT_GROUPS = 8
TOP_GROUPS = 4
EXPERT_DIM = 256
ROUTED_SCALE = 2.5
ALIBI_MAX = 8.0
EPS = 1e-6
NEG_INF = -1e30

LANES = 128
HEADS_PER_LANE_TILE = LANES // HEAD_DIM
DIL_BLOCK = 64
VMEM_LIMIT_BYTES = 56 * 1024 * 1024

F32 = jnp.float32
BF16 = jnp.bfloat16

SH1, SC1, GT1, SH2, SC2, GT2 = range(6)


def _params(sem):
    return pltpu.CompilerParams(dimension_semantics=sem, vmem_limit_bytes=VMEM_LIMIT_BYTES)


def _modulated_norm(x, g, scale, shift):
    r = lax.rsqrt(jnp.mean(x * x, axis=-1, keepdims=True) + EPS)
    return (x * r * g) * (1.0 + scale) + shift


def _ada_kernel(c_ref, w_ref, b_ref, o_ref):
    c = c_ref[...]
    act = c * jax.nn.sigmoid(c)
    o_ref[0] = jnp.dot(act, w_ref[0], preferred_element_type=F32,
                       precision=lax.Precision.HIGHEST) + b_ref[0]


def _ada(c, w_ada, b_ada):
    depth, d, six_d = w_ada.shape
    b = c.shape[0]
    tn = d
    return pl.pallas_call(
        _ada_kernel,
        grid=(depth, six_d // tn),
        in_specs=[
            pl.BlockSpec((b, d), lambda l, j: (0, 0)),
            pl.BlockSpec((1, d, tn), lambda l, j: (l, 0, j)),
            pl.BlockSpec((1, 1, tn), lambda l, j: (l, 0, j)),
        ],
        out_specs=pl.BlockSpec((1, b, tn), lambda l, j: (l, 0, j)),
        out_shape=jax.ShapeDtypeStruct((depth, b, six_d), F32),
        compiler_params=_params(("arbitrary", "arbitrary")),
        name="ada_mod",
    )(c, w_ada, b_ada.reshape(depth, 1, six_d))


def _inproj_kernel(x_ref, mod_ref, g_ref, w_ref, o_ref, h_ref):
    @pl.when(pl.program_id(1) == 0)
    def _():
        h = _modulated_norm(x_ref[...], g_ref[...], mod_ref[0, SC1:SC1 + 1, :], mod_ref[0, SH1:SH1 + 1, :])
        h_ref[...] = h.astype(BF16)

    o_ref[...] = jnp.dot(h_ref[...], w_ref[...], preferred_element_type=F32).astype(BF16)


def _inproj(x2, mod, g, w_qkv, seq, tm=1024, tn=768):
    n, d = x2.shape
    cols = w_qkv.shape[1]
    blocks_per_batch = seq // tm
    return pl.pallas_call(
        _inproj_kernel,
        grid=(n // tm, cols // tn),
        in_specs=[
            pl.BlockSpec((tm, d), lambda i, j: (i, 0)),
            pl.BlockSpec((1, 6, d), lambda i, j: (i // blocks_per_batch, 0, 0)),
            pl.BlockSpec((1, d), lambda i, j: (0, 0)),
            pl.BlockSpec((d, tn), lambda i, j: (0, j)),
        ],
        out_specs=pl.BlockSpec((tm, tn), lambda i, j: (i, j)),
        out_shape=jax.ShapeDtypeStruct((n, cols), BF16),
        scratch_shapes=[pltpu.VMEM((tm, d), BF16)],
        compiler_params=_params(("arbitrary", "arbitrary")),
        name="in_proj",
    )(x2, mod, g, w_qkv)


def _na_bias_table(rpb):
    heads = rpb.shape[0]
    cols = np.arange(GRID_W)
    col_start = np.clip(cols - NA_WIN_COLS // 2, 0, GRID_W - NA_WIN_COLS)
    col_mask = (cols[None, :] >= col_start[:, None]) & (cols[None, :] < col_start[:, None] + NA_WIN_COLS)
    dc = np.clip(cols[None, :] - cols[:, None], -(NA_WIN_COLS - 1), NA_WIN_COLS - 1) + NA_WIN_COLS - 1
    rpb_cols = rpb[:, :, dc].astype(F32)
    dr = np.arange(NA_WIN_ROWS)[None, :] - np.arange(NA_WIN_ROWS)[:, None] + NA_WIN_ROWS - 1
    t = rpb_cols[:, dr]
    t = t.transpose(0, 1, 3, 2, 4)
    t = jnp.where(col_mask[:, None, :], t, NEG_INF)
    return t.reshape(heads, NA_WIN_ROWS, GRID_W, NA_WIN_ROWS * GRID_W)


def _softmax_pv(s, vw):
    m = jnp.max(s, axis=-1, keepdims=True)
    p = jnp.exp(s - m)
    z = jnp.sum(p, axis=-1, keepdims=True)
    o = jnp.dot(p.astype(BF16), vw, preferred_element_type=F32)
    return o / z, m, z


def _na_kernel(q_ref, k_ref, v_ref, bias_ref, o_ref, *, rows):
    kr = NA_WIN_ROWS
    lane = lax.broadcasted_iota(jnp.int32, (GRID_W, LANES), 1)
    low = lane < HEAD_DIM
    scale = HEAD_DIM ** -0.5

    def body(r, carry):
        rs = jnp.clip(r - kr // 2, 0, rows - kr)
        off = r - rs
        q = q_ref[pl.ds(pl.multiple_of(r * GRID_W, GRID_W), GRID_W), :]
        kw = k_ref[pl.ds(pl.multiple_of(rs * GRID_W, GRID_W), kr * GRID_W), :]
        vw = v_ref[pl.ds(pl.multiple_of(rs * GRID_W, GRID_W), kr * GRID_W), :]
        qs = q * scale
        outs = []
        for h in range(HEADS_PER_LANE_TILE):
            qh = jnp.where(low if h == 0 else jnp.logical_not(low), qs, jnp.zeros_like(qs))
            s = lax.dot_general(qh, kw, (((1,), (1,)), ((), ())), preferred_element_type=F32)
            s = s + bias_ref[h, off]
            o, _, _ = _softmax_pv(s, vw)
            outs.append(o)
        o = jnp.where(low, outs[0], outs[1])
        o_ref[pl.ds(pl.multiple_of(r * GRID_W, GRID_W), GRID_W), :] = o.astype(o_ref.dtype)
        return carry

    lax.fori_loop(0, rows, body, 0)


def _neighbourhood_attention(qkv, bias, batch, seq):
    n = qkv.shape[0]
    rows = seq // GRID_W
    pairs = NA_WIDTH // LANES
    return pl.pallas_call(
        functools.partial(_na_kernel, rows=rows),
        grid=(pairs, batch),
        in_specs=[
            pl.BlockSpec((seq, LANES), lambda p, b: (b, p)),
            pl.BlockSpec((seq, LANES), lambda p, b: (b, pairs + p)),
            pl.BlockSpec((seq, LANES), lambda p, b: (b, 2 * pairs + p)),
            pl.BlockSpec((HEADS_PER_LANE_TILE, NA_WIN_ROWS, GRID_W, NA_WIN_ROWS * GRID_W),
                         lambda p, b: (p, 0, 0, 0)),
        ],
        out_specs=pl.BlockSpec((seq, LANES), lambda p, b: (b, p)),
        out_shape=jax.ShapeDtypeStruct((n, NA_WIDTH), BF16),
        compiler_params=_params(("arbitrary", "arbitrary")),
        name="na_attn",
    )(qkv, qkv, qkv, bias)


def _alibi_slopes():
    n = N_DIL_GROUPS * DIL_HEADS_PER_GROUP
    s = np.exp2(-ALIBI_MAX * np.arange(1, n + 1, dtype=np.float64) / n).astype(np.float32)
    return s.reshape(N_DIL_GROUPS, DIL_HEADS_PER_GROUP)


def _dil_kernel(q_ref, k_ref, v_ref, o_ref, lse_ref, *, nb, dilation, slopes):
    blk = DIL_BLOCK
    win = 3 * blk
    pair = pl.program_id(1)
    qi = lax.broadcasted_iota(jnp.int32, (blk, win), 0)
    kj = lax.broadcasted_iota(jnp.int32, (blk, win), 1)
    base = kj - qi
    lane = lax.broadcasted_iota(jnp.int32, (blk, LANES), 1)
    low = lane < HEAD_DIM
    scale = HEAD_DIM ** -0.5

    def body(n, carry):
        wb = jnp.clip(n - 1, 0, nb - 3)
        arel = jnp.abs(base - (n - wb) * blk)
        valid = arel <= blk
        dist = arel.astype(F32) * float(dilation)
        q = q_ref[0, pl.ds(pl.multiple_of(n * blk, blk), blk), :]
        kw = k_ref[0, pl.ds(pl.multiple_of(wb * blk, blk), win), :]
        vw = v_ref[0, pl.ds(pl.multiple_of(wb * blk, blk), win), :]
        qs = q * scale
        outs, lses = [], []
        for h in range(HEADS_PER_LANE_TILE):
            slope = jnp.where(pair == 0, slopes[h], slopes[HEADS_PER_LANE_TILE + h])
            qh = jnp.where(low if h == 0 else jnp.logical_not(low), qs, jnp.zeros_like(qs))
            s = lax.dot_general(qh, kw, (((1,), (1,)), ((), ())), preferred_element_type=F32)
            s = jnp.where(valid, s - slope * dist, NEG_INF)
            o, m, z = _softmax_pv(s, vw)
            outs.append(o)
            lses.append(m + jnp.log(z))
        rows = pl.ds(pl.multiple_of(n * blk, blk), blk)
        o_ref[0, rows, :] = jnp.where(low, outs[0], outs[1])
        lse_ref[0, rows, :] = jnp.where(low, lses[0], lses[1])
        return carry

    lax.fori_loop(0, nb, body, 0)


def _dilated_attention(qkv3, col_offsets, group):
    nseq, length, _ = qkv3.shape
    dilation = DIL_GROUPS[group][1]
    nb = length // DIL_BLOCK
    pairs = DIL_OUT_WIDTH // LANES
    qo, ko, vo = col_offsets
    slopes = tuple(float(v) for v in _alibi_slopes()[group])
    out = jax.ShapeDtypeStruct((nseq, length, DIL_OUT_WIDTH), F32)
    return pl.pallas_call(
        functools.partial(_dil_kernel, nb=nb, dilation=dilation, slopes=slopes),
        grid=(nseq, pairs),
        in_specs=[
            pl.BlockSpec((1, length, LANES), lambda b, p: (b, 0, qo + p)),
            pl.BlockSpec((1, length, LANES), lambda b, p: (b, 0, ko + p)),
            pl.BlockSpec((1, length, LANES), lambda b, p: (b, 0, vo + p)),
        ],
        out_specs=[pl.BlockSpec((1, length, LANES), lambda b, p: (b, 0, p))] * 2,
        out_shape=[out, out],
        compiler_params=_params(("arbitrary", "arbitrary")),
        name=f"dil_attn_g{group}",
    )(qkv3, qkv3, qkv3)


def _to_residue(a, batch, dilation):
    n, c = a.shape
    length = n // batch // dilation
    return a.reshape(batch, length, dilation, c).transpose(0, 2, 1, 3).reshape(batch * dilation, length, c)


def _from_residue(a, batch, dilation):
    nseq, length, c = a.shape
    return a.reshape(batch, dilation, length, c).transpose(0, 2, 1, 3).reshape(batch * dilation * length, c)


def _outproj_kernel(x_ref, oa_ref, o0_ref, o1_ref, o2_ref, l0_ref, l1_ref, l2_ref, mod_ref, g_ref,
                    wg_ref, bg_ref, wpa_ref, wpb_ref, wo_ref, out_ref):
    d = x_ref.shape[1]
    x = x_ref[...]
    h = _modulated_norm(x, g_ref[...], mod_ref[0, SC1:SC1 + 1, :], mod_ref[0, SH1:SH1 + 1, :]).astype(BF16)

    lses = [l0_ref[...], l1_ref[...], l2_ref[...]]
    outs = [o0_ref[...], o1_ref[...], o2_ref[...]]
    top = jnp.maximum(jnp.maximum(lses[0], lses[1]), lses[2])
    es = [jnp.exp(l - top) for l in lses]
    den = es[0] + es[1] + es[2]
    ob = (es[0] * outs[0] + es[1] * outs[1] + es[2] * outs[2]) / den

    ya = jnp.dot(oa_ref[...], wpa_ref[...], preferred_element_type=F32)
    yb = jnp.dot(ob.astype(BF16), wpb_ref[...], preferred_element_type=F32)
    ga = jax.nn.sigmoid(jnp.dot(h, wg_ref[:, :d], preferred_element_type=F32) + bg_ref[:, :d])
    mix = ga * ya
    gb = jax.nn.sigmoid(jnp.dot(h, wg_ref[:, d:], preferred_element_type=F32) + bg_ref[:, d:])
    mix = mix + gb * yb
    y = jnp.dot(mix.astype(BF16), wo_ref[...], preferred_element_type=F32)
    out_ref[...] = x + mod_ref[0, GT1:GT1 + 1, :] * y


def _outproj(x2, oa, o_groups, lse_groups, mod, g, w_gate, b_gate, w_pa, w_pb, w_o, seq, tm=512):
    n, d = x2.shape
    blocks_per_batch = seq // tm
    row = lambda c: pl.BlockSpec((tm, c), lambda i: (i, 0))
    full = lambda a: pl.BlockSpec(a.shape, lambda i: (0,) * a.ndim)
    return pl.pallas_call(
        _outproj_kernel,
        grid=(n // tm,),
        in_specs=[row(d), row(NA_WIDTH)] + [row(DIL_OUT_WIDTH)] * 6 + [
            pl.BlockSpec((1, 6, d), lambda i: (i // blocks_per_batch, 0, 0)),
            full(g), full(w_gate), full(b_gate), full(w_pa), full(w_pb), full(w_o),
        ],
        out_specs=row(d),
        out_shape=jax.ShapeDtypeStruct((n, d), F32),
        compiler_params=_params(("arbitrary",)),
        name="out_proj",
    )(x2, oa, *o_groups, *lse_groups, mod, g, w_gate, b_gate, w_pa, w_pb, w_o)


def _first_index_of_max(cur, idx, size):
    m = jnp.max(cur, axis=0, keepdims=True)
    first = jnp.min(jnp.where(cur == m, idx, size), axis=0, keepdims=True)
    return m, first


def _route_transposed(logits_t, e_bias):
    tokens = logits_t.shape[1]
    per_group = N_EXPERTS // N_EXPERT_GROUPS
    scores = jax.nn.sigmoid(logits_t)
    biased = scores + e_bias
    midx = lax.broadcasted_iota(jnp.int32, (per_group, tokens), 0)
    grp_scores = []
    for g in range(N_EXPERT_GROUPS):
        vals = biased[g * per_group:(g + 1) * per_group, :]
        m1, first = _first_index_of_max(vals, midx, per_group)
        m2 = jnp.max(jnp.where(midx == first, -jnp.inf, vals), axis=0, keepdims=True)
        grp_scores.append(m1 + m2)
    cur = jnp.concatenate(grp_scores, axis=0)
    gidx = lax.broadcasted_iota(jnp.int32, (N_EXPERT_GROUPS, tokens), 0)
    grp_sel = jnp.zeros((N_EXPERT_GROUPS, tokens), jnp.bool_)
    for _ in range(TOP_GROUPS):
        _, first = _first_index_of_max(cur, gidx, N_EXPERT_GROUPS)
        pick = gidx == first
        grp_sel = jnp.logical_or(grp_sel, pick)
        cur = jnp.where(pick, -jnp.inf, cur)
    rows = []
    for g in range(N_EXPERT_GROUPS):
        vals = biased[g * per_group:(g + 1) * per_group, :]
        rows.append(jnp.where(grp_sel[g:g + 1, :], vals, NEG_INF))
    cur = jnp.concatenate(rows, axis=0)
    eidx = lax.broadcasted_iota(jnp.int32, (N_EXPERTS, tokens), 0)
    sel = jnp.zeros((N_EXPERTS, tokens), jnp.bool_)
    for _ in range(TOP_K):
        _, first = _first_index_of_max(cur, eidx, N_EXPERTS)
        pick = eidx == first
        sel = jnp.logical_or(sel, pick)
        cur = jnp.where(pick, -jnp.inf, cur)
    w = jnp.where(sel, scores, 0.0)
    return w / jnp.sum(w, axis=0, keepdims=True) * ROUTED_SCALE


def _moe_kernel(x_ref, mod_ref, g_ref, wr_ref, eb_ref, wgu_ref, wd_ref, o_ref, h_ref, cw_ref, acc_ref):
    e = pl.program_id(1)
    tm = x_ref.shape[0]

    @pl.when(e == 0)
    def _():
        h = _modulated_norm(x_ref[...], g_ref[...], mod_ref[0, SC2:SC2 + 1, :], mod_ref[0, SH2:SH2 + 1, :])
        h_ref[...] = h.astype(BF16)
        logits_t = lax.dot_general(wr_ref[...], h, (((1,), (1,)), ((), ())), preferred_element_type=F32,
                                   precision=lax.Precision.HIGHEST)
        cw_t = _route_transposed(logits_t, eb_ref[...])
        pad = jnp.concatenate([cw_t, jnp.ones((1, tm), F32), jnp.zeros((LANES - N_EXPERTS - 1, tm), F32)],
                              axis=0)
        cw_ref[...] = pad.T
        acc_ref[...] = jnp.zeros_like(acc_ref)

    gu = jnp.dot(h_ref[...], wgu_ref[0], preferred_element_type=F32)
    gate = gu[:, :EXPERT_DIM]
    up = gu[:, EXPERT_DIM:]
    lane = lax.broadcasted_iota(jnp.int32, (tm, LANES), 1)
    col = jnp.sum(jnp.where(lane == e, cw_ref[...], 0.0), axis=1, keepdims=True)
    a = (gate * jax.nn.sigmoid(gate)) * up * col
    acc_ref[...] += jnp.dot(a.astype(BF16), wd_ref[0], preferred_element_type=F32)

    @pl.when(e == pl.num_programs(1) - 1)
    def _():
        o_ref[...] = x_ref[...] + mod_ref[0, GT2:GT2 + 1, :] * acc_ref[...]


def _moe(x2, mod, g, w_router_t, e_bias, w_gate_up, w_down, seq, tm=1024):
    n, d = x2.shape
    n_all = w_gate_up.shape[0]
    blocks_per_batch = seq // tm
    return pl.pallas_call(
        _moe_kernel,
        grid=(n // tm, n_all),
        in_specs=[
            pl.BlockSpec((tm, d), lambda i, e: (i, 0)),
            pl.BlockSpec((1, 6, d), lambda i, e: (i // blocks_per_batch, 0, 0)),
            pl.BlockSpec((1, d), lambda i, e: (0, 0)),
            pl.BlockSpec(w_router_t.shape, lambda i, e: (0, 0)),
            pl.BlockSpec(e_bias.shape, lambda i, e: (0, 0)),
            pl.BlockSpec((1, d, 2 * EXPERT_DIM), lambda i, e: (e, 0, 0)),
            pl.BlockSpec((1, EXPERT_DIM, d), lambda i, e: (e, 0, 0)),
        ],
        out_specs=pl.BlockSpec((tm, d), lambda i, e: (i, 0)),
        out_shape=jax.ShapeDtypeStruct((n, d), F32),
        scratch_shapes=[pltpu.VMEM((tm, d), BF16), pltpu.VMEM((tm, LANES), F32), pltpu.VMEM((tm, d), F32)],
        compiler_params=_params(("arbitrary", "arbitrary")),
        name="moe_ffn",
    )(x2, mod, g, w_router_t, e_bias, w_gate_up, w_down)


def _final_norm_kernel(x_ref, g_ref, o_ref):
    x = x_ref[...]
    o_ref[...] = x * lax.rsqrt(jnp.mean(x * x, axis=-1, keepdims=True) + EPS) * g_ref[...]


def _final_norm(x2, g, tm=1024):
    n, d = x2.shape
    return pl.pallas_call(
        _final_norm_kernel,
        grid=(n // tm,),
        in_specs=[pl.BlockSpec((tm, d), lambda i: (i, 0)), pl.BlockSpec((1, d), lambda i: (0, 0))],
        out_specs=pl.BlockSpec((tm, d), lambda i: (i, 0)),
        out_shape=jax.ShapeDtypeStruct((n, d), F32),
        compiler_params=_params(("arbitrary",)),
        name="final_norm",
    )(x2, g)


def _token_mixer(x2, mod, g_mix, w_in, b_gate, rpb, w_pa, w_pb, w_o, batch, seq):
    d = x2.shape[1]
    qkv = _inproj(x2, mod, g_mix.reshape(1, d), w_in[:, :QKV_COLS].astype(BF16), seq)
    o_a = _neighbourhood_attention(qkv, _na_bias_table(rpb), batch, seq)

    dil0 = 3 * NA_WIDTH
    o_groups, lse_groups = [], []
    for grp, (_, dilation) in enumerate(DIL_GROUPS):
        cols = [dil0 + part * DIL_WIDTH + grp * DIL_OUT_WIDTH for part in range(3)]
        if dilation == 1:
            seqs = qkv.reshape(batch, seq, QKV_COLS)
            offsets = tuple(c // LANES for c in cols)
        else:
            parts = jnp.concatenate([qkv[:, c:c + DIL_OUT_WIDTH] for c in cols], axis=1)
            seqs = _to_residue(parts, batch, dilation)
            offsets = tuple(part * DIL_OUT_WIDTH // LANES for part in range(3))
        o_g, lse_g = _dilated_attention(seqs, offsets, grp)
        o_groups.append(_from_residue(o_g, batch, dilation))
        lse_groups.append(_from_residue(lse_g, batch, dilation))

    return _outproj(x2, o_a, o_groups, lse_groups, mod, g_mix.reshape(1, d),
                    w_in[:, QKV_COLS:].astype(BF16), b_gate.reshape(1, -1),
                    w_pa.astype(BF16), w_pb.astype(BF16), w_o.astype(BF16), seq)


def _moe_layer(x2, mod, g_ffn, w_router, e_bias, we_gate, we_up, we_down, ws_gate, ws_up, ws_down, seq):
    d = x2.shape[1]
    w_gate_up = jnp.concatenate(
        [jnp.concatenate([we_gate, we_up], axis=-1), jnp.concatenate([ws_gate, ws_up], axis=-1)[None]],
        axis=0).astype(BF16)
    w_down = jnp.concatenate([we_down, ws_down[None]], axis=0).astype(BF16)
    return _moe(x2, mod, g_ffn.reshape(1, d), w_router.T, e_bias.reshape(-1, 1), w_gate_up, w_down, seq)


def kernel(x, c, w_ada, b_ada, g_mix, w_in, b_gate, rpb, w_pa, w_pb, w_o, g_ffn, w_router, e_bias,
           we_gate, we_up, we_down, ws_gate, ws_up, ws_down, g_final):
    batch, seq, d = x.shape
    depth = w_ada.shape[0]
    mods = _ada(c, w_ada, b_ada).reshape(depth, batch, 6, d)
    x2 = x.reshape(batch * seq, d)
    for l in range(depth):
        x2 = _token_mixer(x2, mods[l], g_mix[l], w_in[l], b_gate[l], rpb[l], w_pa[l], w_pb[l], w_o[l],
                          batch, seq)
        x2 = _moe_layer(x2, mods[l], g_ffn[l], w_router[l], e_bias[l], we_gate[l], we_up[l], we_down[l],
                        ws_gate[l], ws_up[l], ws_down[l], seq)
    return _final_norm(x2, g_final.reshape(1, d)).reshape(batch, seq, d)
```

```python
import functools

import numpy as np
import jax
import jax.numpy as jnp
from jax import lax
from jax.experimental import pallas as pl
from jax.experimental.pallas import tpu as pltpu

HEAD_DIM = 64
GRID_W = 64
NA_HEADS = 8
NA_WIN_ROWS = 8
NA_WIN_COLS = 16
DIL_GROUPS = ((128, 1), (512, 4), (2048, 16))
DIL_HEADS_PER_GROUP = 4
N_DIL_GROUPS = len(DIL_GROUPS)
NA_WIDTH = NA_HEADS * HEAD_DIM
DIL_WIDTH = N_DIL_GROUPS * DIL_HEADS_PER_GROUP * HEAD_DIM
DIL_OUT_WIDTH = DIL_HEADS_PER_GROUP * HEAD_DIM
QKV_COLS = 3 * (NA_WIDTH + DIL_WIDTH)
N_EXPERTS = 64
TOP_K = 8
N_EXPERT_GROUPS = 8
TOP_GROUPS = 4
EXPERT_DIM = 256
ROUTED_SCALE = 2.5
ALIBI_MAX = 8.0
EPS = 1e-6
NEG_INF = -1e30

LANES = 128
HEADS_PER_LANE_TILE = LANES // HEAD_DIM
DIL_BLOCK = 64
VMEM_LIMIT_BYTES = 56 * 1024 * 1024

F32 = jnp.float32
BF16 = jnp.bfloat16

SH1, SC1, GT1, SH2, SC2, GT2 = range(6)


def _params(sem):
    return pltpu.CompilerParams(dimension_semantics=sem, vmem_limit_bytes=VMEM_LIMIT_BYTES)


def _modulated_norm(x, g, scale, shift):
    r = lax.rsqrt(jnp.mean(x * x, axis=-1, keepdims=True) + EPS)
    return (x * r * g) * (1.0 + scale) + shift


def _ada_kernel(c_ref, w_ref, b_ref, o_ref):
    c = c_ref[...]
    act = c * jax.nn.sigmoid(c)
    o_ref[0] = jnp.dot(act, w_ref[0], preferred_element_type=F32,
                       precision=lax.Precision.HIGHEST) + b_ref[0]


def _ada(c, w_ada, b_ada):
    depth, d, six_d = w_ada.shape
    b = c.shape[0]
    tn = d
    return pl.pallas_call(
        _ada_kernel,
        grid=(depth, six_d // tn),
        in_specs=[
            pl.BlockSpec((b, d), lambda l, j: (0, 0)),
            pl.BlockSpec((1, d, tn), lambda l, j: (l, 0, j)),
            pl.BlockSpec((1, 1, tn), lambda l, j: (l, 0, j)),
        ],
        out_specs=pl.BlockSpec((1, b, tn), lambda l, j: (l, 0, j)),
        out_shape=jax.ShapeDtypeStruct((depth, b, six_d), F32),
        compiler_params=_params(("arbitrary", "arbitrary")),
        name="ada_mod",
    )(c, w_ada, b_ada.reshape(depth, 1, six_d))


def _inproj_kernel(x_ref, mod_ref, g_ref, w_ref, o_ref, h_ref):
    @pl.when(pl.program_id(1) == 0)
    def _():
        h = _modulated_norm(x_ref[...], g_ref[...], mod_ref[0, SC1:SC1 + 1, :], mod_ref[0, SH1:SH1 + 1, :])
        h_ref[...] = h.astype(BF16)

    o_ref[...] = jnp.dot(h_ref[...], w_ref[...], preferred_element_type=F32).astype(BF16)


def _inproj(x2, mod, g, w_qkv, seq, tm=1024, tn=768):
    n, d = x2.shape
    cols = w_qkv.shape[1]
    blocks_per_batch = seq // tm
    return pl.pallas_call(
        _inproj_kernel,
        grid=(n // tm, cols // tn),
        in_specs=[
            pl.BlockSpec((tm, d), lambda i, j: (i, 0)),
            pl.BlockSpec((1, 6, d), lambda i, j: (i // blocks_per_batch, 0, 0)),
            pl.BlockSpec((1, d), lambda i, j: (0, 0)),
            pl.BlockSpec((d, tn), lambda i, j: (0, j)),
        ],
        out_specs=pl.BlockSpec((tm, tn), lambda i, j: (i, j)),
        out_shape=jax.ShapeDtypeStruct((n, cols), BF16),
        scratch_shapes=[pltpu.VMEM((tm, d), BF16)],
        compiler_params=_params(("arbitrary", "arbitrary")),
        name="in_proj",
    )(x2, mod, g, w_qkv)


def _na_bias_table(rpb):
    heads = rpb.shape[0]
    cols = np.arange(GRID_W)
    col_start = np.clip(cols - NA_WIN_COLS // 2, 0, GRID_W - NA_WIN_COLS)
    col_mask = (cols[None, :] >= col_start[:, None]) & (cols[None, :] < col_start[:, None] + NA_WIN_COLS)
    dc = np.clip(cols[None, :] - cols[:, None], -(NA_WIN_COLS - 1), NA_WIN_COLS - 1) + NA_WIN_COLS - 1
    rpb_cols = rpb[:, :, dc].astype(F32)
    dr = np.arange(NA_WIN_ROWS)[None, :] - np.arange(NA_WIN_ROWS)[:, None] + NA_WIN_ROWS - 1
    t = rpb_cols[:, dr]
    t = t.transpose(0, 1, 3, 2, 4)
    t = jnp.where(col_mask[:, None, :], t, NEG_INF)
    t = t.reshape(heads // HEADS_PER_LANE_TILE, HEADS_PER_LANE_TILE, NA_WIN_ROWS, GRID_W, NA_WIN_ROWS * GRID_W)
    return t.transpose(0, 2, 1, 3, 4).reshape(heads // HEADS_PER_LANE_TILE, NA_WIN_ROWS,
                                              HEADS_PER_LANE_TILE * GRID_W, NA_WIN_ROWS * GRID_W)


def _stack_heads(q, low):
    scaled = q * (HEAD_DIM ** -0.5)
    zero = jnp.zeros_like(scaled)
    return jnp.concatenate([jnp.where(low, scaled, zero), jnp.where(low, zero, scaled)], axis=0)


def _stacked_attention(items):
    scores = [lax.dot_general(q2, kw, (((1,), (1,)), ((), ())), preferred_element_type=F32) + bias
              for q2, kw, _, bias in items]
    probs = []
    for s in scores:
        m = jnp.max(s, axis=-1, keepdims=True)
        p = jnp.exp(s - m)
        probs.append((p.astype(BF16), m, jnp.sum(p, axis=-1, keepdims=True)))
    return [(jnp.dot(p, vw, preferred_element_type=F32) / z, m, z)
            for (p, m, z), (_, _, vw, _) in zip(probs, items)]


def _unstack_heads(a, low):
    half = a.shape[0] // HEADS_PER_LANE_TILE
    return jnp.where(low, a[:half], a[half:])


def _na_kernel(q_ref, k_ref, v_ref, bias_ref, o_ref, *, rows, rows_per_step):
    kr = NA_WIN_ROWS
    low = lax.broadcasted_iota(jnp.int32, (GRID_W, LANES), 1) < HEAD_DIM

    def body(i, carry):
        items, qrows = [], []
        for u in range(rows_per_step):
            r = i * rows_per_step + u
            rs = jnp.clip(r - kr // 2, 0, rows - kr)
            qrows.append(pl.ds(pl.multiple_of(r * GRID_W, GRID_W), GRID_W))
            wrows = pl.ds(pl.multiple_of(rs * GRID_W, GRID_W), kr * GRID_W)
            items.append((_stack_heads(q_ref[qrows[-1], :], low), k_ref[wrows, :], v_ref[wrows, :],
                          bias_ref[0, r - rs]))
        for rows_u, (o, _, _) in zip(qrows, _stacked_attention(items)):
            o_ref[rows_u, :] = _unstack_heads(o, low).astype(o_ref.dtype)
        return carry

    lax.fori_loop(0, rows // rows_per_step, body, 0)


def _neighbourhood_attention(qkv, bias, batch, seq, rows_per_step=8):
    n = qkv.shape[0]
    rows = seq // GRID_W
    pairs = NA_WIDTH // LANES
    return pl.pallas_call(
        functools.partial(_na_kernel, rows=rows, rows_per_step=rows_per_step),
        grid=(pairs, batch),
        in_specs=[
            pl.BlockSpec((seq, LANES), lambda p, b: (b, p)),
            pl.BlockSpec((seq, LANES), lambda p, b: (b, pairs + p)),
            pl.BlockSpec((seq, LANES), lambda p, b: (b, 2 * pairs + p)),
            pl.BlockSpec((1, NA_WIN_ROWS, HEADS_PER_LANE_TILE * GRID_W, NA_WIN_ROWS * GRID_W),
                         lambda p, b: (p, 0, 0, 0)),
        ],
        out_specs=pl.BlockSpec((seq, LANES), lambda p, b: (b, p)),
        out_shape=jax.ShapeDtypeStruct((n, NA_WIDTH), BF16),
        compiler_params=_params(("arbitrary", "arbitrary")),
        name="na_attn",
    )(qkv, qkv, qkv, bias)


def _alibi_slopes():
    n = N_DIL_GROUPS * DIL_HEADS_PER_GROUP
    s = np.exp2(-ALIBI_MAX * np.arange(1, n + 1, dtype=np.float64) / n).astype(np.float32)
    return s.reshape(N_DIL_GROUPS, DIL_HEADS_PER_GROUP)


def _dil_bias_table(group):
    blk = DIL_BLOCK
    dilation = DIL_GROUPS[group][1]
    slopes = _alibi_slopes()[group]
    qi = np.arange(blk)[:, None]
    kj = np.arange(3 * blk)[None, :]
    tables = []
    for shift in range(3):
        arel = np.abs(kj - qi - shift * blk)
        dist = (dilation * arel).astype(np.float32)
        per_head = [np.where(arel <= blk, -slopes[h] * dist, np.float32(NEG_INF)) for h in range(DIL_HEADS_PER_GROUP)]
        tables.append(np.stack(per_head))
    t = np.stack(tables, axis=1).astype(np.float32)
    pairs = DIL_HEADS_PER_GROUP // HEADS_PER_LANE_TILE
    t = t.reshape(pairs, HEADS_PER_LANE_TILE, 3, blk, 3 * blk).transpose(0, 2, 1, 3, 4)
    return t.reshape(pairs, 3, HEADS_PER_LANE_TILE * blk, 3 * blk)


def _dil_kernel(q_ref, k_ref, v_ref, bias_ref, o_ref, lse_ref, *, nb, blocks_per_step):
    blk = DIL_BLOCK
    win = 3 * blk
    low = lax.broadcasted_iota(jnp.int32, (blk, LANES), 1) < HEAD_DIM
    steps = q_ref.shape[0] * nb

    def body(i, carry):
        items, dst = [], []
        for u in range(blocks_per_step):
            t = i * blocks_per_step + u
            sq = t // nb
            n = t % nb
            wb = jnp.clip(n - 1, 0, nb - 3)
            qrows = pl.ds(pl.multiple_of(n * blk, blk), blk)
            wrows = pl.ds(pl.multiple_of(wb * blk, blk), win)
            dst.append((sq, qrows))
            items.append((_stack_heads(q_ref[sq, qrows, :], low), k_ref[sq, wrows, :], v_ref[sq, wrows, :],
                          bias_ref[0, n - wb]))
        for (sq, qrows), (o, m, z) in zip(dst, _stacked_attention(items)):
            o_ref[sq, qrows, :] = _unstack_heads(o, low)
            lse_ref[sq, qrows, :] = _unstack_heads(jnp.broadcast_to(m + jnp.log(z), o.shape), low)
        return carry

    lax.fori_loop(0, steps // blocks_per_step, body, 0)


def _dilated_attention(qkv3, col_offsets, group, tokens_per_step=4096, blocks_per_step=8):
    nseq, length, _ = qkv3.shape
    nb = length // DIL_BLOCK
    sb = max(1, tokens_per_step // length)
    pairs = DIL_OUT_WIDTH // LANES
    qo, ko, vo = col_offsets
    bias = jnp.asarray(_dil_bias_table(group))
    out = jax.ShapeDtypeStruct((nseq, length, DIL_OUT_WIDTH), F32)
    return pl.pallas_call(
        functools.partial(_dil_kernel, nb=nb, blocks_per_step=blocks_per_step),
        grid=(pairs, nseq // sb),
        in_specs=[
            pl.BlockSpec((sb, length, LANES), lambda p, b: (b, 0, qo + p)),
            pl.BlockSpec((sb, length, LANES), lambda p, b: (b, 0, ko + p)),
            pl.BlockSpec((sb, length, LANES), lambda p, b: (b, 0, vo + p)),
            pl.BlockSpec((1,) + bias.shape[1:], lambda p, b: (p, 0, 0, 0)),
        ],
        out_specs=[pl.BlockSpec((sb, length, LANES), lambda p, b: (b, 0, p))] * 2,
        out_shape=[out, out],
        compiler_params=_params(("arbitrary", "arbitrary")),
        name=f"dil_attn_g{group}",
    )(qkv3, qkv3, qkv3, bias)


def _to_residue(a, batch, dilation):
    n, c = a.shape
    length = n // batch // dilation
    return a.reshape(batch, length, dilation, c).transpose(0, 2, 1, 3).reshape(batch * dilation, length, c)


def _from_residue(a, batch, dilation):
    nseq, length, c = a.shape
    return a.reshape(batch, dilation, length, c).transpose(0, 2, 1, 3).reshape(batch * dilation * length, c)


def _outproj_kernel(x_ref, oa_ref, o0_ref, o1_ref, o2_ref, l0_ref, l1_ref, l2_ref, mod_ref, g_ref,
                    wg_ref, bg_ref, wpa_ref, wpb_ref, wo_ref, out_ref):
    d = x_ref.shape[1]
    x = x_ref[...]
    h = _modulated_norm(x, g_ref[...], mod_ref[0, SC1:SC1 + 1, :], mod_ref[0, SH1:SH1 + 1, :]).astype(BF16)

    lses = [l0_ref[...], l1_ref[...], l2_ref[...]]
    outs = [o0_ref[...], o1_ref[...], o2_ref[...]]
    top = jnp.maximum(jnp.maximum(lses[0], lses[1]), lses[2])
    es = [jnp.exp(l - top) for l in lses]
    den = es[0] + es[1] + es[2]
    ob = (es[0] * outs[0] + es[1] * outs[1] + es[2] * outs[2]) / den

    ya = jnp.dot(oa_ref[...], wpa_ref[...], preferred_element_type=F32)
    yb = jnp.dot(ob.astype(BF16), wpb_ref[...], preferred_element_type=F32)
    ga = jax.nn.sigmoid(jnp.dot(h, wg_ref[:, :d], preferred_element_type=F32) + bg_ref[:, :d])
    mix = ga * ya
    gb = jax.nn.sigmoid(jnp.dot(h, wg_ref[:, d:], preferred_element_type=F32) + bg_ref[:, d:])
    mix = mix + gb * yb
    y = jnp.dot(mix.astype(BF16), wo_ref[...], preferred_element_type=F32)
    out_ref[...] = x + mod_ref[0, GT1:GT1 + 1, :] * y


def _outproj(x2, oa, o_groups, lse_groups, mod, g, w_gate, b_gate, w_pa, w_pb, w_o, seq, tm=512):
    n, d = x2.shape
    blocks_per_batch = seq // tm
    row = lambda c: pl.BlockSpec((tm, c), lambda i: (i, 0))
    full = lambda a: pl.BlockSpec(a.shape, lambda i: (0,) * a.ndim)
    return pl.pallas_call(
        _outproj_kernel,
        grid=(n // tm,),
        in_specs=[row(d), row(NA_WIDTH)] + [row(DIL_OUT_WIDTH)] * 6 + [
            pl.BlockSpec((1, 6, d), lambda i: (i // blocks_per_batch, 0, 0)),
            full(g), full(w_gate), full(b_gate), full(w_pa), full(w_pb), full(w_o),
        ],
        out_specs=row(d),
        out_shape=jax.ShapeDtypeStruct((n, d), F32),
        compiler_params=_params(("arbitrary",)),
        name="out_proj",
    )(x2, oa, *o_groups, *lse_groups, mod, g, w_gate, b_gate, w_pa, w_pb, w_o)


def _first_index_of_max(cur, idx, size):
    m = jnp.max(cur, axis=0, keepdims=True)
    first = jnp.min(jnp.where(cur == m, idx, size), axis=0, keepdims=True)
    return m, first


def _route_transposed(logits_t, e_bias):
    tokens = logits_t.shape[1]
    per_group = N_EXPERTS // N_EXPERT_GROUPS
    scores = jax.nn.sigmoid(logits_t)
    biased = scores + e_bias
    midx = lax.broadcasted_iota(jnp.int32, (per_group, tokens), 0)
    grp_scores = []
    for g in range(N_EXPERT_GROUPS):
        vals = biased[g * per_group:(g + 1) * per_group, :]
        m1, first = _first_index_of_max(vals, midx, per_group)
        m2 = jnp.max(jnp.where(midx == first, -jnp.inf, vals), axis=0, keepdims=True)
        grp_scores.append(m1 + m2)
    cur = jnp.concatenate(grp_scores, axis=0)
    gidx = lax.broadcasted_iota(jnp.int32, (N_EXPERT_GROUPS, tokens), 0)
    grp_sel = jnp.zeros((N_EXPERT_GROUPS, tokens), jnp.bool_)
    for _ in range(TOP_GROUPS):
        _, first = _first_index_of_max(cur, gidx, N_EXPERT_GROUPS)
        pick = gidx == first
        grp_sel = jnp.logical_or(grp_sel, pick)
        cur = jnp.where(pick, -jnp.inf, cur)
    rows = []
    for g in range(N_EXPERT_GROUPS):
        vals = biased[g * per_group:(g + 1) * per_group, :]
        rows.append(jnp.where(grp_sel[g:g + 1, :], vals, NEG_INF))
    cur = jnp.concatenate(rows, axis=0)
    eidx = lax.broadcasted_iota(jnp.int32, (N_EXPERTS, tokens), 0)
    sel = jnp.zeros((N_EXPERTS, tokens), jnp.bool_)
    for _ in range(TOP_K):
        _, first = _first_index_of_max(cur, eidx, N_EXPERTS)
        pick = eidx == first
        sel = jnp.logical_or(sel, pick)
        cur = jnp.where(pick, -jnp.inf, cur)
    w = jnp.where(sel, scores, 0.0)
    return w / jnp.sum(w, axis=0, keepdims=True) * ROUTED_SCALE


def _moe_kernel(x_ref, mod_ref, g_ref, wr_ref, eb_ref, wgu_ref, wd_ref, o_ref, h_ref, cw_ref, acc_ref):
    e = pl.program_id(1)
    tm = x_ref.shape[0]

    @pl.when(e == 0)
    def _():
        h = _modulated_norm(x_ref[...], g_ref[...], mod_ref[0, SC2:SC2 + 1, :], mod_ref[0, SH2:SH2 + 1, :])
        h_ref[...] = h.astype(BF16)
        logits_t = lax.dot_general(wr_ref[...], h, (((1,), (1,)), ((), ())), preferred_element_type=F32,
                                   precision=lax.Precision.HIGHEST)
        cw_t = _route_transposed(logits_t, eb_ref[...])
        pad = jnp.concatenate([cw_t, jnp.ones((1, tm), F32), jnp.zeros((LANES - N_EXPERTS - 1, tm), F32)],
                              axis=0)
        cw_ref[...] = pad.T
        acc_ref[...] = jnp.zeros_like(acc_ref)

    gu = jnp.dot(h_ref[...], wgu_ref[0], preferred_element_type=F32)
    gate = gu[:, :EXPERT_DIM]
    up = gu[:, EXPERT_DIM:]
    lane = lax.broadcasted_iota(jnp.int32, (tm, LANES), 1)
    col = jnp.sum(jnp.where(lane == e, cw_ref[...], 0.0), axis=1, keepdims=True)
    a = (gate * jax.nn.sigmoid(gate)) * up * col
    acc_ref[...] += jnp.dot(a.astype(BF16), wd_ref[0], preferred_element_type=F32)

    @pl.when(e == pl.num_programs(1) - 1)
    def _():
        o_ref[...] = x_ref[...] + mod_ref[0, GT2:GT2 + 1, :] * acc_ref[...]


def _moe(x2, mod, g, w_router_t, e_bias, w_gate_up, w_down, seq, tm=1024):
    n, d = x2.shape
    n_all = w_gate_up.shape[0]
    blocks_per_batch = seq // tm
    return pl.pallas_call(
        _moe_kernel,
        grid=(n // tm, n_all),
        in_specs=[
            pl.BlockSpec((tm, d), lambda i, e: (i, 0)),
            pl.BlockSpec((1, 6, d), lambda i, e: (i // blocks_per_batch, 0, 0)),
            pl.BlockSpec((1, d), lambda i, e: (0, 0)),
            pl.BlockSpec(w_router_t.shape, lambda i, e: (0, 0)),
            pl.BlockSpec(e_bias.shape, lambda i, e: (0, 0)),
            pl.BlockSpec((1, d, 2 * EXPERT_DIM), lambda i, e: (e, 0, 0)),
            pl.BlockSpec((1, EXPERT_DIM, d), lambda i, e: (e, 0, 0)),
        ],
        out_specs=pl.BlockSpec((tm, d), lambda i, e: (i, 0)),
        out_shape=jax.ShapeDtypeStruct((n, d), F32),
        scratch_shapes=[pltpu.VMEM((tm, d), BF16), pltpu.VMEM((tm, LANES), F32), pltpu.VMEM((tm, d), F32)],
        compiler_params=_params(("arbitrary", "arbitrary")),
        name="moe_ffn",
    )(x2, mod, g, w_router_t, e_bias, w_gate_up, w_down)


def _final_norm_kernel(x_ref, g_ref, o_ref):
    x = x_ref[...]
    o_ref[...] = x * lax.rsqrt(jnp.mean(x * x, axis=-1, keepdims=True) + EPS) * g_ref[...]


def _final_norm(x2, g, tm=1024):
    n, d = x2.shape
    return pl.pallas_call(
        _final_norm_kernel,
        grid=(n // tm,),
        in_specs=[pl.BlockSpec((tm, d), lambda i: (i, 0)), pl.BlockSpec((1, d), lambda i: (0, 0))],
        out_specs=pl.BlockSpec((tm, d), lambda i: (i, 0)),
        out_shape=jax.ShapeDtypeStruct((n, d), F32),
        compiler_params=_params(("arbitrary",)),
        name="final_norm",
    )(x2, g)


def _token_mixer(x2, mod, g_mix, w_in, b_gate, rpb, w_pa, w_pb, w_o, batch, seq):
    d = x2.shape[1]
    qkv = _inproj(x2, mod, g_mix.reshape(1, d), w_in[:, :QKV_COLS].astype(BF16), seq)
    o_a = _neighbourhood_attention(qkv, _na_bias_table(rpb), batch, seq)

    dil0 = 3 * NA_WIDTH
    o_groups, lse_groups = [], []
    for grp, (_, dilation) in enumerate(DIL_GROUPS):
        cols = [dil0 + part * DIL_WIDTH + grp * DIL_OUT_WIDTH for part in range(3)]
        if dilation == 1:
            seqs = qkv.reshape(batch, seq, QKV_COLS)
            offsets = tuple(c // LANES for c in cols)
        else:
            parts = jnp.concatenate([qkv[:, c:c + DIL_OUT_WIDTH] for c in cols], axis=1)
            seqs = _to_residue(parts, batch, dilation)
            offsets = tuple(part * DIL_OUT_WIDTH // LANES for part in range(3))
        o_g, lse_g = _dilated_attention(seqs, offsets, grp)
        o_groups.append(_from_residue(o_g, batch, dilation))
        lse_groups.append(_from_residue(lse_g, batch, dilation))

    return _outproj(x2, o_a, o_groups, lse_groups, mod, g_mix.reshape(1, d),
                    w_in[:, QKV_COLS:].astype(BF16), b_gate.reshape(1, -1),
                    w_pa.astype(BF16), w_pb.astype(BF16), w_o.astype(BF16), seq)


def _moe_layer(x2, mod, g_ffn, w_router, e_bias, we_gate, we_up, we_down, ws_gate, ws_up, ws_down, seq):
    d = x2.shape[1]
    w_gate_up = jnp.concatenate(
        [jnp.concatenate([we_gate, we_up], axis=-1), jnp.concatenate([ws_gate, ws_up], axis=-1)[None]],
        axis=0).astype(BF16)
    w_down = jnp.concatenate([we_down, ws_down[None]], axis=0).astype(BF16)
    return _moe(x2, mod, g_ffn.reshape(1, d), w_router.T, e_bias.reshape(-1, 1), w_gate_up, w_down, seq)


def kernel(x, c, w_ada, b_ada, g_mix, w_in, b_gate, rpb, w_pa, w_pb, w_o, g_ffn, w_router, e_bias,
           we_gate, we_up, we_down, ws_gate, ws_up, ws_down, g_final):
    batch, seq, d = x.shape
    depth = w_ada.shape[0]
    mods = _ada(c, w_ada, b_ada).reshape(depth, batch, 6, d)
    x2 = x.reshape(batch * seq, d)
    for l in range(depth):
        x2 = _token_mixer(x2, mods[l], g_mix[l], w_in[l], b_gate[l], rpb[l], w_pa[l], w_pb[l], w_o[l],
                          batch, seq)
        x2 = _moe_layer(x2, mods[l], g_ffn[l], w_router[l], e_bias[l], we_gate[l], we_up[l], we_down[l],
                        ws_gate[l], ws_up[l], ws_down[l], seq)
    return _final_norm(x2, g_final.reshape(1, d)).reshape(batch, seq, d)
```

```python
import functools

import numpy as np
import jax
import jax.numpy as jnp
from jax import lax
from jax.experimental import pallas as pl
from jax.experimental.pallas import tpu as pltpu
from jax.experimental.pallas import tpu_sc as plsc

HEAD_DIM = 64
GRID_W = 64
NA_HEADS = 8
NA_WIN_ROWS = 8
NA_WIN_COLS = 16
DIL_GROUPS = ((128, 1), (512, 4), (2048, 16))
DIL_HEADS_PER_GROUP = 4
N_DIL_GROUPS = len(DIL_GROUPS)
NA_WIDTH = NA_HEADS * HEAD_DIM
DIL_WIDTH = N_DIL_GROUPS * DIL_HEADS_PER_GROUP * HEAD_DIM
DIL_OUT_WIDTH = DIL_HEADS_PER_GROUP * HEAD_DIM
QKV_COLS = 3 * (NA_WIDTH + DIL_WIDTH)
N_EXPERTS = 64
TOP_K = 8
N_EXPERT_GROUPS = 8
TOP_GROUPS = 4
EXPERT_DIM = 256
ROUTED_SCALE = 2.5
ALIBI_MAX = 8.0
EPS = 1e-6
NEG_INF = -1e30

LANES = 128
HEADS_PER_LANE_TILE = LANES // HEAD_DIM
DIL_BLOCK = 64
VMEM_LIMIT_BYTES = 56 * 1024 * 1024

SC_CORES = 2
SC_SUBCORES = 16
SC_WORKERS = SC_CORES * SC_SUBCORES
SC_CHUNK = 64
EXPERT_ROW_BLOCK = 512

F32 = jnp.float32
BF16 = jnp.bfloat16

SH1, SC1, GT1, SH2, SC2, GT2 = range(6)


def _params(sem):
    return pltpu.CompilerParams(dimension_semantics=sem, vmem_limit_bytes=VMEM_LIMIT_BYTES)


def _modulated_norm(x, g, scale, shift):
    r = lax.rsqrt(jnp.mean(x * x, axis=-1, keepdims=True) + EPS)
    return (x * r * g) * (1.0 + scale) + shift


def _ada_kernel(c_ref, w_ref, b_ref, o_ref):
    c = c_ref[...]
    act = c * jax.nn.sigmoid(c)
    o_ref[0] = jnp.dot(act, w_ref[0], preferred_element_type=F32,
                       precision=lax.Precision.HIGHEST) + b_ref[0]


def _ada(c, w_ada, b_ada):
    depth, d, six_d = w_ada.shape
    b = c.shape[0]
    tn = d
    return pl.pallas_call(
        _ada_kernel,
        grid=(depth, six_d // tn),
        in_specs=[
            pl.BlockSpec((b, d), lambda l, j: (0, 0)),
            pl.BlockSpec((1, d, tn), lambda l, j: (l, 0, j)),
            pl.BlockSpec((1, 1, tn), lambda l, j: (l, 0, j)),
        ],
        out_specs=pl.BlockSpec((1, b, tn), lambda l, j: (l, 0, j)),
        out_shape=jax.ShapeDtypeStruct((depth, b, six_d), F32),
        compiler_params=_params(("arbitrary", "arbitrary")),
        name="ada_mod",
    )(c, w_ada, b_ada.reshape(depth, 1, six_d))


def _inproj_kernel(x_ref, mod_ref, g_ref, w_ref, o_ref, h_ref):
    @pl.when(pl.program_id(1) == 0)
    def _():
        h = _modulated_norm(x_ref[...], g_ref[...], mod_ref[0, SC1:SC1 + 1, :], mod_ref[0, SH1:SH1 + 1, :])
        h_ref[...] = h.astype(BF16)

    o_ref[...] = jnp.dot(h_ref[...], w_ref[...], preferred_element_type=F32).astype(BF16)


def _inproj(x2, mod, g, w_qkv, seq, tm=1024, tn=768):
    n, d = x2.shape
    cols = w_qkv.shape[1]
    blocks_per_batch = seq // tm
    return pl.pallas_call(
        _inproj_kernel,
        grid=(n // tm, cols // tn),
        in_specs=[
            pl.BlockSpec((tm, d), lambda i, j: (i, 0)),
            pl.BlockSpec((1, 6, d), lambda i, j: (i // blocks_per_batch, 0, 0)),
            pl.BlockSpec((1, d), lambda i, j: (0, 0)),
            pl.BlockSpec((d, tn), lambda i, j: (0, j)),
        ],
        out_specs=pl.BlockSpec((tm, tn), lambda i, j: (i, j)),
        out_shape=jax.ShapeDtypeStruct((n, cols), BF16),
        scratch_shapes=[pltpu.VMEM((tm, d), BF16)],
        compiler_params=_params(("arbitrary", "arbitrary")),
        name="in_proj",
    )(x2, mod, g, w_qkv)


def _na_bias_table(rpb):
    heads = rpb.shape[0]
    cols = np.arange(GRID_W)
    col_start = np.clip(cols - NA_WIN_COLS // 2, 0, GRID_W - NA_WIN_COLS)
    col_mask = (cols[None, :] >= col_start[:, None]) & (cols[None, :] < col_start[:, None] + NA_WIN_COLS)
    dc = np.clip(cols[None, :] - cols[:, None], -(NA_WIN_COLS - 1), NA_WIN_COLS - 1) + NA_WIN_COLS - 1
    rpb_cols = rpb[:, :, dc].astype(F32)
    dr = np.arange(NA_WIN_ROWS)[None, :] - np.arange(NA_WIN_ROWS)[:, None] + NA_WIN_ROWS - 1
    t = rpb_cols[:, dr]
    t = t.transpose(0, 1, 3, 2, 4)
    t = jnp.where(col_mask[:, None, :], t, NEG_INF)
    t = t.reshape(heads // HEADS_PER_LANE_TILE, HEADS_PER_LANE_TILE, NA_WIN_ROWS, GRID_W, NA_WIN_ROWS * GRID_W)
    return t.transpose(0, 2, 1, 3, 4).reshape(heads // HEADS_PER_LANE_TILE, NA_WIN_ROWS,
                                              HEADS_PER_LANE_TILE * GRID_W, NA_WIN_ROWS * GRID_W)


def _stack_heads(q, low):
    scaled = q * (HEAD_DIM ** -0.5)
    zero = jnp.zeros_like(scaled)
    return jnp.concatenate([jnp.where(low, scaled, zero), jnp.where(low, zero, scaled)], axis=0)


def _stacked_attention(items):
    scores = [lax.dot_general(q2, kw, (((1,), (1,)), ((), ())), preferred_element_type=F32) + bias
              for q2, kw, _, bias in items]
    probs = []
    for s in scores:
        m = jnp.max(s, axis=-1, keepdims=True)
        p = jnp.exp(s - m)
        probs.append((p.astype(BF16), m, jnp.sum(p, axis=-1, keepdims=True)))
    return [(jnp.dot(p, vw, preferred_element_type=F32) / z, m, z)
            for (p, m, z), (_, _, vw, _) in zip(probs, items)]


def _unstack_heads(a, low):
    half = a.shape[0] // HEADS_PER_LANE_TILE
    return jnp.where(low, a[:half], a[half:])


def _na_kernel(q_ref, k_ref, v_ref, bias_ref, o_ref, *, rows, rows_per_step):
    kr = NA_WIN_ROWS
    low = lax.broadcasted_iota(jnp.int32, (GRID_W, LANES), 1) < HEAD_DIM

    def body(i, carry):
        items, qrows = [], []
        for u in range(rows_per_step):
            r = i * rows_per_step + u
            rs = jnp.clip(r - kr // 2, 0, rows - kr)
            qrows.append(pl.ds(pl.multiple_of(r * GRID_W, GRID_W), GRID_W))
            wrows = pl.ds(pl.multiple_of(rs * GRID_W, GRID_W), kr * GRID_W)
            items.append((_stack_heads(q_ref[qrows[-1], :], low), k_ref[wrows, :], v_ref[wrows, :],
                          bias_ref[0, r - rs]))
        for rows_u, (o, _, _) in zip(qrows, _stacked_attention(items)):
            o_ref[rows_u, :] = _unstack_heads(o, low).astype(o_ref.dtype)
        return carry

    lax.fori_loop(0, rows // rows_per_step, body, 0)


def _neighbourhood_attention(qkv, bias, batch, seq, rows_per_step=8):
    n = qkv.shape[0]
    rows = seq // GRID_W
    pairs = NA_WIDTH // LANES
    return pl.pallas_call(
        functools.partial(_na_kernel, rows=rows, rows_per_step=rows_per_step),
        grid=(pairs, batch),
        in_specs=[
            pl.BlockSpec((seq, LANES), lambda p, b: (b, p)),
            pl.BlockSpec((seq, LANES), lambda p, b: (b, pairs + p)),
            pl.BlockSpec((seq, LANES), lambda p, b: (b, 2 * pairs + p)),
            pl.BlockSpec((1, NA_WIN_ROWS, HEADS_PER_LANE_TILE * GRID_W, NA_WIN_ROWS * GRID_W),
                         lambda p, b: (p, 0, 0, 0)),
        ],
        out_specs=pl.BlockSpec((seq, LANES), lambda p, b: (b, p)),
        out_shape=jax.ShapeDtypeStruct((n, NA_WIDTH), BF16),
        compiler_params=_params(("arbitrary", "arbitrary")),
        name="na_attn",
    )(qkv, qkv, qkv, bias)


def _alibi_slopes():
    n = N_DIL_GROUPS * DIL_HEADS_PER_GROUP
    s = np.exp2(-ALIBI_MAX * np.arange(1, n + 1, dtype=np.float64) / n).astype(np.float32)
    return s.reshape(N_DIL_GROUPS, DIL_HEADS_PER_GROUP)


def _dil_bias_table(group):
    blk = DIL_BLOCK
    dilation = DIL_GROUPS[group][1]
    slopes = _alibi_slopes()[group]
    qi = np.arange(blk)[:, None]
    kj = np.arange(3 * blk)[None, :]
    tables = []
    for shift in range(3):
        arel = np.abs(kj - qi - shift * blk)
        dist = (dilation * arel).astype(np.float32)
        per_head = [np.where(arel <= blk, -slopes[h] * dist, np.float32(NEG_INF)) for h in range(DIL_HEADS_PER_GROUP)]
        tables.append(np.stack(per_head))
    t = np.stack(tables, axis=1).astype(np.float32)
    pairs = DIL_HEADS_PER_GROUP // HEADS_PER_LANE_TILE
    t = t.reshape(pairs, HEADS_PER_LANE_TILE, 3, blk, 3 * blk).transpose(0, 2, 1, 3, 4)
    return t.reshape(pairs, 3, HEADS_PER_LANE_TILE * blk, 3 * blk)


def _dil_kernel(q_ref, k_ref, v_ref, bias_ref, o_ref, lse_ref, *, nb, blocks_per_step):
    blk = DIL_BLOCK
    win = 3 * blk
    low = lax.broadcasted_iota(jnp.int32, (blk, LANES), 1) < HEAD_DIM
    steps = q_ref.shape[0] * nb

    def body(i, carry):
        items, dst = [], []
        for u in range(blocks_per_step):
            t = i * blocks_per_step + u
            sq = t // nb
            n = t % nb
            wb = jnp.clip(n - 1, 0, nb - 3)
            qrows = pl.ds(pl.multiple_of(n * blk, blk), blk)
            wrows = pl.ds(pl.multiple_of(wb * blk, blk), win)
            dst.append((sq, qrows))
            items.append((_stack_heads(q_ref[sq, qrows, :], low), k_ref[sq, wrows, :], v_ref[sq, wrows, :],
                          bias_ref[0, n - wb]))
        for (sq, qrows), (o, m, z) in zip(dst, _stacked_attention(items)):
            o_ref[sq, qrows, :] = _unstack_heads(o, low)
            lse_ref[sq, qrows, :] = _unstack_heads(jnp.broadcast_to(m + jnp.log(z), o.shape), low)
        return carry

    lax.fori_loop(0, steps // blocks_per_step, body, 0)


def _dilated_attention(qkv3, col_offsets, group, tokens_per_step=4096, blocks_per_step=8):
    nseq, length, _ = qkv3.shape
    nb = length // DIL_BLOCK
    sb = max(1, tokens_per_step // length)
    pairs = DIL_OUT_WIDTH // LANES
    qo, ko, vo = col_offsets
    bias = jnp.asarray(_dil_bias_table(group))
    out = jax.ShapeDtypeStruct((nseq, length, DIL_OUT_WIDTH), F32)
    return pl.pallas_call(
        functools.partial(_dil_kernel, nb=nb, blocks_per_step=blocks_per_step),
        grid=(pairs, nseq // sb),
        in_specs=[
            pl.BlockSpec((sb, length, LANES), lambda p, b: (b, 0, qo + p)),
            pl.BlockSpec((sb, length, LANES), lambda p, b: (b, 0, ko + p)),
            pl.BlockSpec((sb, length, LANES), lambda p, b: (b, 0, vo + p)),
            pl.BlockSpec((1,) + bias.shape[1:], lambda p, b: (p, 0, 0, 0)),
        ],
        out_specs=[pl.BlockSpec((sb, length, LANES), lambda p, b: (b, 0, p))] * 2,
        out_shape=[out, out],
        compiler_params=_params(("arbitrary", "arbitrary")),
        name=f"dil_attn_g{group}",
    )(qkv3, qkv3, qkv3, bias)


def _to_residue(a, batch, dilation):
    n, c = a.shape
    length = n // batch // dilation
    return a.reshape(batch, length, dilation, c).transpose(0, 2, 1, 3).reshape(batch * dilation, length, c)


def _from_residue(a, batch, dilation):
    nseq, length, c = a.shape
    return a.reshape(batch, dilation, length, c).transpose(0, 2, 1, 3).reshape(batch * dilation * length, c)


def _outproj_kernel(x_ref, oa_ref, o0_ref, o1_ref, o2_ref, l0_ref, l1_ref, l2_ref, mod_ref, g_ref,
                    wg_ref, bg_ref, wpa_ref, wpb_ref, wo_ref, out_ref):
    d = x_ref.shape[1]
    x = x_ref[...]
    h = _modulated_norm(x, g_ref[...], mod_ref[0, SC1:SC1 + 1, :], mod_ref[0, SH1:SH1 + 1, :]).astype(BF16)

    lses = [l0_ref[...], l1_ref[...], l2_ref[...]]
    outs = [o0_ref[...], o1_ref[...], o2_ref[...]]
    top = jnp.maximum(jnp.maximum(lses[0], lses[1]), lses[2])
    es = [jnp.exp(l - top) for l in lses]
    den = es[0] + es[1] + es[2]
    ob = (es[0] * outs[0] + es[1] * outs[1] + es[2] * outs[2]) / den

    ya = jnp.dot(oa_ref[...], wpa_ref[...], preferred_element_type=F32)
    yb = jnp.dot(ob.astype(BF16), wpb_ref[...], preferred_element_type=F32)
    ga = jax.nn.sigmoid(jnp.dot(h, wg_ref[:, :d], preferred_element_type=F32) + bg_ref[:, :d])
    mix = ga * ya
    gb = jax.nn.sigmoid(jnp.dot(h, wg_ref[:, d:], preferred_element_type=F32) + bg_ref[:, d:])
    mix = mix + gb * yb
    y = jnp.dot(mix.astype(BF16), wo_ref[...], preferred_element_type=F32)
    out_ref[...] = x + mod_ref[0, GT1:GT1 + 1, :] * y


def _outproj(x2, oa, o_groups, lse_groups, mod, g, w_gate, b_gate, w_pa, w_pb, w_o, seq, tm=512):
    n, d = x2.shape
    blocks_per_batch = seq // tm
    row = lambda c: pl.BlockSpec((tm, c), lambda i: (i, 0))
    full = lambda a: pl.BlockSpec(a.shape, lambda i: (0,) * a.ndim)
    return pl.pallas_call(
        _outproj_kernel,
        grid=(n // tm,),
        in_specs=[row(d), row(NA_WIDTH)] + [row(DIL_OUT_WIDTH)] * 6 + [
            pl.BlockSpec((1, 6, d), lambda i: (i // blocks_per_batch, 0, 0)),
            full(g), full(w_gate), full(b_gate), full(w_pa), full(w_pb), full(w_o),
        ],
        out_specs=row(d),
        out_shape=jax.ShapeDtypeStruct((n, d), F32),
        compiler_params=_params(("arbitrary",)),
        name="out_proj",
    )(x2, oa, *o_groups, *lse_groups, mod, g, w_gate, b_gate, w_pa, w_pb, w_o)


def _first_index_of_max(cur, idx, size):
    m = jnp.max(cur, axis=0, keepdims=True)
    first = jnp.min(jnp.where(cur == m, idx, size), axis=0, keepdims=True)
    return m, first


def _route_transposed(logits_t, e_bias):
    tokens = logits_t.shape[1]
    per_group = N_EXPERTS // N_EXPERT_GROUPS
    scores = jax.nn.sigmoid(logits_t)
    biased = scores + e_bias
    midx = lax.broadcasted_iota(jnp.int32, (per_group, tokens), 0)
    grp_scores = []
    for g in range(N_EXPERT_GROUPS):
        vals = biased[g * per_group:(g + 1) * per_group, :]
        m1, first = _first_index_of_max(vals, midx, per_group)
        m2 = jnp.max(jnp.where(midx == first, -jnp.inf, vals), axis=0, keepdims=True)
        grp_scores.append(m1 + m2)
    cur = jnp.concatenate(grp_scores, axis=0)
    gidx = lax.broadcasted_iota(jnp.int32, (N_EXPERT_GROUPS, tokens), 0)
    grp_sel = jnp.zeros((N_EXPERT_GROUPS, tokens), jnp.bool_)
    for _ in range(TOP_GROUPS):
        _, first = _first_index_of_max(cur, gidx, N_EXPERT_GROUPS)
        pick = gidx == first
        grp_sel = jnp.logical_or(grp_sel, pick)
        cur = jnp.where(pick, -jnp.inf, cur)
    rows = []
    for g in range(N_EXPERT_GROUPS):
        vals = biased[g * per_group:(g + 1) * per_group, :]
        rows.append(jnp.where(grp_sel[g:g + 1, :], vals, NEG_INF))
    cur = jnp.concatenate(rows, axis=0)
    eidx = lax.broadcasted_iota(jnp.int32, (N_EXPERTS, tokens), 0)
    firsts, picks, weights = [], [], []
    for _ in range(TOP_K):
        _, first = _first_index_of_max(cur, eidx, N_EXPERTS)
        pick = eidx == first
        firsts.append(first)
        picks.append(pick)
        weights.append(jnp.sum(jnp.where(pick, scores, 0.0), axis=0, keepdims=True))
        cur = jnp.where(pick, -jnp.inf, cur)
    total = functools.reduce(lambda a, b: a + b, weights)
    return firsts, picks, [w / total * ROUTED_SCALE for w in weights]


def _pack_halves(a):
    half = a.shape[1] // 2
    bits = lax.bitcast_convert_type(a.astype(BF16).astype(F32), jnp.int32)
    return lax.shift_right_logical(bits[:, :half], 16) | bits[:, half:]


def _unpack_halves(w):
    low = lax.bitcast_convert_type(lax.shift_left(w, 16), F32)
    high = lax.bitcast_convert_type(w & jnp.int32(-65536), F32)
    return low, high


def _router_kernel(x_ref, mod_ref, g_ref, wr_ref, eb_ref, tri_ref, hp_ref, eidx_ref, rank_ref, w_ref, cnt_ref):
    tm = x_ref.shape[0]

    @pl.when(pl.program_id(0) == 0)
    def _():
        cnt_ref[...] = jnp.zeros_like(cnt_ref)

    h = _modulated_norm(x_ref[...], g_ref[...], mod_ref[0, SC2:SC2 + 1, :], mod_ref[0, SH2:SH2 + 1, :])
    hp_ref[...] = _pack_halves(h)
    logits_t = lax.dot_general(wr_ref[...], h, (((1,), (1,)), ((), ())), preferred_element_type=F32,
                               precision=lax.Precision.HIGHEST)
    firsts, picks, weights = _route_transposed(logits_t, eb_ref[...])
    sel = functools.reduce(jnp.logical_or, picks)
    sel_f = jnp.where(sel, 1.0, 0.0)
    incl = jnp.dot(sel_f.astype(BF16), tri_ref[...], preferred_element_type=F32)
    before = cnt_ref[:, 0:1] + incl - sel_f
    eidx_ref[...] = jnp.concatenate(firsts, axis=0)
    rank_ref[...] = jnp.concatenate(
        [jnp.sum(jnp.where(p, before, 0.0), axis=0, keepdims=True) for p in picks], axis=0).astype(jnp.int32)
    pad = jnp.concatenate(weights + [jnp.zeros((LANES - TOP_K, tm), F32)], axis=0)
    w_ref[...] = pad.T
    cnt_ref[...] = cnt_ref[...] + incl[:, tm - 1:tm]


def _router(x2, mod, g, w_router_t, e_bias, seq, tm=512):
    n, d = x2.shape
    blocks_per_batch = seq // tm
    tri = jnp.asarray(np.triu(np.ones((tm, tm), np.float32)), BF16)
    tok = lambda r: pl.BlockSpec((r, tm), lambda i: (0, i))
    const = lambda a: pl.BlockSpec(a.shape, lambda i: (0,) * a.ndim)
    return pl.pallas_call(
        _router_kernel,
        grid=(n // tm,),
        in_specs=[
            pl.BlockSpec((tm, d), lambda i: (i, 0)),
            pl.BlockSpec((1, 6, d), lambda i: (i // blocks_per_batch, 0, 0)),
            const(g), const(w_router_t), const(e_bias), const(tri),
        ],
        out_specs=[
            pl.BlockSpec((tm, d // 2), lambda i: (i, 0)),
            tok(TOP_K), tok(TOP_K),
            pl.BlockSpec((tm, LANES), lambda i: (i, 0)),
            pl.BlockSpec((N_EXPERTS, LANES), lambda i: (0, 0)),
        ],
        out_shape=[
            jax.ShapeDtypeStruct((n, d // 2), jnp.int32),
            jax.ShapeDtypeStruct((TOP_K, n), jnp.int32),
            jax.ShapeDtypeStruct((TOP_K, n), jnp.int32),
            jax.ShapeDtypeStruct((n, LANES), F32),
            jax.ShapeDtypeStruct((N_EXPERTS, LANES), F32),
        ],
        compiler_params=_params(("arbitrary",)),
        name="moe_router",
    )(x2, mod, g, w_router_t, e_bias, tri)


def _sc_worker_id():
    return lax.axis_index("subcore") * SC_CORES + lax.axis_index("core")


def _sc_scatter_rows(src, idx3, n_out):
    n, w = src.shape
    per_worker = n // SC_CHUNK // SC_WORKERS
    mesh = plsc.VectorSubcoreMesh(core_axis_name="core", subcore_axis_name="subcore")

    @functools.partial(
        pl.kernel, mesh=mesh, out_type=jax.ShapeDtypeStruct((n_out, w), src.dtype), name="moe_dispatch",
        scratch_types=[pltpu.VMEM((TOP_K, SC_CHUNK), jnp.int32), pltpu.VMEM((SC_CHUNK, w), src.dtype),
                       pltpu.SemaphoreType.DMA])
    def scatter(src_hbm, idx_hbm, out_hbm, idx_v, rows_v, sem):
        first = _sc_worker_id() * per_worker

        @pl.loop(0, per_worker)
        def _(i):
            chunk = first + i
            pltpu.sync_copy(idx_hbm.at[chunk], idx_v)
            pltpu.sync_copy(src_hbm.at[pl.ds(chunk * SC_CHUNK, SC_CHUNK)], rows_v)
            copies = [pltpu.make_async_copy(rows_v, out_hbm.at[idx_v.at[k]], sem) for k in range(TOP_K)]
            for cp in copies:
                cp.start()
            for cp in copies:
                cp.wait()

    return scatter(src, idx3)


def _sc_gather_rows(src, idx3):
    _, w = src.shape
    chunks = idx3.shape[0]
    per_worker = chunks // SC_WORKERS
    mesh = plsc.VectorSubcoreMesh(core_axis_name="core", subcore_axis_name="subcore")

    @functools.partial(
        pl.kernel, mesh=mesh, out_type=jax.ShapeDtypeStruct((TOP_K, chunks * SC_CHUNK, w), src.dtype),
        name="moe_collect",
        scratch_types=[pltpu.VMEM((TOP_K, SC_CHUNK), jnp.int32), pltpu.VMEM((SC_CHUNK, w), src.dtype),
                       pltpu.SemaphoreType.DMA])
    def gather(src_hbm, idx_hbm, out_hbm, idx_v, rows_v, sem):
        first = _sc_worker_id() * per_worker

        @pl.loop(0, per_worker)
        def _(i):
            chunk = first + i
            pltpu.sync_copy(idx_hbm.at[chunk], idx_v)
            for k in range(TOP_K):
                pltpu.async_copy(src_hbm.at[idx_v.at[k]], rows_v, sem).wait()
                pltpu.sync_copy(rows_v, out_hbm.at[k, pl.ds(chunk * SC_CHUNK, SC_CHUNK)])

    return gather(src, idx3)


def _swiglu(x, w_gate_up, w_down):
    gu = jnp.dot(x, w_gate_up, preferred_element_type=F32)
    gate = gu[:, :EXPERT_DIM]
    act = (gate * jax.nn.sigmoid(gate)) * gu[:, EXPERT_DIM:]
    return jnp.dot(act.astype(BF16), w_down, preferred_element_type=F32)


def _expert_ffn_kernel(be_ref, nv_ref, xs_ref, wgu_ref, wd_ref, ys_ref):
    nvalid = nv_ref[pl.program_id(0)]

    @pl.when(nvalid > 0)
    def _():
        packed = xs_ref[...]
        row = lax.broadcasted_iota(jnp.int32, packed.shape, 0)
        low, high = _unpack_halves(jnp.where(row < nvalid, packed, 0))
        x = jnp.concatenate([low, high], axis=1).astype(BF16)
        ys_ref[...] = _pack_halves(_swiglu(x, wgu_ref[0], wd_ref[0]))

    @pl.when(nvalid == 0)
    def _():
        ys_ref[...] = jnp.zeros_like(ys_ref)


def _expert_ffn(xs, block_expert, block_valid, w_gate_up, w_down, tb):
    p, half = xs.shape
    d = 2 * half
    grid_spec = pltpu.PrefetchScalarGridSpec(
        num_scalar_prefetch=2,
        grid=(p // tb,),
        in_specs=[
            pl.BlockSpec((tb, half), lambda i, be, nv: (i, 0)),
            pl.BlockSpec((1, d, 2 * EXPERT_DIM), lambda i, be, nv: (be[i], 0, 0)),
            pl.BlockSpec((1, EXPERT_DIM, d), lambda i, be, nv: (be[i], 0, 0)),
        ],
        out_specs=pl.BlockSpec((tb, half), lambda i, be, nv: (i, 0)),
    )
    return pl.pallas_call(
        _expert_ffn_kernel,
        grid_spec=grid_spec,
        out_shape=jax.ShapeDtypeStruct((p, half), jnp.int32),
        compiler_params=_params(("arbitrary",)),
        name="moe_expert_ffn",
    )(block_expert, block_valid, xs, w_gate_up, w_down)


def _combine_kernel(x_ref, hp_ref, yg_ref, w_ref, mod_ref, wgu_ref, wd_ref, gf_ref, o_ref, *, final_norm):
    low, high = _unpack_halves(hp_ref[...])
    h = jnp.concatenate([low, high], axis=1).astype(BF16)
    shared = _swiglu(h, wgu_ref[...], wd_ref[...])
    half = hp_ref.shape[1]
    acc_low, acc_high = shared[:, :half], shared[:, half:]
    w = w_ref[...]
    for k in range(TOP_K):
        low, high = _unpack_halves(yg_ref[k])
        wk = w[:, k:k + 1]
        acc_low = acc_low + wk * low
        acc_high = acc_high + wk * high
    y = jnp.concatenate([acc_low, acc_high], axis=1)
    out = x_ref[...] + mod_ref[0, GT2:GT2 + 1, :] * y
    if final_norm:
        out = out * lax.rsqrt(jnp.mean(out * out, axis=-1, keepdims=True) + EPS) * gf_ref[...]
    o_ref[...] = out


def _combine(x2, hp, yg, w, mod, ws_gate_up, ws_down, g_final, seq, final_norm, tm=512):
    n, d = x2.shape
    blocks_per_batch = seq // tm
    const = lambda a: pl.BlockSpec(a.shape, lambda i: (0,) * a.ndim)
    return pl.pallas_call(
        functools.partial(_combine_kernel, final_norm=final_norm),
        grid=(n // tm,),
        in_specs=[
            pl.BlockSpec((tm, d), lambda i: (i, 0)),
            pl.BlockSpec((tm, d // 2), lambda i: (i, 0)),
            pl.BlockSpec((TOP_K, tm, d // 2), lambda i: (0, i, 0)),
            pl.BlockSpec((tm, LANES), lambda i: (i, 0)),
            pl.BlockSpec((1, 6, d), lambda i: (i // blocks_per_batch, 0, 0)),
            const(ws_gate_up), const(ws_down), const(g_final),
        ],
        out_specs=pl.BlockSpec((tm, d), lambda i: (i, 0)),
        out_shape=jax.ShapeDtypeStruct((n, d), F32),
        compiler_params=_params(("arbitrary",)),
        name="moe_combine",
    )(x2, hp, yg, w, mod, ws_gate_up, ws_down, g_final)


def _token_mixer(x2, mod, g_mix, w_in, b_gate, rpb, w_pa, w_pb, w_o, batch, seq):
    d = x2.shape[1]
    qkv = _inproj(x2, mod, g_mix.reshape(1, d), w_in[:, :QKV_COLS].astype(BF16), seq)
    o_a = _neighbourhood_attention(qkv, _na_bias_table(rpb), batch, seq)

    dil0 = 3 * NA_WIDTH
    o_groups, lse_groups = [], []
    for grp, (_, dilation) in enumerate(DIL_GROUPS):
        cols = [dil0 + part * DIL_WIDTH + grp * DIL_OUT_WIDTH for part in range(3)]
        if dilation == 1:
            seqs = qkv.reshape(batch, seq, QKV_COLS)
            offsets = tuple(c // LANES for c in cols)
        else:
            parts = jnp.concatenate([qkv[:, c:c + DIL_OUT_WIDTH] for c in cols], axis=1)
            seqs = _to_residue(parts, batch, dilation)
            offsets = tuple(part * DIL_OUT_WIDTH // LANES for part in range(3))
        o_g, lse_g = _dilated_attention(seqs, offsets, grp)
        o_groups.append(_from_residue(o_g, batch, dilation))
        lse_groups.append(_from_residue(lse_g, batch, dilation))

    return _outproj(x2, o_a, o_groups, lse_groups, mod, g_mix.reshape(1, d),
                    w_in[:, QKV_COLS:].astype(BF16), b_gate.reshape(1, -1),
                    w_pa.astype(BF16), w_pb.astype(BF16), w_o.astype(BF16), seq)


def _dispatch_plan(eidx_t, rank_t, counts, tb):
    n = eidx_t.shape[1]
    n_blocks = -(-(n * TOP_K + N_EXPERTS * (tb - 1)) // tb)
    padded = (counts + tb - 1) // tb * tb
    seg_end = jnp.cumsum(padded)
    seg_start = seg_end - padded
    dest_t = seg_start[eidx_t] + rank_t
    idx3 = dest_t.reshape(TOP_K, n // SC_CHUNK, SC_CHUNK).transpose(1, 0, 2)
    block_start = jnp.arange(n_blocks, dtype=jnp.int32) * tb
    block_expert = jnp.minimum(jnp.searchsorted(seg_end, block_start, side='right'), N_EXPERTS - 1)
    block_expert = block_expert.astype(jnp.int32)
    block_valid = jnp.clip(counts[block_expert] - (block_start - seg_start[block_expert]), 0, tb)
    return idx3, block_expert, block_valid.astype(jnp.int32), n_blocks


def _moe_layer(x2, mod, g_ffn, w_router, e_bias, we_gate, we_up, we_down, ws_gate, ws_up, ws_down, g_final, seq,
               final_norm, tb=EXPERT_ROW_BLOCK):
    d = x2.shape[1]
    hp, eidx_t, rank_t, w, cnt = _router(x2, mod, g_ffn.reshape(1, d), w_router.T, e_bias.reshape(-1, 1), seq)
    counts = cnt[:, 0].astype(jnp.int32)
    idx3, block_expert, block_valid, n_blocks = _dispatch_plan(eidx_t, rank_t, counts, tb)
    xs = _sc_scatter_rows(hp, idx3, n_blocks * tb)
    ys = _expert_ffn(xs, block_expert, block_valid, jnp.concatenate([we_gate, we_up], axis=-1).astype(BF16),
                     we_down.astype(BF16), tb)
    yg = _sc_gather_rows(ys, idx3)
    return _combine(x2, hp, yg, w, mod, jnp.concatenate([ws_gate, ws_up], axis=-1).astype(BF16),
                    ws_down.astype(BF16), g_final.reshape(1, d), seq, final_norm)


def kernel(x, c, w_ada, b_ada, g_mix, w_in, b_gate, rpb, w_pa, w_pb, w_o, g_ffn, w_router, e_bias,
           we_gate, we_up, we_down, ws_gate, ws_up, ws_down, g_final):
    batch, seq, d = x.shape
    depth = w_ada.shape[0]
    mods = _ada(c, w_ada, b_ada).reshape(depth, batch, 6, d)
    x2 = x.reshape(batch * seq, d)
    for l in range(depth):
        x2 = _token_mixer(x2, mods[l], g_mix[l], w_in[l], b_gate[l], rpb[l], w_pa[l], w_pb[l], w_o[l],
                          batch, seq)
        x2 = _moe_layer(x2, mods[l], g_ffn[l], w_router[l], e_bias[l], we_gate[l], we_up[l], we_down[l],
                        ws_gate[l], ws_up[l], ws_down[l], g_final, seq, final_norm=(l == depth - 1))
    return x2.reshape(batch, seq, d)
```

```python
import functools

import numpy as np
import jax
import jax.numpy as jnp
from jax import lax
from jax.experimental import pallas as pl
from jax.experimental.pallas import tpu as pltpu
from jax.experimental.pallas import tpu_sc as plsc

HEAD_DIM = 64
GRID_W = 64
NA_HEADS = 8
NA_WIN_ROWS = 8
NA_WIN_COLS = 16
DIL_GROUPS = ((128, 1), (512, 4), (2048, 16))
DIL_HEADS_PER_GROUP = 4
N_DIL_GROUPS = len(DIL_GROUPS)
NA_WIDTH = NA_HEADS * HEAD_DIM
DIL_WIDTH = N_DIL_GROUPS * DIL_HEADS_PER_GROUP * HEAD_DIM
DIL_OUT_WIDTH = DIL_HEADS_PER_GROUP * HEAD_DIM
QKV_COLS = 3 * (NA_WIDTH + DIL_WIDTH)
N_EXPERTS = 64
TOP_K = 8
N_EXPERT_GROUPS = 8
TOP_GROUPS = 4
EXPERT_DIM = 256
ROUTED_SCALE = 2.5
ALIBI_MAX = 8.0
EPS = 1e-6
NEG_INF = -1e30

LANES = 128
HEADS_PER_LANE_TILE = LANES // HEAD_DIM
DIL_BLOCK = 64
VMEM_LIMIT_BYTES = 56 * 1024 * 1024

SC_CORES = 2
SC_SUBCORES = 16
SC_WORKERS = SC_CORES * SC_SUBCORES
SC_CHUNK = 64
EXPERT_ROW_BLOCK = 512

F32 = jnp.float32
BF16 = jnp.bfloat16

SH1, SC1, GT1, SH2, SC2, GT2 = range(6)


def _params(sem):
    return pltpu.CompilerParams(dimension_semantics=sem, vmem_limit_bytes=VMEM_LIMIT_BYTES)


def _modulated_norm(x, g, scale, shift):
    r = lax.rsqrt(jnp.mean(x * x, axis=-1, keepdims=True) + EPS)
    return (x * r * g) * (1.0 + scale) + shift


def _ada_kernel(c_ref, w_ref, b_ref, o_ref):
    c = c_ref[...]
    act = c * jax.nn.sigmoid(c)
    o_ref[0] = jnp.dot(act, w_ref[0], preferred_element_type=F32,
                       precision=lax.Precision.HIGHEST) + b_ref[0]


def _ada(c, w_ada, b_ada):
    depth, d, six_d = w_ada.shape
    b = c.shape[0]
    tn = d
    return pl.pallas_call(
        _ada_kernel,
        grid=(depth, six_d // tn),
        in_specs=[
            pl.BlockSpec((b, d), lambda l, j: (0, 0)),
            pl.BlockSpec((1, d, tn), lambda l, j: (l, 0, j)),
            pl.BlockSpec((1, 1, tn), lambda l, j: (l, 0, j)),
        ],
        out_specs=pl.BlockSpec((1, b, tn), lambda l, j: (l, 0, j)),
        out_shape=jax.ShapeDtypeStruct((depth, b, six_d), F32),
        compiler_params=_params(("arbitrary", "arbitrary")),
        name="ada_mod",
    )(c, w_ada, b_ada.reshape(depth, 1, six_d))


def _inproj_kernel(x_ref, mod_ref, g_ref, w_ref, o_ref, h_ref):
    @pl.when(pl.program_id(1) == 0)
    def _():
        h = _modulated_norm(x_ref[...], g_ref[...], mod_ref[0, SC1:SC1 + 1, :], mod_ref[0, SH1:SH1 + 1, :])
        h_ref[...] = h.astype(BF16)

    o_ref[...] = jnp.dot(h_ref[...], w_ref[...], preferred_element_type=F32).astype(BF16)


def _inproj(x2, mod, g, w_qkv, seq, tm=1024, tn=768):
    n, d = x2.shape
    cols = w_qkv.shape[1]
    blocks_per_batch = seq // tm
    return pl.pallas_call(
        _inproj_kernel,
        grid=(n // tm, cols // tn),
        in_specs=[
            pl.BlockSpec((tm, d), lambda i, j: (i, 0)),
            pl.BlockSpec((1, 6, d), lambda i, j: (i // blocks_per_batch, 0, 0)),
            pl.BlockSpec((1, d), lambda i, j: (0, 0)),
            pl.BlockSpec((d, tn), lambda i, j: (0, j)),
        ],
        out_specs=pl.BlockSpec((tm, tn), lambda i, j: (i, j)),
        out_shape=jax.ShapeDtypeStruct((n, cols), BF16),
        scratch_shapes=[pltpu.VMEM((tm, d), BF16)],
        compiler_params=_params(("arbitrary", "arbitrary")),
        name="in_proj",
    )(x2, mod, g, w_qkv)


def _na_bias_table(rpb):
    heads = rpb.shape[0]
    cols = np.arange(GRID_W)
    col_start = np.clip(cols - NA_WIN_COLS // 2, 0, GRID_W - NA_WIN_COLS)
    col_mask = (cols[None, :] >= col_start[:, None]) & (cols[None, :] < col_start[:, None] + NA_WIN_COLS)
    dc = np.clip(cols[None, :] - cols[:, None], -(NA_WIN_COLS - 1), NA_WIN_COLS - 1) + NA_WIN_COLS - 1
    rpb_cols = rpb[:, :, dc].astype(F32)
    dr = np.arange(NA_WIN_ROWS)[None, :] - np.arange(NA_WIN_ROWS)[:, None] + NA_WIN_ROWS - 1
    t = rpb_cols[:, dr]
    t = t.transpose(0, 1, 3, 2, 4)
    t = jnp.where(col_mask[:, None, :], t, NEG_INF)
    t = t.reshape(heads // HEADS_PER_LANE_TILE, HEADS_PER_LANE_TILE, NA_WIN_ROWS, GRID_W, NA_WIN_ROWS * GRID_W)
    return t.transpose(0, 2, 1, 3, 4).reshape(heads // HEADS_PER_LANE_TILE, NA_WIN_ROWS,
                                              HEADS_PER_LANE_TILE * GRID_W, NA_WIN_ROWS * GRID_W)


def _stack_heads(q, low):
    scaled = q * (HEAD_DIM ** -0.5)
    zero = jnp.zeros_like(scaled)
    return jnp.concatenate([jnp.where(low, scaled, zero), jnp.where(low, zero, scaled)], axis=0)


def _stacked_attention(items):
    scores = [lax.dot_general(q2, kw, (((1,), (1,)), ((), ())), preferred_element_type=F32) + bias
              for q2, kw, _, bias in items]
    probs = []
    for s in scores:
        m = jnp.max(s, axis=-1, keepdims=True)
        p = jnp.exp(s - m)
        probs.append((p.astype(BF16), m, jnp.sum(p, axis=-1, keepdims=True)))
    return [(jnp.dot(p, vw, preferred_element_type=F32) / z, m, z)
            for (p, m, z), (_, _, vw, _) in zip(probs, items)]


def _unstack_heads(a, low):
    half = a.shape[0] // HEADS_PER_LANE_TILE
    return jnp.where(low, a[:half], a[half:])


def _na_kernel(q_ref, k_ref, v_ref, bias_ref, o_ref, *, rows, rows_per_step):
    kr = NA_WIN_ROWS
    low = lax.broadcasted_iota(jnp.int32, (GRID_W, LANES), 1) < HEAD_DIM

    def body(i, carry):
        items, qrows = [], []
        for u in range(rows_per_step):
            r = i * rows_per_step + u
            rs = jnp.clip(r - kr // 2, 0, rows - kr)
            qrows.append(pl.ds(pl.multiple_of(r * GRID_W, GRID_W), GRID_W))
            wrows = pl.ds(pl.multiple_of(rs * GRID_W, GRID_W), kr * GRID_W)
            items.append((_stack_heads(q_ref[qrows[-1], :], low), k_ref[wrows, :], v_ref[wrows, :],
                          bias_ref[0, r - rs]))
        for rows_u, (o, _, _) in zip(qrows, _stacked_attention(items)):
            o_ref[rows_u, :] = _unstack_heads(o, low).astype(o_ref.dtype)
        return carry

    lax.fori_loop(0, rows // rows_per_step, body, 0)


def _neighbourhood_attention(qkv, bias, batch, seq, rows_per_step=8):
    n = qkv.shape[0]
    rows = seq // GRID_W
    pairs = NA_WIDTH // LANES
    return pl.pallas_call(
        functools.partial(_na_kernel, rows=rows, rows_per_step=rows_per_step),
        grid=(pairs, batch),
        in_specs=[
            pl.BlockSpec((seq, LANES), lambda p, b: (b, p)),
            pl.BlockSpec((seq, LANES), lambda p, b: (b, pairs + p)),
            pl.BlockSpec((seq, LANES), lambda p, b: (b, 2 * pairs + p)),
            pl.BlockSpec((1, NA_WIN_ROWS, HEADS_PER_LANE_TILE * GRID_W, NA_WIN_ROWS * GRID_W),
                         lambda p, b: (p, 0, 0, 0)),
        ],
        out_specs=pl.BlockSpec((seq, LANES), lambda p, b: (b, p)),
        out_shape=jax.ShapeDtypeStruct((n, NA_WIDTH), BF16),
        compiler_params=_params(("arbitrary", "arbitrary")),
        name="na_attn",
    )(qkv, qkv, qkv, bias)


def _alibi_slopes():
    n = N_DIL_GROUPS * DIL_HEADS_PER_GROUP
    s = np.exp2(-ALIBI_MAX * np.arange(1, n + 1, dtype=np.float64) / n).astype(np.float32)
    return s.reshape(N_DIL_GROUPS, DIL_HEADS_PER_GROUP)


def _dil_bias_table(group):
    blk = DIL_BLOCK
    dilation = DIL_GROUPS[group][1]
    slopes = _alibi_slopes()[group]
    qi = np.arange(blk)[:, None]
    kj = np.arange(3 * blk)[None, :]
    tables = []
    for shift in range(3):
        arel = np.abs(kj - qi - shift * blk)
        dist = (dilation * arel).astype(np.float32)
        per_head = [np.where(arel <= blk, -slopes[h] * dist, np.float32(NEG_INF)) for h in range(DIL_HEADS_PER_GROUP)]
        tables.append(np.stack(per_head))
    t = np.stack(tables, axis=1).astype(np.float32)
    pairs = DIL_HEADS_PER_GROUP // HEADS_PER_LANE_TILE
    t = t.reshape(pairs, HEADS_PER_LANE_TILE, 3, blk, 3 * blk).transpose(0, 2, 1, 3, 4)
    return t.reshape(pairs, 3, HEADS_PER_LANE_TILE * blk, 3 * blk)


def _dil_kernel(q_ref, k_ref, v_ref, bias_ref, o_ref, lse_ref, *, nb, blocks_per_step):
    blk = DIL_BLOCK
    win = 3 * blk
    low = lax.broadcasted_iota(jnp.int32, (blk, LANES), 1) < HEAD_DIM
    steps = q_ref.shape[0] * nb

    def body(i, carry):
        items, dst = [], []
        for u in range(blocks_per_step):
            t = i * blocks_per_step + u
            sq = t // nb
            n = t % nb
            wb = jnp.clip(n - 1, 0, nb - 3)
            qrows = pl.ds(pl.multiple_of(n * blk, blk), blk)
            wrows = pl.ds(pl.multiple_of(wb * blk, blk), win)
            dst.append((sq, qrows))
            items.append((_stack_heads(q_ref[sq, qrows, :], low), k_ref[sq, wrows, :], v_ref[sq, wrows, :],
                          bias_ref[0, n - wb]))
        for (sq, qrows), (o, m, z) in zip(dst, _stacked_attention(items)):
            o_ref[sq, qrows, :] = _unstack_heads(o, low)
            lse_ref[sq, qrows, :] = _unstack_heads(jnp.broadcast_to(m + jnp.log(z), o.shape), low)
        return carry

    lax.fori_loop(0, steps // blocks_per_step, body, 0)


def _dilated_attention(qkv3, col_offsets, group, tokens_per_step=4096, blocks_per_step=8):
    nseq, length, _ = qkv3.shape
    nb = length // DIL_BLOCK
    sb = max(1, tokens_per_step // length)
    pairs = DIL_OUT_WIDTH // LANES
    qo, ko, vo = col_offsets
    bias = jnp.asarray(_dil_bias_table(group))
    out = jax.ShapeDtypeStruct((nseq, length, DIL_OUT_WIDTH), F32)
    return pl.pallas_call(
        functools.partial(_dil_kernel, nb=nb, blocks_per_step=blocks_per_step),
        grid=(pairs, nseq // sb),
        in_specs=[
            pl.BlockSpec((sb, length, LANES), lambda p, b: (b, 0, qo + p)),
            pl.BlockSpec((sb, length, LANES), lambda p, b: (b, 0, ko + p)),
            pl.BlockSpec((sb, length, LANES), lambda p, b: (b, 0, vo + p)),
            pl.BlockSpec((1,) + bias.shape[1:], lambda p, b: (p, 0, 0, 0)),
        ],
        out_specs=[pl.BlockSpec((sb, length, LANES), lambda p, b: (b, 0, p))] * 2,
        out_shape=[out, out],
        compiler_params=_params(("arbitrary", "arbitrary")),
        name=f"dil_attn_g{group}",
    )(qkv3, qkv3, qkv3, bias)


def _to_residue(a, batch, dilation):
    n, c = a.shape
    length = n // batch // dilation
    return a.reshape(batch, length, dilation, c).transpose(0, 2, 1, 3).reshape(batch * dilation, length, c)


def _from_residue(a, batch, dilation):
    nseq, length, c = a.shape
    return a.reshape(batch, dilation, length, c).transpose(0, 2, 1, 3).reshape(batch * dilation * length, c)


def _outproj_kernel(x_ref, oa_ref, o0_ref, o1_ref, o2_ref, l0_ref, l1_ref, l2_ref, mod_ref, g_ref,
                    wg_ref, bg_ref, wpa_ref, wpb_ref, wo_ref, out_ref):
    d = x_ref.shape[1]
    x = x_ref[...]
    h = _modulated_norm(x, g_ref[...], mod_ref[0, SC1:SC1 + 1, :], mod_ref[0, SH1:SH1 + 1, :]).astype(BF16)

    lses = [l0_ref[...], l1_ref[...], l2_ref[...]]
    outs = [o0_ref[...], o1_ref[...], o2_ref[...]]
    top = jnp.maximum(jnp.maximum(lses[0], lses[1]), lses[2])
    es = [jnp.exp(l - top) for l in lses]
    den = es[0] + es[1] + es[2]
    ob = (es[0] * outs[0] + es[1] * outs[1] + es[2] * outs[2]) / den

    ya = jnp.dot(oa_ref[...], wpa_ref[...], preferred_element_type=F32)
    yb = jnp.dot(ob.astype(BF16), wpb_ref[...], preferred_element_type=F32)
    ga = jax.nn.sigmoid(jnp.dot(h, wg_ref[:, :d], preferred_element_type=F32) + bg_ref[:, :d])
    mix = ga * ya
    gb = jax.nn.sigmoid(jnp.dot(h, wg_ref[:, d:], preferred_element_type=F32) + bg_ref[:, d:])
    mix = mix + gb * yb
    y = jnp.dot(mix.astype(BF16), wo_ref[...], preferred_element_type=F32)
    out_ref[...] = x + mod_ref[0, GT1:GT1 + 1, :] * y


def _outproj(x2, oa, o_groups, lse_groups, mod, g, w_gate, b_gate, w_pa, w_pb, w_o, seq, tm=512):
    n, d = x2.shape
    blocks_per_batch = seq // tm
    row = lambda c: pl.BlockSpec((tm, c), lambda i: (i, 0))
    full = lambda a: pl.BlockSpec(a.shape, lambda i: (0,) * a.ndim)
    return pl.pallas_call(
        _outproj_kernel,
        grid=(n // tm,),
        in_specs=[row(d), row(NA_WIDTH)] + [row(DIL_OUT_WIDTH)] * 6 + [
            pl.BlockSpec((1, 6, d), lambda i: (i // blocks_per_batch, 0, 0)),
            full(g), full(w_gate), full(b_gate), full(w_pa), full(w_pb), full(w_o),
        ],
        out_specs=row(d),
        out_shape=jax.ShapeDtypeStruct((n, d), F32),
        compiler_params=_params(("arbitrary",)),
        name="out_proj",
    )(x2, oa, *o_groups, *lse_groups, mod, g, w_gate, b_gate, w_pa, w_pb, w_o)


def _first_index_of_max(cur, idx, size):
    m = jnp.max(cur, axis=0, keepdims=True)
    first = jnp.min(jnp.where(cur == m, idx, size), axis=0, keepdims=True)
    return m, first


def _route_transposed(logits_t, e_bias):
    tokens = logits_t.shape[1]
    per_group = N_EXPERTS // N_EXPERT_GROUPS
    scores = jax.nn.sigmoid(logits_t)
    biased = scores + e_bias
    midx = lax.broadcasted_iota(jnp.int32, (per_group, tokens), 0)
    grp_scores = []
    for g in range(N_EXPERT_GROUPS):
        vals = biased[g * per_group:(g + 1) * per_group, :]
        m1, first = _first_index_of_max(vals, midx, per_group)
        m2 = jnp.max(jnp.where(midx == first, -jnp.inf, vals), axis=0, keepdims=True)
        grp_scores.append(m1 + m2)
    cur = jnp.concatenate(grp_scores, axis=0)
    gidx = lax.broadcasted_iota(jnp.int32, (N_EXPERT_GROUPS, tokens), 0)
    grp_sel = jnp.zeros((N_EXPERT_GROUPS, tokens), jnp.bool_)
    for _ in range(TOP_GROUPS):
        _, first = _first_index_of_max(cur, gidx, N_EXPERT_GROUPS)
        pick = gidx == first
        grp_sel = jnp.logical_or(grp_sel, pick)
        cur = jnp.where(pick, -jnp.inf, cur)
    rows = []
    for g in range(N_EXPERT_GROUPS):
        vals = biased[g * per_group:(g + 1) * per_group, :]
        rows.append(jnp.where(grp_sel[g:g + 1, :], vals, NEG_INF))
    cur = jnp.concatenate(rows, axis=0)
    eidx = lax.broadcasted_iota(jnp.int32, (N_EXPERTS, tokens), 0)
    firsts, picks, weights = [], [], []
    for _ in range(TOP_K):
        _, first = _first_index_of_max(cur, eidx, N_EXPERTS)
        pick = eidx == first
        firsts.append(first)
        picks.append(pick)
        weights.append(jnp.sum(jnp.where(pick, scores, 0.0), axis=0, keepdims=True))
        cur = jnp.where(pick, -jnp.inf, cur)
    total = functools.reduce(lambda a, b: a + b, weights)
    return firsts, picks, [w / total * ROUTED_SCALE for w in weights]


def _pack_halves(a):
    half = a.shape[1] // 2
    bits = lax.bitcast_convert_type(a.astype(BF16).astype(F32), jnp.int32)
    return lax.shift_right_logical(bits[:, :half], 16) | bits[:, half:]


def _unpack_halves(w):
    low = lax.bitcast_convert_type(lax.shift_left(w, 16), F32)
    high = lax.bitcast_convert_type(w & jnp.int32(-65536), F32)
    return low, high


def _router_kernel(x_ref, mod_ref, g_ref, wr_ref, eb_ref, tri_ref, hp_ref, eidx_ref, rank_ref, w_ref, cnt_ref):
    tm = x_ref.shape[0]

    @pl.when(pl.program_id(0) == 0)
    def _():
        cnt_ref[...] = jnp.zeros_like(cnt_ref)

    h = _modulated_norm(x_ref[...], g_ref[...], mod_ref[0, SC2:SC2 + 1, :], mod_ref[0, SH2:SH2 + 1, :])
    hp_ref[...] = _pack_halves(h)
    logits_t = lax.dot_general(wr_ref[...], h, (((1,), (1,)), ((), ())), preferred_element_type=F32,
                               precision=lax.Precision.HIGHEST)
    firsts, picks, weights = _route_transposed(logits_t, eb_ref[...])
    sel = functools.reduce(jnp.logical_or, picks)
    sel_f = jnp.where(sel, 1.0, 0.0)
    incl = jnp.dot(sel_f.astype(BF16), tri_ref[...], preferred_element_type=F32)
    before = cnt_ref[:, 0:1] + incl - sel_f
    eidx_ref[...] = jnp.concatenate(firsts, axis=0)
    rank_ref[...] = jnp.concatenate(
        [jnp.sum(jnp.where(p, before, 0.0), axis=0, keepdims=True) for p in picks], axis=0).astype(jnp.int32)
    pad = jnp.concatenate(weights + [jnp.zeros((LANES - TOP_K, tm), F32)], axis=0)
    w_ref[...] = pad.T
    cnt_ref[...] = cnt_ref[...] + incl[:, tm - 1:tm]


def _router(x2, mod, g, w_router_t, e_bias, seq, tm=512):
    n, d = x2.shape
    blocks_per_batch = seq // tm
    tri = jnp.asarray(np.triu(np.ones((tm, tm), np.float32)), BF16)
    tok = lambda r: pl.BlockSpec((r, tm), lambda i: (0, i))
    const = lambda a: pl.BlockSpec(a.shape, lambda i: (0,) * a.ndim)
    return pl.pallas_call(
        _router_kernel,
        grid=(n // tm,),
        in_specs=[
            pl.BlockSpec((tm, d), lambda i: (i, 0)),
            pl.BlockSpec((1, 6, d), lambda i: (i // blocks_per_batch, 0, 0)),
            const(g), const(w_router_t), const(e_bias), const(tri),
        ],
        out_specs=[
            pl.BlockSpec((tm, d // 2), lambda i: (i, 0)),
            tok(TOP_K), tok(TOP_K),
            pl.BlockSpec((tm, LANES), lambda i: (i, 0)),
            pl.BlockSpec((N_EXPERTS, LANES), lambda i: (0, 0)),
        ],
        out_shape=[
            jax.ShapeDtypeStruct((n, d // 2), jnp.int32),
            jax.ShapeDtypeStruct((TOP_K, n), jnp.int32),
            jax.ShapeDtypeStruct((TOP_K, n), jnp.int32),
            jax.ShapeDtypeStruct((n, LANES), F32),
            jax.ShapeDtypeStruct((N_EXPERTS, LANES), F32),
        ],
        compiler_params=_params(("arbitrary",)),
        name="moe_router",
    )(x2, mod, g, w_router_t, e_bias, tri)


def _sc_worker_id():
    return lax.axis_index("subcore") * SC_CORES + lax.axis_index("core")


def _sc_scatter_rows(src, idx3, n_out):
    n, w = src.shape
    per_worker = n // SC_CHUNK // SC_WORKERS
    mesh = plsc.VectorSubcoreMesh(core_axis_name="core", subcore_axis_name="subcore")

    @functools.partial(
        pl.kernel, mesh=mesh, out_type=jax.ShapeDtypeStruct((n_out, w), src.dtype), name="moe_dispatch",
        scratch_types=[pltpu.VMEM((TOP_K, SC_CHUNK), jnp.int32), pltpu.VMEM((SC_CHUNK, w), src.dtype),
                       pltpu.SemaphoreType.DMA])
    def scatter(src_hbm, idx_hbm, out_hbm, idx_v, rows_v, sem):
        first = _sc_worker_id() * per_worker

        @pl.loop(0, per_worker)
        def _(i):
            chunk = first + i
            pltpu.sync_copy(idx_hbm.at[chunk], idx_v)
            pltpu.sync_copy(src_hbm.at[pl.ds(chunk * SC_CHUNK, SC_CHUNK)], rows_v)
            copies = [pltpu.make_async_copy(rows_v, out_hbm.at[idx_v.at[k]], sem) for k in range(TOP_K)]
            for cp in copies:
                cp.start()
            for cp in copies:
                cp.wait()

    return scatter(src, idx3)


def _sc_gather_rows(src, idx3):
    _, w = src.shape
    chunks = idx3.shape[0]
    per_worker = chunks // SC_WORKERS
    mesh = plsc.VectorSubcoreMesh(core_axis_name="core", subcore_axis_name="subcore")

    @functools.partial(
        pl.kernel, mesh=mesh, out_type=jax.ShapeDtypeStruct((TOP_K, chunks * SC_CHUNK, w), src.dtype),
        name="moe_collect",
        scratch_types=[pltpu.VMEM((TOP_K, SC_CHUNK), jnp.int32), pltpu.VMEM((SC_CHUNK, w), src.dtype),
                       pltpu.SemaphoreType.DMA])
    def gather(src_hbm, idx_hbm, out_hbm, idx_v, rows_v, sem):
        first = _sc_worker_id() * per_worker

        @pl.loop(0, per_worker)
        def _(i):
            chunk = first + i
            pltpu.sync_copy(idx_hbm.at[chunk], idx_v)
            for k in range(TOP_K):
                pltpu.async_copy(src_hbm.at[idx_v.at[k]], rows_v, sem).wait()
                pltpu.sync_copy(rows_v, out_hbm.at[k, pl.ds(chunk * SC_CHUNK, SC_CHUNK)])

    return gather(src, idx3)


def _swiglu(x, w_gate_up, w_down):
    gu = jnp.dot(x, w_gate_up, preferred_element_type=F32)
    gate = gu[:, :EXPERT_DIM]
    act = (gate * jax.nn.sigmoid(gate)) * gu[:, EXPERT_DIM:]
    return jnp.dot(act.astype(BF16), w_down, preferred_element_type=F32)


def _expert_ffn_kernel(be_ref, nv_ref, xs_ref, wgu_ref, wd_ref, ys_ref):
    nvalid = nv_ref[pl.program_id(0)]

    @pl.when(nvalid > 0)
    def _():
        packed = xs_ref[...]
        row = lax.broadcasted_iota(jnp.int32, packed.shape, 0)
        low, high = _unpack_halves(jnp.where(row < nvalid, packed, 0))
        x = jnp.concatenate([low, high], axis=1).astype(BF16)
        ys_ref[...] = _pack_halves(_swiglu(x, wgu_ref[0], wd_ref[0]))

    @pl.when(nvalid == 0)
    def _():
        ys_ref[...] = jnp.zeros_like(ys_ref)


def _expert_ffn(xs, block_expert, block_valid, w_gate_up, w_down, tb):
    p, half = xs.shape
    d = 2 * half
    grid_spec = pltpu.PrefetchScalarGridSpec(
        num_scalar_prefetch=2,
        grid=(p // tb,),
        in_specs=[
            pl.BlockSpec((tb, half), lambda i, be, nv: (i, 0)),
            pl.BlockSpec((1, d, 2 * EXPERT_DIM), lambda i, be, nv: (be[i], 0, 0)),
            pl.BlockSpec((1, EXPERT_DIM, d), lambda i, be, nv: (be[i], 0, 0)),
        ],
        out_specs=pl.BlockSpec((tb, half), lambda i, be, nv: (i, 0)),
    )
    return pl.pallas_call(
        _expert_ffn_kernel,
        grid_spec=grid_spec,
        out_shape=jax.ShapeDtypeStruct((p, half), jnp.int32),
        compiler_params=_params(("arbitrary",)),
        name="moe_expert_ffn",
    )(block_expert, block_valid, xs, w_gate_up, w_down)


def _combine_kernel(x_ref, hp_ref, yg_ref, w_ref, mod_ref, wgu_ref, wd_ref, gf_ref, o_ref, *, final_norm):
    low, high = _unpack_halves(hp_ref[...])
    h = jnp.concatenate([low, high], axis=1).astype(BF16)
    shared = _swiglu(h, wgu_ref[...], wd_ref[...])
    half = hp_ref.shape[1]
    acc_low, acc_high = shared[:, :half], shared[:, half:]
    w = w_ref[...]
    for k in range(TOP_K):
        low, high = _unpack_halves(yg_ref[k])
        wk = w[:, k:k + 1]
        acc_low = acc_low + wk * low
        acc_high = acc_high + wk * high
    y = jnp.concatenate([acc_low, acc_high], axis=1)
    out = x_ref[...] + mod_ref[0, GT2:GT2 + 1, :] * y
    if final_norm:
        out = out * lax.rsqrt(jnp.mean(out * out, axis=-1, keepdims=True) + EPS) * gf_ref[...]
    o_ref[...] = out


def _combine(x2, hp, yg, w, mod, ws_gate_up, ws_down, g_final, seq, final_norm, tm=512):
    n, d = x2.shape
    blocks_per_batch = seq // tm
    const = lambda a: pl.BlockSpec(a.shape, lambda i: (0,) * a.ndim)
    return pl.pallas_call(
        functools.partial(_combine_kernel, final_norm=final_norm),
        grid=(n // tm,),
        in_specs=[
            pl.BlockSpec((tm, d), lambda i: (i, 0)),
            pl.BlockSpec((tm, d // 2), lambda i: (i, 0)),
            pl.BlockSpec((TOP_K, tm, d // 2), lambda i: (0, i, 0)),
            pl.BlockSpec((tm, LANES), lambda i: (i, 0)),
            pl.BlockSpec((1, 6, d), lambda i: (i // blocks_per_batch, 0, 0)),
            const(ws_gate_up), const(ws_down), const(g_final),
        ],
        out_specs=pl.BlockSpec((tm, d), lambda i: (i, 0)),
        out_shape=jax.ShapeDtypeStruct((n, d), F32),
        compiler_params=_params(("arbitrary",)),
        name="moe_combine",
    )(x2, hp, yg, w, mod, ws_gate_up, ws_down, g_final)


def _token_mixer(x2, mod, g_mix, w_in, b_gate, rpb, w_pa, w_pb, w_o, batch, seq):
    d = x2.shape[1]
    qkv = _inproj(x2, mod, g_mix.reshape(1, d), w_in[:, :QKV_COLS].astype(BF16), seq)
    o_a = _neighbourhood_attention(qkv, _na_bias_table(rpb), batch, seq)

    dil0 = 3 * NA_WIDTH
    o_groups, lse_groups = [], []
    for grp, (_, dilation) in enumerate(DIL_GROUPS):
        cols = [dil0 + part * DIL_WIDTH + grp * DIL_OUT_WIDTH for part in range(3)]
        if dilation == 1:
            seqs = qkv.reshape(batch, seq, QKV_COLS)
            offsets = tuple(c // LANES for c in cols)
        else:
            parts = jnp.concatenate([qkv[:, c:c + DIL_OUT_WIDTH] for c in cols], axis=1)
            seqs = _to_residue(parts, batch, dilation)
            offsets = tuple(part * DIL_OUT_WIDTH // LANES for part in range(3))
        o_g, lse_g = _dilated_attention(seqs, offsets, grp)
        o_groups.append(_from_residue(o_g, batch, dilation))
        lse_groups.append(_from_residue(lse_g, batch, dilation))

    return _outproj(x2, o_a, o_groups, lse_groups, mod, g_mix.reshape(1, d),
                    w_in[:, QKV_COLS:].astype(BF16), b_gate.reshape(1, -1),
                    w_pa.astype(BF16), w_pb.astype(BF16), w_o.astype(BF16), seq)


def _dispatch_plan(eidx_t, rank_t, counts, tb):
    n = eidx_t.shape[1]
    n_blocks = -(-(n * TOP_K + N_EXPERTS * (tb - 1)) // tb)
    padded = (counts + tb - 1) // tb * tb
    seg_end = jnp.cumsum(padded)
    seg_start = seg_end - padded
    experts = jnp.arange(N_EXPERTS, dtype=jnp.int32)

    def lookup(table, idx):
        sel = idx[None] == experts.reshape((N_EXPERTS,) + (1,) * idx.ndim)
        return jnp.sum(jnp.where(sel, table.reshape((N_EXPERTS,) + (1,) * idx.ndim), 0), axis=0)

    dest_t = lookup(seg_start, eidx_t) + rank_t
    idx3 = dest_t.reshape(TOP_K, n // SC_CHUNK, SC_CHUNK).transpose(1, 0, 2)
    block_start = jnp.arange(n_blocks, dtype=jnp.int32) * tb
    block_expert = jnp.sum((seg_end[:, None] <= block_start[None, :]).astype(jnp.int32), axis=0)
    block_expert = jnp.minimum(block_expert, N_EXPERTS - 1)
    block_valid = jnp.clip(lookup(counts, block_expert) - (block_start - lookup(seg_start, block_expert)), 0, tb)
    return idx3, block_expert, block_valid.astype(jnp.int32), n_blocks


def _moe_layer(x2, mod, g_ffn, w_router, e_bias, we_gate, we_up, we_down, ws_gate, ws_up, ws_down, g_final, seq,
               final_norm, tb=EXPERT_ROW_BLOCK):
    d = x2.shape[1]
    hp, eidx_t, rank_t, w, cnt = _router(x2, mod, g_ffn.reshape(1, d), w_router.T, e_bias.reshape(-1, 1), seq)
    counts = cnt[:, 0].astype(jnp.int32)
    idx3, block_expert, block_valid, n_blocks = _dispatch_plan(eidx_t, rank_t, counts, tb)
    xs = _sc_scatter_rows(hp, idx3, n_blocks * tb)
    ys = _expert_ffn(xs, block_expert, block_valid, jnp.concatenate([we_gate, we_up], axis=-1).astype(BF16),
                     we_down.astype(BF16), tb)
    yg = _sc_gather_rows(ys, idx3)
    return _combine(x2, hp, yg, w, mod, jnp.concatenate([ws_gate, ws_up], axis=-1).astype(BF16),
                    ws_down.astype(BF16), g_final.reshape(1, d), seq, final_norm)


def kernel(x, c, w_ada, b_ada, g_mix, w_in, b_gate, rpb, w_pa, w_pb, w_o, g_ffn, w_router, e_bias,
           we_gate, we_up, we_down, ws_gate, ws_up, ws_down, g_final):
    batch, seq, d = x.shape
    depth = w_ada.shape[0]
    mods = _ada(c, w_ada, b_ada).reshape(depth, batch, 6, d)
    x2 = x.reshape(batch * seq, d)
    for l in range(depth):
        x2 = _token_mixer(x2, mods[l], g_mix[l], w_in[l], b_gate[l], rpb[l], w_pa[l], w_pb[l], w_o[l],
                          batch, seq)
        x2 = _moe_layer(x2, mods[l], g_ffn[l], w_router[l], e_bias[l], we_gate[l], we_up[l], we_down[l],
                        ws_gate[l], ws_up[l], ws_down[l], g_final, seq, final_norm=(l == depth - 1))
    return x2.reshape(batch, seq, d)
```

```python
import functools

import numpy as np
import jax
import jax.numpy as jnp
from jax import lax
from jax.experimental import pallas as pl
from jax.experimental.pallas import tpu as pltpu
from jax.experimental.pallas import tpu_sc as plsc

HEAD_DIM = 64
GRID_W = 64
NA_HEADS = 8
NA_WIN_ROWS = 8
NA_WIN_COLS = 16
DIL_GROUPS = ((128, 1), (512, 4), (2048, 16))
DIL_HEADS_PER_GROUP = 4
N_DIL_GROUPS = len(DIL_GROUPS)
NA_WIDTH = NA_HEADS * HEAD_DIM
DIL_WIDTH = N_DIL_GROUPS * DIL_HEADS_PER_GROUP * HEAD_DIM
DIL_OUT_WIDTH = DIL_HEADS_PER_GROUP * HEAD_DIM
QKV_COLS = 3 * (NA_WIDTH + DIL_WIDTH)
N_EXPERTS = 64
TOP_K = 8
N_EXPERT_GROUPS = 8
TOP_GROUPS = 4
EXPERT_DIM = 256
ROUTED_SCALE = 2.5
ALIBI_MAX = 8.0
EPS = 1e-6
NEG_INF = -1e30

LANES = 128
HEADS_PER_LANE_TILE = LANES // HEAD_DIM
DIL_BLOCK = 64
VMEM_LIMIT_BYTES = 56 * 1024 * 1024

SC_CORES = 2
SC_SUBCORES = 16
SC_WORKERS = SC_CORES * SC_SUBCORES
SC_CHUNK = 64
EXPERT_ROW_BLOCK = 512
BATCH_PARTS = 2

F32 = jnp.float32
BF16 = jnp.bfloat16

SH1, SC1, GT1, SH2, SC2, GT2 = range(6)


def _params(sem):
    return pltpu.CompilerParams(dimension_semantics=sem, vmem_limit_bytes=VMEM_LIMIT_BYTES)


def _modulated_norm(x, g, scale, shift):
    r = lax.rsqrt(jnp.mean(x * x, axis=-1, keepdims=True) + EPS)
    return (x * r * g) * (1.0 + scale) + shift


def _ada_kernel(c_ref, w_ref, b_ref, o_ref):
    c = c_ref[...]
    act = c * jax.nn.sigmoid(c)
    o_ref[0] = jnp.dot(act, w_ref[0], preferred_element_type=F32,
                       precision=lax.Precision.HIGHEST) + b_ref[0]


def _ada(c, w_ada, b_ada):
    depth, d, six_d = w_ada.shape
    b = c.shape[0]
    tn = d
    return pl.pallas_call(
        _ada_kernel,
        grid=(depth, six_d // tn),
        in_specs=[
            pl.BlockSpec((b, d), lambda l, j: (0, 0)),
            pl.BlockSpec((1, d, tn), lambda l, j: (l, 0, j)),
            pl.BlockSpec((1, 1, tn), lambda l, j: (l, 0, j)),
        ],
        out_specs=pl.BlockSpec((1, b, tn), lambda l, j: (l, 0, j)),
        out_shape=jax.ShapeDtypeStruct((depth, b, six_d), F32),
        compiler_params=_params(("arbitrary", "arbitrary")),
        name="ada_mod",
    )(c, w_ada, b_ada.reshape(depth, 1, six_d))


def _inproj_kernel(x_ref, mod_ref, g_ref, w_ref, o_ref, h_ref):
    @pl.when(pl.program_id(1) == 0)
    def _():
        h = _modulated_norm(x_ref[...], g_ref[...], mod_ref[0, SC1:SC1 + 1, :], mod_ref[0, SH1:SH1 + 1, :])
        h_ref[...] = h.astype(BF16)

    o_ref[...] = jnp.dot(h_ref[...], w_ref[...], preferred_element_type=F32).astype(BF16)


def _inproj(x2, mod, g, w_qkv, seq, tm=1024, tn=768):
    n, d = x2.shape
    cols = w_qkv.shape[1]
    blocks_per_batch = seq // tm
    return pl.pallas_call(
        _inproj_kernel,
        grid=(n // tm, cols // tn),
        in_specs=[
            pl.BlockSpec((tm, d), lambda i, j: (i, 0)),
            pl.BlockSpec((1, 6, d), lambda i, j: (i // blocks_per_batch, 0, 0)),
            pl.BlockSpec((1, d), lambda i, j: (0, 0)),
            pl.BlockSpec((d, tn), lambda i, j: (0, j)),
        ],
        out_specs=pl.BlockSpec((tm, tn), lambda i, j: (i, j)),
        out_shape=jax.ShapeDtypeStruct((n, cols), BF16),
        scratch_shapes=[pltpu.VMEM((tm, d), BF16)],
        compiler_params=_params(("arbitrary", "arbitrary")),
        name="in_proj",
    )(x2, mod, g, w_qkv)


def _na_bias_table(rpb):
    heads = rpb.shape[0]
    cols = np.arange(GRID_W)
    col_start = np.clip(cols - NA_WIN_COLS // 2, 0, GRID_W - NA_WIN_COLS)
    col_mask = (cols[None, :] >= col_start[:, None]) & (cols[None, :] < col_start[:, None] + NA_WIN_COLS)
    dc = np.clip(cols[None, :] - cols[:, None], -(NA_WIN_COLS - 1), NA_WIN_COLS - 1) + NA_WIN_COLS - 1
    rpb_cols = rpb[:, :, dc].astype(F32)
    dr = np.arange(NA_WIN_ROWS)[None, :] - np.arange(NA_WIN_ROWS)[:, None] + NA_WIN_ROWS - 1
    t = rpb_cols[:, dr]
    t = t.transpose(0, 1, 3, 2, 4)
    t = jnp.where(col_mask[:, None, :], t, NEG_INF)
    t = t.reshape(heads // HEADS_PER_LANE_TILE, HEADS_PER_LANE_TILE, NA_WIN_ROWS, GRID_W, NA_WIN_ROWS * GRID_W)
    return t.transpose(0, 2, 1, 3, 4).reshape(heads // HEADS_PER_LANE_TILE, NA_WIN_ROWS,
                                              HEADS_PER_LANE_TILE * GRID_W, NA_WIN_ROWS * GRID_W)


def _stack_heads(q, low):
    scaled = q * (HEAD_DIM ** -0.5)
    zero = jnp.zeros_like(scaled)
    return jnp.concatenate([jnp.where(low, scaled, zero), jnp.where(low, zero, scaled)], axis=0)


def _stacked_attention(items):
    scores = [lax.dot_general(q2, kw, (((1,), (1,)), ((), ())), preferred_element_type=F32) + bias
              for q2, kw, _, bias in items]
    probs = []
    for s in scores:
        m = jnp.max(s, axis=-1, keepdims=True)
        p = jnp.exp(s - m)
        probs.append((p.astype(BF16), m, jnp.sum(p, axis=-1, keepdims=True)))
    return [(jnp.dot(p, vw, preferred_element_type=F32) / z, m, z)
            for (p, m, z), (_, _, vw, _) in zip(probs, items)]


def _unstack_heads(a, low):
    half = a.shape[0] // HEADS_PER_LANE_TILE
    return jnp.where(low, a[:half], a[half:])


def _na_kernel(q_ref, k_ref, v_ref, bias_ref, o_ref, *, rows, rows_per_step):
    kr = NA_WIN_ROWS
    low = lax.broadcasted_iota(jnp.int32, (GRID_W, LANES), 1) < HEAD_DIM

    def body(i, carry):
        items, qrows = [], []
        for u in range(rows_per_step):
            r = i * rows_per_step + u
            rs = jnp.clip(r - kr // 2, 0, rows - kr)
            qrows.append(pl.ds(pl.multiple_of(r * GRID_W, GRID_W), GRID_W))
            wrows = pl.ds(pl.multiple_of(rs * GRID_W, GRID_W), kr * GRID_W)
            items.append((_stack_heads(q_ref[qrows[-1], :], low), k_ref[wrows, :], v_ref[wrows, :],
                          bias_ref[0, r - rs]))
        for rows_u, (o, _, _) in zip(qrows, _stacked_attention(items)):
            o_ref[rows_u, :] = _unstack_heads(o, low).astype(o_ref.dtype)
        return carry

    lax.fori_loop(0, rows // rows_per_step, body, 0)


def _neighbourhood_attention(qkv, bias, batch, seq, rows_per_step=8):
    n = qkv.shape[0]
    rows = seq // GRID_W
    pairs = NA_WIDTH // LANES
    return pl.pallas_call(
        functools.partial(_na_kernel, rows=rows, rows_per_step=rows_per_step),
        grid=(pairs, batch),
        in_specs=[
            pl.BlockSpec((seq, LANES), lambda p, b: (b, p)),
            pl.BlockSpec((seq, LANES), lambda p, b: (b, pairs + p)),
            pl.BlockSpec((seq, LANES), lambda p, b: (b, 2 * pairs + p)),
            pl.BlockSpec((1, NA_WIN_ROWS, HEADS_PER_LANE_TILE * GRID_W, NA_WIN_ROWS * GRID_W),
                         lambda p, b: (p, 0, 0, 0)),
        ],
        out_specs=pl.BlockSpec((seq, LANES), lambda p, b: (b, p)),
        out_shape=jax.ShapeDtypeStruct((n, NA_WIDTH), BF16),
        compiler_params=_params(("arbitrary", "arbitrary")),
        name="na_attn",
    )(qkv, qkv, qkv, bias)


def _alibi_slopes():
    n = N_DIL_GROUPS * DIL_HEADS_PER_GROUP
    s = np.exp2(-ALIBI_MAX * np.arange(1, n + 1, dtype=np.float64) / n).astype(np.float32)
    return s.reshape(N_DIL_GROUPS, DIL_HEADS_PER_GROUP)


def _dil_bias_table(group):
    blk = DIL_BLOCK
    dilation = DIL_GROUPS[group][1]
    slopes = _alibi_slopes()[group]
    qi = np.arange(blk)[:, None]
    kj = np.arange(3 * blk)[None, :]
    tables = []
    for shift in range(3):
        arel = np.abs(kj - qi - shift * blk)
        dist = (dilation * arel).astype(np.float32)
        per_head = [np.where(arel <= blk, -slopes[h] * dist, np.float32(NEG_INF)) for h in range(DIL_HEADS_PER_GROUP)]
        tables.append(np.stack(per_head))
    t = np.stack(tables, axis=1).astype(np.float32)
    pairs = DIL_HEADS_PER_GROUP // HEADS_PER_LANE_TILE
    t = t.reshape(pairs, HEADS_PER_LANE_TILE, 3, blk, 3 * blk).transpose(0, 2, 1, 3, 4)
    return t.reshape(pairs, 3, HEADS_PER_LANE_TILE * blk, 3 * blk)


def _dil_kernel(q_ref, k_ref, v_ref, bias_ref, o_ref, lse_ref, *, nb, blocks_per_step):
    blk = DIL_BLOCK
    win = 3 * blk
    low = lax.broadcasted_iota(jnp.int32, (blk, LANES), 1) < HEAD_DIM
    steps = q_ref.shape[0] * nb

    def body(i, carry):
        items, dst = [], []
        for u in range(blocks_per_step):
            t = i * blocks_per_step + u
            sq = t // nb
            n = t % nb
            wb = jnp.clip(n - 1, 0, nb - 3)
            qrows = pl.ds(pl.multiple_of(n * blk, blk), blk)
            wrows = pl.ds(pl.multiple_of(wb * blk, blk), win)
            dst.append((sq, qrows))
            items.append((_stack_heads(q_ref[sq, qrows, :], low), k_ref[sq, wrows, :], v_ref[sq, wrows, :],
                          bias_ref[0, n - wb]))
        for (sq, qrows), (o, m, z) in zip(dst, _stacked_attention(items)):
            o_ref[sq, qrows, :] = _unstack_heads(o, low)
            lse_ref[sq, qrows, :] = _unstack_heads(jnp.broadcast_to(m + jnp.log(z), o.shape), low)
        return carry

    lax.fori_loop(0, steps // blocks_per_step, body, 0)


def _dilated_attention(qkv3, col_offsets, group, tokens_per_step=4096, blocks_per_step=8):
    nseq, length, _ = qkv3.shape
    nb = length // DIL_BLOCK
    sb = max(1, tokens_per_step // length)
    pairs = DIL_OUT_WIDTH // LANES
    qo, ko, vo = col_offsets
    bias = jnp.asarray(_dil_bias_table(group))
    out = jax.ShapeDtypeStruct((nseq, length, DIL_OUT_WIDTH), F32)
    return pl.pallas_call(
        functools.partial(_dil_kernel, nb=nb, blocks_per_step=blocks_per_step),
        grid=(pairs, nseq // sb),
        in_specs=[
            pl.BlockSpec((sb, length, LANES), lambda p, b: (b, 0, qo + p)),
            pl.BlockSpec((sb, length, LANES), lambda p, b: (b, 0, ko + p)),
            pl.BlockSpec((sb, length, LANES), lambda p, b: (b, 0, vo + p)),
            pl.BlockSpec((1,) + bias.shape[1:], lambda p, b: (p, 0, 0, 0)),
        ],
        out_specs=[pl.BlockSpec((sb, length, LANES), lambda p, b: (b, 0, p))] * 2,
        out_shape=[out, out],
        compiler_params=_params(("arbitrary", "arbitrary")),
        name=f"dil_attn_g{group}",
    )(qkv3, qkv3, qkv3, bias)


def _to_residue(a, batch, dilation):
    n, c = a.shape
    length = n // batch // dilation
    return a.reshape(batch, length, dilation, c).transpose(0, 2, 1, 3).reshape(batch * dilation, length, c)


def _from_residue(a, batch, dilation):
    nseq, length, c = a.shape
    return a.reshape(batch, dilation, length, c).transpose(0, 2, 1, 3).reshape(batch * dilation * length, c)


def _outproj_kernel(x_ref, oa_ref, o0_ref, o1_ref, o2_ref, l0_ref, l1_ref, l2_ref, mod_ref, g_ref,
                    wg_ref, bg_ref, wpa_ref, wpb_ref, wo_ref, out_ref):
    d = x_ref.shape[1]
    x = x_ref[...]
    h = _modulated_norm(x, g_ref[...], mod_ref[0, SC1:SC1 + 1, :], mod_ref[0, SH1:SH1 + 1, :]).astype(BF16)

    lses = [l0_ref[...], l1_ref[...], l2_ref[...]]
    outs = [o0_ref[...], o1_ref[...], o2_ref[...]]
    top = jnp.maximum(jnp.maximum(lses[0], lses[1]), lses[2])
    es = [jnp.exp(l - top) for l in lses]
    den = es[0] + es[1] + es[2]
    ob = (es[0] * outs[0] + es[1] * outs[1] + es[2] * outs[2]) / den

    ya = jnp.dot(oa_ref[...], wpa_ref[...], preferred_element_type=F32)
    yb = jnp.dot(ob.astype(BF16), wpb_ref[...], preferred_element_type=F32)
    ga = jax.nn.sigmoid(jnp.dot(h, wg_ref[:, :d], preferred_element_type=F32) + bg_ref[:, :d])
    mix = ga * ya
    gb = jax.nn.sigmoid(jnp.dot(h, wg_ref[:, d:], preferred_element_type=F32) + bg_ref[:, d:])
    mix = mix + gb * yb
    y = jnp.dot(mix.astype(BF16), wo_ref[...], preferred_element_type=F32)
    out_ref[...] = x + mod_ref[0, GT1:GT1 + 1, :] * y


def _outproj(x2, oa, o_groups, lse_groups, mod, g, w_gate, b_gate, w_pa, w_pb, w_o, seq, tm=512):
    n, d = x2.shape
    blocks_per_batch = seq // tm
    row = lambda c: pl.BlockSpec((tm, c), lambda i: (i, 0))
    full = lambda a: pl.BlockSpec(a.shape, lambda i: (0,) * a.ndim)
    return pl.pallas_call(
        _outproj_kernel,
        grid=(n // tm,),
        in_specs=[row(d), row(NA_WIDTH)] + [row(DIL_OUT_WIDTH)] * 6 + [
            pl.BlockSpec((1, 6, d), lambda i: (i // blocks_per_batch, 0, 0)),
            full(g), full(w_gate), full(b_gate), full(w_pa), full(w_pb), full(w_o),
        ],
        out_specs=row(d),
        out_shape=jax.ShapeDtypeStruct((n, d), F32),
        compiler_params=_params(("arbitrary",)),
        name="out_proj",
    )(x2, oa, *o_groups, *lse_groups, mod, g, w_gate, b_gate, w_pa, w_pb, w_o)


def _first_index_of_max(cur, idx, size):
    m = jnp.max(cur, axis=0, keepdims=True)
    first = jnp.min(jnp.where(cur == m, idx, size), axis=0, keepdims=True)
    return m, first


def _route_transposed(logits_t, e_bias):
    tokens = logits_t.shape[1]
    per_group = N_EXPERTS // N_EXPERT_GROUPS
    scores = jax.nn.sigmoid(logits_t)
    biased = scores + e_bias
    midx = lax.broadcasted_iota(jnp.int32, (per_group, tokens), 0)
    grp_scores = []
    for g in range(N_EXPERT_GROUPS):
        vals = biased[g * per_group:(g + 1) * per_group, :]
        m1, first = _first_index_of_max(vals, midx, per_group)
        m2 = jnp.max(jnp.where(midx == first, -jnp.inf, vals), axis=0, keepdims=True)
        grp_scores.append(m1 + m2)
    cur = jnp.concatenate(grp_scores, axis=0)
    gidx = lax.broadcasted_iota(jnp.int32, (N_EXPERT_GROUPS, tokens), 0)
    grp_sel = jnp.zeros((N_EXPERT_GROUPS, tokens), jnp.bool_)
    for _ in range(TOP_GROUPS):
        _, first = _first_index_of_max(cur, gidx, N_EXPERT_GROUPS)
        pick = gidx == first
        grp_sel = jnp.logical_or(grp_sel, pick)
        cur = jnp.where(pick, -jnp.inf, cur)
    rows = []
    for g in range(N_EXPERT_GROUPS):
        vals = biased[g * per_group:(g + 1) * per_group, :]
        rows.append(jnp.where(grp_sel[g:g + 1, :], vals, NEG_INF))
    cur = jnp.concatenate(rows, axis=0)
    eidx = lax.broadcasted_iota(jnp.int32, (N_EXPERTS, tokens), 0)
    firsts, picks, weights = [], [], []
    for _ in range(TOP_K):
        _, first = _first_index_of_max(cur, eidx, N_EXPERTS)
        pick = eidx == first
        firsts.append(first)
        picks.append(pick)
        weights.append(jnp.sum(jnp.where(pick, scores, 0.0), axis=0, keepdims=True))
        cur = jnp.where(pick, -jnp.inf, cur)
    total = functools.reduce(lambda a, b: a + b, weights)
    return firsts, picks, [w / total * ROUTED_SCALE for w in weights]


def _pack_halves(a):
    half = a.shape[1] // 2
    bits = lax.bitcast_convert_type(a.astype(BF16).astype(F32), jnp.int32)
    return lax.shift_right_logical(bits[:, :half], 16) | bits[:, half:]


def _unpack_halves(w):
    low = lax.bitcast_convert_type(lax.shift_left(w, 16), F32)
    high = lax.bitcast_convert_type(w & jnp.int32(-65536), F32)
    return low, high


def _router_kernel(x_ref, mod_ref, g_ref, wr_ref, eb_ref, tri_ref, hp_ref, eidx_ref, rank_ref, w_ref, cnt_ref):
    tm = x_ref.shape[0]

    @pl.when(pl.program_id(0) == 0)
    def _():
        cnt_ref[...] = jnp.zeros_like(cnt_ref)

    h = _modulated_norm(x_ref[...], g_ref[...], mod_ref[0, SC2:SC2 + 1, :], mod_ref[0, SH2:SH2 + 1, :])
    hp_ref[...] = _pack_halves(h)
    logits_t = lax.dot_general(wr_ref[...], h, (((1,), (1,)), ((), ())), preferred_element_type=F32,
                               precision=lax.Precision.HIGHEST)
    firsts, picks, weights = _route_transposed(logits_t, eb_ref[...])
    sel = functools.reduce(jnp.logical_or, picks)
    sel_f = jnp.where(sel, 1.0, 0.0)
    incl = jnp.dot(sel_f.astype(BF16), tri_ref[...], preferred_element_type=F32)
    before = cnt_ref[:, 0:1] + incl - sel_f
    eidx_ref[...] = jnp.concatenate(firsts, axis=0)
    rank_ref[...] = jnp.concatenate(
        [jnp.sum(jnp.where(p, before, 0.0), axis=0, keepdims=True) for p in picks], axis=0).astype(jnp.int32)
    pad = jnp.concatenate(weights + [jnp.zeros((LANES - TOP_K, tm), F32)], axis=0)
    w_ref[...] = pad.T
    cnt_ref[...] = cnt_ref[...] + incl[:, tm - 1:tm]


def _router(x2, mod, g, w_router_t, e_bias, seq, tm=512):
    n, d = x2.shape
    blocks_per_batch = seq // tm
    tri = jnp.asarray(np.triu(np.ones((tm, tm), np.float32)), BF16)
    tok = lambda r: pl.BlockSpec((r, tm), lambda i: (0, i))
    const = lambda a: pl.BlockSpec(a.shape, lambda i: (0,) * a.ndim)
    return pl.pallas_call(
        _router_kernel,
        grid=(n // tm,),
        in_specs=[
            pl.BlockSpec((tm, d), lambda i: (i, 0)),
            pl.BlockSpec((1, 6, d), lambda i: (i // blocks_per_batch, 0, 0)),
            const(g), const(w_router_t), const(e_bias), const(tri),
        ],
        out_specs=[
            pl.BlockSpec((tm, d // 2), lambda i: (i, 0)),
            tok(TOP_K), tok(TOP_K),
            pl.BlockSpec((tm, LANES), lambda i: (i, 0)),
            pl.BlockSpec((N_EXPERTS, LANES), lambda i: (0, 0)),
        ],
        out_shape=[
            jax.ShapeDtypeStruct((n, d // 2), jnp.int32),
            jax.ShapeDtypeStruct((TOP_K, n), jnp.int32),
            jax.ShapeDtypeStruct((TOP_K, n), jnp.int32),
            jax.ShapeDtypeStruct((n, LANES), F32),
            jax.ShapeDtypeStruct((N_EXPERTS, LANES), F32),
        ],
        compiler_params=_params(("arbitrary",)),
        name="moe_router",
    )(x2, mod, g, w_router_t, e_bias, tri)


def _sc_worker_id():
    return lax.axis_index("subcore") * SC_CORES + lax.axis_index("core")


def _sc_scatter_rows(src, idx3, n_out):
    n, w = src.shape
    per_worker = n // SC_CHUNK // SC_WORKERS
    mesh = plsc.VectorSubcoreMesh(core_axis_name="core", subcore_axis_name="subcore")

    @functools.partial(
        pl.kernel, mesh=mesh, out_type=jax.ShapeDtypeStruct((n_out, w), src.dtype), name="moe_dispatch",
        scratch_types=[pltpu.VMEM((TOP_K, SC_CHUNK), jnp.int32), pltpu.VMEM((SC_CHUNK, w), src.dtype),
                       pltpu.SemaphoreType.DMA])
    def scatter(src_hbm, idx_hbm, out_hbm, idx_v, rows_v, sem):
        first = _sc_worker_id() * per_worker

        @pl.loop(0, per_worker)
        def _(i):
            chunk = first + i
            pltpu.sync_copy(idx_hbm.at[chunk], idx_v)
            pltpu.sync_copy(src_hbm.at[pl.ds(chunk * SC_CHUNK, SC_CHUNK)], rows_v)
            copies = [pltpu.make_async_copy(rows_v, out_hbm.at[idx_v.at[k]], sem) for k in range(TOP_K)]
            for cp in copies:
                cp.start()
            for cp in copies:
                cp.wait()

    return scatter(src, idx3)


def _sc_gather_rows(src, idx3):
    _, w = src.shape
    chunks = idx3.shape[0]
    per_worker = chunks // SC_WORKERS
    mesh = plsc.VectorSubcoreMesh(core_axis_name="core", subcore_axis_name="subcore")

    @functools.partial(
        pl.kernel, mesh=mesh, out_type=jax.ShapeDtypeStruct((TOP_K, chunks * SC_CHUNK, w), src.dtype),
        name="moe_collect",
        scratch_types=[pltpu.VMEM((TOP_K, SC_CHUNK), jnp.int32), pltpu.VMEM((SC_CHUNK, w), src.dtype),
                       pltpu.SemaphoreType.DMA])
    def gather(src_hbm, idx_hbm, out_hbm, idx_v, rows_v, sem):
        first = _sc_worker_id() * per_worker

        @pl.loop(0, per_worker)
        def _(i):
            chunk = first + i
            pltpu.sync_copy(idx_hbm.at[chunk], idx_v)
            for k in range(TOP_K):
                pltpu.async_copy(src_hbm.at[idx_v.at[k]], rows_v, sem).wait()
                pltpu.sync_copy(rows_v, out_hbm.at[k, pl.ds(chunk * SC_CHUNK, SC_CHUNK)])

    return gather(src, idx3)


def _swiglu(x, w_gate_up, w_down):
    gu = jnp.dot(x, w_gate_up, preferred_element_type=F32)
    gate = gu[:, :EXPERT_DIM]
    act = (gate * jax.nn.sigmoid(gate)) * gu[:, EXPERT_DIM:]
    return jnp.dot(act.astype(BF16), w_down, preferred_element_type=F32)


def _expert_ffn_kernel(be_ref, nv_ref, xs_ref, wgu_ref, wd_ref, ys_ref):
    nvalid = nv_ref[pl.program_id(0)]

    @pl.when(nvalid > 0)
    def _():
        packed = xs_ref[...]
        row = lax.broadcasted_iota(jnp.int32, packed.shape, 0)
        low, high = _unpack_halves(jnp.where(row < nvalid, packed, 0))
        x = jnp.concatenate([low, high], axis=1).astype(BF16)
        ys_ref[...] = _pack_halves(_swiglu(x, wgu_ref[0], wd_ref[0]))

    @pl.when(nvalid == 0)
    def _():
        ys_ref[...] = jnp.zeros_like(ys_ref)


def _expert_ffn(xs, block_expert, block_valid, w_gate_up, w_down, tb):
    p, half = xs.shape
    d = 2 * half
    grid_spec = pltpu.PrefetchScalarGridSpec(
        num_scalar_prefetch=2,
        grid=(p // tb,),
        in_specs=[
            pl.BlockSpec((tb, half), lambda i, be, nv: (i, 0)),
            pl.BlockSpec((1, d, 2 * EXPERT_DIM), lambda i, be, nv: (be[i], 0, 0)),
            pl.BlockSpec((1, EXPERT_DIM, d), lambda i, be, nv: (be[i], 0, 0)),
        ],
        out_specs=pl.BlockSpec((tb, half), lambda i, be, nv: (i, 0)),
    )
    return pl.pallas_call(
        _expert_ffn_kernel,
        grid_spec=grid_spec,
        out_shape=jax.ShapeDtypeStruct((p, half), jnp.int32),
        compiler_params=_params(("arbitrary",)),
        name="moe_expert_ffn",
    )(block_expert, block_valid, xs, w_gate_up, w_down)


def _combine_kernel(x_ref, hp_ref, yg_ref, w_ref, mod_ref, wgu_ref, wd_ref, gf_ref, o_ref, *, final_norm):
    low, high = _unpack_halves(hp_ref[...])
    h = jnp.concatenate([low, high], axis=1).astype(BF16)
    shared = _swiglu(h, wgu_ref[...], wd_ref[...])
    half = hp_ref.shape[1]
    acc_low, acc_high = shared[:, :half], shared[:, half:]
    w = w_ref[...]
    for k in range(TOP_K):
        low, high = _unpack_halves(yg_ref[k])
        wk = w[:, k:k + 1]
        acc_low = acc_low + wk * low
        acc_high = acc_high + wk * high
    y = jnp.concatenate([acc_low, acc_high], axis=1)
    out = x_ref[...] + mod_ref[0, GT2:GT2 + 1, :] * y
    if final_norm:
        out = out * lax.rsqrt(jnp.mean(out * out, axis=-1, keepdims=True) + EPS) * gf_ref[...]
    o_ref[...] = out


def _combine(x2, hp, yg, w, mod, ws_gate_up, ws_down, g_final, seq, final_norm, tm=512):
    n, d = x2.shape
    blocks_per_batch = seq // tm
    const = lambda a: pl.BlockSpec(a.shape, lambda i: (0,) * a.ndim)
    return pl.pallas_call(
        functools.partial(_combine_kernel, final_norm=final_norm),
        grid=(n // tm,),
        in_specs=[
            pl.BlockSpec((tm, d), lambda i: (i, 0)),
            pl.BlockSpec((tm, d // 2), lambda i: (i, 0)),
            pl.BlockSpec((TOP_K, tm, d // 2), lambda i: (0, i, 0)),
            pl.BlockSpec((tm, LANES), lambda i: (i, 0)),
            pl.BlockSpec((1, 6, d), lambda i: (i // blocks_per_batch, 0, 0)),
            const(ws_gate_up), const(ws_down), const(g_final),
        ],
        out_specs=pl.BlockSpec((tm, d), lambda i: (i, 0)),
        out_shape=jax.ShapeDtypeStruct((n, d), F32),
        compiler_params=_params(("arbitrary",)),
        name="moe_combine",
    )(x2, hp, yg, w, mod, ws_gate_up, ws_down, g_final)


def _token_mixer(x2, mod, g_mix, w_in, b_gate, rpb, w_pa, w_pb, w_o, batch, seq):
    d = x2.shape[1]
    qkv = _inproj(x2, mod, g_mix.reshape(1, d), w_in[:, :QKV_COLS].astype(BF16), seq)
    o_a = _neighbourhood_attention(qkv, _na_bias_table(rpb), batch, seq)

    dil0 = 3 * NA_WIDTH
    o_groups, lse_groups = [], []
    for grp, (_, dilation) in enumerate(DIL_GROUPS):
        cols = [dil0 + part * DIL_WIDTH + grp * DIL_OUT_WIDTH for part in range(3)]
        if dilation == 1:
            seqs = qkv.reshape(batch, seq, QKV_COLS)
            offsets = tuple(c // LANES for c in cols)
        else:
            parts = jnp.concatenate([qkv[:, c:c + DIL_OUT_WIDTH] for c in cols], axis=1)
            seqs = _to_residue(parts, batch, dilation)
            offsets = tuple(part * DIL_OUT_WIDTH // LANES for part in range(3))
        o_g, lse_g = _dilated_attention(seqs, offsets, grp)
        o_groups.append(_from_residue(o_g, batch, dilation))
        lse_groups.append(_from_residue(lse_g, batch, dilation))

    return _outproj(x2, o_a, o_groups, lse_groups, mod, g_mix.reshape(1, d),
                    w_in[:, QKV_COLS:].astype(BF16), b_gate.reshape(1, -1),
                    w_pa.astype(BF16), w_pb.astype(BF16), w_o.astype(BF16), seq)


def _dispatch_plan(eidx_t, rank_t, counts, tb):
    n = eidx_t.shape[1]
    n_blocks = -(-(n * TOP_K + N_EXPERTS * (tb - 1)) // tb)
    padded = (counts + tb - 1) // tb * tb
    seg_end = jnp.cumsum(padded)
    seg_start = seg_end - padded
    experts = jnp.arange(N_EXPERTS, dtype=jnp.int32)

    def lookup(table, idx):
        sel = idx[None] == experts.reshape((N_EXPERTS,) + (1,) * idx.ndim)
        return jnp.sum(jnp.where(sel, table.reshape((N_EXPERTS,) + (1,) * idx.ndim), 0), axis=0)

    dest_t = lookup(seg_start, eidx_t) + rank_t
    idx3 = dest_t.reshape(TOP_K, n // SC_CHUNK, SC_CHUNK).transpose(1, 0, 2)
    block_start = jnp.arange(n_blocks, dtype=jnp.int32) * tb
    block_expert = jnp.sum((seg_end[:, None] <= block_start[None, :]).astype(jnp.int32), axis=0)
    block_expert = jnp.minimum(block_expert, N_EXPERTS - 1)
    block_valid = jnp.clip(lookup(counts, block_expert) - (block_start - lookup(seg_start, block_expert)), 0, tb)
    return idx3, block_expert, block_valid.astype(jnp.int32), n_blocks


def _moe_layer(x2, mod, g_ffn, w_router, e_bias, we_gate, we_up, we_down, ws_gate, ws_up, ws_down, g_final, seq,
               final_norm, tb=EXPERT_ROW_BLOCK):
    d = x2.shape[1]
    hp, eidx_t, rank_t, w, cnt = _router(x2, mod, g_ffn.reshape(1, d), w_router.T, e_bias.reshape(-1, 1), seq)
    counts = cnt[:, 0].astype(jnp.int32)
    idx3, block_expert, block_valid, n_blocks = _dispatch_plan(eidx_t, rank_t, counts, tb)
    xs = _sc_scatter_rows(hp, idx3, n_blocks * tb)
    ys = _expert_ffn(xs, block_expert, block_valid, jnp.concatenate([we_gate, we_up], axis=-1).astype(BF16),
                     we_down.astype(BF16), tb)
    yg = _sc_gather_rows(ys, idx3)
    return _combine(x2, hp, yg, w, mod, jnp.concatenate([ws_gate, ws_up], axis=-1).astype(BF16),
                    ws_down.astype(BF16), g_final.reshape(1, d), seq, final_norm)


def kernel(x, c, w_ada, b_ada, g_mix, w_in, b_gate, rpb, w_pa, w_pb, w_o, g_ffn, w_router, e_bias,
           we_gate, we_up, we_down, ws_gate, ws_up, ws_down, g_final):
    batch, seq, d = x.shape
    depth = w_ada.shape[0]
    mods = _ada(c, w_ada, b_ada).reshape(depth, batch, 6, d)
    parts = BATCH_PARTS if batch % BATCH_PARTS == 0 else 1
    pb = batch // parts
    outs = []
    for part in range(parts):
        x2 = x[part * pb:(part + 1) * pb].reshape(pb * seq, d)
        for l in range(depth):
            mod = mods[l, part * pb:(part + 1) * pb]
            x2 = _token_mixer(x2, mod, g_mix[l], w_in[l], b_gate[l], rpb[l], w_pa[l], w_pb[l], w_o[l], pb, seq)
            x2 = _moe_layer(x2, mod, g_ffn[l], w_router[l], e_bias[l], we_gate[l], we_up[l], we_down[l],
                            ws_gate[l], ws_up[l], ws_down[l], g_final, seq, final_norm=(l == depth - 1))
        outs.append(x2.reshape(pb, seq, d))
    return jnp.concatenate(outs, axis=0)
```

```python
import functools

import numpy as np
import jax
import jax.numpy as jnp
from jax import lax
from jax.experimental import pallas as pl
from jax.experimental.pallas import tpu as pltpu
from jax.experimental.pallas import tpu_sc as plsc

HEAD_DIM = 64
GRID_W = 64
NA_HEADS = 8
NA_WIN_ROWS = 8
NA_WIN_COLS = 16
DIL_GROUPS = ((128, 1), (512, 4), (2048, 16))
DIL_HEADS_PER_GROUP = 4
N_DIL_GROUPS = len(DIL_GROUPS)
NA_WIDTH = NA_HEADS * HEAD_DIM
DIL_WIDTH = N_DIL_GROUPS * DIL_HEADS_PER_GROUP * HEAD_DIM
DIL_OUT_WIDTH = DIL_HEADS_PER_GROUP * HEAD_DIM
QKV_COLS = 3 * (NA_WIDTH + DIL_WIDTH)
N_EXPERTS = 64
TOP_K = 8
N_EXPERT_GROUPS = 8
TOP_GROUPS = 4
EXPERT_DIM = 256
ROUTED_SCALE = 2.5
ALIBI_MAX = 8.0
EPS = 1e-6
NEG_INF = -1e30

LANES = 128
HEADS_PER_LANE_TILE = LANES // HEAD_DIM
DIL_BLOCK = 64
VMEM_LIMIT_BYTES = 56 * 1024 * 1024

SC_CORES = 2
SC_SUBCORES = 16
SC_WORKERS = SC_CORES * SC_SUBCORES
SC_CHUNK = 64
EXPERT_ROW_BLOCK = 1024

F32 = jnp.float32
BF16 = jnp.bfloat16

SH1, SC1, GT1, SH2, SC2, GT2 = range(6)


def _params(sem):
    return pltpu.CompilerParams(dimension_semantics=sem, vmem_limit_bytes=VMEM_LIMIT_BYTES)


def _modulated_norm(x, g, scale, shift):
    r = lax.rsqrt(jnp.mean(x * x, axis=-1, keepdims=True) + EPS)
    return (x * r * g) * (1.0 + scale) + shift


def _ada_kernel(c_ref, w_ref, b_ref, o_ref):
    c = c_ref[...]
    act = c * jax.nn.sigmoid(c)
    o_ref[0] = jnp.dot(act, w_ref[0], preferred_element_type=F32,
                       precision=lax.Precision.HIGHEST) + b_ref[0]


def _ada(c, w_ada, b_ada):
    depth, d, six_d = w_ada.shape
    b = c.shape[0]
    tn = d
    return pl.pallas_call(
        _ada_kernel,
        grid=(depth, six_d // tn),
        in_specs=[
            pl.BlockSpec((b, d), lambda l, j: (0, 0)),
            pl.BlockSpec((1, d, tn), lambda l, j: (l, 0, j)),
            pl.BlockSpec((1, 1, tn), lambda l, j: (l, 0, j)),
        ],
        out_specs=pl.BlockSpec((1, b, tn), lambda l, j: (l, 0, j)),
        out_shape=jax.ShapeDtypeStruct((depth, b, six_d), F32),
        compiler_params=_params(("arbitrary", "arbitrary")),
        name="ada_mod",
    )(c, w_ada, b_ada.reshape(depth, 1, six_d))


def _inproj_kernel(x_ref, mod_ref, g_ref, w_ref, o_ref, h_ref):
    @pl.when(pl.program_id(1) == 0)
    def _():
        h = _modulated_norm(x_ref[...], g_ref[...], mod_ref[0, SC1:SC1 + 1, :], mod_ref[0, SH1:SH1 + 1, :])
        h_ref[...] = h.astype(BF16)

    o_ref[...] = jnp.dot(h_ref[...], w_ref[...], preferred_element_type=F32).astype(BF16)


def _inproj(x2, mod, g, w_qkv, seq, tm=1024, tn=768):
    n, d = x2.shape
    cols = w_qkv.shape[1]
    blocks_per_batch = seq // tm
    return pl.pallas_call(
        _inproj_kernel,
        grid=(n // tm, cols // tn),
        in_specs=[
            pl.BlockSpec((tm, d), lambda i, j: (i, 0)),
            pl.BlockSpec((1, 6, d), lambda i, j: (i // blocks_per_batch, 0, 0)),
            pl.BlockSpec((1, d), lambda i, j: (0, 0)),
            pl.BlockSpec((d, tn), lambda i, j: (0, j)),
        ],
        out_specs=pl.BlockSpec((tm, tn), lambda i, j: (i, j)),
        out_shape=jax.ShapeDtypeStruct((n, cols), BF16),
        scratch_shapes=[pltpu.VMEM((tm, d), BF16)],
        compiler_params=_params(("arbitrary", "arbitrary")),
        name="in_proj",
    )(x2, mod, g, w_qkv)


def _na_bias_table(rpb):
    heads = rpb.shape[0]
    cols = np.arange(GRID_W)
    col_start = np.clip(cols - NA_WIN_COLS // 2, 0, GRID_W - NA_WIN_COLS)
    col_mask = (cols[None, :] >= col_start[:, None]) & (cols[None, :] < col_start[:, None] + NA_WIN_COLS)
    dc = np.clip(cols[None, :] - cols[:, None], -(NA_WIN_COLS - 1), NA_WIN_COLS - 1) + NA_WIN_COLS - 1
    rpb_cols = rpb[:, :, dc].astype(F32)
    dr = np.arange(NA_WIN_ROWS)[None, :] - np.arange(NA_WIN_ROWS)[:, None] + NA_WIN_ROWS - 1
    t = rpb_cols[:, dr]
    t = t.transpose(0, 1, 3, 2, 4)
    t = jnp.where(col_mask[:, None, :], t, NEG_INF)
    t = t.reshape(heads // HEADS_PER_LANE_TILE, HEADS_PER_LANE_TILE, NA_WIN_ROWS, GRID_W, NA_WIN_ROWS * GRID_W)
    return t.transpose(0, 2, 1, 3, 4).reshape(heads // HEADS_PER_LANE_TILE, NA_WIN_ROWS,
                                              HEADS_PER_LANE_TILE * GRID_W, NA_WIN_ROWS * GRID_W)


def _stack_heads(q, low):
    scaled = q * (HEAD_DIM ** -0.5)
    zero = jnp.zeros_like(scaled)
    return jnp.concatenate([jnp.where(low, scaled, zero), jnp.where(low, zero, scaled)], axis=0)


def _stacked_attention(items):
    scores = [lax.dot_general(q2, kw, (((1,), (1,)), ((), ())), preferred_element_type=F32) + bias
              for q2, kw, _, bias in items]
    probs = []
    for s in scores:
        m = jnp.max(s, axis=-1, keepdims=True)
        p = jnp.exp(s - m)
        probs.append((p.astype(BF16), m, jnp.sum(p, axis=-1, keepdims=True)))
    return [(jnp.dot(p, vw, preferred_element_type=F32) / z, m, z)
            for (p, m, z), (_, _, vw, _) in zip(probs, items)]


def _unstack_heads(a, low):
    half = a.shape[0] // HEADS_PER_LANE_TILE
    return jnp.where(low, a[:half], a[half:])


def _na_kernel(q_ref, k_ref, v_ref, bias_ref, o_ref, *, rows, rows_per_step):
    kr = NA_WIN_ROWS
    low = lax.broadcasted_iota(jnp.int32, (GRID_W, LANES), 1) < HEAD_DIM

    def body(i, carry):
        items, qrows = [], []
        for u in range(rows_per_step):
            r = i * rows_per_step + u
            rs = jnp.clip(r - kr // 2, 0, rows - kr)
            qrows.append(pl.ds(pl.multiple_of(r * GRID_W, GRID_W), GRID_W))
            wrows = pl.ds(pl.multiple_of(rs * GRID_W, GRID_W), kr * GRID_W)
            items.append((_stack_heads(q_ref[qrows[-1], :], low), k_ref[wrows, :], v_ref[wrows, :],
                          bias_ref[0, r - rs]))
        for rows_u, (o, _, _) in zip(qrows, _stacked_attention(items)):
            o_ref[rows_u, :] = _unstack_heads(o, low).astype(o_ref.dtype)
        return carry

    lax.fori_loop(0, rows // rows_per_step, body, 0)


def _neighbourhood_attention(qkv, bias, batch, seq, rows_per_step=8):
    n = qkv.shape[0]
    rows = seq // GRID_W
    pairs = NA_WIDTH // LANES
    return pl.pallas_call(
        functools.partial(_na_kernel, rows=rows, rows_per_step=rows_per_step),
        grid=(pairs, batch),
        in_specs=[
            pl.BlockSpec((seq, LANES), lambda p, b: (b, p)),
            pl.BlockSpec((seq, LANES), lambda p, b: (b, pairs + p)),
            pl.BlockSpec((seq, LANES), lambda p, b: (b, 2 * pairs + p)),
            pl.BlockSpec((1, NA_WIN_ROWS, HEADS_PER_LANE_TILE * GRID_W, NA_WIN_ROWS * GRID_W),
                         lambda p, b: (p, 0, 0, 0)),
        ],
        out_specs=pl.BlockSpec((seq, LANES), lambda p, b: (b, p)),
        out_shape=jax.ShapeDtypeStruct((n, NA_WIDTH), BF16),
        compiler_params=_params(("arbitrary", "arbitrary")),
        name="na_attn",
    )(qkv, qkv, qkv, bias)


def _alibi_slopes():
    n = N_DIL_GROUPS * DIL_HEADS_PER_GROUP
    s = np.exp2(-ALIBI_MAX * np.arange(1, n + 1, dtype=np.float64) / n).astype(np.float32)
    return s.reshape(N_DIL_GROUPS, DIL_HEADS_PER_GROUP)


def _dil_bias_table(group):
    blk = DIL_BLOCK
    dilation = DIL_GROUPS[group][1]
    slopes = _alibi_slopes()[group]
    qi = np.arange(blk)[:, None]
    kj = np.arange(3 * blk)[None, :]
    tables = []
    for shift in range(3):
        arel = np.abs(kj - qi - shift * blk)
        dist = (dilation * arel).astype(np.float32)
        per_head = [np.where(arel <= blk, -slopes[h] * dist, np.float32(NEG_INF)) for h in range(DIL_HEADS_PER_GROUP)]
        tables.append(np.stack(per_head))
    t = np.stack(tables, axis=1).astype(np.float32)
    pairs = DIL_HEADS_PER_GROUP // HEADS_PER_LANE_TILE
    t = t.reshape(pairs, HEADS_PER_LANE_TILE, 3, blk, 3 * blk).transpose(0, 2, 1, 3, 4)
    return t.reshape(pairs, 3, HEADS_PER_LANE_TILE * blk, 3 * blk)


def _dil_kernel(q_ref, k_ref, v_ref, bias_ref, o_ref, lse_ref, *, nb, blocks_per_step):
    blk = DIL_BLOCK
    win = 3 * blk
    low = lax.broadcasted_iota(jnp.int32, (blk, LANES), 1) < HEAD_DIM
    steps = q_ref.shape[0] * nb

    def body(i, carry):
        items, dst = [], []
        for u in range(blocks_per_step):
            t = i * blocks_per_step + u
            sq = t // nb
            n = t % nb
            wb = jnp.clip(n - 1, 0, nb - 3)
            qrows = pl.ds(pl.multiple_of(n * blk, blk), blk)
            wrows = pl.ds(pl.multiple_of(wb * blk, blk), win)
            dst.append((sq, qrows))
            items.append((_stack_heads(q_ref[sq, qrows, :], low), k_ref[sq, wrows, :], v_ref[sq, wrows, :],
                          bias_ref[0, n - wb]))
        for (sq, qrows), (o, m, z) in zip(dst, _stacked_attention(items)):
            o_ref[sq, qrows, :] = _unstack_heads(o, low)
            lse_ref[sq, qrows, :] = _unstack_heads(jnp.broadcast_to(m + jnp.log(z), o.shape), low)
        return carry

    lax.fori_loop(0, steps // blocks_per_step, body, 0)


def _dilated_attention(qkv3, col_offsets, group, tokens_per_step=4096, blocks_per_step=8):
    nseq, length, _ = qkv3.shape
    nb = length // DIL_BLOCK
    sb = max(1, tokens_per_step // length)
    pairs = DIL_OUT_WIDTH // LANES
    qo, ko, vo = col_offsets
    bias = jnp.asarray(_dil_bias_table(group))
    out = jax.ShapeDtypeStruct((nseq, length, DIL_OUT_WIDTH), F32)
    return pl.pallas_call(
        functools.partial(_dil_kernel, nb=nb, blocks_per_step=blocks_per_step),
        grid=(pairs, nseq // sb),
        in_specs=[
            pl.BlockSpec((sb, length, LANES), lambda p, b: (b, 0, qo + p)),
            pl.BlockSpec((sb, length, LANES), lambda p, b: (b, 0, ko + p)),
            pl.BlockSpec((sb, length, LANES), lambda p, b: (b, 0, vo + p)),
            pl.BlockSpec((1,) + bias.shape[1:], lambda p, b: (p, 0, 0, 0)),
        ],
        out_specs=[pl.BlockSpec((sb, length, LANES), lambda p, b: (b, 0, p))] * 2,
        out_shape=[out, out],
        compiler_params=_params(("arbitrary", "arbitrary")),
        name=f"dil_attn_g{group}",
    )(qkv3, qkv3, qkv3, bias)


def _to_residue(a, batch, dilation):
    n, c = a.shape
    length = n // batch // dilation
    return a.reshape(batch, length, dilation, c).transpose(0, 2, 1, 3).reshape(batch * dilation, length, c)


def _from_residue(a, batch, dilation):
    nseq, length, c = a.shape
    return a.reshape(batch, dilation, length, c).transpose(0, 2, 1, 3).reshape(batch * dilation * length, c)


def _outproj_kernel(x_ref, oa_ref, o0_ref, o1_ref, o2_ref, l0_ref, l1_ref, l2_ref, mod_ref, g_ref,
                    wg_ref, bg_ref, wpa_ref, wpb_ref, wo_ref, out_ref):
    d = x_ref.shape[1]
    x = x_ref[...]
    h = _modulated_norm(x, g_ref[...], mod_ref[0, SC1:SC1 + 1, :], mod_ref[0, SH1:SH1 + 1, :]).astype(BF16)

    lses = [l0_ref[...], l1_ref[...], l2_ref[...]]
    outs = [o0_ref[...], o1_ref[...], o2_ref[...]]
    top = jnp.maximum(jnp.maximum(lses[0], lses[1]), lses[2])
    es = [jnp.exp(l - top) for l in lses]
    den = es[0] + es[1] + es[2]
    ob = (es[0] * outs[0] + es[1] * outs[1] + es[2] * outs[2]) / den

    ya = jnp.dot(oa_ref[...], wpa_ref[...], preferred_element_type=F32)
    yb = jnp.dot(ob.astype(BF16), wpb_ref[...], preferred_element_type=F32)
    ga = jax.nn.sigmoid(jnp.dot(h, wg_ref[:, :d], preferred_element_type=F32) + bg_ref[:, :d])
    mix = ga * ya
    gb = jax.nn.sigmoid(jnp.dot(h, wg_ref[:, d:], preferred_element_type=F32) + bg_ref[:, d:])
    mix = mix + gb * yb
    y = jnp.dot(mix.astype(BF16), wo_ref[...], preferred_element_type=F32)
    out_ref[...] = x + mod_ref[0, GT1:GT1 + 1, :] * y


def _outproj(x2, oa, o_groups, lse_groups, mod, g, w_gate, b_gate, w_pa, w_pb, w_o, seq, tm=512):
    n, d = x2.shape
    blocks_per_batch = seq // tm
    row = lambda c: pl.BlockSpec((tm, c), lambda i: (i, 0))
    full = lambda a: pl.BlockSpec(a.shape, lambda i: (0,) * a.ndim)
    return pl.pallas_call(
        _outproj_kernel,
        grid=(n // tm,),
        in_specs=[row(d), row(NA_WIDTH)] + [row(DIL_OUT_WIDTH)] * 6 + [
            pl.BlockSpec((1, 6, d), lambda i: (i // blocks_per_batch, 0, 0)),
            full(g), full(w_gate), full(b_gate), full(w_pa), full(w_pb), full(w_o),
        ],
        out_specs=row(d),
        out_shape=jax.ShapeDtypeStruct((n, d), F32),
        compiler_params=_params(("arbitrary",)),
        name="out_proj",
    )(x2, oa, *o_groups, *lse_groups, mod, g, w_gate, b_gate, w_pa, w_pb, w_o)


def _first_index_of_max(cur, idx, size):
    m = jnp.max(cur, axis=0, keepdims=True)
    first = jnp.min(jnp.where(cur == m, idx, size), axis=0, keepdims=True)
    return m, first


def _route_transposed(logits_t, e_bias):
    tokens = logits_t.shape[1]
    per_group = N_EXPERTS // N_EXPERT_GROUPS
    scores = jax.nn.sigmoid(logits_t)
    biased = scores + e_bias
    midx = lax.broadcasted_iota(jnp.int32, (per_group, tokens), 0)
    grp_scores = []
    for g in range(N_EXPERT_GROUPS):
        vals = biased[g * per_group:(g + 1) * per_group, :]
        m1, first = _first_index_of_max(vals, midx, per_group)
        m2 = jnp.max(jnp.where(midx == first, -jnp.inf, vals), axis=0, keepdims=True)
        grp_scores.append(m1 + m2)
    cur = jnp.concatenate(grp_scores, axis=0)
    gidx = lax.broadcasted_iota(jnp.int32, (N_EXPERT_GROUPS, tokens), 0)
    grp_sel = jnp.zeros((N_EXPERT_GROUPS, tokens), jnp.bool_)
    for _ in range(TOP_GROUPS):
        _, first = _first_index_of_max(cur, gidx, N_EXPERT_GROUPS)
        pick = gidx == first
        grp_sel = jnp.logical_or(grp_sel, pick)
        cur = jnp.where(pick, -jnp.inf, cur)
    rows = []
    for g in range(N_EXPERT_GROUPS):
        vals = biased[g * per_group:(g + 1) * per_group, :]
        rows.append(jnp.where(grp_sel[g:g + 1, :], vals, NEG_INF))
    cur = jnp.concatenate(rows, axis=0)
    eidx = lax.broadcasted_iota(jnp.int32, (N_EXPERTS, tokens), 0)
    firsts, picks, weights = [], [], []
    for _ in range(TOP_K):
        _, first = _first_index_of_max(cur, eidx, N_EXPERTS)
        pick = eidx == first
        firsts.append(first)
        picks.append(pick)
        weights.append(jnp.sum(jnp.where(pick, scores, 0.0), axis=0, keepdims=True))
        cur = jnp.where(pick, -jnp.inf, cur)
    total = functools.reduce(lambda a, b: a + b, weights)
    return firsts, picks, [w / total * ROUTED_SCALE for w in weights]


def _pack_halves(a):
    half = a.shape[1] // 2
    bits = lax.bitcast_convert_type(a.astype(BF16).astype(F32), jnp.int32)
    return lax.shift_right_logical(bits[:, :half], 16) | bits[:, half:]


def _unpack_halves(w):
    low = lax.bitcast_convert_type(lax.shift_left(w, 16), F32)
    high = lax.bitcast_convert_type(w & jnp.int32(-65536), F32)
    return low, high


def _router_kernel(x_ref, mod_ref, g_ref, wr_ref, eb_ref, tri_ref, hp_ref, eidx_ref, rank_ref, w_ref, cnt_ref):
    tm = x_ref.shape[0]

    @pl.when(pl.program_id(0) == 0)
    def _():
        cnt_ref[...] = jnp.zeros_like(cnt_ref)

    h = _modulated_norm(x_ref[...], g_ref[...], mod_ref[0, SC2:SC2 + 1, :], mod_ref[0, SH2:SH2 + 1, :])
    hp_ref[...] = _pack_halves(h)
    logits_t = lax.dot_general(wr_ref[...], h, (((1,), (1,)), ((), ())), preferred_element_type=F32,
                               precision=lax.Precision.HIGHEST)
    firsts, picks, weights = _route_transposed(logits_t, eb_ref[...])
    sel = functools.reduce(jnp.logical_or, picks)
    sel_f = jnp.where(sel, 1.0, 0.0)
    incl = jnp.dot(sel_f.astype(BF16), tri_ref[...], preferred_element_type=F32)
    before = cnt_ref[:, 0:1] + incl - sel_f
    eidx_ref[...] = jnp.concatenate(firsts, axis=0)
    rank_ref[...] = jnp.concatenate(
        [jnp.sum(jnp.where(p, before, 0.0), axis=0, keepdims=True) for p in picks], axis=0).astype(jnp.int32)
    pad = jnp.concatenate(weights + [jnp.zeros((LANES - TOP_K, tm), F32)], axis=0)
    w_ref[...] = pad.T
    cnt_ref[...] = cnt_ref[...] + incl[:, tm - 1:tm]


def _router(x2, mod, g, w_router_t, e_bias, seq, tm=512):
    n, d = x2.shape
    blocks_per_batch = seq // tm
    tri = jnp.asarray(np.triu(np.ones((tm, tm), np.float32)), BF16)
    tok = lambda r: pl.BlockSpec((r, tm), lambda i: (0, i))
    const = lambda a: pl.BlockSpec(a.shape, lambda i: (0,) * a.ndim)
    return pl.pallas_call(
        _router_kernel,
        grid=(n // tm,),
        in_specs=[
            pl.BlockSpec((tm, d), lambda i: (i, 0)),
            pl.BlockSpec((1, 6, d), lambda i: (i // blocks_per_batch, 0, 0)),
            const(g), const(w_router_t), const(e_bias), const(tri),
        ],
        out_specs=[
            pl.BlockSpec((tm, d // 2), lambda i: (i, 0)),
            tok(TOP_K), tok(TOP_K),
            pl.BlockSpec((tm, LANES), lambda i: (i, 0)),
            pl.BlockSpec((N_EXPERTS, LANES), lambda i: (0, 0)),
        ],
        out_shape=[
            jax.ShapeDtypeStruct((n, d // 2), jnp.int32),
            jax.ShapeDtypeStruct((TOP_K, n), jnp.int32),
            jax.ShapeDtypeStruct((TOP_K, n), jnp.int32),
            jax.ShapeDtypeStruct((n, LANES), F32),
            jax.ShapeDtypeStruct((N_EXPERTS, LANES), F32),
        ],
        compiler_params=_params(("arbitrary",)),
        name="moe_router",
    )(x2, mod, g, w_router_t, e_bias, tri)


def _sc_worker_id():
    return lax.axis_index("subcore") * SC_CORES + lax.axis_index("core")


def _sc_scatter_rows(src, idx3, n_out):
    n, w = src.shape
    per_worker = n // SC_CHUNK // SC_WORKERS
    mesh = plsc.VectorSubcoreMesh(core_axis_name="core", subcore_axis_name="subcore")

    @functools.partial(
        pl.kernel, mesh=mesh, out_type=jax.ShapeDtypeStruct((n_out, w), src.dtype), name="moe_dispatch",
        scratch_types=[pltpu.VMEM((TOP_K, SC_CHUNK), jnp.int32), pltpu.VMEM((SC_CHUNK, w), src.dtype),
                       pltpu.SemaphoreType.DMA])
    def scatter(src_hbm, idx_hbm, out_hbm, idx_v, rows_v, sem):
        first = _sc_worker_id() * per_worker

        @pl.loop(0, per_worker)
        def _(i):
            chunk = first + i
            pltpu.sync_copy(idx_hbm.at[chunk], idx_v)
            pltpu.sync_copy(src_hbm.at[pl.ds(chunk * SC_CHUNK, SC_CHUNK)], rows_v)
            copies = [pltpu.make_async_copy(rows_v, out_hbm.at[idx_v.at[k]], sem) for k in range(TOP_K)]
            for cp in copies:
                cp.start()
            for cp in copies:
                cp.wait()

    return scatter(src, idx3)


def _sc_gather_rows(src, idx3):
    _, w = src.shape
    chunks = idx3.shape[0]
    per_worker = chunks // SC_WORKERS
    mesh = plsc.VectorSubcoreMesh(core_axis_name="core", subcore_axis_name="subcore")

    @functools.partial(
        pl.kernel, mesh=mesh, out_type=jax.ShapeDtypeStruct((TOP_K, chunks * SC_CHUNK, w), src.dtype),
        name="moe_collect",
        scratch_types=[pltpu.VMEM((TOP_K, SC_CHUNK), jnp.int32), pltpu.VMEM((SC_CHUNK, w), src.dtype),
                       pltpu.SemaphoreType.DMA])
    def gather(src_hbm, idx_hbm, out_hbm, idx_v, rows_v, sem):
        first = _sc_worker_id() * per_worker

        @pl.loop(0, per_worker)
        def _(i):
            chunk = first + i
            pltpu.sync_copy(idx_hbm.at[chunk], idx_v)
            for k in range(TOP_K):
                pltpu.async_copy(src_hbm.at[idx_v.at[k]], rows_v, sem).wait()
                pltpu.sync_copy(rows_v, out_hbm.at[k, pl.ds(chunk * SC_CHUNK, SC_CHUNK)])

    return gather(src, idx3)


def _swiglu(x, w_gate_up, w_down):
    gu = jnp.dot(x, w_gate_up, preferred_element_type=F32)
    gate = gu[:, :EXPERT_DIM]
    act = (gate * jax.nn.sigmoid(gate)) * gu[:, EXPERT_DIM:]
    return jnp.dot(act.astype(BF16), w_down, preferred_element_type=F32)


def _expert_ffn_kernel(be_ref, nv_ref, xs_ref, wg_ref, wu_ref, wd_ref, ys_ref, wgu_bf, wd_bf):
    i = pl.program_id(0)
    nvalid = nv_ref[i]

    @pl.when(jnp.logical_or(i == 0, be_ref[i] != be_ref[jnp.maximum(i - 1, 0)]))
    def _():
        wgu_bf[:, :EXPERT_DIM] = wg_ref[0].astype(BF16)
        wgu_bf[:, EXPERT_DIM:] = wu_ref[0].astype(BF16)
        wd_bf[...] = wd_ref[0].astype(BF16)

    @pl.when(nvalid > 0)
    def _():
        packed = xs_ref[...]
        row = lax.broadcasted_iota(jnp.int32, packed.shape, 0)
        low, high = _unpack_halves(jnp.where(row < nvalid, packed, 0))
        x = jnp.concatenate([low, high], axis=1).astype(BF16)
        ys_ref[...] = _pack_halves(_swiglu(x, wgu_bf[...], wd_bf[...]))

    @pl.when(nvalid == 0)
    def _():
        ys_ref[...] = jnp.zeros_like(ys_ref)


def _expert_ffn(xs, block_expert, block_valid, w_gate, w_up, w_down, tb):
    p, half = xs.shape
    d = 2 * half
    grid_spec = pltpu.PrefetchScalarGridSpec(
        num_scalar_prefetch=2,
        grid=(p // tb,),
        in_specs=[
            pl.BlockSpec((tb, half), lambda i, be, nv: (i, 0)),
            pl.BlockSpec((1, d, EXPERT_DIM), lambda i, be, nv: (be[i], 0, 0)),
            pl.BlockSpec((1, d, EXPERT_DIM), lambda i, be, nv: (be[i], 0, 0)),
            pl.BlockSpec((1, EXPERT_DIM, d), lambda i, be, nv: (be[i], 0, 0)),
        ],
        out_specs=pl.BlockSpec((tb, half), lambda i, be, nv: (i, 0)),
        scratch_shapes=[pltpu.VMEM((d, 2 * EXPERT_DIM), BF16), pltpu.VMEM((EXPERT_DIM, d), BF16)],
    )
    return pl.pallas_call(
        _expert_ffn_kernel,
        grid_spec=grid_spec,
        out_shape=jax.ShapeDtypeStruct((p, half), jnp.int32),
        compiler_params=_params(("arbitrary",)),
        name="moe_expert_ffn",
    )(block_expert, block_valid, xs, w_gate, w_up, w_down)


def _combine_kernel(x_ref, hp_ref, yg_ref, w_ref, mod_ref, wgu_ref, wd_ref, gf_ref, o_ref, *, final_norm):
    low, high = _unpack_halves(hp_ref[...])
    h = jnp.concatenate([low, high], axis=1).astype(BF16)
    shared = _swiglu(h, wgu_ref[...], wd_ref[...])
    half = hp_ref.shape[1]
    acc_low, acc_high = shared[:, :half], shared[:, half:]
    w = w_ref[...]
    for k in range(TOP_K):
        low, high = _unpack_halves(yg_ref[k])
        wk = w[:, k:k + 1]
        acc_low = acc_low + wk * low
        acc_high = acc_high + wk * high
    y = jnp.concatenate([acc_low, acc_high], axis=1)
    out = x_ref[...] + mod_ref[0, GT2:GT2 + 1, :] * y
    if final_norm:
        out = out * lax.rsqrt(jnp.mean(out * out, axis=-1, keepdims=True) + EPS) * gf_ref[...]
    o_ref[...] = out


def _combine(x2, hp, yg, w, mod, ws_gate_up, ws_down, g_final, seq, final_norm, tm=512):
    n, d = x2.shape
    blocks_per_batch = seq // tm
    const = lambda a: pl.BlockSpec(a.shape, lambda i: (0,) * a.ndim)
    return pl.pallas_call(
        functools.partial(_combine_kernel, final_norm=final_norm),
        grid=(n // tm,),
        in_specs=[
            pl.BlockSpec((tm, d), lambda i: (i, 0)),
            pl.BlockSpec((tm, d // 2), lambda i: (i, 0)),
            pl.BlockSpec((TOP_K, tm, d // 2), lambda i: (0, i, 0)),
            pl.BlockSpec((tm, LANES), lambda i: (i, 0)),
            pl.BlockSpec((1, 6, d), lambda i: (i // blocks_per_batch, 0, 0)),
            const(ws_gate_up), const(ws_down), const(g_final),
        ],
        out_specs=pl.BlockSpec((tm, d), lambda i: (i, 0)),
        out_shape=jax.ShapeDtypeStruct((n, d), F32),
        compiler_params=_params(("arbitrary",)),
        name="moe_combine",
    )(x2, hp, yg, w, mod, ws_gate_up, ws_down, g_final)


def _token_mixer(x2, mod, g_mix, w_in, b_gate, rpb, w_pa, w_pb, w_o, batch, seq):
    d = x2.shape[1]
    qkv = _inproj(x2, mod, g_mix.reshape(1, d), w_in[:, :QKV_COLS].astype(BF16), seq)
    o_a = _neighbourhood_attention(qkv, _na_bias_table(rpb), batch, seq)

    dil0 = 3 * NA_WIDTH
    o_groups, lse_groups = [], []
    for grp, (_, dilation) in enumerate(DIL_GROUPS):
        cols = [dil0 + part * DIL_WIDTH + grp * DIL_OUT_WIDTH for part in range(3)]
        if dilation == 1:
            seqs = qkv.reshape(batch, seq, QKV_COLS)
            offsets = tuple(c // LANES for c in cols)
        else:
            parts = jnp.concatenate([qkv[:, c:c + DIL_OUT_WIDTH] for c in cols], axis=1)
            seqs = _to_residue(parts, batch, dilation)
            offsets = tuple(part * DIL_OUT_WIDTH // LANES for part in range(3))
        o_g, lse_g = _dilated_attention(seqs, offsets, grp)
        o_groups.append(_from_residue(o_g, batch, dilation))
        lse_groups.append(_from_residue(lse_g, batch, dilation))

    return _outproj(x2, o_a, o_groups, lse_groups, mod, g_mix.reshape(1, d),
                    w_in[:, QKV_COLS:].astype(BF16), b_gate.reshape(1, -1),
                    w_pa.astype(BF16), w_pb.astype(BF16), w_o.astype(BF16), seq)


def _dispatch_plan(eidx_t, rank_t, counts, tb):
    n = eidx_t.shape[1]
    n_blocks = -(-(n * TOP_K + N_EXPERTS * (tb - 1)) // tb)
    padded = (counts + tb - 1) // tb * tb
    seg_end = jnp.cumsum(padded)
    seg_start = seg_end - padded
    experts = jnp.arange(N_EXPERTS, dtype=jnp.int32)

    def lookup(table, idx):
        sel = idx[None] == experts.reshape((N_EXPERTS,) + (1,) * idx.ndim)
        return jnp.sum(jnp.where(sel, table.reshape((N_EXPERTS,) + (1,) * idx.ndim), 0), axis=0)

    dest_t = lookup(seg_start, eidx_t) + rank_t
    idx3 = dest_t.reshape(TOP_K, n // SC_CHUNK, SC_CHUNK).transpose(1, 0, 2)
    block_start = jnp.arange(n_blocks, dtype=jnp.int32) * tb
    block_expert = jnp.sum((seg_end[:, None] <= block_start[None, :]).astype(jnp.int32), axis=0)
    block_expert = jnp.minimum(block_expert, N_EXPERTS - 1)
    block_valid = jnp.clip(lookup(counts, block_expert) - (block_start - lookup(seg_start, block_expert)), 0, tb)
    return idx3, block_expert, block_valid.astype(jnp.int32), n_blocks


def _moe_layer(x2, mod, g_ffn, w_router, e_bias, we_gate, we_up, we_down, ws_gate, ws_up, ws_down, g_final, seq,
               final_norm, tb=EXPERT_ROW_BLOCK):
    d = x2.shape[1]
    hp, eidx_t, rank_t, w, cnt = _router(x2, mod, g_ffn.reshape(1, d), w_router.T, e_bias.reshape(-1, 1), seq)
    counts = cnt[:, 0].astype(jnp.int32)
    idx3, block_expert, block_valid, n_blocks = _dispatch_plan(eidx_t, rank_t, counts, tb)
    xs = _sc_scatter_rows(hp, idx3, n_blocks * tb)
    ys = _expert_ffn(xs, block_expert, block_valid, we_gate, we_up, we_down, tb)
    yg = _sc_gather_rows(ys, idx3)
    return _combine(x2, hp, yg, w, mod, jnp.concatenate([ws_gate, ws_up], axis=-1).astype(BF16),
                    ws_down.astype(BF16), g_final.reshape(1, d), seq, final_norm)


def kernel(x, c, w_ada, b_ada, g_mix, w_in, b_gate, rpb, w_pa, w_pb, w_o, g_ffn, w_router, e_bias,
           we_gate, we_up, we_down, ws_gate, ws_up, ws_down, g_final):
    batch, seq, d = x.shape
    depth = w_ada.shape[0]
    mods = _ada(c, w_ada, b_ada).reshape(depth, batch, 6, d)
    x2 = x.reshape(batch * seq, d)
    for l in range(depth):
        x2 = _token_mixer(x2, mods[l], g_mix[l], w_in[l], b_gate[l], rpb[l], w_pa[l], w_pb[l], w_o[l],
                          batch, seq)
        x2 = _moe_layer(x2, mods[l], g_ffn[l], w_router[l], e_bias[l], we_gate[l], we_up[l], we_down[l],
                        ws_gate[l], ws_up[l], ws_down[l], g_final, seq, final_norm=(l == depth - 1))
    return x2.reshape(batch, seq, d)
```

```python
import functools

import numpy as np
import jax
import jax.numpy as jnp
from jax import lax
from jax.experimental import pallas as pl
from jax.experimental.pallas import tpu as pltpu
from jax.experimental.pallas import tpu_sc as plsc

HEAD_DIM = 64
GRID_W = 64
NA_HEADS = 8
NA_WIN_ROWS = 8
NA_WIN_COLS = 16
DIL_GROUPS = ((128, 1), (512, 4), (2048, 16))
DIL_HEADS_PER_GROUP = 4
N_DIL_GROUPS = len(DIL_GROUPS)
NA_WIDTH = NA_HEADS * HEAD_DIM
DIL_WIDTH = N_DIL_GROUPS * DIL_HEADS_PER_GROUP * HEAD_DIM
DIL_OUT_WIDTH = DIL_HEADS_PER_GROUP * HEAD_DIM
QKV_COLS = 3 * (NA_WIDTH + DIL_WIDTH)
N_EXPERTS = 64
TOP_K = 8
N_EXPERT_GROUPS = 8
TOP_GROUPS = 4
EXPERT_DIM = 256
ROUTED_SCALE = 2.5
ALIBI_MAX = 8.0
EPS = 1e-6
NEG_INF = -1e30

LANES = 128
HEADS_PER_LANE_TILE = LANES // HEAD_DIM
DIL_BLOCK = 64
VMEM_LIMIT_BYTES = 56 * 1024 * 1024

SC_CORES = 2
SC_SUBCORES = 16
SC_WORKERS = SC_CORES * SC_SUBCORES
SC_CHUNK = 64
EXPERT_ROW_BLOCK = 1024
FFN_SUB_BLOCKS = 4

F32 = jnp.float32
BF16 = jnp.bfloat16

SH1, SC1, GT1, SH2, SC2, GT2 = range(6)


def _params(sem):
    return pltpu.CompilerParams(dimension_semantics=sem, vmem_limit_bytes=VMEM_LIMIT_BYTES)


def _modulated_norm(x, g, scale, shift):
    r = lax.rsqrt(jnp.mean(x * x, axis=-1, keepdims=True) + EPS)
    return (x * r * g) * (1.0 + scale) + shift


def _ada_kernel(c_ref, w_ref, b_ref, o_ref):
    c = c_ref[...]
    act = c * jax.nn.sigmoid(c)
    o_ref[0] = jnp.dot(act, w_ref[0], preferred_element_type=F32,
                       precision=lax.Precision.HIGHEST) + b_ref[0]


def _ada(c, w_ada, b_ada):
    depth, d, six_d = w_ada.shape
    b = c.shape[0]
    tn = d
    return pl.pallas_call(
        _ada_kernel,
        grid=(depth, six_d // tn),
        in_specs=[
            pl.BlockSpec((b, d), lambda l, j: (0, 0)),
            pl.BlockSpec((1, d, tn), lambda l, j: (l, 0, j)),
            pl.BlockSpec((1, 1, tn), lambda l, j: (l, 0, j)),
        ],
        out_specs=pl.BlockSpec((1, b, tn), lambda l, j: (l, 0, j)),
        out_shape=jax.ShapeDtypeStruct((depth, b, six_d), F32),
        compiler_params=_params(("arbitrary", "arbitrary")),
        name="ada_mod",
    )(c, w_ada, b_ada.reshape(depth, 1, six_d))


def _inproj_kernel(x_ref, mod_ref, g_ref, w_ref, o_ref, h_ref):
    @pl.when(pl.program_id(1) == 0)
    def _():
        h = _modulated_norm(x_ref[...], g_ref[...], mod_ref[0, SC1:SC1 + 1, :], mod_ref[0, SH1:SH1 + 1, :])
        h_ref[...] = h.astype(BF16)

    o_ref[...] = jnp.dot(h_ref[...], w_ref[...], preferred_element_type=F32).astype(BF16)


def _inproj(x2, mod, g, w_qkv, seq, tm=1024, tn=768):
    n, d = x2.shape
    cols = w_qkv.shape[1]
    blocks_per_batch = seq // tm
    return pl.pallas_call(
        _inproj_kernel,
        grid=(n // tm, cols // tn),
        in_specs=[
            pl.BlockSpec((tm, d), lambda i, j: (i, 0)),
            pl.BlockSpec((1, 6, d), lambda i, j: (i // blocks_per_batch, 0, 0)),
            pl.BlockSpec((1, d), lambda i, j: (0, 0)),
            pl.BlockSpec((d, tn), lambda i, j: (0, j)),
        ],
        out_specs=pl.BlockSpec((tm, tn), lambda i, j: (i, j)),
        out_shape=jax.ShapeDtypeStruct((n, cols), BF16),
        scratch_shapes=[pltpu.VMEM((tm, d), BF16)],
        compiler_params=_params(("arbitrary", "arbitrary")),
        name="in_proj",
    )(x2, mod, g, w_qkv)


def _na_bias_table(rpb):
    heads = rpb.shape[0]
    cols = np.arange(GRID_W)
    col_start = np.clip(cols - NA_WIN_COLS // 2, 0, GRID_W - NA_WIN_COLS)
    col_mask = (cols[None, :] >= col_start[:, None]) & (cols[None, :] < col_start[:, None] + NA_WIN_COLS)
    dc = np.clip(cols[None, :] - cols[:, None], -(NA_WIN_COLS - 1), NA_WIN_COLS - 1) + NA_WIN_COLS - 1
    rpb_cols = rpb[:, :, dc].astype(F32)
    dr = np.arange(NA_WIN_ROWS)[None, :] - np.arange(NA_WIN_ROWS)[:, None] + NA_WIN_ROWS - 1
    t = rpb_cols[:, dr]
    t = t.transpose(0, 1, 3, 2, 4)
    t = jnp.where(col_mask[:, None, :], t, NEG_INF)
    t = t.reshape(heads // HEADS_PER_LANE_TILE, HEADS_PER_LANE_TILE, NA_WIN_ROWS, GRID_W, NA_WIN_ROWS * GRID_W)
    return t.transpose(0, 2, 1, 3, 4).reshape(heads // HEADS_PER_LANE_TILE, NA_WIN_ROWS,
                                              HEADS_PER_LANE_TILE * GRID_W, NA_WIN_ROWS * GRID_W)


def _stack_heads(q, low):
    scaled = q * (HEAD_DIM ** -0.5)
    zero = jnp.zeros_like(scaled)
    return jnp.concatenate([jnp.where(low, scaled, zero), jnp.where(low, zero, scaled)], axis=0)


def _stacked_attention(items):
    scores = [lax.dot_general(q2, kw, (((1,), (1,)), ((), ())), preferred_element_type=F32) + bias
              for q2, kw, _, bias in items]
    probs = []
    for s in scores:
        m = jnp.max(s, axis=-1, keepdims=True)
        p = jnp.exp(s - m)
        probs.append((p.astype(BF16), m, jnp.sum(p, axis=-1, keepdims=True)))
    return [(jnp.dot(p, vw, preferred_element_type=F32) / z, m, z)
            for (p, m, z), (_, _, vw, _) in zip(probs, items)]


def _unstack_heads(a, low):
    half = a.shape[0] // HEADS_PER_LANE_TILE
    return jnp.where(low, a[:half], a[half:])


def _na_kernel(q_ref, k_ref, v_ref, bias_ref, o_ref, *, rows, rows_per_step):
    kr = NA_WIN_ROWS
    low = lax.broadcasted_iota(jnp.int32, (GRID_W, LANES), 1) < HEAD_DIM

    def body(i, carry):
        items, qrows = [], []
        for u in range(rows_per_step):
            r = i * rows_per_step + u
            rs = jnp.clip(r - kr // 2, 0, rows - kr)
            qrows.append(pl.ds(pl.multiple_of(r * GRID_W, GRID_W), GRID_W))
            wrows = pl.ds(pl.multiple_of(rs * GRID_W, GRID_W), kr * GRID_W)
            items.append((_stack_heads(q_ref[qrows[-1], :], low), k_ref[wrows, :], v_ref[wrows, :],
                          bias_ref[0, r - rs]))
        for rows_u, (o, _, _) in zip(qrows, _stacked_attention(items)):
            o_ref[rows_u, :] = _unstack_heads(o, low).astype(o_ref.dtype)
        return carry

    lax.fori_loop(0, rows // rows_per_step, body, 0)


def _neighbourhood_attention(qkv, bias, batch, seq, rows_per_step=8):
    n = qkv.shape[0]
    rows = seq // GRID_W
    pairs = NA_WIDTH // LANES
    return pl.pallas_call(
        functools.partial(_na_kernel, rows=rows, rows_per_step=rows_per_step),
        grid=(pairs, batch),
        in_specs=[
            pl.BlockSpec((seq, LANES), lambda p, b: (b, p)),
            pl.BlockSpec((seq, LANES), lambda p, b: (b, pairs + p)),
            pl.BlockSpec((seq, LANES), lambda p, b: (b, 2 * pairs + p)),
            pl.BlockSpec((1, NA_WIN_ROWS, HEADS_PER_LANE_TILE * GRID_W, NA_WIN_ROWS * GRID_W),
                         lambda p, b: (p, 0, 0, 0)),
        ],
        out_specs=pl.BlockSpec((seq, LANES), lambda p, b: (b, p)),
        out_shape=jax.ShapeDtypeStruct((n, NA_WIDTH), BF16),
        compiler_params=_params(("arbitrary", "arbitrary")),
        name="na_attn",
    )(qkv, qkv, qkv, bias)


def _alibi_slopes():
    n = N_DIL_GROUPS * DIL_HEADS_PER_GROUP
    s = np.exp2(-ALIBI_MAX * np.arange(1, n + 1, dtype=np.float64) / n).astype(np.float32)
    return s.reshape(N_DIL_GROUPS, DIL_HEADS_PER_GROUP)


def _dil_bias_table(group):
    blk = DIL_BLOCK
    dilation = DIL_GROUPS[group][1]
    slopes = _alibi_slopes()[group]
    qi = np.arange(blk)[:, None]
    kj = np.arange(3 * blk)[None, :]
    tables = []
    for shift in range(3):
        arel = np.abs(kj - qi - shift * blk)
        dist = (dilation * arel).astype(np.float32)
        per_head = [np.where(arel <= blk, -slopes[h] * dist, np.float32(NEG_INF)) for h in range(DIL_HEADS_PER_GROUP)]
        tables.append(np.stack(per_head))
    t = np.stack(tables, axis=1).astype(np.float32)
    pairs = DIL_HEADS_PER_GROUP // HEADS_PER_LANE_TILE
    t = t.reshape(pairs, HEADS_PER_LANE_TILE, 3, blk, 3 * blk).transpose(0, 2, 1, 3, 4)
    return t.reshape(pairs, 3, HEADS_PER_LANE_TILE * blk, 3 * blk)


def _dil_kernel(q_ref, k_ref, v_ref, bias_ref, o_ref, lse_ref, *, nb, blocks_per_step):
    blk = DIL_BLOCK
    win = 3 * blk
    low = lax.broadcasted_iota(jnp.int32, (blk, LANES), 1) < HEAD_DIM
    steps = q_ref.shape[0] * nb

    def body(i, carry):
        items, dst = [], []
        for u in range(blocks_per_step):
            t = i * blocks_per_step + u
            sq = t // nb
            n = t % nb
            wb = jnp.clip(n - 1, 0, nb - 3)
            qrows = pl.ds(pl.multiple_of(n * blk, blk), blk)
            wrows = pl.ds(pl.multiple_of(wb * blk, blk), win)
            dst.append((sq, qrows))
            items.append((_stack_heads(q_ref[sq, qrows, :], low), k_ref[sq, wrows, :], v_ref[sq, wrows, :],
                          bias_ref[0, n - wb]))
        for (sq, qrows), (o, m, z) in zip(dst, _stacked_attention(items)):
            o_ref[sq, qrows, :] = _unstack_heads(o, low)
            lse_ref[sq, qrows, :] = _unstack_heads(jnp.broadcast_to(m + jnp.log(z), o.shape), low)
        return carry

    lax.fori_loop(0, steps // blocks_per_step, body, 0)


def _dilated_attention(qkv3, col_offsets, group, tokens_per_step=4096, blocks_per_step=8):
    nseq, length, _ = qkv3.shape
    nb = length // DIL_BLOCK
    sb = max(1, tokens_per_step // length)
    pairs = DIL_OUT_WIDTH // LANES
    qo, ko, vo = col_offsets
    bias = jnp.asarray(_dil_bias_table(group))
    out = jax.ShapeDtypeStruct((nseq, length, DIL_OUT_WIDTH), F32)
    return pl.pallas_call(
        functools.partial(_dil_kernel, nb=nb, blocks_per_step=blocks_per_step),
        grid=(pairs, nseq // sb),
        in_specs=[
            pl.BlockSpec((sb, length, LANES), lambda p, b: (b, 0, qo + p)),
            pl.BlockSpec((sb, length, LANES), lambda p, b: (b, 0, ko + p)),
            pl.BlockSpec((sb, length, LANES), lambda p, b: (b, 0, vo + p)),
            pl.BlockSpec((1,) + bias.shape[1:], lambda p, b: (p, 0, 0, 0)),
        ],
        out_specs=[pl.BlockSpec((sb, length, LANES), lambda p, b: (b, 0, p))] * 2,
        out_shape=[out, out],
        compiler_params=_params(("arbitrary", "arbitrary")),
        name=f"dil_attn_g{group}",
    )(qkv3, qkv3, qkv3, bias)


def _to_residue(a, batch, dilation):
    n, c = a.shape
    length = n // batch // dilation
    return a.reshape(batch, length, dilation, c).transpose(0, 2, 1, 3).reshape(batch * dilation, length, c)


def _from_residue(a, batch, dilation):
    nseq, length, c = a.shape
    return a.reshape(batch, dilation, length, c).transpose(0, 2, 1, 3).reshape(batch * dilation * length, c)


def _outproj_kernel(x_ref, oa_ref, o0_ref, o1_ref, o2_ref, l0_ref, l1_ref, l2_ref, mod_ref, g_ref,
                    wg_ref, bg_ref, wpa_ref, wpb_ref, wo_ref, out_ref):
    d = x_ref.shape[1]
    x = x_ref[...]
    h = _modulated_norm(x, g_ref[...], mod_ref[0, SC1:SC1 + 1, :], mod_ref[0, SH1:SH1 + 1, :]).astype(BF16)

    lses = [l0_ref[...], l1_ref[...], l2_ref[...]]
    outs = [o0_ref[...], o1_ref[...], o2_ref[...]]
    top = jnp.maximum(jnp.maximum(lses[0], lses[1]), lses[2])
    es = [jnp.exp(l - top) for l in lses]
    den = es[0] + es[1] + es[2]
    ob = (es[0] * outs[0] + es[1] * outs[1] + es[2] * outs[2]) / den

    ya = jnp.dot(oa_ref[...], wpa_ref[...], preferred_element_type=F32)
    yb = jnp.dot(ob.astype(BF16), wpb_ref[...], preferred_element_type=F32)
    ga = jax.nn.sigmoid(jnp.dot(h, wg_ref[:, :d], preferred_element_type=F32) + bg_ref[:, :d])
    mix = ga * ya
    gb = jax.nn.sigmoid(jnp.dot(h, wg_ref[:, d:], preferred_element_type=F32) + bg_ref[:, d:])
    mix = mix + gb * yb
    y = jnp.dot(mix.astype(BF16), wo_ref[...], preferred_element_type=F32)
    out_ref[...] = x + mod_ref[0, GT1:GT1 + 1, :] * y


def _outproj(x2, oa, o_groups, lse_groups, mod, g, w_gate, b_gate, w_pa, w_pb, w_o, seq, tm=512):
    n, d = x2.shape
    blocks_per_batch = seq // tm
    row = lambda c: pl.BlockSpec((tm, c), lambda i: (i, 0))
    full = lambda a: pl.BlockSpec(a.shape, lambda i: (0,) * a.ndim)
    return pl.pallas_call(
        _outproj_kernel,
        grid=(n // tm,),
        in_specs=[row(d), row(NA_WIDTH)] + [row(DIL_OUT_WIDTH)] * 6 + [
            pl.BlockSpec((1, 6, d), lambda i: (i // blocks_per_batch, 0, 0)),
            full(g), full(w_gate), full(b_gate), full(w_pa), full(w_pb), full(w_o),
        ],
        out_specs=row(d),
        out_shape=jax.ShapeDtypeStruct((n, d), F32),
        compiler_params=_params(("arbitrary",)),
        name="out_proj",
    )(x2, oa, *o_groups, *lse_groups, mod, g, w_gate, b_gate, w_pa, w_pb, w_o)


def _first_index_of_max(cur, idx, size):
    m = jnp.max(cur, axis=0, keepdims=True)
    first = jnp.min(jnp.where(cur == m, idx, size), axis=0, keepdims=True)
    return m, first


def _route_transposed(logits_t, e_bias):
    tokens = logits_t.shape[1]
    per_group = N_EXPERTS // N_EXPERT_GROUPS
    scores = jax.nn.sigmoid(logits_t)
    biased = scores + e_bias
    midx = lax.broadcasted_iota(jnp.int32, (per_group, tokens), 0)
    grp_scores = []
    for g in range(N_EXPERT_GROUPS):
        vals = biased[g * per_group:(g + 1) * per_group, :]
        m1, first = _first_index_of_max(vals, midx, per_group)
        m2 = jnp.max(jnp.where(midx == first, -jnp.inf, vals), axis=0, keepdims=True)
        grp_scores.append(m1 + m2)
    cur = jnp.concatenate(grp_scores, axis=0)
    gidx = lax.broadcasted_iota(jnp.int32, (N_EXPERT_GROUPS, tokens), 0)
    grp_sel = jnp.zeros((N_EXPERT_GROUPS, tokens), jnp.bool_)
    for _ in range(TOP_GROUPS):
        _, first = _first_index_of_max(cur, gidx, N_EXPERT_GROUPS)
        pick = gidx == first
        grp_sel = jnp.logical_or(grp_sel, pick)
        cur = jnp.where(pick, -jnp.inf, cur)
    rows = []
    for g in range(N_EXPERT_GROUPS):
        vals = biased[g * per_group:(g + 1) * per_group, :]
        rows.append(jnp.where(grp_sel[g:g + 1, :], vals, NEG_INF))
    cur = jnp.concatenate(rows, axis=0)
    eidx = lax.broadcasted_iota(jnp.int32, (N_EXPERTS, tokens), 0)
    firsts, picks, weights = [], [], []
    for _ in range(TOP_K):
        _, first = _first_index_of_max(cur, eidx, N_EXPERTS)
        pick = eidx == first
        firsts.append(first)
        picks.append(pick)
        weights.append(jnp.sum(jnp.where(pick, scores, 0.0), axis=0, keepdims=True))
        cur = jnp.where(pick, -jnp.inf, cur)
    total = functools.reduce(lambda a, b: a + b, weights)
    return firsts, picks, [w / total * ROUTED_SCALE for w in weights]


def _pack_halves(a):
    half = a.shape[1] // 2
    bits = lax.bitcast_convert_type(a.astype(BF16).astype(F32), jnp.int32)
    return lax.shift_right_logical(bits[:, :half], 16) | bits[:, half:]


def _unpack_halves(w):
    low = lax.bitcast_convert_type(lax.shift_left(w, 16), F32)
    high = lax.bitcast_convert_type(w & jnp.int32(-65536), F32)
    return low, high


def _router_kernel(x_ref, mod_ref, g_ref, wr_ref, eb_ref, tri_ref, hp_ref, eidx_ref, rank_ref, w_ref, cnt_ref):
    tm = x_ref.shape[0]

    @pl.when(pl.program_id(0) == 0)
    def _():
        cnt_ref[...] = jnp.zeros_like(cnt_ref)

    h = _modulated_norm(x_ref[...], g_ref[...], mod_ref[0, SC2:SC2 + 1, :], mod_ref[0, SH2:SH2 + 1, :])
    hp_ref[...] = _pack_halves(h)
    logits_t = lax.dot_general(wr_ref[...], h, (((1,), (1,)), ((), ())), preferred_element_type=F32,
                               precision=lax.Precision.HIGHEST)
    firsts, picks, weights = _route_transposed(logits_t, eb_ref[...])
    sel = functools.reduce(jnp.logical_or, picks)
    sel_f = jnp.where(sel, 1.0, 0.0)
    incl = jnp.dot(sel_f.astype(BF16), tri_ref[...], preferred_element_type=F32)
    before = cnt_ref[:, 0:1] + incl - sel_f
    eidx_ref[...] = jnp.concatenate(firsts, axis=0)
    rank_ref[...] = jnp.concatenate(
        [jnp.sum(jnp.where(p, before, 0.0), axis=0, keepdims=True) for p in picks], axis=0).astype(jnp.int32)
    pad = jnp.concatenate(weights + [jnp.zeros((LANES - TOP_K, tm), F32)], axis=0)
    w_ref[...] = pad.T
    cnt_ref[...] = cnt_ref[...] + incl[:, tm - 1:tm]


def _router(x2, mod, g, w_router_t, e_bias, seq, tm=512):
    n, d = x2.shape
    blocks_per_batch = seq // tm
    tri = jnp.asarray(np.triu(np.ones((tm, tm), np.float32)), BF16)
    tok = lambda r: pl.BlockSpec((r, tm), lambda i: (0, i))
    const = lambda a: pl.BlockSpec(a.shape, lambda i: (0,) * a.ndim)
    return pl.pallas_call(
        _router_kernel,
        grid=(n // tm,),
        in_specs=[
            pl.BlockSpec((tm, d), lambda i: (i, 0)),
            pl.BlockSpec((1, 6, d), lambda i: (i // blocks_per_batch, 0, 0)),
            const(g), const(w_router_t), const(e_bias), const(tri),
        ],
        out_specs=[
            pl.BlockSpec((tm, d // 2), lambda i: (i, 0)),
            tok(TOP_K), tok(TOP_K),
            pl.BlockSpec((tm, LANES), lambda i: (i, 0)),
            pl.BlockSpec((N_EXPERTS, LANES), lambda i: (0, 0)),
        ],
        out_shape=[
            jax.ShapeDtypeStruct((n, d // 2), jnp.int32),
            jax.ShapeDtypeStruct((TOP_K, n), jnp.int32),
            jax.ShapeDtypeStruct((TOP_K, n), jnp.int32),
            jax.ShapeDtypeStruct((n, LANES), F32),
            jax.ShapeDtypeStruct((N_EXPERTS, LANES), F32),
        ],
        compiler_params=_params(("arbitrary",)),
        name="moe_router",
    )(x2, mod, g, w_router_t, e_bias, tri)


def _sc_worker_id():
    return lax.axis_index("subcore") * SC_CORES + lax.axis_index("core")


def _sc_scatter_rows(src, idx3, n_out):
    n, w = src.shape
    per_worker = n // SC_CHUNK // SC_WORKERS
    mesh = plsc.VectorSubcoreMesh(core_axis_name="core", subcore_axis_name="subcore")

    @functools.partial(
        pl.kernel, mesh=mesh, out_type=jax.ShapeDtypeStruct((n_out, w), src.dtype), name="moe_dispatch",
        scratch_types=[pltpu.VMEM((2, TOP_K, SC_CHUNK), jnp.int32), pltpu.VMEM((2, SC_CHUNK, w), src.dtype),
                       pltpu.SemaphoreType.DMA((2,)), pltpu.SemaphoreType.DMA((2,)), pltpu.SemaphoreType.DMA])
    def scatter(src_hbm, idx_hbm, out_hbm, idx_v, rows_v, idx_sem, row_sem, out_sem):
        first = _sc_worker_id() * per_worker

        def loads(chunk, slot):
            return (pltpu.make_async_copy(idx_hbm.at[chunk], idx_v.at[slot], idx_sem.at[slot]),
                    pltpu.make_async_copy(src_hbm.at[pl.ds(chunk * SC_CHUNK, SC_CHUNK)], rows_v.at[slot],
                                          row_sem.at[slot]))

        for cp in loads(first, 0):
            cp.start()

        @pl.loop(0, per_worker, step=2)
        def _(i):
            for slot in range(2):
                chunk = first + i + slot
                for cp in loads(chunk, slot):
                    cp.wait()

                @pl.when(i + slot + 1 < per_worker)
                def _():
                    for cp in loads(chunk + 1, 1 - slot):
                        cp.start()

                copies = [pltpu.make_async_copy(rows_v.at[slot], out_hbm.at[idx_v.at[slot, k]], out_sem)
                          for k in range(TOP_K)]
                for cp in copies:
                    cp.start()
                for cp in copies:
                    cp.wait()

    return scatter(src, idx3)


def _sc_gather_rows(src, idx3):
    _, w = src.shape
    chunks = idx3.shape[0]
    per_worker = chunks // SC_WORKERS
    mesh = plsc.VectorSubcoreMesh(core_axis_name="core", subcore_axis_name="subcore")

    @functools.partial(
        pl.kernel, mesh=mesh, out_type=jax.ShapeDtypeStruct((TOP_K, chunks * SC_CHUNK, w), src.dtype),
        name="moe_collect",
        scratch_types=[pltpu.VMEM((TOP_K, SC_CHUNK), jnp.int32), pltpu.VMEM((2, SC_CHUNK, w), src.dtype),
                       pltpu.SemaphoreType.DMA((2,)), pltpu.SemaphoreType.DMA((2,))])
    def gather(src_hbm, idx_hbm, out_hbm, idx_v, rows_v, in_sem, out_sem):
        first = _sc_worker_id() * per_worker

        @pl.loop(0, per_worker)
        def _(i):
            chunk = first + i
            pltpu.sync_copy(idx_hbm.at[chunk], idx_v)
            reads = [pltpu.make_async_copy(src_hbm.at[idx_v.at[k]], rows_v.at[k % 2], in_sem.at[k % 2])
                     for k in range(TOP_K)]
            writes = [pltpu.make_async_copy(rows_v.at[k % 2], out_hbm.at[k, pl.ds(chunk * SC_CHUNK, SC_CHUNK)],
                                            out_sem.at[k % 2]) for k in range(TOP_K)]
            reads[0].start()
            for k in range(TOP_K):
                if k + 1 < TOP_K:
                    if k >= 1:
                        writes[k - 1].wait()
                    reads[k + 1].start()
                reads[k].wait()
                writes[k].start()
            writes[TOP_K - 2].wait()
            writes[TOP_K - 1].wait()

    return gather(src, idx3)


def _swiglu(x, w_gate_up, w_down):
    gu = jnp.dot(x, w_gate_up, preferred_element_type=F32)
    gate = gu[:, :EXPERT_DIM]
    act = (gate * jax.nn.sigmoid(gate)) * gu[:, EXPERT_DIM:]
    return jnp.dot(act.astype(BF16), w_down, preferred_element_type=F32)


def _expert_ffn_kernel(be_ref, nv_ref, xs_ref, wg_ref, wu_ref, wd_ref, ys_ref, wgu_bf, wd_bf):
    i = pl.program_id(0)
    nvalid = nv_ref[i]

    @pl.when(jnp.logical_or(i == 0, be_ref[i] != be_ref[jnp.maximum(i - 1, 0)]))
    def _():
        wgu_bf[:, :EXPERT_DIM] = wg_ref[0, 0].astype(BF16)
        wgu_bf[:, EXPERT_DIM:] = wu_ref[0, 0].astype(BF16)
        wd_bf[...] = wd_ref[0, 0].astype(BF16)

    @pl.when(nvalid > 0)
    def _():
        tb = xs_ref.shape[0]
        sub = tb // FFN_SUB_BLOCKS
        row = lax.broadcasted_iota(jnp.int32, (sub, xs_ref.shape[1]), 0)
        spans = [pl.ds(s * sub, sub) for s in range(FFN_SUB_BLOCKS)]
        xs = []
        for s, span in enumerate(spans):
            low, high = _unpack_halves(jnp.where(row < nvalid - s * sub, xs_ref[span, :], 0))
            xs.append(jnp.concatenate([low, high], axis=1).astype(BF16))
        gus = [jnp.dot(x, wgu_bf[...], preferred_element_type=F32) for x in xs]
        acts = [((gu[:, :EXPERT_DIM] * jax.nn.sigmoid(gu[:, :EXPERT_DIM])) * gu[:, EXPERT_DIM:]).astype(BF16)
                for gu in gus]
        ys = [jnp.dot(a, wd_bf[...], preferred_element_type=F32) for a in acts]
        for span, y in zip(spans, ys):
            ys_ref[span, :] = _pack_halves(y)

    @pl.when(nvalid == 0)
    def _():
        ys_ref[...] = jnp.zeros_like(ys_ref)


def _expert_ffn(xs, block_expert, block_valid, w_gate, w_up, w_down, layer, tb):
    p, half = xs.shape
    d = 2 * half
    grid_spec = pltpu.PrefetchScalarGridSpec(
        num_scalar_prefetch=2,
        grid=(p // tb,),
        in_specs=[
            pl.BlockSpec((tb, half), lambda i, be, nv: (i, 0)),
            pl.BlockSpec((1, 1, d, EXPERT_DIM), lambda i, be, nv: (layer, be[i], 0, 0)),
            pl.BlockSpec((1, 1, d, EXPERT_DIM), lambda i, be, nv: (layer, be[i], 0, 0)),
            pl.BlockSpec((1, 1, EXPERT_DIM, d), lambda i, be, nv: (layer, be[i], 0, 0)),
        ],
        out_specs=pl.BlockSpec((tb, half), lambda i, be, nv: (i, 0)),
        scratch_shapes=[pltpu.VMEM((d, 2 * EXPERT_DIM), BF16), pltpu.VMEM((EXPERT_DIM, d), BF16)],
    )
    return pl.pallas_call(
        _expert_ffn_kernel,
        grid_spec=grid_spec,
        out_shape=jax.ShapeDtypeStruct((p, half), jnp.int32),
        compiler_params=_params(("arbitrary",)),
        name="moe_expert_ffn",
    )(block_expert, block_valid, xs, w_gate, w_up, w_down)


def _combine_kernel(x_ref, hp_ref, yg_ref, w_ref, mod_ref, wgu_ref, wd_ref, gf_ref, o_ref, *, final_norm):
    low, high = _unpack_halves(hp_ref[...])
    h = jnp.concatenate([low, high], axis=1).astype(BF16)
    shared = _swiglu(h, wgu_ref[...], wd_ref[...])
    half = hp_ref.shape[1]
    acc_low, acc_high = shared[:, :half], shared[:, half:]
    w = w_ref[...]
    for k in range(TOP_K):
        low, high = _unpack_halves(yg_ref[k])
        wk = w[:, k:k + 1]
        acc_low = acc_low + wk * low
        acc_high = acc_high + wk * high
    y = jnp.concatenate([acc_low, acc_high], axis=1)
    out = x_ref[...] + mod_ref[0, GT2:GT2 + 1, :] * y
    if final_norm:
        out = out * lax.rsqrt(jnp.mean(out * out, axis=-1, keepdims=True) + EPS) * gf_ref[...]
    o_ref[...] = out


def _combine(x2, hp, yg, w, mod, ws_gate_up, ws_down, g_final, seq, final_norm, tm=512):
    n, d = x2.shape
    blocks_per_batch = seq // tm
    const = lambda a: pl.BlockSpec(a.shape, lambda i: (0,) * a.ndim)
    return pl.pallas_call(
        functools.partial(_combine_kernel, final_norm=final_norm),
        grid=(n // tm,),
        in_specs=[
            pl.BlockSpec((tm, d), lambda i: (i, 0)),
            pl.BlockSpec((tm, d // 2), lambda i: (i, 0)),
            pl.BlockSpec((TOP_K, tm, d // 2), lambda i: (0, i, 0)),
            pl.BlockSpec((tm, LANES), lambda i: (i, 0)),
            pl.BlockSpec((1, 6, d), lambda i: (i // blocks_per_batch, 0, 0)),
            const(ws_gate_up), const(ws_down), const(g_final),
        ],
        out_specs=pl.BlockSpec((tm, d), lambda i: (i, 0)),
        out_shape=jax.ShapeDtypeStruct((n, d), F32),
        compiler_params=_params(("arbitrary",)),
        name="moe_combine",
    )(x2, hp, yg, w, mod, ws_gate_up, ws_down, g_final)


def _token_mixer(x2, mod, g_mix, w_in, b_gate, rpb, w_pa, w_pb, w_o, batch, seq):
    d = x2.shape[1]
    qkv = _inproj(x2, mod, g_mix.reshape(1, d), w_in[:, :QKV_COLS].astype(BF16), seq)
    o_a = _neighbourhood_attention(qkv, _na_bias_table(rpb), batch, seq)

    dil0 = 3 * NA_WIDTH
    o_groups, lse_groups = [], []
    for grp, (_, dilation) in enumerate(DIL_GROUPS):
        cols = [dil0 + part * DIL_WIDTH + grp * DIL_OUT_WIDTH for part in range(3)]
        if dilation == 1:
            seqs = qkv.reshape(batch, seq, QKV_COLS)
            offsets = tuple(c // LANES for c in cols)
        else:
            parts = jnp.concatenate([qkv[:, c:c + DIL_OUT_WIDTH] for c in cols], axis=1)
            seqs = _to_residue(parts, batch, dilation)
            offsets = tuple(part * DIL_OUT_WIDTH // LANES for part in range(3))
        o_g, lse_g = _dilated_attention(seqs, offsets, grp)
        o_groups.append(_from_residue(o_g, batch, dilation))
        lse_groups.append(_from_residue(lse_g, batch, dilation))

    return _outproj(x2, o_a, o_groups, lse_groups, mod, g_mix.reshape(1, d),
                    w_in[:, QKV_COLS:].astype(BF16), b_gate.reshape(1, -1),
                    w_pa.astype(BF16), w_pb.astype(BF16), w_o.astype(BF16), seq)


def _dispatch_plan(eidx_t, rank_t, counts, tb):
    n = eidx_t.shape[1]
    n_blocks = -(-(n * TOP_K + N_EXPERTS * (tb - 1)) // tb)
    padded = (counts + tb - 1) // tb * tb
    seg_end = jnp.cumsum(padded)
    seg_start = seg_end - padded
    experts = jnp.arange(N_EXPERTS, dtype=jnp.int32)

    def lookup(table, idx):
        sel = idx[None] == experts.reshape((N_EXPERTS,) + (1,) * idx.ndim)
        return jnp.sum(jnp.where(sel, table.reshape((N_EXPERTS,) + (1,) * idx.ndim), 0), axis=0)

    dest_t = lookup(seg_start, eidx_t) + rank_t
    idx3 = dest_t.reshape(TOP_K, n // SC_CHUNK, SC_CHUNK).transpose(1, 0, 2)
    block_start = jnp.arange(n_blocks, dtype=jnp.int32) * tb
    block_expert = jnp.sum((seg_end[:, None] <= block_start[None, :]).astype(jnp.int32), axis=0)
    block_expert = jnp.minimum(block_expert, N_EXPERTS - 1)
    block_valid = jnp.clip(lookup(counts, block_expert) - (block_start - lookup(seg_start, block_expert)), 0, tb)
    return idx3, block_expert, block_valid.astype(jnp.int32), n_blocks


def _moe_layer(x2, mod, g_ffn, w_router, e_bias, we_gate, we_up, we_down, layer, ws_gate, ws_up, ws_down, g_final,
               seq, final_norm, tb=EXPERT_ROW_BLOCK):
    d = x2.shape[1]
    hp, eidx_t, rank_t, w, cnt = _router(x2, mod, g_ffn.reshape(1, d), w_router.T, e_bias.reshape(-1, 1), seq)
    counts = cnt[:, 0].astype(jnp.int32)
    idx3, block_expert, block_valid, n_blocks = _dispatch_plan(eidx_t, rank_t, counts, tb)
    xs = _sc_scatter_rows(hp, idx3, n_blocks * tb)
    ys = _expert_ffn(xs, block_expert, block_valid, we_gate, we_up, we_down, layer, tb)
    yg = _sc_gather_rows(ys, idx3)
    return _combine(x2, hp, yg, w, mod, jnp.concatenate([ws_gate, ws_up], axis=-1).astype(BF16),
                    ws_down.astype(BF16), g_final.reshape(1, d), seq, final_norm)


def kernel(x, c, w_ada, b_ada, g_mix, w_in, b_gate, rpb, w_pa, w_pb, w_o, g_ffn, w_router, e_bias,
           we_gate, we_up, we_down, ws_gate, ws_up, ws_down, g_final):
    batch, seq, d = x.shape
    depth = w_ada.shape[0]
    mods = _ada(c, w_ada, b_ada).reshape(depth, batch, 6, d)
    x2 = x.reshape(batch * seq, d)
    for l in range(depth):
        x2 = _token_mixer(x2, mods[l], g_mix[l], w_in[l], b_gate[l], rpb[l], w_pa[l], w_pb[l], w_o[l],
                          batch, seq)
        x2 = _moe_layer(x2, mods[l], g_ffn[l], w_router[l], e_bias[l], we_gate, we_up, we_down, l,
                        ws_gate[l], ws_up[l], ws_down[l], g_final, seq, final_norm=(l == depth - 1))
    return x2.reshape(batch, seq, d)
```

```python
import functools

import numpy as np
import jax
import jax.numpy as jnp
from jax import lax
from jax.experimental import pallas as pl
from jax.experimental.pallas import tpu as pltpu
from jax.experimental.pallas import tpu_sc as plsc

HEAD_DIM = 64
GRID_W = 64
NA_HEADS = 8
NA_WIN_ROWS = 8
NA_WIN_COLS = 16
DIL_GROUPS = ((128, 1), (512, 4), (2048, 16))
DIL_HEADS_PER_GROUP = 4
N_DIL_GROUPS = len(DIL_GROUPS)
NA_WIDTH = NA_HEADS * HEAD_DIM
DIL_WIDTH = N_DIL_GROUPS * DIL_HEADS_PER_GROUP * HEAD_DIM
DIL_OUT_WIDTH = DIL_HEADS_PER_GROUP * HEAD_DIM
QKV_COLS = 3 * (NA_WIDTH + DIL_WIDTH)
N_EXPERTS = 64
TOP_K = 8
N_EXPERT_GROUPS = 8
TOP_GROUPS = 4
EXPERT_DIM = 256
ROUTED_SCALE = 2.5
ALIBI_MAX = 8.0
EPS = 1e-6
NEG_INF = -1e30

LANES = 128
HEADS_PER_LANE_TILE = LANES // HEAD_DIM
DIL_BLOCK = 64
VMEM_LIMIT_BYTES = 56 * 1024 * 1024

SC_CORES = 2
SC_SUBCORES = 16
SC_WORKERS = SC_CORES * SC_SUBCORES
SC_CHUNK = 64
EXPERT_ROW_BLOCK = 1024
FFN_SUB_BLOCKS = 4

F32 = jnp.float32
BF16 = jnp.bfloat16

SH1, SC1, GT1, SH2, SC2, GT2 = range(6)


def _params(sem):
    return pltpu.CompilerParams(dimension_semantics=sem, vmem_limit_bytes=VMEM_LIMIT_BYTES)


def _modulated_norm(x, g, scale, shift):
    r = lax.rsqrt(jnp.mean(x * x, axis=-1, keepdims=True) + EPS)
    return (x * r * g) * (1.0 + scale) + shift


def _ada_kernel(c_ref, w_ref, b_ref, o_ref):
    c = c_ref[...]
    act = c * jax.nn.sigmoid(c)
    o_ref[0] = jnp.dot(act, w_ref[0], preferred_element_type=F32,
                       precision=lax.Precision.HIGHEST) + b_ref[0]


def _ada(c, w_ada, b_ada):
    depth, d, six_d = w_ada.shape
    b = c.shape[0]
    tn = d
    return pl.pallas_call(
        _ada_kernel,
        grid=(depth, six_d // tn),
        in_specs=[
            pl.BlockSpec((b, d), lambda l, j: (0, 0)),
            pl.BlockSpec((1, d, tn), lambda l, j: (l, 0, j)),
            pl.BlockSpec((1, 1, tn), lambda l, j: (l, 0, j)),
        ],
        out_specs=pl.BlockSpec((1, b, tn), lambda l, j: (l, 0, j)),
        out_shape=jax.ShapeDtypeStruct((depth, b, six_d), F32),
        compiler_params=_params(("arbitrary", "arbitrary")),
        name="ada_mod",
    )(c, w_ada, b_ada.reshape(depth, 1, six_d))


def _inproj_kernel(x_ref, mod_ref, g_ref, w_ref, tok_ref, *rest, tok_blocks, dilations):
    res_refs, (h_ref, acc_ref) = rest[:len(dilations)], rest[len(dilations):]
    j = pl.program_id(1)
    tm = x_ref.shape[0]

    @pl.when(j == 0)
    def _():
        h = _modulated_norm(x_ref[...], g_ref[...], mod_ref[0, SC1:SC1 + 1, :], mod_ref[0, SH1:SH1 + 1, :])
        h_ref[...] = h.astype(BF16)

    res = jnp.dot(h_ref[...], w_ref[...], preferred_element_type=F32)

    @pl.when(j < tok_blocks)
    def _():
        tok_ref[...] = res.astype(BF16)

    for g, (res_ref, dilation) in enumerate(zip(res_refs, dilations)):
        @pl.when(j == tok_blocks + g)
        def _():
            for c in range(acc_ref.shape[0]):
                acc_ref[c] = res[:, c * LANES:(c + 1) * LANES]
            for r in range(dilation):
                for c in range(acc_ref.shape[0]):
                    res_ref[0, r, :, c * LANES:(c + 1) * LANES] = (
                        acc_ref[c, pl.ds(r, tm // dilation, stride=dilation), :].astype(BF16))


def _inproj(x2, mod, g, w_qkv, batch, seq, tm=1024):
    n, d = x2.shape
    tn = 3 * DIL_OUT_WIDTH
    dilations = tuple(dil for _, dil in DIL_GROUPS if dil > 1)
    tok_cols = w_qkv.shape[1] - tn * len(dilations)
    tok_blocks = tok_cols // tn
    blocks_per_batch = seq // tm
    res_specs = [pl.BlockSpec((1, dil, tm // dil, tn),
                              lambda i, j: (i // blocks_per_batch, 0, i % blocks_per_batch, 0)) for dil in dilations]
    res_shapes = [jax.ShapeDtypeStruct((batch, dil, seq // dil, tn), BF16) for dil in dilations]
    return pl.pallas_call(
        functools.partial(_inproj_kernel, tok_blocks=tok_blocks, dilations=dilations),
        grid=(n // tm, tok_blocks + len(dilations)),
        in_specs=[
            pl.BlockSpec((tm, d), lambda i, j: (i, 0)),
            pl.BlockSpec((1, 6, d), lambda i, j: (i // blocks_per_batch, 0, 0)),
            pl.BlockSpec((1, d), lambda i, j: (0, 0)),
            pl.BlockSpec((d, tn), lambda i, j: (0, j)),
        ],
        out_specs=[pl.BlockSpec((tm, tn), lambda i, j: (i, jnp.minimum(j, tok_blocks - 1)))] + res_specs,
        out_shape=[jax.ShapeDtypeStruct((n, tok_cols), BF16)] + res_shapes,
        scratch_shapes=[pltpu.VMEM((tm, d), BF16), pltpu.VMEM((tn // LANES, tm, LANES), F32)],
        compiler_params=_params(("arbitrary", "arbitrary")),
        name="in_proj",
    )(x2, mod, g, w_qkv)


def _na_bias_table(rpb):
    heads = rpb.shape[0]
    cols = np.arange(GRID_W)
    col_start = np.clip(cols - NA_WIN_COLS // 2, 0, GRID_W - NA_WIN_COLS)
    col_mask = (cols[None, :] >= col_start[:, None]) & (cols[None, :] < col_start[:, None] + NA_WIN_COLS)
    dc = np.clip(cols[None, :] - cols[:, None], -(NA_WIN_COLS - 1), NA_WIN_COLS - 1) + NA_WIN_COLS - 1
    rpb_cols = rpb[:, :, dc].astype(F32)
    dr = np.arange(NA_WIN_ROWS)[None, :] - np.arange(NA_WIN_ROWS)[:, None] + NA_WIN_ROWS - 1
    t = rpb_cols[:, dr]
    t = t.transpose(0, 1, 3, 2, 4)
    t = jnp.where(col_mask[:, None, :], t, NEG_INF)
    t = t.reshape(heads // HEADS_PER_LANE_TILE, HEADS_PER_LANE_TILE, NA_WIN_ROWS, GRID_W, NA_WIN_ROWS * GRID_W)
    return t.transpose(0, 2, 1, 3, 4).reshape(heads // HEADS_PER_LANE_TILE, NA_WIN_ROWS,
                                              HEADS_PER_LANE_TILE * GRID_W, NA_WIN_ROWS * GRID_W)


def _stack_heads(q, low):
    scaled = q * (HEAD_DIM ** -0.5)
    zero = jnp.zeros_like(scaled)
    return jnp.concatenate([jnp.where(low, scaled, zero), jnp.where(low, zero, scaled)], axis=0)


def _stacked_attention(items):
    scores = [lax.dot_general(q2, kw, (((1,), (1,)), ((), ())), preferred_element_type=F32) + bias
              for q2, kw, _, bias in items]
    probs = []
    for s in scores:
        m = jnp.max(s, axis=-1, keepdims=True)
        p = jnp.exp(s - m)
        probs.append((p.astype(BF16), m, jnp.sum(p, axis=-1, keepdims=True)))
    return [(jnp.dot(p, vw, preferred_element_type=F32) / z, m, z)
            for (p, m, z), (_, _, vw, _) in zip(probs, items)]


def _unstack_heads(a, low):
    half = a.shape[0] // HEADS_PER_LANE_TILE
    return jnp.where(low, a[:half], a[half:])


def _na_kernel(q_ref, k_ref, v_ref, bias_ref, o_ref, *, rows, rows_per_step):
    kr = NA_WIN_ROWS
    low = lax.broadcasted_iota(jnp.int32, (GRID_W, LANES), 1) < HEAD_DIM

    def body(i, carry):
        items, qrows = [], []
        for u in range(rows_per_step):
            r = i * rows_per_step + u
            rs = jnp.clip(r - kr // 2, 0, rows - kr)
            qrows.append(pl.ds(pl.multiple_of(r * GRID_W, GRID_W), GRID_W))
            wrows = pl.ds(pl.multiple_of(rs * GRID_W, GRID_W), kr * GRID_W)
            items.append((_stack_heads(q_ref[qrows[-1], :], low), k_ref[wrows, :], v_ref[wrows, :],
                          bias_ref[0, r - rs]))
        for rows_u, (o, _, _) in zip(qrows, _stacked_attention(items)):
            o_ref[rows_u, :] = _unstack_heads(o, low).astype(o_ref.dtype)
        return carry

    lax.fori_loop(0, rows // rows_per_step, body, 0)


def _neighbourhood_attention(qkv, bias, batch, seq, rows_per_step=8):
    n = qkv.shape[0]
    rows = seq // GRID_W
    pairs = NA_WIDTH // LANES
    return pl.pallas_call(
        functools.partial(_na_kernel, rows=rows, rows_per_step=rows_per_step),
        grid=(pairs, batch),
        in_specs=[
            pl.BlockSpec((seq, LANES), lambda p, b: (b, p)),
            pl.BlockSpec((seq, LANES), lambda p, b: (b, pairs + p)),
            pl.BlockSpec((seq, LANES), lambda p, b: (b, 2 * pairs + p)),
            pl.BlockSpec((1, NA_WIN_ROWS, HEADS_PER_LANE_TILE * GRID_W, NA_WIN_ROWS * GRID_W),
                         lambda p, b: (p, 0, 0, 0)),
        ],
        out_specs=pl.BlockSpec((seq, LANES), lambda p, b: (b, p)),
        out_shape=jax.ShapeDtypeStruct((n, NA_WIDTH), BF16),
        compiler_params=_params(("arbitrary", "arbitrary")),
        name="na_attn",
    )(qkv, qkv, qkv, bias)


def _alibi_slopes():
    n = N_DIL_GROUPS * DIL_HEADS_PER_GROUP
    s = np.exp2(-ALIBI_MAX * np.arange(1, n + 1, dtype=np.float64) / n).astype(np.float32)
    return s.reshape(N_DIL_GROUPS, DIL_HEADS_PER_GROUP)


def _dil_bias_table(group):
    blk = DIL_BLOCK
    dilation = DIL_GROUPS[group][1]
    slopes = _alibi_slopes()[group]
    qi = np.arange(blk)[:, None]
    kj = np.arange(3 * blk)[None, :]
    tables = []
    for shift in range(3):
        arel = np.abs(kj - qi - shift * blk)
        dist = (dilation * arel).astype(np.float32)
        per_head = [np.where(arel <= blk, -slopes[h] * dist, np.float32(NEG_INF)) for h in range(DIL_HEADS_PER_GROUP)]
        tables.append(np.stack(per_head))
    t = np.stack(tables, axis=1).astype(np.float32)
    pairs = DIL_HEADS_PER_GROUP // HEADS_PER_LANE_TILE
    t = t.reshape(pairs, HEADS_PER_LANE_TILE, 3, blk, 3 * blk).transpose(0, 2, 1, 3, 4)
    return t.reshape(pairs, 3, HEADS_PER_LANE_TILE * blk, 3 * blk)


def _dil_kernel(q_ref, k_ref, v_ref, bias_ref, o_ref, lse_ref, *, nb, blocks_per_step):
    blk = DIL_BLOCK
    win = 3 * blk
    low = lax.broadcasted_iota(jnp.int32, (blk, LANES), 1) < HEAD_DIM
    dilation = q_ref.shape[0]
    steps = dilation * nb

    def token_rows(sq, n):
        if dilation == 1:
            return pl.ds(pl.multiple_of(n * blk, blk), blk)
        return pl.ds(n * (blk * dilation) + sq, blk, stride=dilation)

    def body(i, carry):
        items, dst = [], []
        for u in range(blocks_per_step):
            t = i * blocks_per_step + u
            sq = t // nb
            n = t % nb
            wb = jnp.clip(n - 1, 0, nb - 3)
            qrows = pl.ds(pl.multiple_of(n * blk, blk), blk)
            wrows = pl.ds(pl.multiple_of(wb * blk, blk), win)
            dst.append(token_rows(sq, n))
            items.append((_stack_heads(q_ref[sq, qrows, :], low), k_ref[sq, wrows, :], v_ref[sq, wrows, :],
                          bias_ref[0, n - wb]))
        for rows, (o, m, z) in zip(dst, _stacked_attention(items)):
            o_ref[rows, :] = _unstack_heads(o, low)
            lse_ref[rows, :] = _unstack_heads(jnp.broadcast_to(m + jnp.log(z), o.shape), low)
        return carry

    lax.fori_loop(0, steps // blocks_per_step, body, 0)


def _dilated_attention(qkv4, col_offsets, group, blocks_per_step=8):
    batch, dilation, length, _ = qkv4.shape
    seq = dilation * length
    nb = length // DIL_BLOCK
    pairs = DIL_OUT_WIDTH // LANES
    qo, ko, vo = col_offsets
    bias = jnp.asarray(_dil_bias_table(group))
    out = jax.ShapeDtypeStruct((batch * seq, DIL_OUT_WIDTH), F32)
    seqs = lambda off: pl.BlockSpec((None, dilation, length, LANES), lambda p, b: (b, 0, 0, off + p))
    return pl.pallas_call(
        functools.partial(_dil_kernel, nb=nb, blocks_per_step=blocks_per_step),
        grid=(pairs, batch),
        in_specs=[seqs(qo), seqs(ko), seqs(vo),
                  pl.BlockSpec((1,) + bias.shape[1:], lambda p, b: (p, 0, 0, 0))],
        out_specs=[pl.BlockSpec((seq, LANES), lambda p, b: (b, p))] * 2,
        out_shape=[out, out],
        compiler_params=_params(("arbitrary", "arbitrary")),
        name=f"dil_attn_g{group}",
    )(qkv4, qkv4, qkv4, bias)


def _outproj_kernel(x_ref, oa_ref, o0_ref, o1_ref, o2_ref, l0_ref, l1_ref, l2_ref, mod_ref, g_ref,
                    wg_ref, bg_ref, wpa_ref, wpb_ref, wo_ref, out_ref):
    d = x_ref.shape[1]
    x = x_ref[...]
    h = _modulated_norm(x, g_ref[...], mod_ref[0, SC1:SC1 + 1, :], mod_ref[0, SH1:SH1 + 1, :]).astype(BF16)

    lses = [l0_ref[...], l1_ref[...], l2_ref[...]]
    outs = [o0_ref[...], o1_ref[...], o2_ref[...]]
    top = jnp.maximum(jnp.maximum(lses[0], lses[1]), lses[2])
    es = [jnp.exp(l - top) for l in lses]
    den = es[0] + es[1] + es[2]
    ob = (es[0] * outs[0] + es[1] * outs[1] + es[2] * outs[2]) / den

    ya = jnp.dot(oa_ref[...], wpa_ref[...], preferred_element_type=F32)
    yb = jnp.dot(ob.astype(BF16), wpb_ref[...], preferred_element_type=F32)
    ga = jax.nn.sigmoid(jnp.dot(h, wg_ref[:, :d], preferred_element_type=F32) + bg_ref[:, :d])
    mix = ga * ya
    gb = jax.nn.sigmoid(jnp.dot(h, wg_ref[:, d:], preferred_element_type=F32) + bg_ref[:, d:])
    mix = mix + gb * yb
    y = jnp.dot(mix.astype(BF16), wo_ref[...], preferred_element_type=F32)
    out_ref[...] = x + mod_ref[0, GT1:GT1 + 1, :] * y


def _outproj(x2, oa, o_groups, lse_groups, mod, g, w_gate, b_gate, w_pa, w_pb, w_o, seq, tm=512):
    n, d = x2.shape
    blocks_per_batch = seq // tm
    row = lambda c: pl.BlockSpec((tm, c), lambda i: (i, 0))
    full = lambda a: pl.BlockSpec(a.shape, lambda i: (0,) * a.ndim)
    return pl.pallas_call(
        _outproj_kernel,
        grid=(n // tm,),
        in_specs=[row(d), row(NA_WIDTH)] + [row(DIL_OUT_WIDTH)] * 6 + [
            pl.BlockSpec((1, 6, d), lambda i: (i // blocks_per_batch, 0, 0)),
            full(g), full(w_gate), full(b_gate), full(w_pa), full(w_pb), full(w_o),
        ],
        out_specs=row(d),
        out_shape=jax.ShapeDtypeStruct((n, d), F32),
        compiler_params=_params(("arbitrary",)),
        name="out_proj",
    )(x2, oa, *o_groups, *lse_groups, mod, g, w_gate, b_gate, w_pa, w_pb, w_o)


def _first_index_of_max(cur, idx, size):
    m = jnp.max(cur, axis=0, keepdims=True)
    first = jnp.min(jnp.where(cur == m, idx, size), axis=0, keepdims=True)
    return m, first


def _route_transposed(logits_t, e_bias):
    tokens = logits_t.shape[1]
    per_group = N_EXPERTS // N_EXPERT_GROUPS
    scores = jax.nn.sigmoid(logits_t)
    biased = scores + e_bias
    midx = lax.broadcasted_iota(jnp.int32, (per_group, tokens), 0)
    grp_scores = []
    for g in range(N_EXPERT_GROUPS):
        vals = biased[g * per_group:(g + 1) * per_group, :]
        m1, first = _first_index_of_max(vals, midx, per_group)
        m2 = jnp.max(jnp.where(midx == first, -jnp.inf, vals), axis=0, keepdims=True)
        grp_scores.append(m1 + m2)
    cur = jnp.concatenate(grp_scores, axis=0)
    gidx = lax.broadcasted_iota(jnp.int32, (N_EXPERT_GROUPS, tokens), 0)
    grp_sel = jnp.zeros((N_EXPERT_GROUPS, tokens), jnp.bool_)
    for _ in range(TOP_GROUPS):
        _, first = _first_index_of_max(cur, gidx, N_EXPERT_GROUPS)
        pick = gidx == first
        grp_sel = jnp.logical_or(grp_sel, pick)
        cur = jnp.where(pick, -jnp.inf, cur)
    rows = []
    for g in range(N_EXPERT_GROUPS):
        vals = biased[g * per_group:(g + 1) * per_group, :]
        rows.append(jnp.where(grp_sel[g:g + 1, :], vals, NEG_INF))
    cur = jnp.concatenate(rows, axis=0)
    eidx = lax.broadcasted_iota(jnp.int32, (N_EXPERTS, tokens), 0)
    firsts, picks, weights = [], [], []
    for _ in range(TOP_K):
        _, first = _first_index_of_max(cur, eidx, N_EXPERTS)
        pick = eidx == first
        firsts.append(first)
        picks.append(pick)
        weights.append(jnp.sum(jnp.where(pick, scores, 0.0), axis=0, keepdims=True))
        cur = jnp.where(pick, -jnp.inf, cur)
    total = functools.reduce(lambda a, b: a + b, weights)
    return firsts, picks, [w / total * ROUTED_SCALE for w in weights]


def _pack_halves(a):
    half = a.shape[1] // 2
    bits = lax.bitcast_convert_type(a.astype(BF16).astype(F32), jnp.int32)
    return lax.shift_right_logical(bits[:, :half], 16) | bits[:, half:]


def _unpack_halves(w):
    low = lax.bitcast_convert_type(lax.shift_left(w, 16), F32)
    high = lax.bitcast_convert_type(w & jnp.int32(-65536), F32)
    return low, high


def _router_kernel(x_ref, mod_ref, g_ref, wr_ref, eb_ref, tri_ref, hp_ref, eidx_ref, rank_ref, w_ref, cnt_ref):
    tm = x_ref.shape[0]

    @pl.when(pl.program_id(0) == 0)
    def _():
        cnt_ref[...] = jnp.zeros_like(cnt_ref)

    h = _modulated_norm(x_ref[...], g_ref[...], mod_ref[0, SC2:SC2 + 1, :], mod_ref[0, SH2:SH2 + 1, :])
    hp_ref[...] = _pack_halves(h)
    logits_t = lax.dot_general(wr_ref[...], h, (((1,), (1,)), ((), ())), preferred_element_type=F32,
                               precision=lax.Precision.HIGHEST)
    firsts, picks, weights = _route_transposed(logits_t, eb_ref[...])
    sel = functools.reduce(jnp.logical_or, picks)
    sel_f = jnp.where(sel, 1.0, 0.0)
    incl = jnp.dot(sel_f.astype(BF16), tri_ref[...], preferred_element_type=F32)
    before = cnt_ref[:, 0:1] + incl - sel_f
    eidx_ref[...] = jnp.concatenate(firsts, axis=0)
    rank_ref[...] = jnp.concatenate(
        [jnp.sum(jnp.where(p, before, 0.0), axis=0, keepdims=True) for p in picks], axis=0).astype(jnp.int32)
    pad = jnp.concatenate(weights + [jnp.zeros((LANES - TOP_K, tm), F32)], axis=0)
    w_ref[...] = pad.T
    cnt_ref[...] = cnt_ref[...] + incl[:, tm - 1:tm]


def _router(x2, mod, g, w_router_t, e_bias, seq, tm=512):
    n, d = x2.shape
    blocks_per_batch = seq // tm
    tri = jnp.asarray(np.triu(np.ones((tm, tm), np.float32)), BF16)
    tok = lambda r: pl.BlockSpec((r, tm), lambda i: (0, i))
    const = lambda a: pl.BlockSpec(a.shape, lambda i: (0,) * a.ndim)
    return pl.pallas_call(
        _router_kernel,
        grid=(n // tm,),
        in_specs=[
            pl.BlockSpec((tm, d), lambda i: (i, 0)),
            pl.BlockSpec((1, 6, d), lambda i: (i // blocks_per_batch, 0, 0)),
            const(g), const(w_router_t), const(e_bias), const(tri),
        ],
        out_specs=[
            pl.BlockSpec((tm, d // 2), lambda i: (i, 0)),
            tok(TOP_K), tok(TOP_K),
            pl.BlockSpec((tm, LANES), lambda i: (i, 0)),
            pl.BlockSpec((N_EXPERTS, LANES), lambda i: (0, 0)),
        ],
        out_shape=[
            jax.ShapeDtypeStruct((n, d // 2), jnp.int32),
            jax.ShapeDtypeStruct((TOP_K, n), jnp.int32),
            jax.ShapeDtypeStruct((TOP_K, n), jnp.int32),
            jax.ShapeDtypeStruct((n, LANES), F32),
            jax.ShapeDtypeStruct((N_EXPERTS, LANES), F32),
        ],
        compiler_params=_params(("arbitrary",)),
        name="moe_router",
    )(x2, mod, g, w_router_t, e_bias, tri)


def _sc_worker_id():
    return lax.axis_index("subcore") * SC_CORES + lax.axis_index("core")


def _sc_scatter_rows(src, idx3, n_out):
    n, w = src.shape
    per_worker = n // SC_CHUNK // SC_WORKERS
    mesh = plsc.VectorSubcoreMesh(core_axis_name="core", subcore_axis_name="subcore")

    @functools.partial(
        pl.kernel, mesh=mesh, out_type=jax.ShapeDtypeStruct((n_out, w), src.dtype), name="moe_dispatch",
        scratch_types=[pltpu.VMEM((2, TOP_K, SC_CHUNK), jnp.int32), pltpu.VMEM((2, SC_CHUNK, w), src.dtype),
                       pltpu.SemaphoreType.DMA((2,)), pltpu.SemaphoreType.DMA((2,)), pltpu.SemaphoreType.DMA])
    def scatter(src_hbm, idx_hbm, out_hbm, idx_v, rows_v, idx_sem, row_sem, out_sem):
        first = _sc_worker_id() * per_worker

        def loads(chunk, slot):
            return (pltpu.make_async_copy(idx_hbm.at[chunk], idx_v.at[slot], idx_sem.at[slot]),
                    pltpu.make_async_copy(src_hbm.at[pl.ds(chunk * SC_CHUNK, SC_CHUNK)], rows_v.at[slot],
                                          row_sem.at[slot]))

        for cp in loads(first, 0):
            cp.start()

        @pl.loop(0, per_worker, step=2)
        def _(i):
            for slot in range(2):
                chunk = first + i + slot
                for cp in loads(chunk, slot):
                    cp.wait()

                @pl.when(i + slot + 1 < per_worker)
                def _():
                    for cp in loads(chunk + 1, 1 - slot):
                        cp.start()

                copies = [pltpu.make_async_copy(rows_v.at[slot], out_hbm.at[idx_v.at[slot, k]], out_sem)
                          for k in range(TOP_K)]
                for cp in copies:
                    cp.start()
                for cp in copies:
                    cp.wait()

    return scatter(src, idx3)


def _sc_gather_rows(src, idx3):
    _, w = src.shape
    chunks = idx3.shape[0]
    per_worker = chunks // SC_WORKERS
    mesh = plsc.VectorSubcoreMesh(core_axis_name="core", subcore_axis_name="subcore")

    @functools.partial(
        pl.kernel, mesh=mesh, out_type=jax.ShapeDtypeStruct((TOP_K, chunks * SC_CHUNK, w), src.dtype),
        name="moe_collect",
        scratch_types=[pltpu.VMEM((TOP_K, SC_CHUNK), jnp.int32), pltpu.VMEM((2, SC_CHUNK, w), src.dtype),
                       pltpu.SemaphoreType.DMA((2,)), pltpu.SemaphoreType.DMA((2,))])
    def gather(src_hbm, idx_hbm, out_hbm, idx_v, rows_v, in_sem, out_sem):
        first = _sc_worker_id() * per_worker

        @pl.loop(0, per_worker)
        def _(i):
            chunk = first + i
            pltpu.sync_copy(idx_hbm.at[chunk], idx_v)
            reads = [pltpu.make_async_copy(src_hbm.at[idx_v.at[k]], rows_v.at[k % 2], in_sem.at[k % 2])
                     for k in range(TOP_K)]
            writes = [pltpu.make_async_copy(rows_v.at[k % 2], out_hbm.at[k, pl.ds(chunk * SC_CHUNK, SC_CHUNK)],
                                            out_sem.at[k % 2]) for k in range(TOP_K)]
            reads[0].start()
            for k in range(TOP_K):
                if k + 1 < TOP_K:
                    if k >= 1:
                        writes[k - 1].wait()
                    reads[k + 1].start()
                reads[k].wait()
                writes[k].start()
            writes[TOP_K - 2].wait()
            writes[TOP_K - 1].wait()

    return gather(src, idx3)


def _swiglu(x, w_gate_up, w_down):
    gu = jnp.dot(x, w_gate_up, preferred_element_type=F32)
    gate = gu[:, :EXPERT_DIM]
    act = (gate * jax.nn.sigmoid(gate)) * gu[:, EXPERT_DIM:]
    return jnp.dot(act.astype(BF16), w_down, preferred_element_type=F32)


def _expert_ffn_kernel(be_ref, nv_ref, xs_ref, wg_ref, wu_ref, wd_ref, ys_ref, wgu_bf, wd_bf):
    i = pl.program_id(0)
    nvalid = nv_ref[i]

    @pl.when(jnp.logical_or(i == 0, be_ref[i] != be_ref[jnp.maximum(i - 1, 0)]))
    def _():
        wgu_bf[:, :EXPERT_DIM] = wg_ref[0, 0].astype(BF16)
        wgu_bf[:, EXPERT_DIM:] = wu_ref[0, 0].astype(BF16)
        wd_bf[...] = wd_ref[0, 0].astype(BF16)

    @pl.when(nvalid > 0)
    def _():
        tb = xs_ref.shape[0]
        sub = tb // FFN_SUB_BLOCKS
        row = lax.broadcasted_iota(jnp.int32, (sub, xs_ref.shape[1]), 0)
        spans = [pl.ds(s * sub, sub) for s in range(FFN_SUB_BLOCKS)]
        xs = []
        for s, span in enumerate(spans):
            low, high = _unpack_halves(jnp.where(row < nvalid - s * sub, xs_ref[span, :], 0))
            xs.append(jnp.concatenate([low, high], axis=1).astype(BF16))
        gus = [jnp.dot(x, wgu_bf[...], preferred_element_type=F32) for x in xs]
        acts = [((gu[:, :EXPERT_DIM] * jax.nn.sigmoid(gu[:, :EXPERT_DIM])) * gu[:, EXPERT_DIM:]).astype(BF16)
                for gu in gus]
        ys = [jnp.dot(a, wd_bf[...], preferred_element_type=F32) for a in acts]
        for span, y in zip(spans, ys):
            ys_ref[span, :] = _pack_halves(y)

    @pl.when(nvalid == 0)
    def _():
        ys_ref[...] = jnp.zeros_like(ys_ref)


def _expert_ffn(xs, block_expert, block_valid, w_gate, w_up, w_down, layer, tb):
    p, half = xs.shape
    d = 2 * half
    grid_spec = pltpu.PrefetchScalarGridSpec(
        num_scalar_prefetch=2,
        grid=(p // tb,),
        in_specs=[
            pl.BlockSpec((tb, half), lambda i, be, nv: (i, 0)),
            pl.BlockSpec((1, 1, d, EXPERT_DIM), lambda i, be, nv: (layer, be[i], 0, 0)),
            pl.BlockSpec((1, 1, d, EXPERT_DIM), lambda i, be, nv: (layer, be[i], 0, 0)),
            pl.BlockSpec((1, 1, EXPERT_DIM, d), lambda i, be, nv: (layer, be[i], 0, 0)),
        ],
        out_specs=pl.BlockSpec((tb, half), lambda i, be, nv: (i, 0)),
        scratch_shapes=[pltpu.VMEM((d, 2 * EXPERT_DIM), BF16), pltpu.VMEM((EXPERT_DIM, d), BF16)],
    )
    return pl.pallas_call(
        _expert_ffn_kernel,
        grid_spec=grid_spec,
        out_shape=jax.ShapeDtypeStruct((p, half), jnp.int32),
        compiler_params=_params(("arbitrary",)),
        name="moe_expert_ffn",
    )(block_expert, block_valid, xs, w_gate, w_up, w_down)


def _combine_kernel(x_ref, hp_ref, yg_ref, w_ref, mod_ref, wgu_ref, wd_ref, gf_ref, o_ref, *, final_norm):
    low, high = _unpack_halves(hp_ref[...])
    h = jnp.concatenate([low, high], axis=1).astype(BF16)
    shared = _swiglu(h, wgu_ref[...], wd_ref[...])
    half = hp_ref.shape[1]
    acc_low, acc_high = shared[:, :half], shared[:, half:]
    w = w_ref[...]
    for k in range(TOP_K):
        low, high = _unpack_halves(yg_ref[k])
        wk = w[:, k:k + 1]
        acc_low = acc_low + wk * low
        acc_high = acc_high + wk * high
    y = jnp.concatenate([acc_low, acc_high], axis=1)
    out = x_ref[...] + mod_ref[0, GT2:GT2 + 1, :] * y
    if final_norm:
        out = out * lax.rsqrt(jnp.mean(out * out, axis=-1, keepdims=True) + EPS) * gf_ref[...]
    o_ref[...] = out


def _combine(x2, hp, yg, w, mod, ws_gate_up, ws_down, g_final, seq, final_norm, tm=512):
    n, d = x2.shape
    blocks_per_batch = seq // tm
    const = lambda a: pl.BlockSpec(a.shape, lambda i: (0,) * a.ndim)
    return pl.pallas_call(
        functools.partial(_combine_kernel, final_norm=final_norm),
        grid=(n // tm,),
        in_specs=[
            pl.BlockSpec((tm, d), lambda i: (i, 0)),
            pl.BlockSpec((tm, d // 2), lambda i: (i, 0)),
            pl.BlockSpec((TOP_K, tm, d // 2), lambda i: (0, i, 0)),
            pl.BlockSpec((tm, LANES), lambda i: (i, 0)),
            pl.BlockSpec((1, 6, d), lambda i: (i // blocks_per_batch, 0, 0)),
            const(ws_gate_up), const(ws_down), const(g_final),
        ],
        out_specs=pl.BlockSpec((tm, d), lambda i: (i, 0)),
        out_shape=jax.ShapeDtypeStruct((n, d), F32),
        compiler_params=_params(("arbitrary",)),
        name="moe_combine",
    )(x2, hp, yg, w, mod, ws_gate_up, ws_down, g_final)


def _token_mixer(x2, mod, g_mix, w_in, b_gate, rpb, w_pa, w_pb, w_o, batch, seq):
    d = x2.shape[1]
    dil0 = 3 * NA_WIDTH
    group_cols = [[dil0 + part * DIL_WIDTH + grp * DIL_OUT_WIDTH for part in range(3)]
                  for grp in range(N_DIL_GROUPS)]
    order = sorted(range(N_DIL_GROUPS), key=lambda grp: DIL_GROUPS[grp][1] > 1)
    w_qkv = jnp.concatenate(
        [w_in[:, :dil0]] + [w_in[:, c:c + DIL_OUT_WIDTH] for grp in order for c in group_cols[grp]],
        axis=1).astype(BF16)
    tok, *residue = _inproj(x2, mod, g_mix.reshape(1, d), w_qkv, batch, seq)
    o_a = _neighbourhood_attention(tok, _na_bias_table(rpb), batch, seq)

    qkv_offsets = tuple(part * DIL_OUT_WIDTH // LANES for part in range(3))
    o_groups, lse_groups = [None] * N_DIL_GROUPS, [None] * N_DIL_GROUPS
    residue = iter(residue)
    tok_offset = dil0 // LANES
    for grp in order:
        if DIL_GROUPS[grp][1] == 1:
            seqs = tok.reshape(batch, 1, seq, tok.shape[1])
            offsets = tuple(tok_offset + o for o in qkv_offsets)
            tok_offset += 3 * DIL_OUT_WIDTH // LANES
        else:
            seqs, offsets = next(residue), qkv_offsets
        o_groups[grp], lse_groups[grp] = _dilated_attention(seqs, offsets, grp)

    return _outproj(x2, o_a, o_groups, lse_groups, mod, g_mix.reshape(1, d),
                    w_in[:, QKV_COLS:].astype(BF16), b_gate.reshape(1, -1),
                    w_pa.astype(BF16), w_pb.astype(BF16), w_o.astype(BF16), seq)


def _dispatch_plan(eidx_t, rank_t, counts, tb):
    n = eidx_t.shape[1]
    n_blocks = -(-(n * TOP_K + N_EXPERTS * (tb - 1)) // tb)
    padded = (counts + tb - 1) // tb * tb
    seg_end = jnp.cumsum(padded)
    seg_start = seg_end - padded
    experts = jnp.arange(N_EXPERTS, dtype=jnp.int32)

    def lookup(table, idx):
        sel = idx[None] == experts.reshape((N_EXPERTS,) + (1,) * idx.ndim)
        return jnp.sum(jnp.where(sel, table.reshape((N_EXPERTS,) + (1,) * idx.ndim), 0), axis=0)

    dest_t = lookup(seg_start, eidx_t) + rank_t
    idx3 = dest_t.reshape(TOP_K, n // SC_CHUNK, SC_CHUNK).transpose(1, 0, 2)
    block_start = jnp.arange(n_blocks, dtype=jnp.int32) * tb
    block_expert = jnp.sum((seg_end[:, None] <= block_start[None, :]).astype(jnp.int32), axis=0)
    block_expert = jnp.minimum(block_expert, N_EXPERTS - 1)
    block_valid = jnp.clip(lookup(counts, block_expert) - (block_start - lookup(seg_start, block_expert)), 0, tb)
    return idx3, block_expert, block_valid.astype(jnp.int32), n_blocks


def _moe_layer(x2, mod, g_ffn, w_router, e_bias, we_gate, we_up, we_down, layer, ws_gate, ws_up, ws_down, g_final,
               seq, final_norm, tb=EXPERT_ROW_BLOCK):
    d = x2.shape[1]
    hp, eidx_t, rank_t, w, cnt = _router(x2, mod, g_ffn.reshape(1, d), w_router.T, e_bias.reshape(-1, 1), seq)
    counts = cnt[:, 0].astype(jnp.int32)
    idx3, block_expert, block_valid, n_blocks = _dispatch_plan(eidx_t, rank_t, counts, tb)
    xs = _sc_scatter_rows(hp, idx3, n_blocks * tb)
    ys = _expert_ffn(xs, block_expert, block_valid, we_gate, we_up, we_down, layer, tb)
    yg = _sc_gather_rows(ys, idx3)
    return _combine(x2, hp, yg, w, mod, jnp.concatenate([ws_gate, ws_up], axis=-1).astype(BF16),
                    ws_down.astype(BF16), g_final.reshape(1, d), seq, final_norm)


def kernel(x, c, w_ada, b_ada, g_mix, w_in, b_gate, rpb, w_pa, w_pb, w_o, g_ffn, w_router, e_bias,
           we_gate, we_up, we_down, ws_gate, ws_up, ws_down, g_final):
    batch, seq, d = x.shape
    depth = w_ada.shape[0]
    mods = _ada(c, w_ada, b_ada).reshape(depth, batch, 6, d)
    x2 = x.reshape(batch * seq, d)
    for l in range(depth):
        x2 = _token_mixer(x2, mods[l], g_mix[l], w_in[l], b_gate[l], rpb[l], w_pa[l], w_pb[l], w_o[l],
                          batch, seq)
        x2 = _moe_layer(x2, mods[l], g_ffn[l], w_router[l], e_bias[l], we_gate, we_up, we_down, l,
                        ws_gate[l], ws_up[l], ws_down[l], g_final, seq, final_norm=(l == depth - 1))
    return x2.reshape(batch, seq, d)
```

```python
import functools

import numpy as np
import jax
import jax.numpy as jnp
from jax import lax
from jax.experimental import pallas as pl
from jax.experimental.pallas import tpu as pltpu
from jax.experimental.pallas import tpu_sc as plsc

HEAD_DIM = 64
GRID_W = 64
NA_HEADS = 8
NA_WIN_ROWS = 8
NA_WIN_COLS = 16
DIL_GROUPS = ((128, 1), (512, 4), (2048, 16))
DIL_HEADS_PER_GROUP = 4
N_DIL_GROUPS = len(DIL_GROUPS)
NA_WIDTH = NA_HEADS * HEAD_DIM
DIL_WIDTH = N_DIL_GROUPS * DIL_HEADS_PER_GROUP * HEAD_DIM
DIL_OUT_WIDTH = DIL_HEADS_PER_GROUP * HEAD_DIM
QKV_COLS = 3 * (NA_WIDTH + DIL_WIDTH)
N_EXPERTS = 64
TOP_K = 8
N_EXPERT_GROUPS = 8
TOP_GROUPS = 4
EXPERT_DIM = 256
ROUTED_SCALE = 2.5
ALIBI_MAX = 8.0
EPS = 1e-6
NEG_INF = -1e30

LANES = 128
HEADS_PER_LANE_TILE = LANES // HEAD_DIM
DIL_BLOCK = 64
VMEM_LIMIT_BYTES = 56 * 1024 * 1024

SC_CORES = 2
SC_SUBCORES = 16
SC_WORKERS = SC_CORES * SC_SUBCORES
SC_CHUNK = 64
EXPERT_ROW_BLOCK = 1024
FFN_SUB_BLOCKS = 4

F32 = jnp.float32
BF16 = jnp.bfloat16

SH1, SC1, GT1, SH2, SC2, GT2 = range(6)


def _params(sem):
    return pltpu.CompilerParams(dimension_semantics=sem, vmem_limit_bytes=VMEM_LIMIT_BYTES)


def _modulated_norm(x, g, scale, shift):
    r = lax.rsqrt(jnp.mean(x * x, axis=-1, keepdims=True) + EPS)
    return (x * r * g) * (1.0 + scale) + shift


def _ada_kernel(c_ref, w_ref, b_ref, o_ref):
    c = c_ref[...]
    act = c * jax.nn.sigmoid(c)
    o_ref[0] = jnp.dot(act, w_ref[0], preferred_element_type=F32,
                       precision=lax.Precision.HIGHEST) + b_ref[0]


def _ada(c, w_ada, b_ada):
    depth, d, six_d = w_ada.shape
    b = c.shape[0]
    tn = d
    return pl.pallas_call(
        _ada_kernel,
        grid=(depth, six_d // tn),
        in_specs=[
            pl.BlockSpec((b, d), lambda l, j: (0, 0)),
            pl.BlockSpec((1, d, tn), lambda l, j: (l, 0, j)),
            pl.BlockSpec((1, 1, tn), lambda l, j: (l, 0, j)),
        ],
        out_specs=pl.BlockSpec((1, b, tn), lambda l, j: (l, 0, j)),
        out_shape=jax.ShapeDtypeStruct((depth, b, six_d), F32),
        compiler_params=_params(("arbitrary", "arbitrary")),
        name="ada_mod",
    )(c, w_ada, b_ada.reshape(depth, 1, six_d))


def _inproj_kernel(x_ref, mod_ref, g_ref, w_ref, tok_ref, *rest, dilations):
    res_refs, acc_refs = rest[:len(dilations)], rest[len(dilations):]
    tm = x_ref.shape[0]
    tok_cols = tok_ref.shape[1]
    tn = (w_ref.shape[1] - tok_cols) // len(dilations)
    h = _modulated_norm(x_ref[...], g_ref[...], mod_ref[0, SC1:SC1 + 1, :], mod_ref[0, SH1:SH1 + 1, :]).astype(BF16)

    for c0 in range(0, tok_cols, tn):
        tok_ref[:, c0:c0 + tn] = jnp.dot(h, w_ref[:, c0:c0 + tn], preferred_element_type=F32).astype(BF16)

    for g, (res_ref, acc_ref, dilation) in enumerate(zip(res_refs, acc_refs, dilations)):
        c0 = tok_cols + g * tn
        res = jnp.dot(h, w_ref[:, c0:c0 + tn], preferred_element_type=F32)
        for c in range(acc_ref.shape[0]):
            acc_ref[c] = res[:, c * LANES:(c + 1) * LANES]
        for r in range(dilation):
            for c in range(acc_ref.shape[0]):
                res_ref[0, r, :, c * LANES:(c + 1) * LANES] = (
                    acc_ref[c, pl.ds(r, tm // dilation, stride=dilation), :].astype(BF16))


def _inproj(x2, mod, g, w_qkv, batch, seq, tm=512):
    n, d = x2.shape
    tn = 3 * DIL_OUT_WIDTH
    dilations = tuple(dil for _, dil in DIL_GROUPS if dil > 1)
    tok_cols = w_qkv.shape[1] - tn * len(dilations)
    blocks_per_batch = seq // tm
    res_specs = [pl.BlockSpec((1, dil, tm // dil, tn),
                              lambda i: (i // blocks_per_batch, 0, i % blocks_per_batch, 0)) for dil in dilations]
    res_shapes = [jax.ShapeDtypeStruct((batch, dil, seq // dil, tn), BF16) for dil in dilations]
    return pl.pallas_call(
        functools.partial(_inproj_kernel, dilations=dilations),
        grid=(n // tm,),
        in_specs=[
            pl.BlockSpec((tm, d), lambda i: (i, 0)),
            pl.BlockSpec((1, 6, d), lambda i: (i // blocks_per_batch, 0, 0)),
            pl.BlockSpec((1, d), lambda i: (0, 0)),
            pl.BlockSpec(w_qkv.shape, lambda i: (0, 0)),
        ],
        out_specs=[pl.BlockSpec((tm, tok_cols), lambda i: (i, 0))] + res_specs,
        out_shape=[jax.ShapeDtypeStruct((n, tok_cols), BF16)] + res_shapes,
        scratch_shapes=[pltpu.VMEM((tn // LANES, tm, LANES), F32) for _ in dilations],
        compiler_params=_params(("arbitrary",)),
        name="in_proj",
    )(x2, mod, g, w_qkv)


def _na_bias_table(rpb):
    heads = rpb.shape[0]
    cols = np.arange(GRID_W)
    col_start = np.clip(cols - NA_WIN_COLS // 2, 0, GRID_W - NA_WIN_COLS)
    col_mask = (cols[None, :] >= col_start[:, None]) & (cols[None, :] < col_start[:, None] + NA_WIN_COLS)
    dc = np.clip(cols[None, :] - cols[:, None], -(NA_WIN_COLS - 1), NA_WIN_COLS - 1) + NA_WIN_COLS - 1
    rpb_cols = rpb[:, :, dc].astype(F32)
    dr = np.arange(NA_WIN_ROWS)[None, :] - np.arange(NA_WIN_ROWS)[:, None] + NA_WIN_ROWS - 1
    t = rpb_cols[:, dr]
    t = t.transpose(0, 1, 3, 2, 4)
    t = jnp.where(col_mask[:, None, :], t, NEG_INF)
    t = t.reshape(heads // HEADS_PER_LANE_TILE, HEADS_PER_LANE_TILE, NA_WIN_ROWS, GRID_W, NA_WIN_ROWS * GRID_W)
    return t.transpose(0, 2, 1, 3, 4).reshape(heads // HEADS_PER_LANE_TILE, NA_WIN_ROWS,
                                              HEADS_PER_LANE_TILE * GRID_W, NA_WIN_ROWS * GRID_W)


def _stack_heads(q, low):
    scaled = q * (HEAD_DIM ** -0.5)
    zero = jnp.zeros_like(scaled)
    return jnp.concatenate([jnp.where(low, scaled, zero), jnp.where(low, zero, scaled)], axis=0)


def _stacked_attention(items):
    scores = [lax.dot_general(q2, kw, (((1,), (1,)), ((), ())), preferred_element_type=F32) + bias
              for q2, kw, _, bias in items]
    probs = []
    for s in scores:
        m = jnp.max(s, axis=-1, keepdims=True)
        p = jnp.exp(s - m)
        probs.append((p.astype(BF16), m, jnp.sum(p, axis=-1, keepdims=True)))
    return [(jnp.dot(p, vw, preferred_element_type=F32) / z, m, z)
            for (p, m, z), (_, _, vw, _) in zip(probs, items)]


def _unstack_heads(a, low):
    half = a.shape[0] // HEADS_PER_LANE_TILE
    return jnp.where(low, a[:half], a[half:])


def _na_kernel(q_ref, k_ref, v_ref, bias_ref, o_ref, *, rows, rows_per_step):
    kr = NA_WIN_ROWS
    low = lax.broadcasted_iota(jnp.int32, (GRID_W, LANES), 1) < HEAD_DIM

    def body(i, carry):
        items, qrows = [], []
        for u in range(rows_per_step):
            r = i * rows_per_step + u
            rs = jnp.clip(r - kr // 2, 0, rows - kr)
            qrows.append(pl.ds(pl.multiple_of(r * GRID_W, GRID_W), GRID_W))
            wrows = pl.ds(pl.multiple_of(rs * GRID_W, GRID_W), kr * GRID_W)
            items.append((_stack_heads(q_ref[qrows[-1], :], low), k_ref[wrows, :], v_ref[wrows, :],
                          bias_ref[0, r - rs]))
        for rows_u, (o, _, _) in zip(qrows, _stacked_attention(items)):
            o_ref[rows_u, :] = _unstack_heads(o, low).astype(o_ref.dtype)
        return carry

    lax.fori_loop(0, rows // rows_per_step, body, 0)


def _neighbourhood_attention(qkv, bias, batch, seq, rows_per_step=16):
    n = qkv.shape[0]
    rows = seq // GRID_W
    pairs = NA_WIDTH // LANES
    return pl.pallas_call(
        functools.partial(_na_kernel, rows=rows, rows_per_step=rows_per_step),
        grid=(pairs, batch),
        in_specs=[
            pl.BlockSpec((seq, LANES), lambda p, b: (b, p)),
            pl.BlockSpec((seq, LANES), lambda p, b: (b, pairs + p)),
            pl.BlockSpec((seq, LANES), lambda p, b: (b, 2 * pairs + p)),
            pl.BlockSpec((1, NA_WIN_ROWS, HEADS_PER_LANE_TILE * GRID_W, NA_WIN_ROWS * GRID_W),
                         lambda p, b: (p, 0, 0, 0)),
        ],
        out_specs=pl.BlockSpec((seq, LANES), lambda p, b: (b, p)),
        out_shape=jax.ShapeDtypeStruct((n, NA_WIDTH), BF16),
        compiler_params=_params(("arbitrary", "arbitrary")),
        name="na_attn",
    )(qkv, qkv, qkv, bias)


def _alibi_slopes():
    n = N_DIL_GROUPS * DIL_HEADS_PER_GROUP
    s = np.exp2(-ALIBI_MAX * np.arange(1, n + 1, dtype=np.float64) / n).astype(np.float32)
    return s.reshape(N_DIL_GROUPS, DIL_HEADS_PER_GROUP)


def _dil_bias_table(group):
    blk = DIL_BLOCK
    dilation = DIL_GROUPS[group][1]
    slopes = _alibi_slopes()[group]
    qi = np.arange(blk)[:, None]
    kj = np.arange(3 * blk)[None, :]
    tables = []
    for shift in range(3):
        arel = np.abs(kj - qi - shift * blk)
        dist = (dilation * arel).astype(np.float32)
        per_head = [np.where(arel <= blk, -slopes[h] * dist, np.float32(NEG_INF)) for h in range(DIL_HEADS_PER_GROUP)]
        tables.append(np.stack(per_head))
    t = np.stack(tables, axis=1).astype(np.float32)
    pairs = DIL_HEADS_PER_GROUP // HEADS_PER_LANE_TILE
    t = t.reshape(pairs, HEADS_PER_LANE_TILE, 3, blk, 3 * blk).transpose(0, 2, 1, 3, 4)
    return t.reshape(pairs, 3, HEADS_PER_LANE_TILE * blk, 3 * blk)


def _dil_kernel(q_ref, k_ref, v_ref, bias_ref, o_ref, lse_ref, *, nb, blocks_per_step):
    blk = DIL_BLOCK
    win = 3 * blk
    low = lax.broadcasted_iota(jnp.int32, (blk, LANES), 1) < HEAD_DIM
    dilation = q_ref.shape[0]
    steps = dilation * nb

    def token_rows(sq, n):
        if dilation == 1:
            return pl.ds(pl.multiple_of(n * blk, blk), blk)
        return pl.ds(n * (blk * dilation) + sq, blk, stride=dilation)

    def body(i, carry):
        items, dst = [], []
        for u in range(blocks_per_step):
            t = i * blocks_per_step + u
            sq = t // nb
            n = t % nb
            wb = jnp.clip(n - 1, 0, nb - 3)
            qrows = pl.ds(pl.multiple_of(n * blk, blk), blk)
            wrows = pl.ds(pl.multiple_of(wb * blk, blk), win)
            dst.append(token_rows(sq, n))
            items.append((_stack_heads(q_ref[sq, qrows, :], low), k_ref[sq, wrows, :], v_ref[sq, wrows, :],
                          bias_ref[0, n - wb]))
        for rows, (o, m, z) in zip(dst, _stacked_attention(items)):
            o_ref[rows, :] = _unstack_heads(o, low)
            lse_ref[rows, :] = _unstack_heads(jnp.broadcast_to(m + jnp.log(z), o.shape), low)
        return carry

    lax.fori_loop(0, steps // blocks_per_step, body, 0)


def _dilated_attention(qkv4, col_offsets, group, blocks_per_step=16):
    batch, dilation, length, _ = qkv4.shape
    seq = dilation * length
    nb = length // DIL_BLOCK
    pairs = DIL_OUT_WIDTH // LANES
    qo, ko, vo = col_offsets
    bias = jnp.asarray(_dil_bias_table(group))
    out = jax.ShapeDtypeStruct((batch * seq, DIL_OUT_WIDTH), F32)
    seqs = lambda off: pl.BlockSpec((None, dilation, length, LANES), lambda p, b: (b, 0, 0, off + p))
    return pl.pallas_call(
        functools.partial(_dil_kernel, nb=nb, blocks_per_step=blocks_per_step),
        grid=(pairs, batch),
        in_specs=[seqs(qo), seqs(ko), seqs(vo),
                  pl.BlockSpec((1,) + bias.shape[1:], lambda p, b: (p, 0, 0, 0))],
        out_specs=[pl.BlockSpec((seq, LANES), lambda p, b: (b, p))] * 2,
        out_shape=[out, out],
        compiler_params=_params(("arbitrary", "arbitrary")),
        name=f"dil_attn_g{group}",
    )(qkv4, qkv4, qkv4, bias)


def _outproj_kernel(x_ref, oa_ref, o0_ref, o1_ref, o2_ref, l0_ref, l1_ref, l2_ref, mod_ref, g_ref,
                    wg_ref, bg_ref, wpa_ref, wpb_ref, wo_ref, out_ref):
    d = x_ref.shape[1]
    x = x_ref[...]
    h = _modulated_norm(x, g_ref[...], mod_ref[0, SC1:SC1 + 1, :], mod_ref[0, SH1:SH1 + 1, :]).astype(BF16)

    lses = [l0_ref[...], l1_ref[...], l2_ref[...]]
    outs = [o0_ref[...], o1_ref[...], o2_ref[...]]
    top = jnp.maximum(jnp.maximum(lses[0], lses[1]), lses[2])
    es = [jnp.exp(l - top) for l in lses]
    den = es[0] + es[1] + es[2]
    ob = (es[0] * outs[0] + es[1] * outs[1] + es[2] * outs[2]) / den

    ya = jnp.dot(oa_ref[...], wpa_ref[...], preferred_element_type=F32)
    yb = jnp.dot(ob.astype(BF16), wpb_ref[...], preferred_element_type=F32)
    ga = jax.nn.sigmoid(jnp.dot(h, wg_ref[:, :d], preferred_element_type=F32) + bg_ref[:, :d])
    mix = ga * ya
    gb = jax.nn.sigmoid(jnp.dot(h, wg_ref[:, d:], preferred_element_type=F32) + bg_ref[:, d:])
    mix = mix + gb * yb
    y = jnp.dot(mix.astype(BF16), wo_ref[...], preferred_element_type=F32)
    out_ref[...] = x + mod_ref[0, GT1:GT1 + 1, :] * y


def _outproj(x2, oa, o_groups, lse_groups, mod, g, w_gate, b_gate, w_pa, w_pb, w_o, seq, tm=512):
    n, d = x2.shape
    blocks_per_batch = seq // tm
    row = lambda c: pl.BlockSpec((tm, c), lambda i: (i, 0))
    full = lambda a: pl.BlockSpec(a.shape, lambda i: (0,) * a.ndim)
    return pl.pallas_call(
        _outproj_kernel,
        grid=(n // tm,),
        in_specs=[row(d), row(NA_WIDTH)] + [row(DIL_OUT_WIDTH)] * 6 + [
            pl.BlockSpec((1, 6, d), lambda i: (i // blocks_per_batch, 0, 0)),
            full(g), full(w_gate), full(b_gate), full(w_pa), full(w_pb), full(w_o),
        ],
        out_specs=row(d),
        out_shape=jax.ShapeDtypeStruct((n, d), F32),
        compiler_params=_params(("arbitrary",)),
        name="out_proj",
    )(x2, oa, *o_groups, *lse_groups, mod, g, w_gate, b_gate, w_pa, w_pb, w_o)


def _first_index_of_max(cur, idx, size):
    m = jnp.max(cur, axis=0, keepdims=True)
    first = jnp.min(jnp.where(cur == m, idx, size), axis=0, keepdims=True)
    return m, first


def _route_transposed(logits_t, e_bias):
    tokens = logits_t.shape[1]
    per_group = N_EXPERTS // N_EXPERT_GROUPS
    scores = jax.nn.sigmoid(logits_t)
    biased = scores + e_bias
    midx = lax.broadcasted_iota(jnp.int32, (per_group, tokens), 0)
    grp_scores = []
    for g in range(N_EXPERT_GROUPS):
        vals = biased[g * per_group:(g + 1) * per_group, :]
        m1, first = _first_index_of_max(vals, midx, per_group)
        m2 = jnp.max(jnp.where(midx == first, -jnp.inf, vals), axis=0, keepdims=True)
        grp_scores.append(m1 + m2)
    cur = jnp.concatenate(grp_scores, axis=0)
    gidx = lax.broadcasted_iota(jnp.int32, (N_EXPERT_GROUPS, tokens), 0)
    grp_sel = jnp.zeros((N_EXPERT_GROUPS, tokens), jnp.bool_)
    for _ in range(TOP_GROUPS):
        _, first = _first_index_of_max(cur, gidx, N_EXPERT_GROUPS)
        pick = gidx == first
        grp_sel = jnp.logical_or(grp_sel, pick)
        cur = jnp.where(pick, -jnp.inf, cur)
    rows = []
    for g in range(N_EXPERT_GROUPS):
        vals = biased[g * per_group:(g + 1) * per_group, :]
        rows.append(jnp.where(grp_sel[g:g + 1, :], vals, NEG_INF))
    cur = jnp.concatenate(rows, axis=0)
    eidx = lax.broadcasted_iota(jnp.int32, (N_EXPERTS, tokens), 0)
    firsts, picks, weights = [], [], []
    for _ in range(TOP_K):
        _, first = _first_index_of_max(cur, eidx, N_EXPERTS)
        pick = eidx == first
        firsts.append(first)
        picks.append(pick)
        weights.append(jnp.sum(jnp.where(pick, scores, 0.0), axis=0, keepdims=True))
        cur = jnp.where(pick, -jnp.inf, cur)
    total = functools.reduce(lambda a, b: a + b, weights)
    return firsts, picks, [w / total * ROUTED_SCALE for w in weights]


def _pack_halves(a):
    half = a.shape[1] // 2
    bits = lax.bitcast_convert_type(a.astype(BF16).astype(F32), jnp.int32)
    return lax.shift_right_logical(bits[:, :half], 16) | bits[:, half:]


def _unpack_halves(w):
    low = lax.bitcast_convert_type(lax.shift_left(w, 16), F32)
    high = lax.bitcast_convert_type(w & jnp.int32(-65536), F32)
    return low, high


def _router_kernel(x_ref, mod_ref, g_ref, wr_ref, eb_ref, tri_ref, hp_ref, eidx_ref, rank_ref, w_ref, cnt_ref):
    tm = x_ref.shape[0]

    @pl.when(pl.program_id(0) == 0)
    def _():
        cnt_ref[...] = jnp.zeros_like(cnt_ref)

    h = _modulated_norm(x_ref[...], g_ref[...], mod_ref[0, SC2:SC2 + 1, :], mod_ref[0, SH2:SH2 + 1, :])
    hp_ref[...] = _pack_halves(h)
    h_hi = h.astype(BF16)
    h_lo = (h - h_hi.astype(F32)).astype(BF16)
    nt = (((1,), (1,)), ((), ()))
    logits_t = (lax.dot_general(wr_ref[0], h_hi, nt, preferred_element_type=F32)
                + lax.dot_general(wr_ref[0], h_lo, nt, preferred_element_type=F32)
                + lax.dot_general(wr_ref[1], h_hi, nt, preferred_element_type=F32))
    firsts, picks, weights = _route_transposed(logits_t, eb_ref[...])
    sel = functools.reduce(jnp.logical_or, picks)
    sel_f = jnp.where(sel, 1.0, 0.0)
    incl = jnp.dot(sel_f.astype(BF16), tri_ref[...], preferred_element_type=F32)
    before = cnt_ref[:, 0:1] + incl - sel_f
    eidx_ref[...] = jnp.concatenate(firsts, axis=0)
    rank_ref[...] = jnp.concatenate(
        [jnp.sum(jnp.where(p, before, 0.0), axis=0, keepdims=True) for p in picks], axis=0).astype(jnp.int32)
    pad = jnp.concatenate(weights + [jnp.zeros((LANES - TOP_K, tm), F32)], axis=0)
    w_ref[...] = pad.T
    cnt_ref[...] = cnt_ref[...] + incl[:, tm - 1:tm]


def _router(x2, mod, g, w_router_t, e_bias, seq, tm=512):
    n, d = x2.shape
    blocks_per_batch = seq // tm
    tri = jnp.asarray(np.triu(np.ones((tm, tm), np.float32)), BF16)
    tok = lambda r: pl.BlockSpec((r, tm), lambda i: (0, i))
    const = lambda a: pl.BlockSpec(a.shape, lambda i: (0,) * a.ndim)
    return pl.pallas_call(
        _router_kernel,
        grid=(n // tm,),
        in_specs=[
            pl.BlockSpec((tm, d), lambda i: (i, 0)),
            pl.BlockSpec((1, 6, d), lambda i: (i // blocks_per_batch, 0, 0)),
            const(g), const(w_router_t), const(e_bias), const(tri),
        ],
        out_specs=[
            pl.BlockSpec((tm, d // 2), lambda i: (i, 0)),
            tok(TOP_K), tok(TOP_K),
            pl.BlockSpec((tm, LANES), lambda i: (i, 0)),
            pl.BlockSpec((N_EXPERTS, LANES), lambda i: (0, 0)),
        ],
        out_shape=[
            jax.ShapeDtypeStruct((n, d // 2), jnp.int32),
            jax.ShapeDtypeStruct((TOP_K, n), jnp.int32),
            jax.ShapeDtypeStruct((TOP_K, n), jnp.int32),
            jax.ShapeDtypeStruct((n, LANES), F32),
            jax.ShapeDtypeStruct((N_EXPERTS, LANES), F32),
        ],
        compiler_params=_params(("arbitrary",)),
        name="moe_router",
    )(x2, mod, g, w_router_t, e_bias, tri)


def _sc_worker_id():
    return lax.axis_index("subcore") * SC_CORES + lax.axis_index("core")


def _sc_scatter_rows(src, idx3, n_out):
    n, w = src.shape
    per_worker = n // SC_CHUNK // SC_WORKERS
    mesh = plsc.VectorSubcoreMesh(core_axis_name="core", subcore_axis_name="subcore")

    @functools.partial(
        pl.kernel, mesh=mesh, out_type=jax.ShapeDtypeStruct((n_out, w), src.dtype), name="moe_dispatch",
        scratch_types=[pltpu.VMEM((2, TOP_K, SC_CHUNK), jnp.int32), pltpu.VMEM((2, SC_CHUNK, w), src.dtype),
                       pltpu.SemaphoreType.DMA((2,)), pltpu.SemaphoreType.DMA((2,)), pltpu.SemaphoreType.DMA])
    def scatter(src_hbm, idx_hbm, out_hbm, idx_v, rows_v, idx_sem, row_sem, out_sem):
        first = _sc_worker_id() * per_worker

        def loads(chunk, slot):
            return (pltpu.make_async_copy(idx_hbm.at[chunk], idx_v.at[slot], idx_sem.at[slot]),
                    pltpu.make_async_copy(src_hbm.at[pl.ds(chunk * SC_CHUNK, SC_CHUNK)], rows_v.at[slot],
                                          row_sem.at[slot]))

        for cp in loads(first, 0):
            cp.start()

        @pl.loop(0, per_worker, step=2)
        def _(i):
            for slot in range(2):
                chunk = first + i + slot
                for cp in loads(chunk, slot):
                    cp.wait()

                @pl.when(i + slot + 1 < per_worker)
                def _():
                    for cp in loads(chunk + 1, 1 - slot):
                        cp.start()

                copies = [pltpu.make_async_copy(rows_v.at[slot], out_hbm.at[idx_v.at[slot, k]], out_sem)
                          for k in range(TOP_K)]
                for cp in copies:
                    cp.start()
                for cp in copies:
                    cp.wait()

    return scatter(src, idx3)


def _sc_gather_rows(src, idx3):
    _, w = src.shape
    chunks = idx3.shape[0]
    per_worker = chunks // SC_WORKERS
    mesh = plsc.VectorSubcoreMesh(core_axis_name="core", subcore_axis_name="subcore")

    @functools.partial(
        pl.kernel, mesh=mesh, out_type=jax.ShapeDtypeStruct((TOP_K, chunks * SC_CHUNK, w), src.dtype),
        name="moe_collect",
        scratch_types=[pltpu.VMEM((TOP_K, SC_CHUNK), jnp.int32), pltpu.VMEM((2, SC_CHUNK, w), src.dtype),
                       pltpu.SemaphoreType.DMA((2,)), pltpu.SemaphoreType.DMA((2,))])
    def gather(src_hbm, idx_hbm, out_hbm, idx_v, rows_v, in_sem, out_sem):
        first = _sc_worker_id() * per_worker

        @pl.loop(0, per_worker)
        def _(i):
            chunk = first + i
            pltpu.sync_copy(idx_hbm.at[chunk], idx_v)
            reads = [pltpu.make_async_copy(src_hbm.at[idx_v.at[k]], rows_v.at[k % 2], in_sem.at[k % 2])
                     for k in range(TOP_K)]
            writes = [pltpu.make_async_copy(rows_v.at[k % 2], out_hbm.at[k, pl.ds(chunk * SC_CHUNK, SC_CHUNK)],
                                            out_sem.at[k % 2]) for k in range(TOP_K)]
            reads[0].start()
            for k in range(TOP_K):
                if k + 1 < TOP_K:
                    if k >= 1:
                        writes[k - 1].wait()
                    reads[k + 1].start()
                reads[k].wait()
                writes[k].start()
            writes[TOP_K - 2].wait()
            writes[TOP_K - 1].wait()

    return gather(src, idx3)


def _swiglu(x, w_gate_up, w_down):
    gu = jnp.dot(x, w_gate_up, preferred_element_type=F32)
    gate = gu[:, :EXPERT_DIM]
    act = (gate * jax.nn.sigmoid(gate)) * gu[:, EXPERT_DIM:]
    return jnp.dot(act.astype(BF16), w_down, preferred_element_type=F32)


def _expert_ffn_kernel(be_ref, nv_ref, xs_ref, wg_ref, wu_ref, wd_ref, ys_ref, wgu_bf, wd_bf):
    i = pl.program_id(0)
    nvalid = nv_ref[i]

    @pl.when(jnp.logical_or(i == 0, be_ref[i] != be_ref[jnp.maximum(i - 1, 0)]))
    def _():
        wgu_bf[:, :EXPERT_DIM] = wg_ref[0, 0].astype(BF16)
        wgu_bf[:, EXPERT_DIM:] = wu_ref[0, 0].astype(BF16)
        wd_bf[...] = wd_ref[0, 0].astype(BF16)

    @pl.when(nvalid > 0)
    def _():
        tb = xs_ref.shape[0]
        sub = tb // FFN_SUB_BLOCKS
        row = lax.broadcasted_iota(jnp.int32, (sub, xs_ref.shape[1]), 0)
        spans = [pl.ds(s * sub, sub) for s in range(FFN_SUB_BLOCKS)]
        xs = []
        for s, span in enumerate(spans):
            low, high = _unpack_halves(jnp.where(row < nvalid - s * sub, xs_ref[span, :], 0))
            xs.append(jnp.concatenate([low, high], axis=1).astype(BF16))
        gus = [jnp.dot(x, wgu_bf[...], preferred_element_type=F32) for x in xs]
        acts = [((gu[:, :EXPERT_DIM] * jax.nn.sigmoid(gu[:, :EXPERT_DIM])) * gu[:, EXPERT_DIM:]).astype(BF16)
                for gu in gus]
        ys = [jnp.dot(a, wd_bf[...], preferred_element_type=F32) for a in acts]
        for span, y in zip(spans, ys):
            ys_ref[span, :] = _pack_halves(y)

    @pl.when(nvalid == 0)
    def _():
        ys_ref[...] = jnp.zeros_like(ys_ref)


def _expert_ffn(xs, block_expert, block_valid, w_gate, w_up, w_down, layer, tb):
    p, half = xs.shape
    d = 2 * half
    grid_spec = pltpu.PrefetchScalarGridSpec(
        num_scalar_prefetch=2,
        grid=(p // tb,),
        in_specs=[
            pl.BlockSpec((tb, half), lambda i, be, nv: (i, 0)),
            pl.BlockSpec((1, 1, d, EXPERT_DIM), lambda i, be, nv: (layer, be[i], 0, 0)),
            pl.BlockSpec((1, 1, d, EXPERT_DIM), lambda i, be, nv: (layer, be[i], 0, 0)),
            pl.BlockSpec((1, 1, EXPERT_DIM, d), lambda i, be, nv: (layer, be[i], 0, 0)),
        ],
        out_specs=pl.BlockSpec((tb, half), lambda i, be, nv: (i, 0)),
        scratch_shapes=[pltpu.VMEM((d, 2 * EXPERT_DIM), BF16), pltpu.VMEM((EXPERT_DIM, d), BF16)],
    )
    return pl.pallas_call(
        _expert_ffn_kernel,
        grid_spec=grid_spec,
        out_shape=jax.ShapeDtypeStruct((p, half), jnp.int32),
        compiler_params=_params(("arbitrary",)),
        name="moe_expert_ffn",
    )(block_expert, block_valid, xs, w_gate, w_up, w_down)


def _combine_kernel(x_ref, hp_ref, yg_ref, w_ref, mod_ref, wgu_ref, wd_ref, gf_ref, o_ref, *, final_norm):
    low, high = _unpack_halves(hp_ref[...])
    h = jnp.concatenate([low, high], axis=1).astype(BF16)
    shared = _swiglu(h, wgu_ref[...], wd_ref[...])
    half = hp_ref.shape[1]
    acc_low, acc_high = shared[:, :half], shared[:, half:]
    w = w_ref[...]
    for k in range(TOP_K):
        low, high = _unpack_halves(yg_ref[k])
        wk = w[:, k:k + 1]
        acc_low = acc_low + wk * low
        acc_high = acc_high + wk * high
    y = jnp.concatenate([acc_low, acc_high], axis=1)
    out = x_ref[...] + mod_ref[0, GT2:GT2 + 1, :] * y
    if final_norm:
        out = out * lax.rsqrt(jnp.mean(out * out, axis=-1, keepdims=True) + EPS) * gf_ref[...]
    o_ref[...] = out


def _combine(x2, hp, yg, w, mod, ws_gate_up, ws_down, g_final, seq, final_norm, tm=512):
    n, d = x2.shape
    blocks_per_batch = seq // tm
    const = lambda a: pl.BlockSpec(a.shape, lambda i: (0,) * a.ndim)
    return pl.pallas_call(
        functools.partial(_combine_kernel, final_norm=final_norm),
        grid=(n // tm,),
        in_specs=[
            pl.BlockSpec((tm, d), lambda i: (i, 0)),
            pl.BlockSpec((tm, d // 2), lambda i: (i, 0)),
            pl.BlockSpec((TOP_K, tm, d // 2), lambda i: (0, i, 0)),
            pl.BlockSpec((tm, LANES), lambda i: (i, 0)),
            pl.BlockSpec((1, 6, d), lambda i: (i // blocks_per_batch, 0, 0)),
            const(ws_gate_up), const(ws_down), const(g_final),
        ],
        out_specs=pl.BlockSpec((tm, d), lambda i: (i, 0)),
        out_shape=jax.ShapeDtypeStruct((n, d), F32),
        compiler_params=_params(("arbitrary",)),
        name="moe_combine",
    )(x2, hp, yg, w, mod, ws_gate_up, ws_down, g_final)


def _token_mixer(x2, mod, g_mix, w_in, b_gate, rpb, w_pa, w_pb, w_o, batch, seq):
    d = x2.shape[1]
    dil0 = 3 * NA_WIDTH
    group_cols = [[dil0 + part * DIL_WIDTH + grp * DIL_OUT_WIDTH for part in range(3)]
                  for grp in range(N_DIL_GROUPS)]
    order = sorted(range(N_DIL_GROUPS), key=lambda grp: DIL_GROUPS[grp][1] > 1)
    w_qkv = jnp.concatenate(
        [w_in[:, :dil0]] + [w_in[:, c:c + DIL_OUT_WIDTH] for grp in order for c in group_cols[grp]],
        axis=1).astype(BF16)
    tok, *residue = _inproj(x2, mod, g_mix.reshape(1, d), w_qkv, batch, seq)
    o_a = _neighbourhood_attention(tok, _na_bias_table(rpb), batch, seq)

    qkv_offsets = tuple(part * DIL_OUT_WIDTH // LANES for part in range(3))
    o_groups, lse_groups = [None] * N_DIL_GROUPS, [None] * N_DIL_GROUPS
    residue = iter(residue)
    tok_offset = dil0 // LANES
    for grp in order:
        if DIL_GROUPS[grp][1] == 1:
            seqs = tok.reshape(batch, 1, seq, tok.shape[1])
            offsets = tuple(tok_offset + o for o in qkv_offsets)
            tok_offset += 3 * DIL_OUT_WIDTH // LANES
        else:
            seqs, offsets = next(residue), qkv_offsets
        o_groups[grp], lse_groups[grp] = _dilated_attention(seqs, offsets, grp)

    return _outproj(x2, o_a, o_groups, lse_groups, mod, g_mix.reshape(1, d),
                    w_in[:, QKV_COLS:].astype(BF16), b_gate.reshape(1, -1),
                    w_pa.astype(BF16), w_pb.astype(BF16), w_o.astype(BF16), seq)


def _dispatch_plan(eidx_t, rank_t, counts, tb):
    n = eidx_t.shape[1]
    n_blocks = -(-(n * TOP_K + N_EXPERTS * (tb - 1)) // tb)
    padded = (counts + tb - 1) // tb * tb
    seg_end = jnp.cumsum(padded)
    seg_start = seg_end - padded
    experts = jnp.arange(N_EXPERTS, dtype=jnp.int32)

    def lookup(table, idx):
        sel = idx[None] == experts.reshape((N_EXPERTS,) + (1,) * idx.ndim)
        return jnp.sum(jnp.where(sel, table.reshape((N_EXPERTS,) + (1,) * idx.ndim), 0), axis=0)

    dest_t = lookup(seg_start, eidx_t) + rank_t
    idx3 = dest_t.reshape(TOP_K, n // SC_CHUNK, SC_CHUNK).transpose(1, 0, 2)
    block_start = jnp.arange(n_blocks, dtype=jnp.int32) * tb
    block_expert = jnp.sum((seg_end[:, None] <= block_start[None, :]).astype(jnp.int32), axis=0)
    block_expert = jnp.minimum(block_expert, N_EXPERTS - 1)
    block_valid = jnp.clip(lookup(counts, block_expert) - (block_start - lookup(seg_start, block_expert)), 0, tb)
    return idx3, block_expert, block_valid.astype(jnp.int32), n_blocks


def _moe_layer(x2, mod, g_ffn, w_router, e_bias, we_gate, we_up, we_down, layer, ws_gate, ws_up, ws_down, g_final,
               seq, final_norm, tb=EXPERT_ROW_BLOCK):
    d = x2.shape[1]
    wr_t = w_router.T
    wr_hi = wr_t.astype(BF16)
    wr_split = jnp.stack([wr_hi, (wr_t - wr_hi.astype(F32)).astype(BF16)])
    hp, eidx_t, rank_t, w, cnt = _router(x2, mod, g_ffn.reshape(1, d), wr_split, e_bias.reshape(-1, 1), seq)
    counts = cnt[:, 0].astype(jnp.int32)
    idx3, block_expert, block_valid, n_blocks = _dispatch_plan(eidx_t, rank_t, counts, tb)
    xs = _sc_scatter_rows(hp, idx3, n_blocks * tb)
    ys = _expert_ffn(xs, block_expert, block_valid, we_gate, we_up, we_down, layer, tb)
    yg = _sc_gather_rows(ys, idx3)
    return _combine(x2, hp, yg, w, mod, jnp.concatenate([ws_gate, ws_up], axis=-1).astype(BF16),
                    ws_down.astype(BF16), g_final.reshape(1, d), seq, final_norm)


def kernel(x, c, w_ada, b_ada, g_mix, w_in, b_gate, rpb, w_pa, w_pb, w_o, g_ffn, w_router, e_bias,
           we_gate, we_up, we_down, ws_gate, ws_up, ws_down, g_final):
    batch, seq, d = x.shape
    depth = w_ada.shape[0]
    mods = _ada(c, w_ada, b_ada).reshape(depth, batch, 6, d)
    x2 = x.reshape(batch * seq, d)
    for l in range(depth):
        x2 = _token_mixer(x2, mods[l], g_mix[l], w_in[l], b_gate[l], rpb[l], w_pa[l], w_pb[l], w_o[l],
                          batch, seq)
        x2 = _moe_layer(x2, mods[l], g_ffn[l], w_router[l], e_bias[l], we_gate, we_up, we_down, l,
                        ws_gate[l], ws_up[l], ws_down[l], g_final, seq, final_norm=(l == depth - 1))
    return x2.reshape(batch, seq, d)
```

```python
import functools

import numpy as np
import jax
import jax.numpy as jnp
from jax import lax
from jax.experimental import pallas as pl
from jax.experimental.pallas import tpu as pltpu
from jax.experimental.pallas import tpu_sc as plsc

HEAD_DIM = 64
GRID_W = 64
NA_HEADS = 8
NA_WIN_ROWS = 8
NA_WIN_COLS = 16
DIL_GROUPS = ((128, 1), (512, 4), (2048, 16))
DIL_HEADS_PER_GROUP = 4
N_DIL_GROUPS = len(DIL_GROUPS)
NA_WIDTH = NA_HEADS * HEAD_DIM
DIL_WIDTH = N_DIL_GROUPS * DIL_HEADS_PER_GROUP * HEAD_DIM
DIL_OUT_WIDTH = DIL_HEADS_PER_GROUP * HEAD_DIM
QKV_COLS = 3 * (NA_WIDTH + DIL_WIDTH)
N_EXPERTS = 64
TOP_K = 8
N_EXPERT_GROUPS = 8
TOP_GROUPS = 4
EXPERT_DIM = 256
ROUTED_SCALE = 2.5
ALIBI_MAX = 8.0
EPS = 1e-6
NEG_INF = -1e30

LANES = 128
HEADS_PER_LANE_TILE = LANES // HEAD_DIM
DIL_BLOCK = 64
VMEM_LIMIT_BYTES = 56 * 1024 * 1024

SC_CORES = 2
SC_SUBCORES = 16
SC_WORKERS = SC_CORES * SC_SUBCORES
SC_CHUNK = 64
EXPERT_ROW_BLOCK = 1024
FFN_SUB_BLOCKS = 4

F32 = jnp.float32
BF16 = jnp.bfloat16

SH1, SC1, GT1, SH2, SC2, GT2 = range(6)


def _params(sem):
    return pltpu.CompilerParams(dimension_semantics=sem, vmem_limit_bytes=VMEM_LIMIT_BYTES)


def _modulated_norm(x, g, scale, shift):
    r = lax.rsqrt(jnp.mean(x * x, axis=-1, keepdims=True) + EPS)
    return (x * r * g) * (1.0 + scale) + shift


def _ada_kernel(c_ref, w_ref, b_ref, o_ref):
    c = c_ref[...]
    act = c * jax.nn.sigmoid(c)
    o_ref[0] = jnp.dot(act, w_ref[0], preferred_element_type=F32,
                       precision=lax.Precision.HIGHEST) + b_ref[0]


def _ada(c, w_ada, b_ada):
    depth, d, six_d = w_ada.shape
    b = c.shape[0]
    tn = d
    return pl.pallas_call(
        _ada_kernel,
        grid=(depth, six_d // tn),
        in_specs=[
            pl.BlockSpec((b, d), lambda l, j: (0, 0)),
            pl.BlockSpec((1, d, tn), lambda l, j: (l, 0, j)),
            pl.BlockSpec((1, 1, tn), lambda l, j: (l, 0, j)),
        ],
        out_specs=pl.BlockSpec((1, b, tn), lambda l, j: (l, 0, j)),
        out_shape=jax.ShapeDtypeStruct((depth, b, six_d), F32),
        compiler_params=_params(("arbitrary", "arbitrary")),
        name="ada_mod",
    )(c, w_ada, b_ada.reshape(depth, 1, six_d))


def _inproj_kernel(x_ref, mod_ref, g_ref, w_ref, tok_ref, *rest, dilations):
    res_refs, acc_refs = rest[:len(dilations)], rest[len(dilations):]
    tm = x_ref.shape[0]
    tok_cols = tok_ref.shape[1]
    tn = (w_ref.shape[1] - tok_cols) // len(dilations)
    h = _modulated_norm(x_ref[...], g_ref[...], mod_ref[0, SC1:SC1 + 1, :], mod_ref[0, SH1:SH1 + 1, :]).astype(BF16)

    for c0 in range(0, tok_cols, tn):
        tok_ref[:, c0:c0 + tn] = jnp.dot(h, w_ref[:, c0:c0 + tn], preferred_element_type=F32).astype(BF16)

    for g, (res_ref, acc_ref, dilation) in enumerate(zip(res_refs, acc_refs, dilations)):
        c0 = tok_cols + g * tn
        res = jnp.dot(h, w_ref[:, c0:c0 + tn], preferred_element_type=F32)
        for c in range(acc_ref.shape[0]):
            acc_ref[c] = res[:, c * LANES:(c + 1) * LANES]
        for r in range(dilation):
            for c in range(acc_ref.shape[0]):
                res_ref[0, r, :, c * LANES:(c + 1) * LANES] = (
                    acc_ref[c, pl.ds(r, tm // dilation, stride=dilation), :].astype(BF16))


def _inproj(x2, mod, g, w_qkv, batch, seq, tm=512):
    n, d = x2.shape
    tn = 3 * DIL_OUT_WIDTH
    dilations = tuple(dil for _, dil in DIL_GROUPS if dil > 1)
    tok_cols = w_qkv.shape[1] - tn * len(dilations)
    blocks_per_batch = seq // tm
    res_specs = [pl.BlockSpec((1, dil, tm // dil, tn),
                              lambda i: (i // blocks_per_batch, 0, i % blocks_per_batch, 0)) for dil in dilations]
    res_shapes = [jax.ShapeDtypeStruct((batch, dil, seq // dil, tn), BF16) for dil in dilations]
    return pl.pallas_call(
        functools.partial(_inproj_kernel, dilations=dilations),
        grid=(n // tm,),
        in_specs=[
            pl.BlockSpec((tm, d), lambda i: (i, 0)),
            pl.BlockSpec((1, 6, d), lambda i: (i // blocks_per_batch, 0, 0)),
            pl.BlockSpec((1, d), lambda i: (0, 0)),
            pl.BlockSpec(w_qkv.shape, lambda i: (0, 0)),
        ],
        out_specs=[pl.BlockSpec((tm, tok_cols), lambda i: (i, 0))] + res_specs,
        out_shape=[jax.ShapeDtypeStruct((n, tok_cols), BF16)] + res_shapes,
        scratch_shapes=[pltpu.VMEM((tn // LANES, tm, LANES), F32) for _ in dilations],
        compiler_params=_params(("arbitrary",)),
        name="in_proj",
    )(x2, mod, g, w_qkv)


def _na_bias_table(rpb):
    heads = rpb.shape[0]
    cols = np.arange(GRID_W)
    col_start = np.clip(cols - NA_WIN_COLS // 2, 0, GRID_W - NA_WIN_COLS)
    col_mask = (cols[None, :] >= col_start[:, None]) & (cols[None, :] < col_start[:, None] + NA_WIN_COLS)
    dc = np.clip(cols[None, :] - cols[:, None], -(NA_WIN_COLS - 1), NA_WIN_COLS - 1) + NA_WIN_COLS - 1
    rpb_cols = rpb[:, :, dc].astype(F32)
    t = jnp.stack([rpb_cols[:, NA_WIN_ROWS - 1 - off:2 * NA_WIN_ROWS - 1 - off] for off in range(NA_WIN_ROWS)],
                  axis=1)
    t = t.transpose(0, 1, 3, 2, 4)
    t = jnp.where(col_mask[:, None, :], t, NEG_INF)
    t = t.reshape(heads // HEADS_PER_LANE_TILE, HEADS_PER_LANE_TILE, NA_WIN_ROWS, GRID_W, NA_WIN_ROWS * GRID_W)
    return t.transpose(0, 2, 1, 3, 4).reshape(heads // HEADS_PER_LANE_TILE, NA_WIN_ROWS,
                                              HEADS_PER_LANE_TILE * GRID_W, NA_WIN_ROWS * GRID_W)


def _stack_heads(q, low):
    scaled = q * (HEAD_DIM ** -0.5)
    zero = jnp.zeros_like(scaled)
    return jnp.concatenate([jnp.where(low, scaled, zero), jnp.where(low, zero, scaled)], axis=0)


def _stacked_attention(items):
    scores = [lax.dot_general(q2, kw, (((1,), (1,)), ((), ())), preferred_element_type=F32) + bias
              for q2, kw, _, bias in items]
    probs = []
    for s in scores:
        m = jnp.max(s, axis=-1, keepdims=True)
        p = jnp.exp(s - m)
        probs.append((p.astype(BF16), m, jnp.sum(p, axis=-1, keepdims=True)))
    return [(jnp.dot(p, vw, preferred_element_type=F32) / z, m, z)
            for (p, m, z), (_, _, vw, _) in zip(probs, items)]


def _unstack_heads(a, low):
    half = a.shape[0] // HEADS_PER_LANE_TILE
    return jnp.where(low, a[:half], a[half:])


def _na_kernel(q_ref, k_ref, v_ref, bias_ref, o_ref, *, rows, rows_per_step):
    kr = NA_WIN_ROWS
    low = lax.broadcasted_iota(jnp.int32, (GRID_W, LANES), 1) < HEAD_DIM

    def body(i, carry):
        items, qrows = [], []
        for u in range(rows_per_step):
            r = i * rows_per_step + u
            rs = jnp.clip(r - kr // 2, 0, rows - kr)
            qrows.append(pl.ds(pl.multiple_of(r * GRID_W, GRID_W), GRID_W))
            wrows = pl.ds(pl.multiple_of(rs * GRID_W, GRID_W), kr * GRID_W)
            items.append((_stack_heads(q_ref[qrows[-1], :], low), k_ref[wrows, :], v_ref[wrows, :],
                          bias_ref[0, r - rs]))
        for rows_u, (o, _, _) in zip(qrows, _stacked_attention(items)):
            o_ref[rows_u, :] = _unstack_heads(o, low).astype(o_ref.dtype)
        return carry

    lax.fori_loop(0, rows // rows_per_step, body, 0)


def _neighbourhood_attention(qkv, bias, batch, seq, rows_per_step=16):
    n = qkv.shape[0]
    rows = seq // GRID_W
    pairs = NA_WIDTH // LANES
    return pl.pallas_call(
        functools.partial(_na_kernel, rows=rows, rows_per_step=rows_per_step),
        grid=(pairs, batch),
        in_specs=[
            pl.BlockSpec((seq, LANES), lambda p, b: (b, p)),
            pl.BlockSpec((seq, LANES), lambda p, b: (b, pairs + p)),
            pl.BlockSpec((seq, LANES), lambda p, b: (b, 2 * pairs + p)),
            pl.BlockSpec((1, NA_WIN_ROWS, HEADS_PER_LANE_TILE * GRID_W, NA_WIN_ROWS * GRID_W),
                         lambda p, b: (p, 0, 0, 0)),
        ],
        out_specs=pl.BlockSpec((seq, LANES), lambda p, b: (b, p)),
        out_shape=jax.ShapeDtypeStruct((n, NA_WIDTH), BF16),
        compiler_params=_params(("arbitrary", "arbitrary")),
        name="na_attn",
    )(qkv, qkv, qkv, bias)


def _alibi_slopes():
    n = N_DIL_GROUPS * DIL_HEADS_PER_GROUP
    s = np.exp2(-ALIBI_MAX * np.arange(1, n + 1, dtype=np.float64) / n).astype(np.float32)
    return s.reshape(N_DIL_GROUPS, DIL_HEADS_PER_GROUP)


def _dil_bias_table(group):
    blk = DIL_BLOCK
    dilation = DIL_GROUPS[group][1]
    slopes = _alibi_slopes()[group]
    qi = np.arange(blk)[:, None]
    kj = np.arange(3 * blk)[None, :]
    tables = []
    for shift in range(3):
        arel = np.abs(kj - qi - shift * blk)
        dist = (dilation * arel).astype(np.float32)
        per_head = [np.where(arel <= blk, -slopes[h] * dist, np.float32(NEG_INF)) for h in range(DIL_HEADS_PER_GROUP)]
        tables.append(np.stack(per_head))
    t = np.stack(tables, axis=1).astype(np.float32)
    pairs = DIL_HEADS_PER_GROUP // HEADS_PER_LANE_TILE
    t = t.reshape(pairs, HEADS_PER_LANE_TILE, 3, blk, 3 * blk).transpose(0, 2, 1, 3, 4)
    return t.reshape(pairs, 3, HEADS_PER_LANE_TILE * blk, 3 * blk)


def _dil_kernel(q_ref, k_ref, v_ref, bias_ref, o_ref, lse_ref, *, nb, blocks_per_step):
    blk = DIL_BLOCK
    win = 3 * blk
    low = lax.broadcasted_iota(jnp.int32, (blk, LANES), 1) < HEAD_DIM
    dilation = q_ref.shape[0]
    steps = dilation * nb

    def token_rows(sq, n):
        if dilation == 1:
            return pl.ds(pl.multiple_of(n * blk, blk), blk)
        return pl.ds(n * (blk * dilation) + sq, blk, stride=dilation)

    def body(i, carry):
        items, dst = [], []
        for u in range(blocks_per_step):
            t = i * blocks_per_step + u
            sq = t // nb
            n = t % nb
            wb = jnp.clip(n - 1, 0, nb - 3)
            qrows = pl.ds(pl.multiple_of(n * blk, blk), blk)
            wrows = pl.ds(pl.multiple_of(wb * blk, blk), win)
            dst.append(token_rows(sq, n))
            items.append((_stack_heads(q_ref[sq, qrows, :], low), k_ref[sq, wrows, :], v_ref[sq, wrows, :],
                          bias_ref[0, n - wb]))
        for rows, (o, m, z) in zip(dst, _stacked_attention(items)):
            o_ref[rows, :] = _unstack_heads(o, low)
            lse_ref[rows, :] = _unstack_heads(jnp.broadcast_to(m + jnp.log(z), o.shape), low)
        return carry

    lax.fori_loop(0, steps // blocks_per_step, body, 0)


def _dilated_attention(qkv4, col_offsets, group, blocks_per_step=16):
    batch, dilation, length, _ = qkv4.shape
    seq = dilation * length
    nb = length // DIL_BLOCK
    pairs = DIL_OUT_WIDTH // LANES
    qo, ko, vo = col_offsets
    bias = jnp.asarray(_dil_bias_table(group))
    out = jax.ShapeDtypeStruct((batch * seq, DIL_OUT_WIDTH), F32)
    seqs = lambda off: pl.BlockSpec((None, dilation, length, LANES), lambda p, b: (b, 0, 0, off + p))
    return pl.pallas_call(
        functools.partial(_dil_kernel, nb=nb, blocks_per_step=blocks_per_step),
        grid=(pairs, batch),
        in_specs=[seqs(qo), seqs(ko), seqs(vo),
                  pl.BlockSpec((1,) + bias.shape[1:], lambda p, b: (p, 0, 0, 0))],
        out_specs=[pl.BlockSpec((seq, LANES), lambda p, b: (b, p))] * 2,
        out_shape=[out, out],
        compiler_params=_params(("arbitrary", "arbitrary")),
        name=f"dil_attn_g{group}",
    )(qkv4, qkv4, qkv4, bias)


def _outproj_kernel(x_ref, oa_ref, o0_ref, o1_ref, o2_ref, l0_ref, l1_ref, l2_ref, mod_ref, g_ref,
                    wg_ref, bg_ref, wpa_ref, wpb_ref, wo_ref, out_ref):
    d = x_ref.shape[1]
    x = x_ref[...]
    h = _modulated_norm(x, g_ref[...], mod_ref[0, SC1:SC1 + 1, :], mod_ref[0, SH1:SH1 + 1, :]).astype(BF16)

    lses = [l0_ref[...], l1_ref[...], l2_ref[...]]
    outs = [o0_ref[...], o1_ref[...], o2_ref[...]]
    top = jnp.maximum(jnp.maximum(lses[0], lses[1]), lses[2])
    es = [jnp.exp(l - top) for l in lses]
    den = es[0] + es[1] + es[2]
    ob = (es[0] * outs[0] + es[1] * outs[1] + es[2] * outs[2]) / den

    ya = jnp.dot(oa_ref[...], wpa_ref[...], preferred_element_type=F32)
    yb = jnp.dot(ob.astype(BF16), wpb_ref[...], preferred_element_type=F32)
    ga = jax.nn.sigmoid(jnp.dot(h, wg_ref[:, :d], preferred_element_type=F32) + bg_ref[:, :d])
    mix = ga * ya
    gb = jax.nn.sigmoid(jnp.dot(h, wg_ref[:, d:], preferred_element_type=F32) + bg_ref[:, d:])
    mix = mix + gb * yb
    y = jnp.dot(mix.astype(BF16), wo_ref[...], preferred_element_type=F32)
    out_ref[...] = x + mod_ref[0, GT1:GT1 + 1, :] * y


def _outproj(x2, oa, o_groups, lse_groups, mod, g, w_gate, b_gate, w_pa, w_pb, w_o, seq, tm=512):
    n, d = x2.shape
    blocks_per_batch = seq // tm
    row = lambda c: pl.BlockSpec((tm, c), lambda i: (i, 0))
    full = lambda a: pl.BlockSpec(a.shape, lambda i: (0,) * a.ndim)
    return pl.pallas_call(
        _outproj_kernel,
        grid=(n // tm,),
        in_specs=[row(d), row(NA_WIDTH)] + [row(DIL_OUT_WIDTH)] * 6 + [
            pl.BlockSpec((1, 6, d), lambda i: (i // blocks_per_batch, 0, 0)),
            full(g), full(w_gate), full(b_gate), full(w_pa), full(w_pb), full(w_o),
        ],
        out_specs=row(d),
        out_shape=jax.ShapeDtypeStruct((n, d), F32),
        compiler_params=_params(("arbitrary",)),
        name="out_proj",
    )(x2, oa, *o_groups, *lse_groups, mod, g, w_gate, b_gate, w_pa, w_pb, w_o)


def _first_index_of_max(cur, idx, size):
    m = jnp.max(cur, axis=0, keepdims=True)
    first = jnp.min(jnp.where(cur == m, idx, size), axis=0, keepdims=True)
    return m, first


def _route_transposed(logits_t, e_bias):
    tokens = logits_t.shape[1]
    per_group = N_EXPERTS // N_EXPERT_GROUPS
    scores = jax.nn.sigmoid(logits_t)
    biased = scores + e_bias
    midx = lax.broadcasted_iota(jnp.int32, (per_group, tokens), 0)
    grp_scores = []
    for g in range(N_EXPERT_GROUPS):
        vals = biased[g * per_group:(g + 1) * per_group, :]
        m1, first = _first_index_of_max(vals, midx, per_group)
        m2 = jnp.max(jnp.where(midx == first, -jnp.inf, vals), axis=0, keepdims=True)
        grp_scores.append(m1 + m2)
    cur = jnp.concatenate(grp_scores, axis=0)
    gidx = lax.broadcasted_iota(jnp.int32, (N_EXPERT_GROUPS, tokens), 0)
    grp_sel = jnp.zeros((N_EXPERT_GROUPS, tokens), jnp.bool_)
    for _ in range(TOP_GROUPS):
        _, first = _first_index_of_max(cur, gidx, N_EXPERT_GROUPS)
        pick = gidx == first
        grp_sel = jnp.logical_or(grp_sel, pick)
        cur = jnp.where(pick, -jnp.inf, cur)
    rows = []
    for g in range(N_EXPERT_GROUPS):
        vals = biased[g * per_group:(g + 1) * per_group, :]
        rows.append(jnp.where(grp_sel[g:g + 1, :], vals, NEG_INF))
    cur = jnp.concatenate(rows, axis=0)
    eidx = lax.broadcasted_iota(jnp.int32, (N_EXPERTS, tokens), 0)
    firsts, picks, weights = [], [], []
    for _ in range(TOP_K):
        _, first = _first_index_of_max(cur, eidx, N_EXPERTS)
        pick = eidx == first
        firsts.append(first)
        picks.append(pick)
        weights.append(jnp.sum(jnp.where(pick, scores, 0.0), axis=0, keepdims=True))
        cur = jnp.where(pick, -jnp.inf, cur)
    total = functools.reduce(lambda a, b: a + b, weights)
    return firsts, picks, [w / total * ROUTED_SCALE for w in weights]


def _pack_halves(a):
    half = a.shape[1] // 2
    bits = lax.bitcast_convert_type(a.astype(BF16).astype(F32), jnp.int32)
    return lax.shift_right_logical(bits[:, :half], 16) | bits[:, half:]


def _unpack_halves(w):
    low = lax.bitcast_convert_type(lax.shift_left(w, 16), F32)
    high = lax.bitcast_convert_type(w & jnp.int32(-65536), F32)
    return low, high


def _router_kernel(x_ref, mod_ref, g_ref, wr_ref, eb_ref, tri_ref, hp_ref, eidx_ref, rank_ref, w_ref, cnt_ref):
    tm = x_ref.shape[0]

    @pl.when(pl.program_id(0) == 0)
    def _():
        cnt_ref[...] = jnp.zeros_like(cnt_ref)

    h = _modulated_norm(x_ref[...], g_ref[...], mod_ref[0, SC2:SC2 + 1, :], mod_ref[0, SH2:SH2 + 1, :])
    hp_ref[...] = _pack_halves(h)
    h_hi = h.astype(BF16)
    h_lo = (h - h_hi.astype(F32)).astype(BF16)
    nt = (((1,), (1,)), ((), ()))
    logits_t = (lax.dot_general(wr_ref[0], h_hi, nt, preferred_element_type=F32)
                + lax.dot_general(wr_ref[0], h_lo, nt, preferred_element_type=F32)
                + lax.dot_general(wr_ref[1], h_hi, nt, preferred_element_type=F32))
    firsts, picks, weights = _route_transposed(logits_t, eb_ref[...])
    sel = functools.reduce(jnp.logical_or, picks)
    sel_f = jnp.where(sel, 1.0, 0.0)
    incl = jnp.dot(sel_f.astype(BF16), tri_ref[...], preferred_element_type=F32)
    before = cnt_ref[:, 0:1] + incl - sel_f
    eidx_ref[...] = jnp.concatenate(firsts, axis=0)
    rank_ref[...] = jnp.concatenate(
        [jnp.sum(jnp.where(p, before, 0.0), axis=0, keepdims=True) for p in picks], axis=0).astype(jnp.int32)
    pad = jnp.concatenate(weights + [jnp.zeros((LANES - TOP_K, tm), F32)], axis=0)
    w_ref[...] = pad.T
    cnt_ref[...] = cnt_ref[...] + incl[:, tm - 1:tm]


def _router(x2, mod, g, w_router_t, e_bias, seq, tm=512):
    n, d = x2.shape
    blocks_per_batch = seq // tm
    tri = jnp.asarray(np.triu(np.ones((tm, tm), np.float32)), BF16)
    tok = lambda r: pl.BlockSpec((r, tm), lambda i: (0, i))
    const = lambda a: pl.BlockSpec(a.shape, lambda i: (0,) * a.ndim)
    return pl.pallas_call(
        _router_kernel,
        grid=(n // tm,),
        in_specs=[
            pl.BlockSpec((tm, d), lambda i: (i, 0)),
            pl.BlockSpec((1, 6, d), lambda i: (i // blocks_per_batch, 0, 0)),
            const(g), const(w_router_t), const(e_bias), const(tri),
        ],
        out_specs=[
            pl.BlockSpec((tm, d // 2), lambda i: (i, 0)),
            tok(TOP_K), tok(TOP_K),
            pl.BlockSpec((tm, LANES), lambda i: (i, 0)),
            pl.BlockSpec((N_EXPERTS, LANES), lambda i: (0, 0)),
        ],
        out_shape=[
            jax.ShapeDtypeStruct((n, d // 2), jnp.int32),
            jax.ShapeDtypeStruct((TOP_K, n), jnp.int32),
            jax.ShapeDtypeStruct((TOP_K, n), jnp.int32),
            jax.ShapeDtypeStruct((n, LANES), F32),
            jax.ShapeDtypeStruct((N_EXPERTS, LANES), F32),
        ],
        compiler_params=_params(("arbitrary",)),
        name="moe_router",
    )(x2, mod, g, w_router_t, e_bias, tri)


def _sc_worker_id():
    return lax.axis_index("subcore") * SC_CORES + lax.axis_index("core")


def _sc_scatter_rows(src, idx3, n_out):
    n, w = src.shape
    per_worker = n // SC_CHUNK // SC_WORKERS
    mesh = plsc.VectorSubcoreMesh(core_axis_name="core", subcore_axis_name="subcore")

    @functools.partial(
        pl.kernel, mesh=mesh, out_type=jax.ShapeDtypeStruct((n_out, w), src.dtype), name="moe_dispatch",
        scratch_types=[pltpu.VMEM((2, TOP_K, SC_CHUNK), jnp.int32), pltpu.VMEM((2, SC_CHUNK, w), src.dtype),
                       pltpu.SemaphoreType.DMA((2,)), pltpu.SemaphoreType.DMA((2,)), pltpu.SemaphoreType.DMA])
    def scatter(src_hbm, idx_hbm, out_hbm, idx_v, rows_v, idx_sem, row_sem, out_sem):
        first = _sc_worker_id() * per_worker

        def loads(chunk, slot):
            return (pltpu.make_async_copy(idx_hbm.at[chunk], idx_v.at[slot], idx_sem.at[slot]),
                    pltpu.make_async_copy(src_hbm.at[pl.ds(chunk * SC_CHUNK, SC_CHUNK)], rows_v.at[slot],
                                          row_sem.at[slot]))

        for cp in loads(first, 0):
            cp.start()

        @pl.loop(0, per_worker, step=2)
        def _(i):
            for slot in range(2):
                chunk = first + i + slot
                for cp in loads(chunk, slot):
                    cp.wait()

                @pl.when(i + slot + 1 < per_worker)
                def _():
                    for cp in loads(chunk + 1, 1 - slot):
                        cp.start()

                copies = [pltpu.make_async_copy(rows_v.at[slot], out_hbm.at[idx_v.at[slot, k]], out_sem)
                          for k in range(TOP_K)]
                for cp in copies:
                    cp.start()
                for cp in copies:
                    cp.wait()

    return scatter(src, idx3)


def _sc_gather_rows(src, idx3):
    _, w = src.shape
    chunks = idx3.shape[0]
    per_worker = chunks // SC_WORKERS
    mesh = plsc.VectorSubcoreMesh(core_axis_name="core", subcore_axis_name="subcore")

    @functools.partial(
        pl.kernel, mesh=mesh, out_type=jax.ShapeDtypeStruct((TOP_K, chunks * SC_CHUNK, w), src.dtype),
        name="moe_collect",
        scratch_types=[pltpu.VMEM((TOP_K, SC_CHUNK), jnp.int32), pltpu.VMEM((2, SC_CHUNK, w), src.dtype),
                       pltpu.SemaphoreType.DMA((2,)), pltpu.SemaphoreType.DMA((2,))])
    def gather(src_hbm, idx_hbm, out_hbm, idx_v, rows_v, in_sem, out_sem):
        first = _sc_worker_id() * per_worker

        @pl.loop(0, per_worker)
        def _(i):
            chunk = first + i
            pltpu.sync_copy(idx_hbm.at[chunk], idx_v)
            reads = [pltpu.make_async_copy(src_hbm.at[idx_v.at[k]], rows_v.at[k % 2], in_sem.at[k % 2])
                     for k in range(TOP_K)]
            writes = [pltpu.make_async_copy(rows_v.at[k % 2], out_hbm.at[k, pl.ds(chunk * SC_CHUNK, SC_CHUNK)],
                                            out_sem.at[k % 2]) for k in range(TOP_K)]
            reads[0].start()
            for k in range(TOP_K):
                if k + 1 < TOP_K:
                    if k >= 1:
                        writes[k - 1].wait()
                    reads[k + 1].start()
                reads[k].wait()
                writes[k].start()
            writes[TOP_K - 2].wait()
            writes[TOP_K - 1].wait()

    return gather(src, idx3)


def _swiglu(x, w_gate_up, w_down):
    gu = jnp.dot(x, w_gate_up, preferred_element_type=F32)
    gate = gu[:, :EXPERT_DIM]
    act = (gate * jax.nn.sigmoid(gate)) * gu[:, EXPERT_DIM:]
    return jnp.dot(act.astype(BF16), w_down, preferred_element_type=F32)


def _expert_ffn_kernel(be_ref, nv_ref, xs_ref, wg_ref, wu_ref, wd_ref, ys_ref, wgu_bf, wd_bf):
    i = pl.program_id(0)
    nvalid = nv_ref[i]

    @pl.when(jnp.logical_or(i == 0, be_ref[i] != be_ref[jnp.maximum(i - 1, 0)]))
    def _():
        wgu_bf[:, :EXPERT_DIM] = wg_ref[0, 0].astype(BF16)
        wgu_bf[:, EXPERT_DIM:] = wu_ref[0, 0].astype(BF16)
        wd_bf[...] = wd_ref[0, 0].astype(BF16)

    tb = xs_ref.shape[0]
    sub = tb // FFN_SUB_BLOCKS
    spans = [pl.ds(s * sub, sub) for s in range(FFN_SUB_BLOCKS)]

    def ffn(packed_rows):
        xs = []
        for packed in packed_rows:
            low, high = _unpack_halves(packed)
            xs.append(jnp.concatenate([low, high], axis=1).astype(BF16))
        gus = [jnp.dot(x, wgu_bf[...], preferred_element_type=F32) for x in xs]
        acts = [((gu[:, :EXPERT_DIM] * jax.nn.sigmoid(gu[:, :EXPERT_DIM])) * gu[:, EXPERT_DIM:]).astype(BF16)
                for gu in gus]
        return [_pack_halves(jnp.dot(a, wd_bf[...], preferred_element_type=F32)) for a in acts]

    @pl.when(nvalid == tb)
    def _():
        for span, y in zip(spans, ffn([xs_ref[span, :] for span in spans])):
            ys_ref[span, :] = y

    for s, span in enumerate(spans):
        @pl.when(jnp.logical_and(nvalid < tb, nvalid > s * sub))
        def _():
            row = lax.broadcasted_iota(jnp.int32, (sub, xs_ref.shape[1]), 0)
            ys_ref[span, :] = ffn([jnp.where(row < nvalid - s * sub, xs_ref[span, :], 0)])[0]

        @pl.when(jnp.logical_and(nvalid > 0, nvalid <= s * sub))
        def _():
            ys_ref[span, :] = jnp.zeros((sub, ys_ref.shape[1]), ys_ref.dtype)

    @pl.when(nvalid == 0)
    def _():
        ys_ref[...] = jnp.zeros_like(ys_ref)


def _expert_ffn(xs, block_expert, block_valid, w_gate, w_up, w_down, layer, tb):
    p, half = xs.shape
    d = 2 * half
    grid_spec = pltpu.PrefetchScalarGridSpec(
        num_scalar_prefetch=2,
        grid=(p // tb,),
        in_specs=[
            pl.BlockSpec((tb, half), lambda i, be, nv: (i, 0)),
            pl.BlockSpec((1, 1, d, EXPERT_DIM), lambda i, be, nv: (layer, be[i], 0, 0)),
            pl.BlockSpec((1, 1, d, EXPERT_DIM), lambda i, be, nv: (layer, be[i], 0, 0)),
            pl.BlockSpec((1, 1, EXPERT_DIM, d), lambda i, be, nv: (layer, be[i], 0, 0)),
        ],
        out_specs=pl.BlockSpec((tb, half), lambda i, be, nv: (i, 0)),
        scratch_shapes=[pltpu.VMEM((d, 2 * EXPERT_DIM), BF16), pltpu.VMEM((EXPERT_DIM, d), BF16)],
    )
    return pl.pallas_call(
        _expert_ffn_kernel,
        grid_spec=grid_spec,
        out_shape=jax.ShapeDtypeStruct((p, half), jnp.int32),
        compiler_params=_params(("arbitrary",)),
        name="moe_expert_ffn",
    )(block_expert, block_valid, xs, w_gate, w_up, w_down)


def _combine_kernel(x_ref, hp_ref, yg_ref, w_ref, mod_ref, wgu_ref, wd_ref, gf_ref, o_ref, *, final_norm):
    low, high = _unpack_halves(hp_ref[...])
    h = jnp.concatenate([low, high], axis=1).astype(BF16)
    shared = _swiglu(h, wgu_ref[...], wd_ref[...])
    half = hp_ref.shape[1]
    acc_low, acc_high = shared[:, :half], shared[:, half:]
    w = w_ref[...]
    for k in range(TOP_K):
        low, high = _unpack_halves(yg_ref[k])
        wk = w[:, k:k + 1]
        acc_low = acc_low + wk * low
        acc_high = acc_high + wk * high
    y = jnp.concatenate([acc_low, acc_high], axis=1)
    out = x_ref[...] + mod_ref[0, GT2:GT2 + 1, :] * y
    if final_norm:
        out = out * lax.rsqrt(jnp.mean(out * out, axis=-1, keepdims=True) + EPS) * gf_ref[...]
    o_ref[...] = out


def _combine(x2, hp, yg, w, mod, ws_gate_up, ws_down, g_final, seq, final_norm, tm=512):
    n, d = x2.shape
    blocks_per_batch = seq // tm
    const = lambda a: pl.BlockSpec(a.shape, lambda i: (0,) * a.ndim)
    return pl.pallas_call(
        functools.partial(_combine_kernel, final_norm=final_norm),
        grid=(n // tm,),
        in_specs=[
            pl.BlockSpec((tm, d), lambda i: (i, 0)),
            pl.BlockSpec((tm, d // 2), lambda i: (i, 0)),
            pl.BlockSpec((TOP_K, tm, d // 2), lambda i: (0, i, 0)),
            pl.BlockSpec((tm, LANES), lambda i: (i, 0)),
            pl.BlockSpec((1, 6, d), lambda i: (i // blocks_per_batch, 0, 0)),
            const(ws_gate_up), const(ws_down), const(g_final),
        ],
        out_specs=pl.BlockSpec((tm, d), lambda i: (i, 0)),
        out_shape=jax.ShapeDtypeStruct((n, d), F32),
        compiler_params=_params(("arbitrary",)),
        name="moe_combine",
    )(x2, hp, yg, w, mod, ws_gate_up, ws_down, g_final)


def _token_mixer(x2, mod, g_mix, w_in, b_gate, rpb, w_pa, w_pb, w_o, batch, seq):
    d = x2.shape[1]
    dil0 = 3 * NA_WIDTH
    group_cols = [[dil0 + part * DIL_WIDTH + grp * DIL_OUT_WIDTH for part in range(3)]
                  for grp in range(N_DIL_GROUPS)]
    order = sorted(range(N_DIL_GROUPS), key=lambda grp: DIL_GROUPS[grp][1] > 1)
    w_qkv = jnp.concatenate(
        [w_in[:, :dil0]] + [w_in[:, c:c + DIL_OUT_WIDTH] for grp in order for c in group_cols[grp]],
        axis=1).astype(BF16)
    tok, *residue = _inproj(x2, mod, g_mix.reshape(1, d), w_qkv, batch, seq)
    o_a = _neighbourhood_attention(tok, _na_bias_table(rpb), batch, seq)

    qkv_offsets = tuple(part * DIL_OUT_WIDTH // LANES for part in range(3))
    o_groups, lse_groups = [None] * N_DIL_GROUPS, [None] * N_DIL_GROUPS
    residue = iter(residue)
    tok_offset = dil0 // LANES
    for grp in order:
        if DIL_GROUPS[grp][1] == 1:
            seqs = tok.reshape(batch, 1, seq, tok.shape[1])
            offsets = tuple(tok_offset + o for o in qkv_offsets)
            tok_offset += 3 * DIL_OUT_WIDTH // LANES
        else:
            seqs, offsets = next(residue), qkv_offsets
        o_groups[grp], lse_groups[grp] = _dilated_attention(seqs, offsets, grp)

    return _outproj(x2, o_a, o_groups, lse_groups, mod, g_mix.reshape(1, d),
                    w_in[:, QKV_COLS:].astype(BF16), b_gate.reshape(1, -1),
                    w_pa.astype(BF16), w_pb.astype(BF16), w_o.astype(BF16), seq)


def _dispatch_plan(eidx_t, rank_t, counts, tb):
    n = eidx_t.shape[1]
    n_blocks = -(-(n * TOP_K + N_EXPERTS * (tb - 1)) // tb)
    padded = (counts + tb - 1) // tb * tb
    seg_end = jnp.cumsum(padded)
    seg_start = seg_end - padded
    experts = jnp.arange(N_EXPERTS, dtype=jnp.int32)

    def lookup(table, idx):
        sel = idx[None] == experts.reshape((N_EXPERTS,) + (1,) * idx.ndim)
        return jnp.sum(jnp.where(sel, table.reshape((N_EXPERTS,) + (1,) * idx.ndim), 0), axis=0)

    dest_t = lookup(seg_start, eidx_t) + rank_t
    idx3 = dest_t.reshape(TOP_K, n // SC_CHUNK, SC_CHUNK).transpose(1, 0, 2)
    block_start = jnp.arange(n_blocks, dtype=jnp.int32) * tb
    block_expert = jnp.sum((seg_end[:, None] <= block_start[None, :]).astype(jnp.int32), axis=0)
    block_expert = jnp.minimum(block_expert, N_EXPERTS - 1)
    block_valid = jnp.clip(lookup(counts, block_expert) - (block_start - lookup(seg_start, block_expert)), 0, tb)
    return idx3, block_expert, block_valid.astype(jnp.int32), n_blocks


def _moe_layer(x2, mod, g_ffn, w_router, e_bias, we_gate, we_up, we_down, layer, ws_gate, ws_up, ws_down, g_final,
               seq, final_norm, tb=EXPERT_ROW_BLOCK):
    d = x2.shape[1]
    wr_t = w_router.T
    wr_hi = wr_t.astype(BF16)
    wr_split = jnp.stack([wr_hi, (wr_t - wr_hi.astype(F32)).astype(BF16)])
    hp, eidx_t, rank_t, w, cnt = _router(x2, mod, g_ffn.reshape(1, d), wr_split, e_bias.reshape(-1, 1), seq)
    counts = cnt[:, 0].astype(jnp.int32)
    idx3, block_expert, block_valid, n_blocks = _dispatch_plan(eidx_t, rank_t, counts, tb)
    xs = _sc_scatter_rows(hp, idx3, n_blocks * tb)
    ys = _expert_ffn(xs, block_expert, block_valid, we_gate, we_up, we_down, layer, tb)
    yg = _sc_gather_rows(ys, idx3)
    return _combine(x2, hp, yg, w, mod, jnp.concatenate([ws_gate, ws_up], axis=-1).astype(BF16),
                    ws_down.astype(BF16), g_final.reshape(1, d), seq, final_norm)


def kernel(x, c, w_ada, b_ada, g_mix, w_in, b_gate, rpb, w_pa, w_pb, w_o, g_ffn, w_router, e_bias,
           we_gate, we_up, we_down, ws_gate, ws_up, ws_down, g_final):
    batch, seq, d = x.shape
    depth = w_ada.shape[0]
    mods = _ada(c, w_ada, b_ada).reshape(depth, batch, 6, d)
    x2 = x.reshape(batch * seq, d)
    for l in range(depth):
        x2 = _token_mixer(x2, mods[l], g_mix[l], w_in[l], b_gate[l], rpb[l], w_pa[l], w_pb[l], w_o[l],
                          batch, seq)
        x2 = _moe_layer(x2, mods[l], g_ffn[l], w_router[l], e_bias[l], we_gate, we_up, we_down, l,
                        ws_gate[l], ws_up[l], ws_down[l], g_final, seq, final_norm=(l == depth - 1))
    return x2.reshape(batch, seq, d)
```

```python
import functools

import numpy as np
import jax
import jax.numpy as jnp
from jax import lax
from jax.experimental import pallas as pl
from jax.experimental.pallas import tpu as pltpu
from jax.experimental.pallas import tpu_sc as plsc

HEAD_DIM = 64
GRID_W = 64
NA_HEADS = 8
NA_WIN_ROWS = 8
NA_WIN_COLS = 16
DIL_GROUPS = ((128, 1), (512, 4), (2048, 16))
DIL_HEADS_PER_GROUP = 4
N_DIL_GROUPS = len(DIL_GROUPS)
NA_WIDTH = NA_HEADS * HEAD_DIM
DIL_WIDTH = N_DIL_GROUPS * DIL_HEADS_PER_GROUP * HEAD_DIM
DIL_OUT_WIDTH = DIL_HEADS_PER_GROUP * HEAD_DIM
QKV_COLS = 3 * (NA_WIDTH + DIL_WIDTH)
N_EXPERTS = 64
TOP_K = 8
N_EXPERT_GROUPS = 8
TOP_GROUPS = 4
EXPERT_DIM = 256
ROUTED_SCALE = 2.5
ALIBI_MAX = 8.0
EPS = 1e-6
NEG_INF = -1e30

LANES = 128
HEADS_PER_LANE_TILE = LANES // HEAD_DIM
DIL_BLOCK = 64
VMEM_LIMIT_BYTES = 56 * 1024 * 1024

SC_CORES = 2
SC_SUBCORES = 16
SC_WORKERS = SC_CORES * SC_SUBCORES
SC_CHUNK = 64
EXPERT_ROW_BLOCK = 1024
FFN_SUB_BLOCKS = 4
BATCH_PARTS = 2

F32 = jnp.float32
BF16 = jnp.bfloat16

SH1, SC1, GT1, SH2, SC2, GT2 = range(6)


def _params(sem):
    return pltpu.CompilerParams(dimension_semantics=sem, vmem_limit_bytes=VMEM_LIMIT_BYTES)


def _modulated_norm(x, g, scale, shift):
    r = lax.rsqrt(jnp.mean(x * x, axis=-1, keepdims=True) + EPS)
    return (x * r * g) * (1.0 + scale) + shift


def _ada_kernel(c_ref, w_ref, b_ref, o_ref):
    c = c_ref[...]
    act = c * jax.nn.sigmoid(c)
    o_ref[0] = jnp.dot(act, w_ref[0], preferred_element_type=F32,
                       precision=lax.Precision.HIGHEST) + b_ref[0]


def _ada(c, w_ada, b_ada):
    depth, d, six_d = w_ada.shape
    b = c.shape[0]
    tn = d
    return pl.pallas_call(
        _ada_kernel,
        grid=(depth, six_d // tn),
        in_specs=[
            pl.BlockSpec((b, d), lambda l, j: (0, 0)),
            pl.BlockSpec((1, d, tn), lambda l, j: (l, 0, j)),
            pl.BlockSpec((1, 1, tn), lambda l, j: (l, 0, j)),
        ],
        out_specs=pl.BlockSpec((1, b, tn), lambda l, j: (l, 0, j)),
        out_shape=jax.ShapeDtypeStruct((depth, b, six_d), F32),
        compiler_params=_params(("arbitrary", "arbitrary")),
        name="ada_mod",
    )(c, w_ada, b_ada.reshape(depth, 1, six_d))


def _inproj_kernel(x_ref, mod_ref, g_ref, w_ref, tok_ref, *rest, dilations):
    res_refs, acc_refs = rest[:len(dilations)], rest[len(dilations):]
    tm = x_ref.shape[0]
    tok_cols = tok_ref.shape[1]
    tn = (w_ref.shape[1] - tok_cols) // len(dilations)
    h = _modulated_norm(x_ref[...], g_ref[...], mod_ref[0, SC1:SC1 + 1, :], mod_ref[0, SH1:SH1 + 1, :]).astype(BF16)

    for c0 in range(0, tok_cols, tn):
        tok_ref[:, c0:c0 + tn] = jnp.dot(h, w_ref[:, c0:c0 + tn], preferred_element_type=F32).astype(BF16)

    for g, (res_ref, acc_ref, dilation) in enumerate(zip(res_refs, acc_refs, dilations)):
        c0 = tok_cols + g * tn
        res = jnp.dot(h, w_ref[:, c0:c0 + tn], preferred_element_type=F32)
        for c in range(acc_ref.shape[0]):
            acc_ref[c] = res[:, c * LANES:(c + 1) * LANES]
        for r in range(dilation):
            for c in range(acc_ref.shape[0]):
                res_ref[0, r, :, c * LANES:(c + 1) * LANES] = (
                    acc_ref[c, pl.ds(r, tm // dilation, stride=dilation), :].astype(BF16))


def _inproj(x2, mod, g, w_qkv, batch, seq, tm=512):
    n, d = x2.shape
    tn = 3 * DIL_OUT_WIDTH
    dilations = tuple(dil for _, dil in DIL_GROUPS if dil > 1)
    tok_cols = w_qkv.shape[1] - tn * len(dilations)
    blocks_per_batch = seq // tm
    res_specs = [pl.BlockSpec((1, dil, tm // dil, tn),
                              lambda i: (i // blocks_per_batch, 0, i % blocks_per_batch, 0)) for dil in dilations]
    res_shapes = [jax.ShapeDtypeStruct((batch, dil, seq // dil, tn), BF16) for dil in dilations]
    return pl.pallas_call(
        functools.partial(_inproj_kernel, dilations=dilations),
        grid=(n // tm,),
        in_specs=[
            pl.BlockSpec((tm, d), lambda i: (i, 0)),
            pl.BlockSpec((1, 6, d), lambda i: (i // blocks_per_batch, 0, 0)),
            pl.BlockSpec((1, d), lambda i: (0, 0)),
            pl.BlockSpec(w_qkv.shape, lambda i: (0, 0)),
        ],
        out_specs=[pl.BlockSpec((tm, tok_cols), lambda i: (i, 0))] + res_specs,
        out_shape=[jax.ShapeDtypeStruct((n, tok_cols), BF16)] + res_shapes,
        scratch_shapes=[pltpu.VMEM((tn // LANES, tm, LANES), F32) for _ in dilations],
        compiler_params=_params(("arbitrary",)),
        name="in_proj",
    )(x2, mod, g, w_qkv)


def _na_bias_table(rpb):
    heads = rpb.shape[0]
    cols = np.arange(GRID_W)
    col_start = np.clip(cols - NA_WIN_COLS // 2, 0, GRID_W - NA_WIN_COLS)
    col_mask = (cols[None, :] >= col_start[:, None]) & (cols[None, :] < col_start[:, None] + NA_WIN_COLS)
    dc = np.clip(cols[None, :] - cols[:, None], -(NA_WIN_COLS - 1), NA_WIN_COLS - 1) + NA_WIN_COLS - 1
    rpb_cols = rpb[:, :, dc].astype(F32)
    t = jnp.stack([rpb_cols[:, NA_WIN_ROWS - 1 - off:2 * NA_WIN_ROWS - 1 - off] for off in range(NA_WIN_ROWS)],
                  axis=1)
    t = t.transpose(0, 1, 3, 2, 4)
    t = jnp.where(col_mask[:, None, :], t, NEG_INF)
    t = t.reshape(heads // HEADS_PER_LANE_TILE, HEADS_PER_LANE_TILE, NA_WIN_ROWS, GRID_W, NA_WIN_ROWS * GRID_W)
    return t.transpose(0, 2, 1, 3, 4).reshape(heads // HEADS_PER_LANE_TILE, NA_WIN_ROWS,
                                              HEADS_PER_LANE_TILE * GRID_W, NA_WIN_ROWS * GRID_W)


def _stack_heads(q, low):
    scaled = q * (HEAD_DIM ** -0.5)
    zero = jnp.zeros_like(scaled)
    return jnp.concatenate([jnp.where(low, scaled, zero), jnp.where(low, zero, scaled)], axis=0)


def _stacked_attention(items):
    scores = [lax.dot_general(q2, kw, (((1,), (1,)), ((), ())), preferred_element_type=F32) + bias
              for q2, kw, _, bias in items]
    probs = []
    for s in scores:
        m = jnp.max(s, axis=-1, keepdims=True)
        p = jnp.exp(s - m)
        probs.append((p.astype(BF16), m, jnp.sum(p, axis=-1, keepdims=True)))
    return [(jnp.dot(p, vw, preferred_element_type=F32) / z, m, z)
            for (p, m, z), (_, _, vw, _) in zip(probs, items)]


def _unstack_heads(a, low):
    half = a.shape[0] // HEADS_PER_LANE_TILE
    return jnp.where(low, a[:half], a[half:])


def _na_kernel(q_ref, k_ref, v_ref, bias_ref, o_ref, *, rows, rows_per_step):
    kr = NA_WIN_ROWS
    low = lax.broadcasted_iota(jnp.int32, (GRID_W, LANES), 1) < HEAD_DIM

    def body(i, carry):
        items, qrows = [], []
        for u in range(rows_per_step):
            r = i * rows_per_step + u
            rs = jnp.clip(r - kr // 2, 0, rows - kr)
            qrows.append(pl.ds(pl.multiple_of(r * GRID_W, GRID_W), GRID_W))
            wrows = pl.ds(pl.multiple_of(rs * GRID_W, GRID_W), kr * GRID_W)
            items.append((_stack_heads(q_ref[qrows[-1], :], low), k_ref[wrows, :], v_ref[wrows, :],
                          bias_ref[0, r - rs]))
        for rows_u, (o, _, _) in zip(qrows, _stacked_attention(items)):
            o_ref[rows_u, :] = _unstack_heads(o, low).astype(o_ref.dtype)
        return carry

    lax.fori_loop(0, rows // rows_per_step, body, 0)


def _neighbourhood_attention(qkv, bias, batch, seq, rows_per_step=16):
    n = qkv.shape[0]
    rows = seq // GRID_W
    pairs = NA_WIDTH // LANES
    return pl.pallas_call(
        functools.partial(_na_kernel, rows=rows, rows_per_step=rows_per_step),
        grid=(pairs, batch),
        in_specs=[
            pl.BlockSpec((seq, LANES), lambda p, b: (b, p)),
            pl.BlockSpec((seq, LANES), lambda p, b: (b, pairs + p)),
            pl.BlockSpec((seq, LANES), lambda p, b: (b, 2 * pairs + p)),
            pl.BlockSpec((1, NA_WIN_ROWS, HEADS_PER_LANE_TILE * GRID_W, NA_WIN_ROWS * GRID_W),
                         lambda p, b: (p, 0, 0, 0)),
        ],
        out_specs=pl.BlockSpec((seq, LANES), lambda p, b: (b, p)),
        out_shape=jax.ShapeDtypeStruct((n, NA_WIDTH), BF16),
        compiler_params=_params(("arbitrary", "arbitrary")),
        name="na_attn",
    )(qkv, qkv, qkv, bias)


def _alibi_slopes():
    n = N_DIL_GROUPS * DIL_HEADS_PER_GROUP
    s = np.exp2(-ALIBI_MAX * np.arange(1, n + 1, dtype=np.float64) / n).astype(np.float32)
    return s.reshape(N_DIL_GROUPS, DIL_HEADS_PER_GROUP)


def _dil_bias_table(group):
    blk = DIL_BLOCK
    dilation = DIL_GROUPS[group][1]
    slopes = _alibi_slopes()[group]
    qi = np.arange(blk)[:, None]
    kj = np.arange(3 * blk)[None, :]
    tables = []
    for shift in range(3):
        arel = np.abs(kj - qi - shift * blk)
        dist = (dilation * arel).astype(np.float32)
        per_head = [np.where(arel <= blk, -slopes[h] * dist, np.float32(NEG_INF)) for h in range(DIL_HEADS_PER_GROUP)]
        tables.append(np.stack(per_head))
    t = np.stack(tables, axis=1).astype(np.float32)
    pairs = DIL_HEADS_PER_GROUP // HEADS_PER_LANE_TILE
    t = t.reshape(pairs, HEADS_PER_LANE_TILE, 3, blk, 3 * blk).transpose(0, 2, 1, 3, 4)
    return t.reshape(pairs, 3, HEADS_PER_LANE_TILE * blk, 3 * blk)


def _dil_kernel(q_ref, k_ref, v_ref, bias_ref, o_ref, lse_ref, *, nb, blocks_per_step):
    blk = DIL_BLOCK
    win = 3 * blk
    low = lax.broadcasted_iota(jnp.int32, (blk, LANES), 1) < HEAD_DIM
    dilation = q_ref.shape[0]
    steps = dilation * nb

    def token_rows(sq, n):
        if dilation == 1:
            return pl.ds(pl.multiple_of(n * blk, blk), blk)
        return pl.ds(n * (blk * dilation) + sq, blk, stride=dilation)

    def body(i, carry):
        items, dst = [], []
        for u in range(blocks_per_step):
            t = i * blocks_per_step + u
            sq = t // nb
            n = t % nb
            wb = jnp.clip(n - 1, 0, nb - 3)
            qrows = pl.ds(pl.multiple_of(n * blk, blk), blk)
            wrows = pl.ds(pl.multiple_of(wb * blk, blk), win)
            dst.append(token_rows(sq, n))
            items.append((_stack_heads(q_ref[sq, qrows, :], low), k_ref[sq, wrows, :], v_ref[sq, wrows, :],
                          bias_ref[0, n - wb]))
        for rows, (o, m, z) in zip(dst, _stacked_attention(items)):
            o_ref[rows, :] = _unstack_heads(o, low)
            lse_ref[rows, :] = _unstack_heads(jnp.broadcast_to(m + jnp.log(z), o.shape), low)
        return carry

    lax.fori_loop(0, steps // blocks_per_step, body, 0)


def _dilated_attention(qkv4, col_offsets, group, blocks_per_step=16):
    batch, dilation, length, _ = qkv4.shape
    seq = dilation * length
    nb = length // DIL_BLOCK
    pairs = DIL_OUT_WIDTH // LANES
    qo, ko, vo = col_offsets
    bias = jnp.asarray(_dil_bias_table(group))
    out = jax.ShapeDtypeStruct((batch * seq, DIL_OUT_WIDTH), F32)
    seqs = lambda off: pl.BlockSpec((None, dilation, length, LANES), lambda p, b: (b, 0, 0, off + p))
    return pl.pallas_call(
        functools.partial(_dil_kernel, nb=nb, blocks_per_step=blocks_per_step),
        grid=(pairs, batch),
        in_specs=[seqs(qo), seqs(ko), seqs(vo),
                  pl.BlockSpec((1,) + bias.shape[1:], lambda p, b: (p, 0, 0, 0))],
        out_specs=[pl.BlockSpec((seq, LANES), lambda p, b: (b, p))] * 2,
        out_shape=[out, out],
        compiler_params=_params(("arbitrary", "arbitrary")),
        name=f"dil_attn_g{group}",
    )(qkv4, qkv4, qkv4, bias)


def _outproj_kernel(x_ref, oa_ref, o0_ref, o1_ref, o2_ref, l0_ref, l1_ref, l2_ref, mod_ref, g_ref,
                    wg_ref, bg_ref, wpa_ref, wpb_ref, wo_ref, out_ref):
    d = x_ref.shape[1]
    x = x_ref[...]
    h = _modulated_norm(x, g_ref[...], mod_ref[0, SC1:SC1 + 1, :], mod_ref[0, SH1:SH1 + 1, :]).astype(BF16)

    lses = [l0_ref[...], l1_ref[...], l2_ref[...]]
    outs = [o0_ref[...], o1_ref[...], o2_ref[...]]
    top = jnp.maximum(jnp.maximum(lses[0], lses[1]), lses[2])
    es = [jnp.exp(l - top) for l in lses]
    den = es[0] + es[1] + es[2]
    ob = (es[0] * outs[0] + es[1] * outs[1] + es[2] * outs[2]) / den

    ya = jnp.dot(oa_ref[...], wpa_ref[...], preferred_element_type=F32)
    yb = jnp.dot(ob.astype(BF16), wpb_ref[...], preferred_element_type=F32)
    ga = jax.nn.sigmoid(jnp.dot(h, wg_ref[:, :d], preferred_element_type=F32) + bg_ref[:, :d])
    mix = ga * ya
    gb = jax.nn.sigmoid(jnp.dot(h, wg_ref[:, d:], preferred_element_type=F32) + bg_ref[:, d:])
    mix = mix + gb * yb
    y = jnp.dot(mix.astype(BF16), wo_ref[...], preferred_element_type=F32)
    out_ref[...] = x + mod_ref[0, GT1:GT1 + 1, :] * y


def _outproj(x2, oa, o_groups, lse_groups, mod, g, w_gate, b_gate, w_pa, w_pb, w_o, seq, tm=512):
    n, d = x2.shape
    blocks_per_batch = seq // tm
    row = lambda c: pl.BlockSpec((tm, c), lambda i: (i, 0))
    full = lambda a: pl.BlockSpec(a.shape, lambda i: (0,) * a.ndim)
    return pl.pallas_call(
        _outproj_kernel,
        grid=(n // tm,),
        in_specs=[row(d), row(NA_WIDTH)] + [row(DIL_OUT_WIDTH)] * 6 + [
            pl.BlockSpec((1, 6, d), lambda i: (i // blocks_per_batch, 0, 0)),
            full(g), full(w_gate), full(b_gate), full(w_pa), full(w_pb), full(w_o),
        ],
        out_specs=row(d),
        out_shape=jax.ShapeDtypeStruct((n, d), F32),
        compiler_params=_params(("arbitrary",)),
        name="out_proj",
    )(x2, oa, *o_groups, *lse_groups, mod, g, w_gate, b_gate, w_pa, w_pb, w_o)


def _first_index_of_max(cur, idx, size):
    m = jnp.max(cur, axis=0, keepdims=True)
    first = jnp.min(jnp.where(cur == m, idx, size), axis=0, keepdims=True)
    return m, first


def _route_transposed(logits_t, e_bias):
    tokens = logits_t.shape[1]
    per_group = N_EXPERTS // N_EXPERT_GROUPS
    scores = jax.nn.sigmoid(logits_t)
    biased = scores + e_bias
    midx = lax.broadcasted_iota(jnp.int32, (per_group, tokens), 0)
    grp_scores = []
    for g in range(N_EXPERT_GROUPS):
        vals = biased[g * per_group:(g + 1) * per_group, :]
        m1, first = _first_index_of_max(vals, midx, per_group)
        m2 = jnp.max(jnp.where(midx == first, -jnp.inf, vals), axis=0, keepdims=True)
        grp_scores.append(m1 + m2)
    cur = jnp.concatenate(grp_scores, axis=0)
    gidx = lax.broadcasted_iota(jnp.int32, (N_EXPERT_GROUPS, tokens), 0)
    grp_sel = jnp.zeros((N_EXPERT_GROUPS, tokens), jnp.bool_)
    for _ in range(TOP_GROUPS):
        _, first = _first_index_of_max(cur, gidx, N_EXPERT_GROUPS)
        pick = gidx == first
        grp_sel = jnp.logical_or(grp_sel, pick)
        cur = jnp.where(pick, -jnp.inf, cur)
    rows = []
    for g in range(N_EXPERT_GROUPS):
        vals = biased[g * per_group:(g + 1) * per_group, :]
        rows.append(jnp.where(grp_sel[g:g + 1, :], vals, NEG_INF))
    cur = jnp.concatenate(rows, axis=0)
    eidx = lax.broadcasted_iota(jnp.int32, (N_EXPERTS, tokens), 0)
    firsts, picks, weights = [], [], []
    for _ in range(TOP_K):
        _, first = _first_index_of_max(cur, eidx, N_EXPERTS)
        pick = eidx == first
        firsts.append(first)
        picks.append(pick)
        weights.append(jnp.sum(jnp.where(pick, scores, 0.0), axis=0, keepdims=True))
        cur = jnp.where(pick, -jnp.inf, cur)
    total = functools.reduce(lambda a, b: a + b, weights)
    return firsts, picks, [w / total * ROUTED_SCALE for w in weights]


def _pack_halves(a):
    half = a.shape[1] // 2
    bits = lax.bitcast_convert_type(a.astype(BF16).astype(F32), jnp.int32)
    return lax.shift_right_logical(bits[:, :half], 16) | bits[:, half:]


def _unpack_halves(w):
    low = lax.bitcast_convert_type(lax.shift_left(w, 16), F32)
    high = lax.bitcast_convert_type(w & jnp.int32(-65536), F32)
    return low, high


def _router_kernel(x_ref, mod_ref, g_ref, wr_ref, eb_ref, tri_ref, hp_ref, eidx_ref, rank_ref, w_ref, cnt_ref):
    tm = x_ref.shape[0]

    @pl.when(pl.program_id(0) == 0)
    def _():
        cnt_ref[...] = jnp.zeros_like(cnt_ref)

    h = _modulated_norm(x_ref[...], g_ref[...], mod_ref[0, SC2:SC2 + 1, :], mod_ref[0, SH2:SH2 + 1, :])
    hp_ref[...] = _pack_halves(h)
    h_hi = h.astype(BF16)
    h_lo = (h - h_hi.astype(F32)).astype(BF16)
    nt = (((1,), (1,)), ((), ()))
    logits_t = (lax.dot_general(wr_ref[0], h_hi, nt, preferred_element_type=F32)
                + lax.dot_general(wr_ref[0], h_lo, nt, preferred_element_type=F32)
                + lax.dot_general(wr_ref[1], h_hi, nt, preferred_element_type=F32))
    firsts, picks, weights = _route_transposed(logits_t, eb_ref[...])
    sel = functools.reduce(jnp.logical_or, picks)
    sel_f = jnp.where(sel, 1.0, 0.0)
    incl = jnp.dot(sel_f.astype(BF16), tri_ref[...], preferred_element_type=F32)
    before = cnt_ref[:, 0:1] + incl - sel_f
    eidx_ref[...] = jnp.concatenate(firsts, axis=0)
    rank_ref[...] = jnp.concatenate(
        [jnp.sum(jnp.where(p, before, 0.0), axis=0, keepdims=True) for p in picks], axis=0).astype(jnp.int32)
    pad = jnp.concatenate(weights + [jnp.zeros((LANES - TOP_K, tm), F32)], axis=0)
    w_ref[...] = pad.T
    cnt_ref[...] = cnt_ref[...] + incl[:, tm - 1:tm]


def _router(x2, mod, g, w_router_t, e_bias, seq, tm=512):
    n, d = x2.shape
    blocks_per_batch = seq // tm
    tri = jnp.asarray(np.triu(np.ones((tm, tm), np.float32)), BF16)
    tok = lambda r: pl.BlockSpec((r, tm), lambda i: (0, i))
    const = lambda a: pl.BlockSpec(a.shape, lambda i: (0,) * a.ndim)
    return pl.pallas_call(
        _router_kernel,
        grid=(n // tm,),
        in_specs=[
            pl.BlockSpec((tm, d), lambda i: (i, 0)),
            pl.BlockSpec((1, 6, d), lambda i: (i // blocks_per_batch, 0, 0)),
            const(g), const(w_router_t), const(e_bias), const(tri),
        ],
        out_specs=[
            pl.BlockSpec((tm, d // 2), lambda i: (i, 0)),
            tok(TOP_K), tok(TOP_K),
            pl.BlockSpec((tm, LANES), lambda i: (i, 0)),
            pl.BlockSpec((N_EXPERTS, LANES), lambda i: (0, 0)),
        ],
        out_shape=[
            jax.ShapeDtypeStruct((n, d // 2), jnp.int32),
            jax.ShapeDtypeStruct((TOP_K, n), jnp.int32),
            jax.ShapeDtypeStruct((TOP_K, n), jnp.int32),
            jax.ShapeDtypeStruct((n, LANES), F32),
            jax.ShapeDtypeStruct((N_EXPERTS, LANES), F32),
        ],
        compiler_params=_params(("arbitrary",)),
        name="moe_router",
    )(x2, mod, g, w_router_t, e_bias, tri)


def _sc_worker_id():
    return lax.axis_index("subcore") * SC_CORES + lax.axis_index("core")


def _sc_scatter_rows(src, idx3, n_out):
    n, w = src.shape
    per_worker = n // SC_CHUNK // SC_WORKERS
    mesh = plsc.VectorSubcoreMesh(core_axis_name="core", subcore_axis_name="subcore")

    @functools.partial(
        pl.kernel, mesh=mesh, out_type=jax.ShapeDtypeStruct((n_out, w), src.dtype), name="moe_dispatch",
        scratch_types=[pltpu.VMEM((2, TOP_K, SC_CHUNK), jnp.int32), pltpu.VMEM((2, SC_CHUNK, w), src.dtype),
                       pltpu.SemaphoreType.DMA((2,)), pltpu.SemaphoreType.DMA((2,)), pltpu.SemaphoreType.DMA])
    def scatter(src_hbm, idx_hbm, out_hbm, idx_v, rows_v, idx_sem, row_sem, out_sem):
        first = _sc_worker_id() * per_worker

        def loads(chunk, slot):
            return (pltpu.make_async_copy(idx_hbm.at[chunk], idx_v.at[slot], idx_sem.at[slot]),
                    pltpu.make_async_copy(src_hbm.at[pl.ds(chunk * SC_CHUNK, SC_CHUNK)], rows_v.at[slot],
                                          row_sem.at[slot]))

        for cp in loads(first, 0):
            cp.start()

        @pl.loop(0, per_worker, step=2)
        def _(i):
            for slot in range(2):
                chunk = first + i + slot
                for cp in loads(chunk, slot):
                    cp.wait()

                @pl.when(i + slot + 1 < per_worker)
                def _():
                    for cp in loads(chunk + 1, 1 - slot):
                        cp.start()

                copies = [pltpu.make_async_copy(rows_v.at[slot], out_hbm.at[idx_v.at[slot, k]], out_sem)
                          for k in range(TOP_K)]
                for cp in copies:
                    cp.start()
                for cp in copies:
                    cp.wait()

    return scatter(src, idx3)


def _sc_gather_rows(src, idx3):
    _, w = src.shape
    chunks = idx3.shape[0]
    per_worker = chunks // SC_WORKERS
    mesh = plsc.VectorSubcoreMesh(core_axis_name="core", subcore_axis_name="subcore")

    @functools.partial(
        pl.kernel, mesh=mesh, out_type=jax.ShapeDtypeStruct((TOP_K, chunks * SC_CHUNK, w), src.dtype),
        name="moe_collect",
        scratch_types=[pltpu.VMEM((TOP_K, SC_CHUNK), jnp.int32), pltpu.VMEM((2, SC_CHUNK, w), src.dtype),
                       pltpu.SemaphoreType.DMA((2,)), pltpu.SemaphoreType.DMA((2,))])
    def gather(src_hbm, idx_hbm, out_hbm, idx_v, rows_v, in_sem, out_sem):
        first = _sc_worker_id() * per_worker

        @pl.loop(0, per_worker)
        def _(i):
            chunk = first + i
            pltpu.sync_copy(idx_hbm.at[chunk], idx_v)
            reads = [pltpu.make_async_copy(src_hbm.at[idx_v.at[k]], rows_v.at[k % 2], in_sem.at[k % 2])
                     for k in range(TOP_K)]
            writes = [pltpu.make_async_copy(rows_v.at[k % 2], out_hbm.at[k, pl.ds(chunk * SC_CHUNK, SC_CHUNK)],
                                            out_sem.at[k % 2]) for k in range(TOP_K)]
            reads[0].start()
            for k in range(TOP_K):
                if k + 1 < TOP_K:
                    if k >= 1:
                        writes[k - 1].wait()
                    reads[k + 1].start()
                reads[k].wait()
                writes[k].start()
            writes[TOP_K - 2].wait()
            writes[TOP_K - 1].wait()

    return gather(src, idx3)


def _swiglu(x, w_gate_up, w_down):
    gu = jnp.dot(x, w_gate_up, preferred_element_type=F32)
    gate = gu[:, :EXPERT_DIM]
    act = (gate * jax.nn.sigmoid(gate)) * gu[:, EXPERT_DIM:]
    return jnp.dot(act.astype(BF16), w_down, preferred_element_type=F32)


def _expert_ffn_kernel(be_ref, nv_ref, xs_ref, wg_ref, wu_ref, wd_ref, ys_ref, wgu_bf, wd_bf):
    i = pl.program_id(0)
    nvalid = nv_ref[i]

    @pl.when(jnp.logical_or(i == 0, be_ref[i] != be_ref[jnp.maximum(i - 1, 0)]))
    def _():
        wgu_bf[:, :EXPERT_DIM] = wg_ref[0, 0].astype(BF16)
        wgu_bf[:, EXPERT_DIM:] = wu_ref[0, 0].astype(BF16)
        wd_bf[...] = wd_ref[0, 0].astype(BF16)

    @pl.when(nvalid > 0)
    def _():
        tb = xs_ref.shape[0]
        sub = tb // FFN_SUB_BLOCKS
        row = lax.broadcasted_iota(jnp.int32, (sub, xs_ref.shape[1]), 0)
        spans = [pl.ds(s * sub, sub) for s in range(FFN_SUB_BLOCKS)]
        xs = []
        for s, span in enumerate(spans):
            low, high = _unpack_halves(jnp.where(row < nvalid - s * sub, xs_ref[span, :], 0))
            xs.append(jnp.concatenate([low, high], axis=1).astype(BF16))
        gus = [jnp.dot(x, wgu_bf[...], preferred_element_type=F32) for x in xs]
        acts = [((gu[:, :EXPERT_DIM] * jax.nn.sigmoid(gu[:, :EXPERT_DIM])) * gu[:, EXPERT_DIM:]).astype(BF16)
                for gu in gus]
        ys = [jnp.dot(a, wd_bf[...], preferred_element_type=F32) for a in acts]
        for span, y in zip(spans, ys):
            ys_ref[span, :] = _pack_halves(y)

    @pl.when(nvalid == 0)
    def _():
        ys_ref[...] = jnp.zeros_like(ys_ref)


def _expert_ffn(xs, block_expert, block_valid, w_gate, w_up, w_down, layer, tb):
    p, half = xs.shape
    d = 2 * half
    grid_spec = pltpu.PrefetchScalarGridSpec(
        num_scalar_prefetch=2,
        grid=(p // tb,),
        in_specs=[
            pl.BlockSpec((tb, half), lambda i, be, nv: (i, 0)),
            pl.BlockSpec((1, 1, d, EXPERT_DIM), lambda i, be, nv: (layer, be[i], 0, 0)),
            pl.BlockSpec((1, 1, d, EXPERT_DIM), lambda i, be, nv: (layer, be[i], 0, 0)),
            pl.BlockSpec((1, 1, EXPERT_DIM, d), lambda i, be, nv: (layer, be[i], 0, 0)),
        ],
        out_specs=pl.BlockSpec((tb, half), lambda i, be, nv: (i, 0)),
        scratch_shapes=[pltpu.VMEM((d, 2 * EXPERT_DIM), BF16), pltpu.VMEM((EXPERT_DIM, d), BF16)],
    )
    return pl.pallas_call(
        _expert_ffn_kernel,
        grid_spec=grid_spec,
        out_shape=jax.ShapeDtypeStruct((p, half), jnp.int32),
        compiler_params=_params(("arbitrary",)),
        name="moe_expert_ffn",
    )(block_expert, block_valid, xs, w_gate, w_up, w_down)


def _combine_kernel(x_ref, hp_ref, yg_ref, w_ref, mod_ref, wgu_ref, wd_ref, gf_ref, o_ref, *, final_norm):
    low, high = _unpack_halves(hp_ref[...])
    h = jnp.concatenate([low, high], axis=1).astype(BF16)
    shared = _swiglu(h, wgu_ref[...], wd_ref[...])
    half = hp_ref.shape[1]
    acc_low, acc_high = shared[:, :half], shared[:, half:]
    w = w_ref[...]
    for k in range(TOP_K):
        low, high = _unpack_halves(yg_ref[k])
        wk = w[:, k:k + 1]
        acc_low = acc_low + wk * low
        acc_high = acc_high + wk * high
    y = jnp.concatenate([acc_low, acc_high], axis=1)
    out = x_ref[...] + mod_ref[0, GT2:GT2 + 1, :] * y
    if final_norm:
        out = out * lax.rsqrt(jnp.mean(out * out, axis=-1, keepdims=True) + EPS) * gf_ref[...]
    o_ref[...] = out


def _combine(x2, hp, yg, w, mod, ws_gate_up, ws_down, g_final, seq, final_norm, tm=512):
    n, d = x2.shape
    blocks_per_batch = seq // tm
    const = lambda a: pl.BlockSpec(a.shape, lambda i: (0,) * a.ndim)
    return pl.pallas_call(
        functools.partial(_combine_kernel, final_norm=final_norm),
        grid=(n // tm,),
        in_specs=[
            pl.BlockSpec((tm, d), lambda i: (i, 0)),
            pl.BlockSpec((tm, d // 2), lambda i: (i, 0)),
            pl.BlockSpec((TOP_K, tm, d // 2), lambda i: (0, i, 0)),
            pl.BlockSpec((tm, LANES), lambda i: (i, 0)),
            pl.BlockSpec((1, 6, d), lambda i: (i // blocks_per_batch, 0, 0)),
            const(ws_gate_up), const(ws_down), const(g_final),
        ],
        out_specs=pl.BlockSpec((tm, d), lambda i: (i, 0)),
        out_shape=jax.ShapeDtypeStruct((n, d), F32),
        compiler_params=_params(("arbitrary",)),
        name="moe_combine",
    )(x2, hp, yg, w, mod, ws_gate_up, ws_down, g_final)


def _token_mixer(x2, mod, g_mix, w_in, b_gate, rpb, w_pa, w_pb, w_o, batch, seq):
    d = x2.shape[1]
    dil0 = 3 * NA_WIDTH
    group_cols = [[dil0 + part * DIL_WIDTH + grp * DIL_OUT_WIDTH for part in range(3)]
                  for grp in range(N_DIL_GROUPS)]
    order = sorted(range(N_DIL_GROUPS), key=lambda grp: DIL_GROUPS[grp][1] > 1)
    w_qkv = jnp.concatenate(
        [w_in[:, :dil0]] + [w_in[:, c:c + DIL_OUT_WIDTH] for grp in order for c in group_cols[grp]],
        axis=1).astype(BF16)
    tok, *residue = _inproj(x2, mod, g_mix.reshape(1, d), w_qkv, batch, seq)
    o_a = _neighbourhood_attention(tok, _na_bias_table(rpb), batch, seq)

    qkv_offsets = tuple(part * DIL_OUT_WIDTH // LANES for part in range(3))
    o_groups, lse_groups = [None] * N_DIL_GROUPS, [None] * N_DIL_GROUPS
    residue = iter(residue)
    tok_offset = dil0 // LANES
    for grp in order:
        if DIL_GROUPS[grp][1] == 1:
            seqs = tok.reshape(batch, 1, seq, tok.shape[1])
            offsets = tuple(tok_offset + o for o in qkv_offsets)
            tok_offset += 3 * DIL_OUT_WIDTH // LANES
        else:
            seqs, offsets = next(residue), qkv_offsets
        o_groups[grp], lse_groups[grp] = _dilated_attention(seqs, offsets, grp)

    return _outproj(x2, o_a, o_groups, lse_groups, mod, g_mix.reshape(1, d),
                    w_in[:, QKV_COLS:].astype(BF16), b_gate.reshape(1, -1),
                    w_pa.astype(BF16), w_pb.astype(BF16), w_o.astype(BF16), seq)


def _dispatch_plan(eidx_t, rank_t, counts, tb):
    n = eidx_t.shape[1]
    n_blocks = -(-(n * TOP_K + N_EXPERTS * (tb - 1)) // tb)
    padded = (counts + tb - 1) // tb * tb
    seg_end = jnp.cumsum(padded)
    seg_start = seg_end - padded
    experts = jnp.arange(N_EXPERTS, dtype=jnp.int32)

    def lookup(table, idx):
        sel = idx[None] == experts.reshape((N_EXPERTS,) + (1,) * idx.ndim)
        return jnp.sum(jnp.where(sel, table.reshape((N_EXPERTS,) + (1,) * idx.ndim), 0), axis=0)

    dest_t = lookup(seg_start, eidx_t) + rank_t
    idx3 = dest_t.reshape(TOP_K, n // SC_CHUNK, SC_CHUNK).transpose(1, 0, 2)
    block_start = jnp.arange(n_blocks, dtype=jnp.int32) * tb
    block_expert = jnp.sum((seg_end[:, None] <= block_start[None, :]).astype(jnp.int32), axis=0)
    block_expert = jnp.minimum(block_expert, N_EXPERTS - 1)
    block_valid = jnp.clip(lookup(counts, block_expert) - (block_start - lookup(seg_start, block_expert)), 0, tb)
    return idx3, block_expert, block_valid.astype(jnp.int32), n_blocks


def _moe_layer(x2, mod, g_ffn, w_router, e_bias, we_gate, we_up, we_down, layer, ws_gate, ws_up, ws_down, g_final,
               seq, final_norm, tb=EXPERT_ROW_BLOCK):
    d = x2.shape[1]
    wr_t = w_router.T
    wr_hi = wr_t.astype(BF16)
    wr_split = jnp.stack([wr_hi, (wr_t - wr_hi.astype(F32)).astype(BF16)])
    hp, eidx_t, rank_t, w, cnt = _router(x2, mod, g_ffn.reshape(1, d), wr_split, e_bias.reshape(-1, 1), seq)
    counts = cnt[:, 0].astype(jnp.int32)
    idx3, block_expert, block_valid, n_blocks = _dispatch_plan(eidx_t, rank_t, counts, tb)
    xs = _sc_scatter_rows(hp, idx3, n_blocks * tb)
    ys = _expert_ffn(xs, block_expert, block_valid, we_gate, we_up, we_down, layer, tb)
    yg = _sc_gather_rows(ys, idx3)
    return _combine(x2, hp, yg, w, mod, jnp.concatenate([ws_gate, ws_up], axis=-1).astype(BF16),
                    ws_down.astype(BF16), g_final.reshape(1, d), seq, final_norm)


def kernel(x, c, w_ada, b_ada, g_mix, w_in, b_gate, rpb, w_pa, w_pb, w_o, g_ffn, w_router, e_bias,
           we_gate, we_up, we_down, ws_gate, ws_up, ws_down, g_final):
    batch, seq, d = x.shape
    depth = w_ada.shape[0]
    mods = _ada(c, w_ada, b_ada).reshape(depth, batch, 6, d)
    parts = BATCH_PARTS if batch % BATCH_PARTS == 0 else 1
    pb = batch // parts
    outs = []
    for part in range(parts):
        x2 = x[part * pb:(part + 1) * pb].reshape(pb * seq, d)
        for l in range(depth):
            mod = mods[l, part * pb:(part + 1) * pb]
            x2 = _token_mixer(x2, mod, g_mix[l], w_in[l], b_gate[l], rpb[l], w_pa[l], w_pb[l], w_o[l], pb, seq)
            x2 = _moe_layer(x2, mod, g_ffn[l], w_router[l], e_bias[l], we_gate, we_up, we_down, l,
                            ws_gate[l], ws_up[l], ws_down[l], g_final, seq, final_norm=(l == depth - 1))
        outs.append(x2.reshape(pb, seq, d))
    return jnp.concatenate(outs, axis=0)
```

```python
import functools

import numpy as np
import jax
import jax.numpy as jnp
from jax import lax
from jax.experimental import pallas as pl
from jax.experimental.pallas import tpu as pltpu
from jax.experimental.pallas import tpu_sc as plsc

HEAD_DIM = 64
GRID_W = 64
NA_HEADS = 8
NA_WIN_ROWS = 8
NA_WIN_COLS = 16
DIL_GROUPS = ((128, 1), (512, 4), (2048, 16))
DIL_HEADS_PER_GROUP = 4
N_DIL_GROUPS = len(DIL_GROUPS)
NA_WIDTH = NA_HEADS * HEAD_DIM
DIL_WIDTH = N_DIL_GROUPS * DIL_HEADS_PER_GROUP * HEAD_DIM
DIL_OUT_WIDTH = DIL_HEADS_PER_GROUP * HEAD_DIM
QKV_COLS = 3 * (NA_WIDTH + DIL_WIDTH)
N_EXPERTS = 64
TOP_K = 8
N_EXPERT_GROUPS = 8
TOP_GROUPS = 4
EXPERT_DIM = 256
ROUTED_SCALE = 2.5
ALIBI_MAX = 8.0
EPS = 1e-6
NEG_INF = -1e30

LANES = 128
HEADS_PER_LANE_TILE = LANES // HEAD_DIM
DIL_BLOCK = 64
VMEM_LIMIT_BYTES = 56 * 1024 * 1024

SC_CORES = 2
SC_SUBCORES = 16
SC_WORKERS = SC_CORES * SC_SUBCORES
SC_CHUNK = 64
EXPERT_ROW_BLOCK = 1024
FFN_SUB_BLOCKS = 4

F32 = jnp.float32
BF16 = jnp.bfloat16

SH1, SC1, GT1, SH2, SC2, GT2 = range(6)


def _params(sem):
    return pltpu.CompilerParams(dimension_semantics=sem, vmem_limit_bytes=VMEM_LIMIT_BYTES)


def _modulated_norm(x, g, scale, shift):
    r = lax.rsqrt(jnp.mean(x * x, axis=-1, keepdims=True) + EPS)
    return (x * r * g) * (1.0 + scale) + shift


def _ada_kernel(c_ref, w_ref, b_ref, o_ref):
    c = c_ref[...]
    act = c * jax.nn.sigmoid(c)
    o_ref[0] = jnp.dot(act, w_ref[0], preferred_element_type=F32,
                       precision=lax.Precision.HIGHEST) + b_ref[0]


def _ada(c, w_ada, b_ada):
    depth, d, six_d = w_ada.shape
    b = c.shape[0]
    tn = d
    return pl.pallas_call(
        _ada_kernel,
        grid=(depth, six_d // tn),
        in_specs=[
            pl.BlockSpec((b, d), lambda l, j: (0, 0)),
            pl.BlockSpec((1, d, tn), lambda l, j: (l, 0, j)),
            pl.BlockSpec((1, 1, tn), lambda l, j: (l, 0, j)),
        ],
        out_specs=pl.BlockSpec((1, b, tn), lambda l, j: (l, 0, j)),
        out_shape=jax.ShapeDtypeStruct((depth, b, six_d), F32),
        compiler_params=_params(("arbitrary", "arbitrary")),
        name="ada_mod",
    )(c, w_ada, b_ada.reshape(depth, 1, six_d))


def _inproj_kernel(x_ref, mod_ref, g_ref, w_ref, tok_ref, *rest, dilations):
    res_refs, acc_refs = rest[:len(dilations)], rest[len(dilations):]
    tm = x_ref.shape[0]
    tok_cols = tok_ref.shape[1]
    tn = (w_ref.shape[1] - tok_cols) // len(dilations)
    h = _modulated_norm(x_ref[...], g_ref[...], mod_ref[0, SC1:SC1 + 1, :], mod_ref[0, SH1:SH1 + 1, :]).astype(BF16)

    for c0 in range(0, tok_cols, tn):
        tok_ref[:, c0:c0 + tn] = jnp.dot(h, w_ref[:, c0:c0 + tn], preferred_element_type=F32).astype(BF16)

    for g, (res_ref, acc_ref, dilation) in enumerate(zip(res_refs, acc_refs, dilations)):
        c0 = tok_cols + g * tn
        res = jnp.dot(h, w_ref[:, c0:c0 + tn], preferred_element_type=F32)
        for c in range(acc_ref.shape[0]):
            acc_ref[c] = res[:, c * LANES:(c + 1) * LANES]
        for r in range(dilation):
            for c in range(acc_ref.shape[0]):
                res_ref[0, r, :, c * LANES:(c + 1) * LANES] = (
                    acc_ref[c, pl.ds(r, tm // dilation, stride=dilation), :].astype(BF16))


def _inproj(x2, mod, g, w_qkv, batch, seq, tm=512):
    n, d = x2.shape
    tn = 3 * DIL_OUT_WIDTH
    dilations = tuple(dil for _, dil in DIL_GROUPS if dil > 1)
    tok_cols = w_qkv.shape[1] - tn * len(dilations)
    blocks_per_batch = seq // tm
    res_specs = [pl.BlockSpec((1, dil, tm // dil, tn),
                              lambda i: (i // blocks_per_batch, 0, i % blocks_per_batch, 0)) for dil in dilations]
    res_shapes = [jax.ShapeDtypeStruct((batch, dil, seq // dil, tn), BF16) for dil in dilations]
    return pl.pallas_call(
        functools.partial(_inproj_kernel, dilations=dilations),
        grid=(n // tm,),
        in_specs=[
            pl.BlockSpec((tm, d), lambda i: (i, 0)),
            pl.BlockSpec((1, 6, d), lambda i: (i // blocks_per_batch, 0, 0)),
            pl.BlockSpec((1, d), lambda i: (0, 0)),
            pl.BlockSpec(w_qkv.shape, lambda i: (0, 0)),
        ],
        out_specs=[pl.BlockSpec((tm, tok_cols), lambda i: (i, 0))] + res_specs,
        out_shape=[jax.ShapeDtypeStruct((n, tok_cols), BF16)] + res_shapes,
        scratch_shapes=[pltpu.VMEM((tn // LANES, tm, LANES), F32) for _ in dilations],
        compiler_params=_params(("arbitrary",)),
        name="in_proj",
    )(x2, mod, g, w_qkv)


def _na_bias_table(rpb):
    heads = rpb.shape[0]
    cols = np.arange(GRID_W)
    col_start = np.clip(cols - NA_WIN_COLS // 2, 0, GRID_W - NA_WIN_COLS)
    col_mask = (cols[None, :] >= col_start[:, None]) & (cols[None, :] < col_start[:, None] + NA_WIN_COLS)
    dc = np.clip(cols[None, :] - cols[:, None], -(NA_WIN_COLS - 1), NA_WIN_COLS - 1) + NA_WIN_COLS - 1
    rpb_cols = rpb[:, :, dc].astype(F32)
    t = jnp.stack([rpb_cols[:, NA_WIN_ROWS - 1 - off:2 * NA_WIN_ROWS - 1 - off] for off in range(NA_WIN_ROWS)],
                  axis=1)
    t = t.transpose(0, 1, 3, 2, 4)
    t = jnp.where(col_mask[:, None, :], t, NEG_INF)
    t = t.reshape(heads // HEADS_PER_LANE_TILE, HEADS_PER_LANE_TILE, NA_WIN_ROWS, GRID_W, NA_WIN_ROWS * GRID_W)
    return t.transpose(0, 2, 1, 3, 4).reshape(heads // HEADS_PER_LANE_TILE, NA_WIN_ROWS,
                                              HEADS_PER_LANE_TILE * GRID_W, NA_WIN_ROWS * GRID_W)


def _stack_heads(q, low):
    scaled = q * (HEAD_DIM ** -0.5)
    zero = jnp.zeros_like(scaled)
    return jnp.concatenate([jnp.where(low, scaled, zero), jnp.where(low, zero, scaled)], axis=0)


def _stacked_attention(items):
    scores = [lax.dot_general(q2, kw, (((1,), (1,)), ((), ())), preferred_element_type=F32) + bias
              for q2, kw, _, bias in items]
    probs = []
    for s in scores:
        m = jnp.max(s, axis=-1, keepdims=True)
        p = jnp.exp(s - m)
        probs.append((p.astype(BF16), m, jnp.sum(p, axis=-1, keepdims=True)))
    return [(jnp.dot(p, vw, preferred_element_type=F32) / z, m, z)
            for (p, m, z), (_, _, vw, _) in zip(probs, items)]


def _unstack_heads(a, low):
    half = a.shape[0] // HEADS_PER_LANE_TILE
    return jnp.where(low, a[:half], a[half:])


def _na_kernel(q_ref, k_ref, v_ref, bias_ref, o_ref, *, rows, rows_per_step):
    kr = NA_WIN_ROWS
    low = lax.broadcasted_iota(jnp.int32, (GRID_W, LANES), 1) < HEAD_DIM

    def body(i, carry):
        items, qrows = [], []
        for u in range(rows_per_step):
            r = i * rows_per_step + u
            rs = jnp.clip(r - kr // 2, 0, rows - kr)
            qrows.append(pl.ds(pl.multiple_of(r * GRID_W, GRID_W), GRID_W))
            wrows = pl.ds(pl.multiple_of(rs * GRID_W, GRID_W), kr * GRID_W)
            items.append((_stack_heads(q_ref[qrows[-1], :], low), k_ref[wrows, :], v_ref[wrows, :],
                          bias_ref[0, r - rs]))
        for rows_u, (o, _, _) in zip(qrows, _stacked_attention(items)):
            o_ref[rows_u, :] = _unstack_heads(o, low).astype(o_ref.dtype)
        return carry

    lax.fori_loop(0, rows // rows_per_step, body, 0)


def _neighbourhood_attention(qkv, bias, batch, seq, rows_per_step=16):
    n = qkv.shape[0]
    rows = seq // GRID_W
    pairs = NA_WIDTH // LANES
    return pl.pallas_call(
        functools.partial(_na_kernel, rows=rows, rows_per_step=rows_per_step),
        grid=(pairs, batch),
        in_specs=[
            pl.BlockSpec((seq, LANES), lambda p, b: (b, p)),
            pl.BlockSpec((seq, LANES), lambda p, b: (b, pairs + p)),
            pl.BlockSpec((seq, LANES), lambda p, b: (b, 2 * pairs + p)),
            pl.BlockSpec((1, NA_WIN_ROWS, HEADS_PER_LANE_TILE * GRID_W, NA_WIN_ROWS * GRID_W),
                         lambda p, b: (p, 0, 0, 0)),
        ],
        out_specs=pl.BlockSpec((seq, LANES), lambda p, b: (b, p)),
        out_shape=jax.ShapeDtypeStruct((n, NA_WIDTH), BF16),
        compiler_params=_params(("arbitrary", "arbitrary")),
        name="na_attn",
    )(qkv, qkv, qkv, bias)


def _alibi_slopes():
    n = N_DIL_GROUPS * DIL_HEADS_PER_GROUP
    s = np.exp2(-ALIBI_MAX * np.arange(1, n + 1, dtype=np.float64) / n).astype(np.float32)
    return s.reshape(N_DIL_GROUPS, DIL_HEADS_PER_GROUP)


def _dil_bias_table(group):
    blk = DIL_BLOCK
    dilation = DIL_GROUPS[group][1]
    slopes = _alibi_slopes()[group]
    qi = np.arange(blk)[:, None]
    kj = np.arange(3 * blk)[None, :]
    tables = []
    for shift in range(3):
        arel = np.abs(kj - qi - shift * blk)
        dist = (dilation * arel).astype(np.float32)
        per_head = [np.where(arel <= blk, -slopes[h] * dist, np.float32(NEG_INF)) for h in range(DIL_HEADS_PER_GROUP)]
        tables.append(np.stack(per_head))
    t = np.stack(tables, axis=1).astype(np.float32)
    pairs = DIL_HEADS_PER_GROUP // HEADS_PER_LANE_TILE
    t = t.reshape(pairs, HEADS_PER_LANE_TILE, 3, blk, 3 * blk).transpose(0, 2, 1, 3, 4)
    return t.reshape(pairs, 3, HEADS_PER_LANE_TILE * blk, 3 * blk)


def _dil_kernel(q_ref, k_ref, v_ref, bias_ref, o_ref, lse_ref, *, nb, blocks_per_step):
    blk = DIL_BLOCK
    win = 3 * blk
    low = lax.broadcasted_iota(jnp.int32, (blk, LANES), 1) < HEAD_DIM
    dilation = q_ref.shape[0]
    steps = dilation * nb

    def token_rows(sq, n):
        if dilation == 1:
            return pl.ds(pl.multiple_of(n * blk, blk), blk)
        return pl.ds(n * (blk * dilation) + sq, blk, stride=dilation)

    def body(i, carry):
        items, dst = [], []
        for u in range(blocks_per_step):
            t = i * blocks_per_step + u
            sq = t // nb
            n = t % nb
            wb = jnp.clip(n - 1, 0, nb - 3)
            qrows = pl.ds(pl.multiple_of(n * blk, blk), blk)
            wrows = pl.ds(pl.multiple_of(wb * blk, blk), win)
            dst.append(token_rows(sq, n))
            items.append((_stack_heads(q_ref[sq, qrows, :], low), k_ref[sq, wrows, :], v_ref[sq, wrows, :],
                          bias_ref[0, n - wb]))
        for rows, (o, m, z) in zip(dst, _stacked_attention(items)):
            o_ref[rows, :] = _unstack_heads(o, low)
            lse_ref[rows, :] = _unstack_heads(jnp.broadcast_to(m + jnp.log(z), o.shape), low)
        return carry

    lax.fori_loop(0, steps // blocks_per_step, body, 0)


def _dilated_attention(qkv4, col_offsets, group, blocks_per_step=16):
    batch, dilation, length, _ = qkv4.shape
    seq = dilation * length
    nb = length // DIL_BLOCK
    pairs = DIL_OUT_WIDTH // LANES
    qo, ko, vo = col_offsets
    bias = jnp.asarray(_dil_bias_table(group))
    out = jax.ShapeDtypeStruct((batch * seq, DIL_OUT_WIDTH), F32)
    seqs = lambda off: pl.BlockSpec((None, dilation, length, LANES), lambda p, b: (b, 0, 0, off + p))
    return pl.pallas_call(
        functools.partial(_dil_kernel, nb=nb, blocks_per_step=blocks_per_step),
        grid=(pairs, batch),
        in_specs=[seqs(qo), seqs(ko), seqs(vo),
                  pl.BlockSpec((1,) + bias.shape[1:], lambda p, b: (p, 0, 0, 0))],
        out_specs=[pl.BlockSpec((seq, LANES), lambda p, b: (b, p))] * 2,
        out_shape=[out, out],
        compiler_params=_params(("arbitrary", "arbitrary")),
        name=f"dil_attn_g{group}",
    )(qkv4, qkv4, qkv4, bias)


def _outproj_kernel(x_ref, oa_ref, o0_ref, o1_ref, o2_ref, l0_ref, l1_ref, l2_ref, mod_ref, g_ref,
                    wg_ref, bg_ref, wpa_ref, wpb_ref, wo_ref, g2_ref, wr_ref, eb_ref, tri_ref,
                    out_ref, hp_ref, eidx_ref, rank_ref, w_ref, cnt_ref):
    d = x_ref.shape[1]
    x = x_ref[...]
    h = _modulated_norm(x, g_ref[...], mod_ref[0, SC1:SC1 + 1, :], mod_ref[0, SH1:SH1 + 1, :]).astype(BF16)

    lses = [l0_ref[...], l1_ref[...], l2_ref[...]]
    outs = [o0_ref[...], o1_ref[...], o2_ref[...]]
    top = jnp.maximum(jnp.maximum(lses[0], lses[1]), lses[2])
    es = [jnp.exp(l - top) for l in lses]
    den = es[0] + es[1] + es[2]
    ob = (es[0] * outs[0] + es[1] * outs[1] + es[2] * outs[2]) / den

    ya = jnp.dot(oa_ref[...], wpa_ref[...], preferred_element_type=F32)
    yb = jnp.dot(ob.astype(BF16), wpb_ref[...], preferred_element_type=F32)
    ga = jax.nn.sigmoid(jnp.dot(h, wg_ref[:, :d], preferred_element_type=F32) + bg_ref[:, :d])
    mix = ga * ya
    gb = jax.nn.sigmoid(jnp.dot(h, wg_ref[:, d:], preferred_element_type=F32) + bg_ref[:, d:])
    mix = mix + gb * yb
    y = jnp.dot(mix.astype(BF16), wo_ref[...], preferred_element_type=F32)
    out = x + mod_ref[0, GT1:GT1 + 1, :] * y
    out_ref[...] = out
    _route_block(out, mod_ref, g2_ref, wr_ref, eb_ref, tri_ref, hp_ref, eidx_ref, rank_ref, w_ref, cnt_ref)


def _outproj(x2, oa, o_groups, lse_groups, mod, g, w_gate, b_gate, w_pa, w_pb, w_o, g_ffn, wr_split, e_bias,
             seq, tm=512):
    n, d = x2.shape
    blocks_per_batch = seq // tm
    tri = jnp.asarray(np.triu(np.ones((tm, tm), np.float32)), BF16)
    row = lambda c: pl.BlockSpec((tm, c), lambda i: (i, 0))
    tok = lambda r: pl.BlockSpec((r, tm), lambda i: (0, i))
    full = lambda a: pl.BlockSpec(a.shape, lambda i: (0,) * a.ndim)
    return pl.pallas_call(
        _outproj_kernel,
        grid=(n // tm,),
        in_specs=[row(d), row(NA_WIDTH)] + [row(DIL_OUT_WIDTH)] * 6 + [
            pl.BlockSpec((1, 6, d), lambda i: (i // blocks_per_batch, 0, 0)),
            full(g), full(w_gate), full(b_gate), full(w_pa), full(w_pb), full(w_o),
            full(g_ffn), full(wr_split), full(e_bias), full(tri),
        ],
        out_specs=[row(d), row(d // 2), tok(TOP_K), tok(TOP_K), row(LANES),
                   pl.BlockSpec((N_EXPERTS, LANES), lambda i: (0, 0))],
        out_shape=[
            jax.ShapeDtypeStruct((n, d), F32),
            jax.ShapeDtypeStruct((n, d // 2), jnp.int32),
            jax.ShapeDtypeStruct((TOP_K, n), jnp.int32),
            jax.ShapeDtypeStruct((TOP_K, n), jnp.int32),
            jax.ShapeDtypeStruct((n, LANES), F32),
            jax.ShapeDtypeStruct((N_EXPERTS, LANES), F32),
        ],
        compiler_params=_params(("arbitrary",)),
        name="out_proj",
    )(x2, oa, *o_groups, *lse_groups, mod, g, w_gate, b_gate, w_pa, w_pb, w_o, g_ffn, wr_split, e_bias, tri)


def _first_index_of_max(cur, idx, size):
    m = jnp.max(cur, axis=0, keepdims=True)
    first = jnp.min(jnp.where(cur == m, idx, size), axis=0, keepdims=True)
    return m, first


def _route_transposed(logits_t, e_bias):
    tokens = logits_t.shape[1]
    per_group = N_EXPERTS // N_EXPERT_GROUPS
    scores = jax.nn.sigmoid(logits_t)
    biased = scores + e_bias
    midx = lax.broadcasted_iota(jnp.int32, (per_group, tokens), 0)
    grp_scores = []
    for g in range(N_EXPERT_GROUPS):
        vals = biased[g * per_group:(g + 1) * per_group, :]
        m1, first = _first_index_of_max(vals, midx, per_group)
        m2 = jnp.max(jnp.where(midx == first, -jnp.inf, vals), axis=0, keepdims=True)
        grp_scores.append(m1 + m2)
    cur = jnp.concatenate(grp_scores, axis=0)
    gidx = lax.broadcasted_iota(jnp.int32, (N_EXPERT_GROUPS, tokens), 0)
    grp_sel = jnp.zeros((N_EXPERT_GROUPS, tokens), jnp.bool_)
    for _ in range(TOP_GROUPS):
        _, first = _first_index_of_max(cur, gidx, N_EXPERT_GROUPS)
        pick = gidx == first
        grp_sel = jnp.logical_or(grp_sel, pick)
        cur = jnp.where(pick, -jnp.inf, cur)
    rows = []
    for g in range(N_EXPERT_GROUPS):
        vals = biased[g * per_group:(g + 1) * per_group, :]
        rows.append(jnp.where(grp_sel[g:g + 1, :], vals, NEG_INF))
    cur = jnp.concatenate(rows, axis=0)
    eidx = lax.broadcasted_iota(jnp.int32, (N_EXPERTS, tokens), 0)
    firsts, picks, weights = [], [], []
    for _ in range(TOP_K):
        _, first = _first_index_of_max(cur, eidx, N_EXPERTS)
        pick = eidx == first
        firsts.append(first)
        picks.append(pick)
        weights.append(jnp.sum(jnp.where(pick, scores, 0.0), axis=0, keepdims=True))
        cur = jnp.where(pick, -jnp.inf, cur)
    total = functools.reduce(lambda a, b: a + b, weights)
    return firsts, picks, [w / total * ROUTED_SCALE for w in weights]


def _pack_halves(a):
    half = a.shape[1] // 2
    bits = lax.bitcast_convert_type(a.astype(BF16).astype(F32), jnp.int32)
    return lax.shift_right_logical(bits[:, :half], 16) | bits[:, half:]


def _unpack_halves(w):
    low = lax.bitcast_convert_type(lax.shift_left(w, 16), F32)
    high = lax.bitcast_convert_type(w & jnp.int32(-65536), F32)
    return low, high


def _route_block(x, mod_ref, g_ref, wr_ref, eb_ref, tri_ref, hp_ref, eidx_ref, rank_ref, w_ref, cnt_ref):
    tm = x.shape[0]

    @pl.when(pl.program_id(0) == 0)
    def _():
        cnt_ref[...] = jnp.zeros_like(cnt_ref)

    h = _modulated_norm(x, g_ref[...], mod_ref[0, SC2:SC2 + 1, :], mod_ref[0, SH2:SH2 + 1, :])
    hp_ref[...] = _pack_halves(h)
    h_hi = h.astype(BF16)
    h_lo = (h - h_hi.astype(F32)).astype(BF16)
    nt = (((1,), (1,)), ((), ()))
    logits_t = (lax.dot_general(wr_ref[0], h_hi, nt, preferred_element_type=F32)
                + lax.dot_general(wr_ref[0], h_lo, nt, preferred_element_type=F32)
                + lax.dot_general(wr_ref[1], h_hi, nt, preferred_element_type=F32))
    firsts, picks, weights = _route_transposed(logits_t, eb_ref[...])
    sel = functools.reduce(jnp.logical_or, picks)
    sel_f = jnp.where(sel, 1.0, 0.0)
    incl = jnp.dot(sel_f.astype(BF16), tri_ref[...], preferred_element_type=F32)
    before = cnt_ref[:, 0:1] + incl - sel_f
    eidx_ref[...] = jnp.concatenate(firsts, axis=0)
    rank_ref[...] = jnp.concatenate(
        [jnp.sum(jnp.where(p, before, 0.0), axis=0, keepdims=True) for p in picks], axis=0).astype(jnp.int32)
    pad = jnp.concatenate(weights + [jnp.zeros((LANES - TOP_K, tm), F32)], axis=0)
    w_ref[...] = pad.T
    cnt_ref[...] = cnt_ref[...] + incl[:, tm - 1:tm]


def _sc_worker_id():
    return lax.axis_index("subcore") * SC_CORES + lax.axis_index("core")


def _sc_scatter_rows(src, idx3, n_out):
    n, w = src.shape
    per_worker = n // SC_CHUNK // SC_WORKERS
    mesh = plsc.VectorSubcoreMesh(core_axis_name="core", subcore_axis_name="subcore")

    @functools.partial(
        pl.kernel, mesh=mesh, out_type=jax.ShapeDtypeStruct((n_out, w), src.dtype), name="moe_dispatch",
        scratch_types=[pltpu.VMEM((2, TOP_K, SC_CHUNK), jnp.int32), pltpu.VMEM((2, SC_CHUNK, w), src.dtype),
                       pltpu.SemaphoreType.DMA((2,)), pltpu.SemaphoreType.DMA((2,)), pltpu.SemaphoreType.DMA])
    def scatter(src_hbm, idx_hbm, out_hbm, idx_v, rows_v, idx_sem, row_sem, out_sem):
        first = _sc_worker_id() * per_worker

        def loads(chunk, slot):
            return (pltpu.make_async_copy(idx_hbm.at[chunk], idx_v.at[slot], idx_sem.at[slot]),
                    pltpu.make_async_copy(src_hbm.at[pl.ds(chunk * SC_CHUNK, SC_CHUNK)], rows_v.at[slot],
                                          row_sem.at[slot]))

        for cp in loads(first, 0):
            cp.start()

        @pl.loop(0, per_worker, step=2)
        def _(i):
            for slot in range(2):
                chunk = first + i + slot
                for cp in loads(chunk, slot):
                    cp.wait()

                @pl.when(i + slot + 1 < per_worker)
                def _():
                    for cp in loads(chunk + 1, 1 - slot):
                        cp.start()

                copies = [pltpu.make_async_copy(rows_v.at[slot], out_hbm.at[idx_v.at[slot, k]], out_sem)
                          for k in range(TOP_K)]
                for cp in copies:
                    cp.start()
                for cp in copies:
                    cp.wait()

    return scatter(src, idx3)


def _sc_gather_rows(src, idx3):
    _, w = src.shape
    chunks = idx3.shape[0]
    per_worker = chunks // SC_WORKERS
    mesh = plsc.VectorSubcoreMesh(core_axis_name="core", subcore_axis_name="subcore")

    @functools.partial(
        pl.kernel, mesh=mesh, out_type=jax.ShapeDtypeStruct((TOP_K, chunks * SC_CHUNK, w), src.dtype),
        name="moe_collect",
        scratch_types=[pltpu.VMEM((TOP_K, SC_CHUNK), jnp.int32), pltpu.VMEM((2, SC_CHUNK, w), src.dtype),
                       pltpu.SemaphoreType.DMA((2,)), pltpu.SemaphoreType.DMA((2,))])
    def gather(src_hbm, idx_hbm, out_hbm, idx_v, rows_v, in_sem, out_sem):
        first = _sc_worker_id() * per_worker

        @pl.loop(0, per_worker)
        def _(i):
            chunk = first + i
            pltpu.sync_copy(idx_hbm.at[chunk], idx_v)
            reads = [pltpu.make_async_copy(src_hbm.at[idx_v.at[k]], rows_v.at[k % 2], in_sem.at[k % 2])
                     for k in range(TOP_K)]
            writes = [pltpu.make_async_copy(rows_v.at[k % 2], out_hbm.at[k, pl.ds(chunk * SC_CHUNK, SC_CHUNK)],
                                            out_sem.at[k % 2]) for k in range(TOP_K)]
            reads[0].start()
            for k in range(TOP_K):
                if k + 1 < TOP_K:
                    if k >= 1:
                        writes[k - 1].wait()
                    reads[k + 1].start()
                reads[k].wait()
                writes[k].start()
            writes[TOP_K - 2].wait()
            writes[TOP_K - 1].wait()

    return gather(src, idx3)


def _swiglu(x, w_gate_up, w_down):
    gu = jnp.dot(x, w_gate_up, preferred_element_type=F32)
    gate = gu[:, :EXPERT_DIM]
    act = (gate * jax.nn.sigmoid(gate)) * gu[:, EXPERT_DIM:]
    return jnp.dot(act.astype(BF16), w_down, preferred_element_type=F32)


def _expert_ffn_kernel(be_ref, nv_ref, xs_ref, wg_ref, wu_ref, wd_ref, ys_ref, wgu_bf, wd_bf):
    i = pl.program_id(0)
    nvalid = nv_ref[i]

    @pl.when(jnp.logical_or(i == 0, be_ref[i] != be_ref[jnp.maximum(i - 1, 0)]))
    def _():
        wgu_bf[:, :EXPERT_DIM] = wg_ref[0, 0].astype(BF16)
        wgu_bf[:, EXPERT_DIM:] = wu_ref[0, 0].astype(BF16)
        wd_bf[...] = wd_ref[0, 0].astype(BF16)

    @pl.when(nvalid > 0)
    def _():
        tb = xs_ref.shape[0]
        sub = tb // FFN_SUB_BLOCKS
        row = lax.broadcasted_iota(jnp.int32, (sub, xs_ref.shape[1]), 0)
        spans = [pl.ds(s * sub, sub) for s in range(FFN_SUB_BLOCKS)]
        xs = []
        for s, span in enumerate(spans):
            low, high = _unpack_halves(jnp.where(row < nvalid - s * sub, xs_ref[span, :], 0))
            xs.append(jnp.concatenate([low, high], axis=1).astype(BF16))
        gus = [jnp.dot(x, wgu_bf[...], preferred_element_type=F32) for x in xs]
        acts = [((gu[:, :EXPERT_DIM] * jax.nn.sigmoid(gu[:, :EXPERT_DIM])) * gu[:, EXPERT_DIM:]).astype(BF16)
                for gu in gus]
        ys = [jnp.dot(a, wd_bf[...], preferred_element_type=F32) for a in acts]
        for span, y in zip(spans, ys):
            ys_ref[span, :] = _pack_halves(y)

    @pl.when(nvalid == 0)
    def _():
        ys_ref[...] = jnp.zeros_like(ys_ref)


def _expert_ffn(xs, block_expert, block_valid, w_gate, w_up, w_down, layer, tb):
    p, half = xs.shape
    d = 2 * half
    grid_spec = pltpu.PrefetchScalarGridSpec(
        num_scalar_prefetch=2,
        grid=(p // tb,),
        in_specs=[
            pl.BlockSpec((tb, half), lambda i, be, nv: (i, 0)),
            pl.BlockSpec((1, 1, d, EXPERT_DIM), lambda i, be, nv: (layer, be[i], 0, 0)),
            pl.BlockSpec((1, 1, d, EXPERT_DIM), lambda i, be, nv: (layer, be[i], 0, 0)),
            pl.BlockSpec((1, 1, EXPERT_DIM, d), lambda i, be, nv: (layer, be[i], 0, 0)),
        ],
        out_specs=pl.BlockSpec((tb, half), lambda i, be, nv: (i, 0)),
        scratch_shapes=[pltpu.VMEM((d, 2 * EXPERT_DIM), BF16), pltpu.VMEM((EXPERT_DIM, d), BF16)],
    )
    return pl.pallas_call(
        _expert_ffn_kernel,
        grid_spec=grid_spec,
        out_shape=jax.ShapeDtypeStruct((p, half), jnp.int32),
        compiler_params=_params(("arbitrary",)),
        name="moe_expert_ffn",
    )(block_expert, block_valid, xs, w_gate, w_up, w_down)


def _combine_kernel(x_ref, hp_ref, yg_ref, w_ref, mod_ref, wgu_ref, wd_ref, gf_ref, o_ref, *, final_norm):
    low, high = _unpack_halves(hp_ref[...])
    h = jnp.concatenate([low, high], axis=1).astype(BF16)
    shared = _swiglu(h, wgu_ref[...], wd_ref[...])
    half = hp_ref.shape[1]
    acc_low, acc_high = shared[:, :half], shared[:, half:]
    w = w_ref[...]
    for k in range(TOP_K):
        low, high = _unpack_halves(yg_ref[k])
        wk = w[:, k:k + 1]
        acc_low = acc_low + wk * low
        acc_high = acc_high + wk * high
    y = jnp.concatenate([acc_low, acc_high], axis=1)
    out = x_ref[...] + mod_ref[0, GT2:GT2 + 1, :] * y
    if final_norm:
        out = out * lax.rsqrt(jnp.mean(out * out, axis=-1, keepdims=True) + EPS) * gf_ref[...]
    o_ref[...] = out


def _combine(x2, hp, yg, w, mod, ws_gate_up, ws_down, g_final, seq, final_norm, tm=512):
    n, d = x2.shape
    blocks_per_batch = seq // tm
    const = lambda a: pl.BlockSpec(a.shape, lambda i: (0,) * a.ndim)
    return pl.pallas_call(
        functools.partial(_combine_kernel, final_norm=final_norm),
        grid=(n // tm,),
        in_specs=[
            pl.BlockSpec((tm, d), lambda i: (i, 0)),
            pl.BlockSpec((tm, d // 2), lambda i: (i, 0)),
            pl.BlockSpec((TOP_K, tm, d // 2), lambda i: (0, i, 0)),
            pl.BlockSpec((tm, LANES), lambda i: (i, 0)),
            pl.BlockSpec((1, 6, d), lambda i: (i // blocks_per_batch, 0, 0)),
            const(ws_gate_up), const(ws_down), const(g_final),
        ],
        out_specs=pl.BlockSpec((tm, d), lambda i: (i, 0)),
        out_shape=jax.ShapeDtypeStruct((n, d), F32),
        compiler_params=_params(("arbitrary",)),
        name="moe_combine",
    )(x2, hp, yg, w, mod, ws_gate_up, ws_down, g_final)


def _token_mixer(x2, mod, g_mix, w_in, b_gate, rpb, w_pa, w_pb, w_o, g_ffn, w_router, e_bias, batch, seq):
    d = x2.shape[1]
    dil0 = 3 * NA_WIDTH
    group_cols = [[dil0 + part * DIL_WIDTH + grp * DIL_OUT_WIDTH for part in range(3)]
                  for grp in range(N_DIL_GROUPS)]
    order = sorted(range(N_DIL_GROUPS), key=lambda grp: DIL_GROUPS[grp][1] > 1)
    w_qkv = jnp.concatenate(
        [w_in[:, :dil0]] + [w_in[:, c:c + DIL_OUT_WIDTH] for grp in order for c in group_cols[grp]],
        axis=1).astype(BF16)
    tok, *residue = _inproj(x2, mod, g_mix.reshape(1, d), w_qkv, batch, seq)
    o_a = _neighbourhood_attention(tok, _na_bias_table(rpb), batch, seq)

    qkv_offsets = tuple(part * DIL_OUT_WIDTH // LANES for part in range(3))
    o_groups, lse_groups = [None] * N_DIL_GROUPS, [None] * N_DIL_GROUPS
    residue = iter(residue)
    tok_offset = dil0 // LANES
    for grp in order:
        if DIL_GROUPS[grp][1] == 1:
            seqs = tok.reshape(batch, 1, seq, tok.shape[1])
            offsets = tuple(tok_offset + o for o in qkv_offsets)
            tok_offset += 3 * DIL_OUT_WIDTH // LANES
        else:
            seqs, offsets = next(residue), qkv_offsets
        o_groups[grp], lse_groups[grp] = _dilated_attention(seqs, offsets, grp)

    wr_t = w_router.T
    wr_hi = wr_t.astype(BF16)
    wr_split = jnp.stack([wr_hi, (wr_t - wr_hi.astype(F32)).astype(BF16)])
    return _outproj(x2, o_a, o_groups, lse_groups, mod, g_mix.reshape(1, d),
                    w_in[:, QKV_COLS:].astype(BF16), b_gate.reshape(1, -1),
                    w_pa.astype(BF16), w_pb.astype(BF16), w_o.astype(BF16),
                    g_ffn.reshape(1, d), wr_split, e_bias.reshape(-1, 1), seq)


def _dispatch_plan(eidx_t, rank_t, counts, tb):
    n = eidx_t.shape[1]
    n_blocks = -(-(n * TOP_K + N_EXPERTS * (tb - 1)) // tb)
    padded = (counts + tb - 1) // tb * tb
    seg_end = jnp.cumsum(padded)
    seg_start = seg_end - padded
    experts = jnp.arange(N_EXPERTS, dtype=jnp.int32)

    def lookup(table, idx):
        sel = idx[None] == experts.reshape((N_EXPERTS,) + (1,) * idx.ndim)
        return jnp.sum(jnp.where(sel, table.reshape((N_EXPERTS,) + (1,) * idx.ndim), 0), axis=0)

    dest_t = lookup(seg_start, eidx_t) + rank_t
    idx3 = dest_t.reshape(TOP_K, n // SC_CHUNK, SC_CHUNK).transpose(1, 0, 2)
    block_start = jnp.arange(n_blocks, dtype=jnp.int32) * tb
    block_expert = jnp.sum((seg_end[:, None] <= block_start[None, :]).astype(jnp.int32), axis=0)
    block_expert = jnp.minimum(block_expert, N_EXPERTS - 1)
    block_valid = jnp.clip(lookup(counts, block_expert) - (block_start - lookup(seg_start, block_expert)), 0, tb)
    return idx3, block_expert, block_valid.astype(jnp.int32), n_blocks


def _moe_layer(x2, routed, mod, we_gate, we_up, we_down, layer, ws_gate, ws_up, ws_down, g_final, seq, final_norm,
               tb=EXPERT_ROW_BLOCK):
    d = x2.shape[1]
    hp, eidx_t, rank_t, w, cnt = routed
    counts = cnt[:, 0].astype(jnp.int32)
    idx3, block_expert, block_valid, n_blocks = _dispatch_plan(eidx_t, rank_t, counts, tb)
    xs = _sc_scatter_rows(hp, idx3, n_blocks * tb)
    ys = _expert_ffn(xs, block_expert, block_valid, we_gate, we_up, we_down, layer, tb)
    yg = _sc_gather_rows(ys, idx3)
    return _combine(x2, hp, yg, w, mod, jnp.concatenate([ws_gate, ws_up], axis=-1).astype(BF16),
                    ws_down.astype(BF16), g_final.reshape(1, d), seq, final_norm)


def kernel(x, c, w_ada, b_ada, g_mix, w_in, b_gate, rpb, w_pa, w_pb, w_o, g_ffn, w_router, e_bias,
           we_gate, we_up, we_down, ws_gate, ws_up, ws_down, g_final):
    batch, seq, d = x.shape
    depth = w_ada.shape[0]
    mods = _ada(c, w_ada, b_ada).reshape(depth, batch, 6, d)
    x2 = x.reshape(batch * seq, d)
    for l in range(depth):
        x2, *routed = _token_mixer(x2, mods[l], g_mix[l], w_in[l], b_gate[l], rpb[l], w_pa[l], w_pb[l], w_o[l],
                                   g_ffn[l], w_router[l], e_bias[l], batch, seq)
        x2 = _moe_layer(x2, routed, mods[l], we_gate, we_up, we_down, l,
                        ws_gate[l], ws_up[l], ws_down[l], g_final, seq, final_norm=(l == depth - 1))
    return x2.reshape(batch, seq, d)
```

```python
import functools

import numpy as np
import jax
import jax.numpy as jnp
from jax import lax
from jax.experimental import pallas as pl
from jax.experimental.pallas import tpu as pltpu
from jax.experimental.pallas import tpu_sc as plsc

HEAD_DIM = 64
GRID_W = 64
NA_HEADS = 8
NA_WIN_ROWS = 8
NA_WIN_COLS = 16
DIL_GROUPS = ((128, 1), (512, 4), (2048, 16))
DIL_HEADS_PER_GROUP = 4
N_DIL_GROUPS = len(DIL_GROUPS)
NA_WIDTH = NA_HEADS * HEAD_DIM
DIL_WIDTH = N_DIL_GROUPS * DIL_HEADS_PER_GROUP * HEAD_DIM
DIL_OUT_WIDTH = DIL_HEADS_PER_GROUP * HEAD_DIM
QKV_COLS = 3 * (NA_WIDTH + DIL_WIDTH)
N_EXPERTS = 64
TOP_K = 8
N_EXPERT_GROUPS = 8
TOP_GROUPS = 4
EXPERT_DIM = 256
ROUTED_SCALE = 2.5
ALIBI_MAX = 8.0
EPS = 1e-6
NEG_INF = -1e30

LANES = 128
HEADS_PER_LANE_TILE = LANES // HEAD_DIM
DIL_BLOCK = 64
VMEM_LIMIT_BYTES = 56 * 1024 * 1024

SC_CORES = 2
SC_SUBCORES = 16
SC_WORKERS = SC_CORES * SC_SUBCORES
SC_CHUNK = 64
EXPERT_ROW_BLOCK = 1024
FFN_SUB_BLOCKS = 4
FFN_BLOCKS_PER_STEP = 2

F32 = jnp.float32
BF16 = jnp.bfloat16

SH1, SC1, GT1, SH2, SC2, GT2 = range(6)


def _params(sem):
    return pltpu.CompilerParams(dimension_semantics=sem, vmem_limit_bytes=VMEM_LIMIT_BYTES)


def _modulated_norm(x, g, scale, shift):
    r = lax.rsqrt(jnp.mean(x * x, axis=-1, keepdims=True) + EPS)
    return (x * r * g) * (1.0 + scale) + shift


def _ada_kernel(c_ref, w_ref, b_ref, o_ref):
    c = c_ref[...]
    act = c * jax.nn.sigmoid(c)
    o_ref[0] = jnp.dot(act, w_ref[0], preferred_element_type=F32,
                       precision=lax.Precision.HIGHEST) + b_ref[0]


def _ada(c, w_ada, b_ada):
    depth, d, six_d = w_ada.shape
    b = c.shape[0]
    tn = d
    return pl.pallas_call(
        _ada_kernel,
        grid=(depth, six_d // tn),
        in_specs=[
            pl.BlockSpec((b, d), lambda l, j: (0, 0)),
            pl.BlockSpec((1, d, tn), lambda l, j: (l, 0, j)),
            pl.BlockSpec((1, 1, tn), lambda l, j: (l, 0, j)),
        ],
        out_specs=pl.BlockSpec((1, b, tn), lambda l, j: (l, 0, j)),
        out_shape=jax.ShapeDtypeStruct((depth, b, six_d), F32),
        compiler_params=_params(("arbitrary", "arbitrary")),
        name="ada_mod",
    )(c, w_ada, b_ada.reshape(depth, 1, six_d))


def _inproj_kernel(x_ref, mod_ref, g_ref, w_ref, tok_ref, *rest, dilations):
    res_refs, acc_refs = rest[:len(dilations)], rest[len(dilations):]
    tm = x_ref.shape[0]
    tok_cols = tok_ref.shape[1]
    tn = (w_ref.shape[1] - tok_cols) // len(dilations)
    h = _modulated_norm(x_ref[...], g_ref[...], mod_ref[0, SC1:SC1 + 1, :], mod_ref[0, SH1:SH1 + 1, :]).astype(BF16)

    for c0 in range(0, tok_cols, tn):
        tok_ref[:, c0:c0 + tn] = jnp.dot(h, w_ref[:, c0:c0 + tn], preferred_element_type=F32).astype(BF16)

    for g, (res_ref, acc_ref, dilation) in enumerate(zip(res_refs, acc_refs, dilations)):
        c0 = tok_cols + g * tn
        res = jnp.dot(h, w_ref[:, c0:c0 + tn], preferred_element_type=F32)
        for c in range(acc_ref.shape[0]):
            acc_ref[c] = res[:, c * LANES:(c + 1) * LANES]
        for r in range(dilation):
            for c in range(acc_ref.shape[0]):
                res_ref[0, r, :, c * LANES:(c + 1) * LANES] = (
                    acc_ref[c, pl.ds(r, tm // dilation, stride=dilation), :].astype(BF16))


def _inproj(x2, mod, g, w_qkv, batch, seq, tm=512):
    n, d = x2.shape
    tn = 3 * DIL_OUT_WIDTH
    dilations = tuple(dil for _, dil in DIL_GROUPS if dil > 1)
    tok_cols = w_qkv.shape[1] - tn * len(dilations)
    blocks_per_batch = seq // tm
    res_specs = [pl.BlockSpec((1, dil, tm // dil, tn),
                              lambda i: (i // blocks_per_batch, 0, i % blocks_per_batch, 0)) for dil in dilations]
    res_shapes = [jax.ShapeDtypeStruct((batch, dil, seq // dil, tn), BF16) for dil in dilations]
    return pl.pallas_call(
        functools.partial(_inproj_kernel, dilations=dilations),
        grid=(n // tm,),
        in_specs=[
            pl.BlockSpec((tm, d), lambda i: (i, 0)),
            pl.BlockSpec((1, 6, d), lambda i: (i // blocks_per_batch, 0, 0)),
            pl.BlockSpec((1, d), lambda i: (0, 0)),
            pl.BlockSpec(w_qkv.shape, lambda i: (0, 0)),
        ],
        out_specs=[pl.BlockSpec((tm, tok_cols), lambda i: (i, 0))] + res_specs,
        out_shape=[jax.ShapeDtypeStruct((n, tok_cols), BF16)] + res_shapes,
        scratch_shapes=[pltpu.VMEM((tn // LANES, tm, LANES), F32) for _ in dilations],
        compiler_params=_params(("arbitrary",)),
        name="in_proj",
    )(x2, mod, g, w_qkv)


def _na_bias_table(rpb):
    heads = rpb.shape[0]
    cols = np.arange(GRID_W)
    col_start = np.clip(cols - NA_WIN_COLS // 2, 0, GRID_W - NA_WIN_COLS)
    col_mask = (cols[None, :] >= col_start[:, None]) & (cols[None, :] < col_start[:, None] + NA_WIN_COLS)
    dc = np.clip(cols[None, :] - cols[:, None], -(NA_WIN_COLS - 1), NA_WIN_COLS - 1) + NA_WIN_COLS - 1
    rpb_cols = rpb[:, :, dc].astype(F32)
    t = jnp.stack([rpb_cols[:, NA_WIN_ROWS - 1 - off:2 * NA_WIN_ROWS - 1 - off] for off in range(NA_WIN_ROWS)],
                  axis=1)
    t = t.transpose(0, 1, 3, 2, 4)
    t = jnp.where(col_mask[:, None, :], t, NEG_INF)
    t = t.reshape(heads // HEADS_PER_LANE_TILE, HEADS_PER_LANE_TILE, NA_WIN_ROWS, GRID_W, NA_WIN_ROWS * GRID_W)
    return t.transpose(0, 2, 1, 3, 4).reshape(heads // HEADS_PER_LANE_TILE, NA_WIN_ROWS,
                                              HEADS_PER_LANE_TILE * GRID_W, NA_WIN_ROWS * GRID_W)


def _stack_heads(q, low):
    scaled = q * (HEAD_DIM ** -0.5)
    zero = jnp.zeros_like(scaled)
    return jnp.concatenate([jnp.where(low, scaled, zero), jnp.where(low, zero, scaled)], axis=0)


def _stacked_attention(items):
    scores = [lax.dot_general(q2, kw, (((1,), (1,)), ((), ())), preferred_element_type=F32) + bias
              for q2, kw, _, bias in items]
    probs = []
    for s in scores:
        m = jnp.max(s, axis=-1, keepdims=True)
        p = jnp.exp(s - m)
        probs.append((p.astype(BF16), m, jnp.sum(p, axis=-1, keepdims=True)))
    return [(jnp.dot(p, vw, preferred_element_type=F32) / z, m, z)
            for (p, m, z), (_, _, vw, _) in zip(probs, items)]


def _unstack_heads(a, low):
    half = a.shape[0] // HEADS_PER_LANE_TILE
    return jnp.where(low, a[:half], a[half:])


def _na_kernel(q_ref, k_ref, v_ref, bias_ref, o_ref, *, rows, rows_per_step):
    kr = NA_WIN_ROWS
    low = lax.broadcasted_iota(jnp.int32, (GRID_W, LANES), 1) < HEAD_DIM

    def body(i, carry):
        items, qrows = [], []
        for u in range(rows_per_step):
            r = i * rows_per_step + u
            rs = jnp.clip(r - kr // 2, 0, rows - kr)
            qrows.append(pl.ds(pl.multiple_of(r * GRID_W, GRID_W), GRID_W))
            wrows = pl.ds(pl.multiple_of(rs * GRID_W, GRID_W), kr * GRID_W)
            items.append((_stack_heads(q_ref[qrows[-1], :], low), k_ref[wrows, :], v_ref[wrows, :],
                          bias_ref[0, r - rs]))
        for rows_u, (o, _, _) in zip(qrows, _stacked_attention(items)):
            o_ref[rows_u, :] = _unstack_heads(o, low).astype(o_ref.dtype)
        return carry

    lax.fori_loop(0, rows // rows_per_step, body, 0)


def _neighbourhood_attention(qkv, bias, batch, seq, rows_per_step=16):
    n = qkv.shape[0]
    rows = seq // GRID_W
    pairs = NA_WIDTH // LANES
    return pl.pallas_call(
        functools.partial(_na_kernel, rows=rows, rows_per_step=rows_per_step),
        grid=(pairs, batch),
        in_specs=[
            pl.BlockSpec((seq, LANES), lambda p, b: (b, p)),
            pl.BlockSpec((seq, LANES), lambda p, b: (b, pairs + p)),
            pl.BlockSpec((seq, LANES), lambda p, b: (b, 2 * pairs + p)),
            pl.BlockSpec((1, NA_WIN_ROWS, HEADS_PER_LANE_TILE * GRID_W, NA_WIN_ROWS * GRID_W),
                         lambda p, b: (p, 0, 0, 0)),
        ],
        out_specs=pl.BlockSpec((seq, LANES), lambda p, b: (b, p)),
        out_shape=jax.ShapeDtypeStruct((n, NA_WIDTH), BF16),
        compiler_params=_params(("arbitrary", "arbitrary")),
        name="na_attn",
    )(qkv, qkv, qkv, bias)


def _alibi_slopes():
    n = N_DIL_GROUPS * DIL_HEADS_PER_GROUP
    s = np.exp2(-ALIBI_MAX * np.arange(1, n + 1, dtype=np.float64) / n).astype(np.float32)
    return s.reshape(N_DIL_GROUPS, DIL_HEADS_PER_GROUP)


def _dil_bias_table(group):
    blk = DIL_BLOCK
    dilation = DIL_GROUPS[group][1]
    slopes = _alibi_slopes()[group]
    qi = np.arange(blk)[:, None]
    kj = np.arange(3 * blk)[None, :]
    tables = []
    for shift in range(3):
        arel = np.abs(kj - qi - shift * blk)
        dist = (dilation * arel).astype(np.float32)
        per_head = [np.where(arel <= blk, -slopes[h] * dist, np.float32(NEG_INF)) for h in range(DIL_HEADS_PER_GROUP)]
        tables.append(np.stack(per_head))
    t = np.stack(tables, axis=1).astype(np.float32)
    pairs = DIL_HEADS_PER_GROUP // HEADS_PER_LANE_TILE
    t = t.reshape(pairs, HEADS_PER_LANE_TILE, 3, blk, 3 * blk).transpose(0, 2, 1, 3, 4)
    return t.reshape(pairs, 3, HEADS_PER_LANE_TILE * blk, 3 * blk)


def _dil_kernel(q_ref, k_ref, v_ref, bias_ref, o_ref, lse_ref, *, nb, blocks_per_step):
    blk = DIL_BLOCK
    win = 3 * blk
    low = lax.broadcasted_iota(jnp.int32, (blk, LANES), 1) < HEAD_DIM
    dilation = q_ref.shape[0]
    steps = dilation * nb

    def token_rows(sq, n):
        if dilation == 1:
            return pl.ds(pl.multiple_of(n * blk, blk), blk)
        return pl.ds(n * (blk * dilation) + sq, blk, stride=dilation)

    def body(i, carry):
        items, dst = [], []
        for u in range(blocks_per_step):
            t = i * blocks_per_step + u
            sq = t // nb
            n = t % nb
            wb = jnp.clip(n - 1, 0, nb - 3)
            qrows = pl.ds(pl.multiple_of(n * blk, blk), blk)
            wrows = pl.ds(pl.multiple_of(wb * blk, blk), win)
            dst.append(token_rows(sq, n))
            items.append((_stack_heads(q_ref[sq, qrows, :], low), k_ref[sq, wrows, :], v_ref[sq, wrows, :],
                          bias_ref[0, n - wb]))
        for rows, (o, m, z) in zip(dst, _stacked_attention(items)):
            o_ref[rows, :] = _unstack_heads(o, low)
            lse_ref[rows, :] = _unstack_heads(jnp.broadcast_to(m + jnp.log(z), o.shape), low)
        return carry

    lax.fori_loop(0, steps // blocks_per_step, body, 0)


def _dilated_attention(qkv4, col_offsets, group, blocks_per_step=16):
    batch, dilation, length, _ = qkv4.shape
    seq = dilation * length
    nb = length // DIL_BLOCK
    pairs = DIL_OUT_WIDTH // LANES
    qo, ko, vo = col_offsets
    bias = jnp.asarray(_dil_bias_table(group))
    out = jax.ShapeDtypeStruct((batch * seq, DIL_OUT_WIDTH), F32)
    seqs = lambda off: pl.BlockSpec((None, dilation, length, LANES), lambda p, b: (b, 0, 0, off + p))
    return pl.pallas_call(
        functools.partial(_dil_kernel, nb=nb, blocks_per_step=blocks_per_step),
        grid=(pairs, batch),
        in_specs=[seqs(qo), seqs(ko), seqs(vo),
                  pl.BlockSpec((1,) + bias.shape[1:], lambda p, b: (p, 0, 0, 0))],
        out_specs=[pl.BlockSpec((seq, LANES), lambda p, b: (b, p))] * 2,
        out_shape=[out, out],
        compiler_params=_params(("arbitrary", "arbitrary")),
        name=f"dil_attn_g{group}",
    )(qkv4, qkv4, qkv4, bias)


def _outproj_kernel(x_ref, oa_ref, o0_ref, o1_ref, o2_ref, l0_ref, l1_ref, l2_ref, mod_ref, g_ref,
                    wg_ref, bg_ref, wpa_ref, wpb_ref, wo_ref, g2_ref, wr_ref, eb_ref, tri_ref,
                    out_ref, hp_ref, eidx_ref, rank_ref, w_ref, cnt_ref):
    d = x_ref.shape[1]
    x = x_ref[...]
    h = _modulated_norm(x, g_ref[...], mod_ref[0, SC1:SC1 + 1, :], mod_ref[0, SH1:SH1 + 1, :]).astype(BF16)

    lses = [l0_ref[...], l1_ref[...], l2_ref[...]]
    outs = [o0_ref[...], o1_ref[...], o2_ref[...]]
    top = jnp.maximum(jnp.maximum(lses[0], lses[1]), lses[2])
    es = [jnp.exp(l - top) for l in lses]
    den = es[0] + es[1] + es[2]
    ob = (es[0] * outs[0] + es[1] * outs[1] + es[2] * outs[2]) / den

    ya = jnp.dot(oa_ref[...], wpa_ref[...], preferred_element_type=F32)
    yb = jnp.dot(ob.astype(BF16), wpb_ref[...], preferred_element_type=F32)
    ga = jax.nn.sigmoid(jnp.dot(h, wg_ref[:, :d], preferred_element_type=F32) + bg_ref[:, :d])
    mix = ga * ya
    gb = jax.nn.sigmoid(jnp.dot(h, wg_ref[:, d:], preferred_element_type=F32) + bg_ref[:, d:])
    mix = mix + gb * yb
    y = jnp.dot(mix.astype(BF16), wo_ref[...], preferred_element_type=F32)
    out = x + mod_ref[0, GT1:GT1 + 1, :] * y
    out_ref[...] = out
    _route_block(out, mod_ref, g2_ref, wr_ref, eb_ref, tri_ref, hp_ref, eidx_ref, rank_ref, w_ref, cnt_ref)


def _outproj(x2, oa, o_groups, lse_groups, mod, g, w_gate, b_gate, w_pa, w_pb, w_o, g_ffn, wr_split, e_bias,
             seq, tm=512):
    n, d = x2.shape
    blocks_per_batch = seq // tm
    tri = jnp.asarray(np.triu(np.ones((tm, tm), np.float32)), BF16)
    row = lambda c: pl.BlockSpec((tm, c), lambda i: (i, 0))
    tok = lambda r: pl.BlockSpec((r, tm), lambda i: (0, i))
    full = lambda a: pl.BlockSpec(a.shape, lambda i: (0,) * a.ndim)
    return pl.pallas_call(
        _outproj_kernel,
        grid=(n // tm,),
        in_specs=[row(d), row(NA_WIDTH)] + [row(DIL_OUT_WIDTH)] * 6 + [
            pl.BlockSpec((1, 6, d), lambda i: (i // blocks_per_batch, 0, 0)),
            full(g), full(w_gate), full(b_gate), full(w_pa), full(w_pb), full(w_o),
            full(g_ffn), full(wr_split), full(e_bias), full(tri),
        ],
        out_specs=[row(d), row(d // 2), tok(TOP_K), tok(TOP_K), row(LANES),
                   pl.BlockSpec((N_EXPERTS, LANES), lambda i: (0, 0))],
        out_shape=[
            jax.ShapeDtypeStruct((n, d), F32),
            jax.ShapeDtypeStruct((n, d // 2), jnp.int32),
            jax.ShapeDtypeStruct((TOP_K, n), jnp.int32),
            jax.ShapeDtypeStruct((TOP_K, n), jnp.int32),
            jax.ShapeDtypeStruct((n, LANES), F32),
            jax.ShapeDtypeStruct((N_EXPERTS, LANES), F32),
        ],
        compiler_params=_params(("arbitrary",)),
        name="out_proj",
    )(x2, oa, *o_groups, *lse_groups, mod, g, w_gate, b_gate, w_pa, w_pb, w_o, g_ffn, wr_split, e_bias, tri)


def _first_index_of_max(cur, idx, size):
    m = jnp.max(cur, axis=0, keepdims=True)
    first = jnp.min(jnp.where(cur == m, idx, size), axis=0, keepdims=True)
    return m, first


def _route_transposed(logits_t, e_bias):
    tokens = logits_t.shape[1]
    per_group = N_EXPERTS // N_EXPERT_GROUPS
    scores = jax.nn.sigmoid(logits_t)
    biased = scores + e_bias
    midx = lax.broadcasted_iota(jnp.int32, (per_group, tokens), 0)
    grp_scores = []
    for g in range(N_EXPERT_GROUPS):
        vals = biased[g * per_group:(g + 1) * per_group, :]
        m1, first = _first_index_of_max(vals, midx, per_group)
        m2 = jnp.max(jnp.where(midx == first, -jnp.inf, vals), axis=0, keepdims=True)
        grp_scores.append(m1 + m2)
    cur = jnp.concatenate(grp_scores, axis=0)
    gidx = lax.broadcasted_iota(jnp.int32, (N_EXPERT_GROUPS, tokens), 0)
    grp_sel = jnp.zeros((N_EXPERT_GROUPS, tokens), jnp.bool_)
    for _ in range(TOP_GROUPS):
        _, first = _first_index_of_max(cur, gidx, N_EXPERT_GROUPS)
        pick = gidx == first
        grp_sel = jnp.logical_or(grp_sel, pick)
        cur = jnp.where(pick, -jnp.inf, cur)
    rows = []
    for g in range(N_EXPERT_GROUPS):
        vals = biased[g * per_group:(g + 1) * per_group, :]
        rows.append(jnp.where(grp_sel[g:g + 1, :], vals, NEG_INF))
    cur = jnp.concatenate(rows, axis=0)
    eidx = lax.broadcasted_iota(jnp.int32, (N_EXPERTS, tokens), 0)
    firsts, picks, weights = [], [], []
    for _ in range(TOP_K):
        _, first = _first_index_of_max(cur, eidx, N_EXPERTS)
        pick = eidx == first
        firsts.append(first)
        picks.append(pick)
        weights.append(jnp.sum(jnp.where(pick, scores, 0.0), axis=0, keepdims=True))
        cur = jnp.where(pick, -jnp.inf, cur)
    total = functools.reduce(lambda a, b: a + b, weights)
    return firsts, picks, [w / total * ROUTED_SCALE for w in weights]


def _pack_halves(a):
    half = a.shape[1] // 2
    bits = lax.bitcast_convert_type(a.astype(BF16).astype(F32), jnp.int32)
    return lax.shift_right_logical(bits[:, :half], 16) | bits[:, half:]


def _unpack_halves(w):
    low = lax.bitcast_convert_type(lax.shift_left(w, 16), F32)
    high = lax.bitcast_convert_type(w & jnp.int32(-65536), F32)
    return low, high


def _route_block(x, mod_ref, g_ref, wr_ref, eb_ref, tri_ref, hp_ref, eidx_ref, rank_ref, w_ref, cnt_ref):
    tm = x.shape[0]

    @pl.when(pl.program_id(0) == 0)
    def _():
        cnt_ref[...] = jnp.zeros_like(cnt_ref)

    h = _modulated_norm(x, g_ref[...], mod_ref[0, SC2:SC2 + 1, :], mod_ref[0, SH2:SH2 + 1, :])
    hp_ref[...] = _pack_halves(h)
    h_hi = h.astype(BF16)
    h_lo = (h - h_hi.astype(F32)).astype(BF16)
    nt = (((1,), (1,)), ((), ()))
    logits_t = (lax.dot_general(wr_ref[0], h_hi, nt, preferred_element_type=F32)
                + lax.dot_general(wr_ref[0], h_lo, nt, preferred_element_type=F32)
                + lax.dot_general(wr_ref[1], h_hi, nt, preferred_element_type=F32))
    firsts, picks, weights = _route_transposed(logits_t, eb_ref[...])
    sel = functools.reduce(jnp.logical_or, picks)
    sel_f = jnp.where(sel, 1.0, 0.0)
    incl = jnp.dot(sel_f.astype(BF16), tri_ref[...], preferred_element_type=F32)
    before = cnt_ref[:, 0:1] + incl - sel_f
    eidx_ref[...] = jnp.concatenate(firsts, axis=0)
    rank_ref[...] = jnp.concatenate(
        [jnp.sum(jnp.where(p, before, 0.0), axis=0, keepdims=True) for p in picks], axis=0).astype(jnp.int32)
    pad = jnp.concatenate(weights + [jnp.zeros((LANES - TOP_K, tm), F32)], axis=0)
    w_ref[...] = pad.T
    cnt_ref[...] = cnt_ref[...] + incl[:, tm - 1:tm]


def _sc_worker_id():
    return lax.axis_index("subcore") * SC_CORES + lax.axis_index("core")


def _sc_scatter_rows(src, idx3, n_out):
    n, w = src.shape
    per_worker = n // SC_CHUNK // SC_WORKERS
    mesh = plsc.VectorSubcoreMesh(core_axis_name="core", subcore_axis_name="subcore")

    @functools.partial(
        pl.kernel, mesh=mesh, out_type=jax.ShapeDtypeStruct((n_out, w), src.dtype), name="moe_dispatch",
        scratch_types=[pltpu.VMEM((2, TOP_K, SC_CHUNK), jnp.int32), pltpu.VMEM((2, SC_CHUNK, w), src.dtype),
                       pltpu.SemaphoreType.DMA((2,)), pltpu.SemaphoreType.DMA((2,)), pltpu.SemaphoreType.DMA])
    def scatter(src_hbm, idx_hbm, out_hbm, idx_v, rows_v, idx_sem, row_sem, out_sem):
        first = _sc_worker_id() * per_worker

        def loads(chunk, slot):
            return (pltpu.make_async_copy(idx_hbm.at[chunk], idx_v.at[slot], idx_sem.at[slot]),
                    pltpu.make_async_copy(src_hbm.at[pl.ds(chunk * SC_CHUNK, SC_CHUNK)], rows_v.at[slot],
                                          row_sem.at[slot]))

        for cp in loads(first, 0):
            cp.start()

        @pl.loop(0, per_worker, step=2)
        def _(i):
            for slot in range(2):
                chunk = first + i + slot
                for cp in loads(chunk, slot):
                    cp.wait()

                @pl.when(i + slot + 1 < per_worker)
                def _():
                    for cp in loads(chunk + 1, 1 - slot):
                        cp.start()

                copies = [pltpu.make_async_copy(rows_v.at[slot], out_hbm.at[idx_v.at[slot, k]], out_sem)
                          for k in range(TOP_K)]
                for cp in copies:
                    cp.start()
                for cp in copies:
                    cp.wait()

    return scatter(src, idx3)


def _sc_gather_rows(src, idx3):
    _, w = src.shape
    chunks = idx3.shape[0]
    per_worker = chunks // SC_WORKERS
    mesh = plsc.VectorSubcoreMesh(core_axis_name="core", subcore_axis_name="subcore")

    @functools.partial(
        pl.kernel, mesh=mesh, out_type=jax.ShapeDtypeStruct((TOP_K, chunks * SC_CHUNK, w), src.dtype),
        name="moe_collect",
        scratch_types=[pltpu.VMEM((TOP_K, SC_CHUNK), jnp.int32), pltpu.VMEM((2, SC_CHUNK, w), src.dtype),
                       pltpu.SemaphoreType.DMA((2,)), pltpu.SemaphoreType.DMA((2,))])
    def gather(src_hbm, idx_hbm, out_hbm, idx_v, rows_v, in_sem, out_sem):
        first = _sc_worker_id() * per_worker

        @pl.loop(0, per_worker)
        def _(i):
            chunk = first + i
            pltpu.sync_copy(idx_hbm.at[chunk], idx_v)
            reads = [pltpu.make_async_copy(src_hbm.at[idx_v.at[k]], rows_v.at[k % 2], in_sem.at[k % 2])
                     for k in range(TOP_K)]
            writes = [pltpu.make_async_copy(rows_v.at[k % 2], out_hbm.at[k, pl.ds(chunk * SC_CHUNK, SC_CHUNK)],
                                            out_sem.at[k % 2]) for k in range(TOP_K)]
            reads[0].start()
            for k in range(TOP_K):
                if k + 1 < TOP_K:
                    if k >= 1:
                        writes[k - 1].wait()
                    reads[k + 1].start()
                reads[k].wait()
                writes[k].start()
            writes[TOP_K - 2].wait()
            writes[TOP_K - 1].wait()

    return gather(src, idx3)


def _swiglu(x, w_gate_up, w_down):
    gu = jnp.dot(x, w_gate_up, preferred_element_type=F32)
    gate = gu[:, :EXPERT_DIM]
    act = (gate * jax.nn.sigmoid(gate)) * gu[:, EXPERT_DIM:]
    return jnp.dot(act.astype(BF16), w_down, preferred_element_type=F32)


def _expert_ffn_kernel(be_ref, nv_ref, xs_ref, *refs, tb):
    slots = FFN_BLOCKS_PER_STEP
    w_refs, ys_ref, wgu_bf, wd_bf = refs[:3 * slots], refs[3 * slots], refs[3 * slots + 1], refs[3 * slots + 2]
    i = pl.program_id(0)
    sub = tb // FFN_SUB_BLOCKS
    row = lax.broadcasted_iota(jnp.int32, (sub, xs_ref.shape[1]), 0)

    for slot in range(slots):
        blk = i * slots + slot
        wg_ref, wu_ref, wd_ref = w_refs[3 * slot:3 * slot + 3]

        @pl.when(jnp.logical_or(i == 0, be_ref[blk] != be_ref[jnp.maximum(blk - slots, 0)]))
        def _():
            wgu_bf[slot, :, :EXPERT_DIM] = wg_ref[0, 0].astype(BF16)
            wgu_bf[slot, :, EXPERT_DIM:] = wu_ref[0, 0].astype(BF16)
            wd_bf[slot] = wd_ref[0, 0].astype(BF16)

    @pl.when(nv_ref[i * slots] > 0)
    def _():
        spans, xs = [], []
        for slot in range(slots):
            nvalid = nv_ref[i * slots + slot]
            for s in range(FFN_SUB_BLOCKS):
                span = pl.ds(slot * tb + s * sub, sub)
                packed = jnp.where(row < nvalid - s * sub, xs_ref[span, :], 0)
                low, high = _unpack_halves(packed)
                spans.append((slot, span))
                xs.append(jnp.concatenate([low, high], axis=1).astype(BF16))
        gus = [jnp.dot(x, wgu_bf[slot], preferred_element_type=F32) for (slot, _), x in zip(spans, xs)]
        acts = [((gu[:, :EXPERT_DIM] * jax.nn.sigmoid(gu[:, :EXPERT_DIM])) * gu[:, EXPERT_DIM:]).astype(BF16)
                for gu in gus]
        ys = [jnp.dot(a, wd_bf[slot], preferred_element_type=F32) for (slot, _), a in zip(spans, acts)]
        for (_, span), y in zip(spans, ys):
            ys_ref[span, :] = _pack_halves(y)

    @pl.when(nv_ref[i * slots] == 0)
    def _():
        ys_ref[...] = jnp.zeros_like(ys_ref)


def _expert_ffn(xs, block_expert, block_valid, w_gate, w_up, w_down, layer, tb):
    p, half = xs.shape
    d = 2 * half
    slots = FFN_BLOCKS_PER_STEP
    weight_specs = []
    for slot in range(slots):
        index = lambda i, be, nv, slot=slot: (layer, be[i * slots + slot], 0, 0)
        weight_specs += [pl.BlockSpec((1, 1, d, EXPERT_DIM), index), pl.BlockSpec((1, 1, d, EXPERT_DIM), index),
                         pl.BlockSpec((1, 1, EXPERT_DIM, d), index)]
    grid_spec = pltpu.PrefetchScalarGridSpec(
        num_scalar_prefetch=2,
        grid=(p // (tb * slots),),
        in_specs=[pl.BlockSpec((tb * slots, half), lambda i, be, nv: (i, 0))] + weight_specs,
        out_specs=pl.BlockSpec((tb * slots, half), lambda i, be, nv: (i, 0)),
        scratch_shapes=[pltpu.VMEM((slots, d, 2 * EXPERT_DIM), BF16), pltpu.VMEM((slots, EXPERT_DIM, d), BF16)],
    )
    return pl.pallas_call(
        functools.partial(_expert_ffn_kernel, tb=tb),
        grid_spec=grid_spec,
        out_shape=jax.ShapeDtypeStruct((p, half), jnp.int32),
        compiler_params=_params(("arbitrary",)),
        name="moe_expert_ffn",
    )(block_expert, block_valid, xs, *([w_gate, w_up, w_down] * slots))


def _combine_kernel(x_ref, hp_ref, yg_ref, w_ref, mod_ref, wgu_ref, wd_ref, gf_ref, o_ref, *, final_norm):
    low, high = _unpack_halves(hp_ref[...])
    h = jnp.concatenate([low, high], axis=1).astype(BF16)
    shared = _swiglu(h, wgu_ref[...], wd_ref[...])
    half = hp_ref.shape[1]
    acc_low, acc_high = shared[:, :half], shared[:, half:]
    w = w_ref[...]
    for k in range(TOP_K):
        low, high = _unpack_halves(yg_ref[k])
        wk = w[:, k:k + 1]
        acc_low = acc_low + wk * low
        acc_high = acc_high + wk * high
    y = jnp.concatenate([acc_low, acc_high], axis=1)
    out = x_ref[...] + mod_ref[0, GT2:GT2 + 1, :] * y
    if final_norm:
        out = out * lax.rsqrt(jnp.mean(out * out, axis=-1, keepdims=True) + EPS) * gf_ref[...]
    o_ref[...] = out


def _combine(x2, hp, yg, w, mod, ws_gate_up, ws_down, g_final, seq, final_norm, tm=512):
    n, d = x2.shape
    blocks_per_batch = seq // tm
    const = lambda a: pl.BlockSpec(a.shape, lambda i: (0,) * a.ndim)
    return pl.pallas_call(
        functools.partial(_combine_kernel, final_norm=final_norm),
        grid=(n // tm,),
        in_specs=[
            pl.BlockSpec((tm, d), lambda i: (i, 0)),
            pl.BlockSpec((tm, d // 2), lambda i: (i, 0)),
            pl.BlockSpec((TOP_K, tm, d // 2), lambda i: (0, i, 0)),
            pl.BlockSpec((tm, LANES), lambda i: (i, 0)),
            pl.BlockSpec((1, 6, d), lambda i: (i // blocks_per_batch, 0, 0)),
            const(ws_gate_up), const(ws_down), const(g_final),
        ],
        out_specs=pl.BlockSpec((tm, d), lambda i: (i, 0)),
        out_shape=jax.ShapeDtypeStruct((n, d), F32),
        compiler_params=_params(("arbitrary",)),
        name="moe_combine",
    )(x2, hp, yg, w, mod, ws_gate_up, ws_down, g_final)


def _token_mixer(x2, mod, g_mix, w_in, b_gate, rpb, w_pa, w_pb, w_o, g_ffn, w_router, e_bias, batch, seq):
    d = x2.shape[1]
    dil0 = 3 * NA_WIDTH
    group_cols = [[dil0 + part * DIL_WIDTH + grp * DIL_OUT_WIDTH for part in range(3)]
                  for grp in range(N_DIL_GROUPS)]
    order = sorted(range(N_DIL_GROUPS), key=lambda grp: DIL_GROUPS[grp][1] > 1)
    w_qkv = jnp.concatenate(
        [w_in[:, :dil0]] + [w_in[:, c:c + DIL_OUT_WIDTH] for grp in order for c in group_cols[grp]],
        axis=1).astype(BF16)
    tok, *residue = _inproj(x2, mod, g_mix.reshape(1, d), w_qkv, batch, seq)
    o_a = _neighbourhood_attention(tok, _na_bias_table(rpb), batch, seq)

    qkv_offsets = tuple(part * DIL_OUT_WIDTH // LANES for part in range(3))
    o_groups, lse_groups = [None] * N_DIL_GROUPS, [None] * N_DIL_GROUPS
    residue = iter(residue)
    tok_offset = dil0 // LANES
    for grp in order:
        if DIL_GROUPS[grp][1] == 1:
            seqs = tok.reshape(batch, 1, seq, tok.shape[1])
            offsets = tuple(tok_offset + o for o in qkv_offsets)
            tok_offset += 3 * DIL_OUT_WIDTH // LANES
        else:
            seqs, offsets = next(residue), qkv_offsets
        o_groups[grp], lse_groups[grp] = _dilated_attention(seqs, offsets, grp)

    wr_t = w_router.T
    wr_hi = wr_t.astype(BF16)
    wr_split = jnp.stack([wr_hi, (wr_t - wr_hi.astype(F32)).astype(BF16)])
    return _outproj(x2, o_a, o_groups, lse_groups, mod, g_mix.reshape(1, d),
                    w_in[:, QKV_COLS:].astype(BF16), b_gate.reshape(1, -1),
                    w_pa.astype(BF16), w_pb.astype(BF16), w_o.astype(BF16),
                    g_ffn.reshape(1, d), wr_split, e_bias.reshape(-1, 1), seq)


def _dispatch_plan(eidx_t, rank_t, counts, tb):
    n = eidx_t.shape[1]
    n_blocks = -(-(n * TOP_K + N_EXPERTS * (tb - 1)) // tb)
    n_blocks = -(-n_blocks // FFN_BLOCKS_PER_STEP) * FFN_BLOCKS_PER_STEP
    padded = (counts + tb - 1) // tb * tb
    seg_end = jnp.cumsum(padded)
    seg_start = seg_end - padded
    experts = jnp.arange(N_EXPERTS, dtype=jnp.int32)

    def lookup(table, idx):
        sel = idx[None] == experts.reshape((N_EXPERTS,) + (1,) * idx.ndim)
        return jnp.sum(jnp.where(sel, table.reshape((N_EXPERTS,) + (1,) * idx.ndim), 0), axis=0)

    dest_t = lookup(seg_start, eidx_t) + rank_t
    idx3 = dest_t.reshape(TOP_K, n // SC_CHUNK, SC_CHUNK).transpose(1, 0, 2)
    block_start = jnp.arange(n_blocks, dtype=jnp.int32) * tb
    block_expert = jnp.sum((seg_end[:, None] <= block_start[None, :]).astype(jnp.int32), axis=0)
    block_expert = jnp.minimum(block_expert, N_EXPERTS - 1)
    block_valid = jnp.clip(lookup(counts, block_expert) - (block_start - lookup(seg_start, block_expert)), 0, tb)
    return idx3, block_expert, block_valid.astype(jnp.int32), n_blocks


def _moe_layer(x2, routed, mod, we_gate, we_up, we_down, layer, ws_gate, ws_up, ws_down, g_final, seq, final_norm,
               tb=EXPERT_ROW_BLOCK):
    d = x2.shape[1]
    hp, eidx_t, rank_t, w, cnt = routed
    counts = cnt[:, 0].astype(jnp.int32)
    idx3, block_expert, block_valid, n_blocks = _dispatch_plan(eidx_t, rank_t, counts, tb)
    xs = _sc_scatter_rows(hp, idx3, n_blocks * tb)
    ys = _expert_ffn(xs, block_expert, block_valid, we_gate, we_up, we_down, layer, tb)
    yg = _sc_gather_rows(ys, idx3)
    return _combine(x2, hp, yg, w, mod, jnp.concatenate([ws_gate, ws_up], axis=-1).astype(BF16),
                    ws_down.astype(BF16), g_final.reshape(1, d), seq, final_norm)


def kernel(x, c, w_ada, b_ada, g_mix, w_in, b_gate, rpb, w_pa, w_pb, w_o, g_ffn, w_router, e_bias,
           we_gate, we_up, we_down, ws_gate, ws_up, ws_down, g_final):
    batch, seq, d = x.shape
    depth = w_ada.shape[0]
    mods = _ada(c, w_ada, b_ada).reshape(depth, batch, 6, d)
    x2 = x.reshape(batch * seq, d)
    for l in range(depth):
        x2, *routed = _token_mixer(x2, mods[l], g_mix[l], w_in[l], b_gate[l], rpb[l], w_pa[l], w_pb[l], w_o[l],
                                   g_ffn[l], w_router[l], e_bias[l], batch, seq)
        x2 = _moe_layer(x2, routed, mods[l], we_gate, we_up, we_down, l,
                        ws_gate[l], ws_up[l], ws_down[l], g_final, seq, final_norm=(l == depth - 1))
    return x2.reshape(batch, seq, d)
```

```python
import functools

import numpy as np
import jax
import jax.numpy as jnp
from jax import lax
from jax.experimental import pallas as pl
from jax.experimental.pallas import tpu as pltpu
from jax.experimental.pallas import tpu_sc as plsc

HEAD_DIM = 64
GRID_W = 64
NA_HEADS = 8
NA_WIN_ROWS = 8
NA_WIN_COLS = 16
DIL_GROUPS = ((128, 1), (512, 4), (2048, 16))
DIL_HEADS_PER_GROUP = 4
N_DIL_GROUPS = len(DIL_GROUPS)
NA_WIDTH = NA_HEADS * HEAD_DIM
DIL_WIDTH = N_DIL_GROUPS * DIL_HEADS_PER_GROUP * HEAD_DIM
DIL_OUT_WIDTH = DIL_HEADS_PER_GROUP * HEAD_DIM
QKV_COLS = 3 * (NA_WIDTH + DIL_WIDTH)
N_EXPERTS = 64
TOP_K = 8
N_EXPERT_GROUPS = 8
TOP_GROUPS = 4
EXPERT_DIM = 256
ROUTED_SCALE = 2.5
ALIBI_MAX = 8.0
EPS = 1e-6
NEG_INF = -1e30

LANES = 128
HEADS_PER_LANE_TILE = LANES // HEAD_DIM
DIL_BLOCK = 64
VMEM_LIMIT_BYTES = 56 * 1024 * 1024

SC_CORES = 2
SC_SUBCORES = 16
SC_WORKERS = SC_CORES * SC_SUBCORES
SC_CHUNK = 64
EXPERT_ROW_BLOCK = 1024
FFN_SUB_BLOCKS = 4
FFN_BLOCKS_PER_STEP = 2

F32 = jnp.float32
BF16 = jnp.bfloat16

SH1, SC1, GT1, SH2, SC2, GT2 = range(6)


def _params(sem):
    return pltpu.CompilerParams(dimension_semantics=sem, vmem_limit_bytes=VMEM_LIMIT_BYTES)


def _modulated_norm(x, g, scale, shift):
    r = lax.rsqrt(jnp.mean(x * x, axis=-1, keepdims=True) + EPS)
    return (x * r * g) * (1.0 + scale) + shift


def _ada_kernel(c_ref, w_ref, b_ref, o_ref):
    c = c_ref[...]
    act = c * jax.nn.sigmoid(c)
    o_ref[0] = jnp.dot(act, w_ref[0], preferred_element_type=F32,
                       precision=lax.Precision.HIGHEST) + b_ref[0]


def _ada(c, w_ada, b_ada):
    depth, d, six_d = w_ada.shape
    b = c.shape[0]
    tn = d
    return pl.pallas_call(
        _ada_kernel,
        grid=(depth, six_d // tn),
        in_specs=[
            pl.BlockSpec((b, d), lambda l, j: (0, 0)),
            pl.BlockSpec((1, d, tn), lambda l, j: (l, 0, j)),
            pl.BlockSpec((1, 1, tn), lambda l, j: (l, 0, j)),
        ],
        out_specs=pl.BlockSpec((1, b, tn), lambda l, j: (l, 0, j)),
        out_shape=jax.ShapeDtypeStruct((depth, b, six_d), F32),
        compiler_params=_params(("arbitrary", "arbitrary")),
        name="ada_mod",
    )(c, w_ada, b_ada.reshape(depth, 1, six_d))


def _inproj_kernel(x_ref, mod_ref, g_ref, w_ref, tok_ref, *rest, dilations):
    res_refs, acc_refs = rest[:len(dilations)], rest[len(dilations):]
    tm = x_ref.shape[0]
    tok_cols = tok_ref.shape[1]
    tn = (w_ref.shape[1] - tok_cols) // len(dilations)
    h = _modulated_norm(x_ref[...], g_ref[...], mod_ref[0, SC1:SC1 + 1, :], mod_ref[0, SH1:SH1 + 1, :]).astype(BF16)

    for c0 in range(0, tok_cols, tn):
        tok_ref[:, c0:c0 + tn] = jnp.dot(h, w_ref[:, c0:c0 + tn], preferred_element_type=F32).astype(BF16)

    for g, (res_ref, acc_ref, dilation) in enumerate(zip(res_refs, acc_refs, dilations)):
        c0 = tok_cols + g * tn
        res = jnp.dot(h, w_ref[:, c0:c0 + tn], preferred_element_type=F32)
        for c in range(acc_ref.shape[0]):
            acc_ref[c] = res[:, c * LANES:(c + 1) * LANES]
        for r in range(dilation):
            for c in range(acc_ref.shape[0]):
                res_ref[0, r, :, c * LANES:(c + 1) * LANES] = (
                    acc_ref[c, pl.ds(r, tm // dilation, stride=dilation), :].astype(BF16))


def _inproj(x2, mod, g, w_qkv, batch, seq, tm=512):
    n, d = x2.shape
    tn = 3 * DIL_OUT_WIDTH
    dilations = tuple(dil for _, dil in DIL_GROUPS if dil > 1)
    tok_cols = w_qkv.shape[1] - tn * len(dilations)
    blocks_per_batch = seq // tm
    res_specs = [pl.BlockSpec((1, dil, tm // dil, tn),
                              lambda i: (i // blocks_per_batch, 0, i % blocks_per_batch, 0)) for dil in dilations]
    res_shapes = [jax.ShapeDtypeStruct((batch, dil, seq // dil, tn), BF16) for dil in dilations]
    return pl.pallas_call(
        functools.partial(_inproj_kernel, dilations=dilations),
        grid=(n // tm,),
        in_specs=[
            pl.BlockSpec((tm, d), lambda i: (i, 0)),
            pl.BlockSpec((1, 6, d), lambda i: (i // blocks_per_batch, 0, 0)),
            pl.BlockSpec((1, d), lambda i: (0, 0)),
            pl.BlockSpec(w_qkv.shape, lambda i: (0, 0)),
        ],
        out_specs=[pl.BlockSpec((tm, tok_cols), lambda i: (i, 0))] + res_specs,
        out_shape=[jax.ShapeDtypeStruct((n, tok_cols), BF16)] + res_shapes,
        scratch_shapes=[pltpu.VMEM((tn // LANES, tm, LANES), F32) for _ in dilations],
        compiler_params=_params(("arbitrary",)),
        name="in_proj",
    )(x2, mod, g, w_qkv)


def _na_bias_table(rpb):
    heads = rpb.shape[0]
    cols = np.arange(GRID_W)
    col_start = np.clip(cols - NA_WIN_COLS // 2, 0, GRID_W - NA_WIN_COLS)
    col_mask = (cols[None, :] >= col_start[:, None]) & (cols[None, :] < col_start[:, None] + NA_WIN_COLS)
    dc = np.clip(cols[None, :] - cols[:, None], -(NA_WIN_COLS - 1), NA_WIN_COLS - 1) + NA_WIN_COLS - 1
    rpb_cols = rpb[:, :, dc].astype(F32)
    t = jnp.stack([rpb_cols[:, NA_WIN_ROWS - 1 - off:2 * NA_WIN_ROWS - 1 - off] for off in range(NA_WIN_ROWS)],
                  axis=1)
    t = t.transpose(0, 1, 3, 2, 4)
    t = jnp.where(col_mask[:, None, :], t, NEG_INF)
    t = t.reshape(heads // HEADS_PER_LANE_TILE, HEADS_PER_LANE_TILE, NA_WIN_ROWS, GRID_W, NA_WIN_ROWS * GRID_W)
    return t.transpose(0, 2, 1, 3, 4).reshape(heads // HEADS_PER_LANE_TILE, NA_WIN_ROWS,
                                              HEADS_PER_LANE_TILE * GRID_W, NA_WIN_ROWS * GRID_W)


def _stack_heads(q, low):
    scaled = q * (HEAD_DIM ** -0.5)
    zero = jnp.zeros_like(scaled)
    return jnp.concatenate([jnp.where(low, scaled, zero), jnp.where(low, zero, scaled)], axis=0)


def _stacked_attention(items):
    scores = [lax.dot_general(q2, kw, (((1,), (1,)), ((), ())), preferred_element_type=F32) + bias
              for q2, kw, _, bias in items]
    probs = []
    for s in scores:
        m = jnp.max(s, axis=-1, keepdims=True)
        p = jnp.exp(s - m)
        probs.append((p.astype(BF16), m, jnp.sum(p, axis=-1, keepdims=True)))
    return [(jnp.dot(p, vw, preferred_element_type=F32) / z, m, z)
            for (p, m, z), (_, _, vw, _) in zip(probs, items)]


def _unstack_heads(a, low):
    half = a.shape[0] // HEADS_PER_LANE_TILE
    return jnp.where(low, a[:half], a[half:])


def _na_kernel(q_ref, k_ref, v_ref, bias_ref, o_ref, *, rows, rows_per_step):
    kr = NA_WIN_ROWS
    low = lax.broadcasted_iota(jnp.int32, (GRID_W, LANES), 1) < HEAD_DIM

    def body(i, carry):
        items, qrows = [], []
        for u in range(rows_per_step):
            r = i * rows_per_step + u
            rs = jnp.clip(r - kr // 2, 0, rows - kr)
            qrows.append(pl.ds(pl.multiple_of(r * GRID_W, GRID_W), GRID_W))
            wrows = pl.ds(pl.multiple_of(rs * GRID_W, GRID_W), kr * GRID_W)
            items.append((_stack_heads(q_ref[qrows[-1], :], low), k_ref[wrows, :], v_ref[wrows, :],
                          bias_ref[0, r - rs]))
        for rows_u, (o, _, _) in zip(qrows, _stacked_attention(items)):
            o_ref[rows_u, :] = _unstack_heads(o, low).astype(o_ref.dtype)
        return carry

    lax.fori_loop(0, rows // rows_per_step, body, 0)


def _neighbourhood_attention(qkv, bias, batch, seq, rows_per_step=16):
    n = qkv.shape[0]
    rows = seq // GRID_W
    pairs = NA_WIDTH // LANES
    return pl.pallas_call(
        functools.partial(_na_kernel, rows=rows, rows_per_step=rows_per_step),
        grid=(pairs, batch),
        in_specs=[
            pl.BlockSpec((seq, LANES), lambda p, b: (b, p)),
            pl.BlockSpec((seq, LANES), lambda p, b: (b, pairs + p)),
            pl.BlockSpec((seq, LANES), lambda p, b: (b, 2 * pairs + p)),
            pl.BlockSpec((1, NA_WIN_ROWS, HEADS_PER_LANE_TILE * GRID_W, NA_WIN_ROWS * GRID_W),
                         lambda p, b: (p, 0, 0, 0)),
        ],
        out_specs=pl.BlockSpec((seq, LANES), lambda p, b: (b, p)),
        out_shape=jax.ShapeDtypeStruct((n, NA_WIDTH), BF16),
        compiler_params=_params(("arbitrary", "arbitrary")),
        name="na_attn",
    )(qkv, qkv, qkv, bias)


def _alibi_slopes():
    n = N_DIL_GROUPS * DIL_HEADS_PER_GROUP
    s = np.exp2(-ALIBI_MAX * np.arange(1, n + 1, dtype=np.float64) / n).astype(np.float32)
    return s.reshape(N_DIL_GROUPS, DIL_HEADS_PER_GROUP)


def _dil_bias_table(group):
    blk = DIL_BLOCK
    dilation = DIL_GROUPS[group][1]
    slopes = _alibi_slopes()[group]
    qi = np.arange(blk)[:, None]
    kj = np.arange(3 * blk)[None, :]
    tables = []
    for shift in range(3):
        arel = np.abs(kj - qi - shift * blk)
        dist = (dilation * arel).astype(np.float32)
        per_head = [np.where(arel <= blk, -slopes[h] * dist, np.float32(NEG_INF)) for h in range(DIL_HEADS_PER_GROUP)]
        tables.append(np.stack(per_head))
    t = np.stack(tables, axis=1).astype(np.float32)
    pairs = DIL_HEADS_PER_GROUP // HEADS_PER_LANE_TILE
    t = t.reshape(pairs, HEADS_PER_LANE_TILE, 3, blk, 3 * blk).transpose(0, 2, 1, 3, 4)
    return t.reshape(pairs, 3, HEADS_PER_LANE_TILE * blk, 3 * blk)


def _dil_group(q_ref, k_ref, v_ref, bias_ref, o_ref, lse_ref, blocks_per_step):
    blk = DIL_BLOCK
    win = 3 * blk
    low = lax.broadcasted_iota(jnp.int32, (blk, LANES), 1) < HEAD_DIM
    dilation, length, _ = q_ref.shape
    nb = length // blk
    steps = dilation * nb

    def token_rows(sq, n):
        if dilation == 1:
            return pl.ds(pl.multiple_of(n * blk, blk), blk)
        return pl.ds(n * (blk * dilation) + sq, blk, stride=dilation)

    def body(i, carry):
        items, dst = [], []
        for u in range(blocks_per_step):
            t = i * blocks_per_step + u
            sq = t // nb
            n = t % nb
            wb = jnp.clip(n - 1, 0, nb - 3)
            qrows = pl.ds(pl.multiple_of(n * blk, blk), blk)
            wrows = pl.ds(pl.multiple_of(wb * blk, blk), win)
            dst.append(token_rows(sq, n))
            items.append((_stack_heads(q_ref[sq, qrows, :], low), k_ref[sq, wrows, :], v_ref[sq, wrows, :],
                          bias_ref[0, n - wb]))
        for rows, (o, m, z) in zip(dst, _stacked_attention(items)):
            o_ref[rows, :] = _unstack_heads(o, low)
            lse_ref[rows, :] = _unstack_heads(jnp.broadcast_to(m + jnp.log(z), o.shape), low)
        return carry

    lax.fori_loop(0, steps // blocks_per_step, body, 0)


def _dil_kernel(*refs, blocks_per_step, merge_rows):
    ng = N_DIL_GROUPS
    qkv_refs, bias_refs = refs[:3 * ng], refs[3 * ng:4 * ng]
    ob_ref, o_scr, lse_scr = refs[4 * ng:]
    for g in range(ng):
        q_ref, k_ref, v_ref = qkv_refs[3 * g:3 * g + 3]
        _dil_group(q_ref, k_ref, v_ref, bias_refs[g], o_scr.at[g], lse_scr.at[g], blocks_per_step)

    def merge(c, carry):
        rows = pl.ds(pl.multiple_of(c * merge_rows, merge_rows), merge_rows)
        lses = [lse_scr[g, rows, :] for g in range(ng)]
        top = functools.reduce(jnp.maximum, lses)
        es = [jnp.exp(l - top) for l in lses]
        num = functools.reduce(lambda a, b: a + b, [e * o_scr[g, rows, :] for g, e in enumerate(es)])
        ob_ref[rows, :] = (num / functools.reduce(lambda a, b: a + b, es)).astype(ob_ref.dtype)
        return carry

    lax.fori_loop(0, ob_ref.shape[0] // merge_rows, merge, 0)


def _dilated_attention(group_inputs, batch, seq, blocks_per_step=16, merge_rows=256):
    pairs = DIL_OUT_WIDTH // LANES
    operands, specs = [], []
    for qkv4, offsets in group_inputs:
        _, dilation, length, _ = qkv4.shape
        for off in offsets:
            operands.append(qkv4)
            specs.append(pl.BlockSpec((None, dilation, length, LANES),
                                      lambda p, b, off=off: (b, 0, 0, off + p)))
    for group in range(N_DIL_GROUPS):
        bias = jnp.asarray(_dil_bias_table(group))
        operands.append(bias)
        specs.append(pl.BlockSpec((1,) + bias.shape[1:], lambda p, b: (p, 0, 0, 0)))
    return pl.pallas_call(
        functools.partial(_dil_kernel, blocks_per_step=blocks_per_step, merge_rows=merge_rows),
        grid=(pairs, batch),
        in_specs=specs,
        out_specs=pl.BlockSpec((seq, LANES), lambda p, b: (b, p)),
        out_shape=jax.ShapeDtypeStruct((batch * seq, DIL_OUT_WIDTH), BF16),
        scratch_shapes=[pltpu.VMEM((N_DIL_GROUPS, seq, LANES), F32), pltpu.VMEM((N_DIL_GROUPS, seq, LANES), F32)],
        compiler_params=_params(("arbitrary", "arbitrary")),
        name="dil_attn",
    )(*operands)


def _outproj_kernel(x_ref, oa_ref, ob_ref, mod_ref, g_ref,
                    wg_ref, bg_ref, wpa_ref, wpb_ref, wo_ref, g2_ref, wr_ref, eb_ref, tri_ref,
                    out_ref, hp_ref, eidx_ref, rank_ref, w_ref, cnt_ref):
    d = x_ref.shape[1]
    x = x_ref[...]
    h = _modulated_norm(x, g_ref[...], mod_ref[0, SC1:SC1 + 1, :], mod_ref[0, SH1:SH1 + 1, :]).astype(BF16)

    ya = jnp.dot(oa_ref[...], wpa_ref[...], preferred_element_type=F32)
    yb = jnp.dot(ob_ref[...], wpb_ref[...], preferred_element_type=F32)
    ga = jax.nn.sigmoid(jnp.dot(h, wg_ref[:, :d], preferred_element_type=F32) + bg_ref[:, :d])
    mix = ga * ya
    gb = jax.nn.sigmoid(jnp.dot(h, wg_ref[:, d:], preferred_element_type=F32) + bg_ref[:, d:])
    mix = mix + gb * yb
    y = jnp.dot(mix.astype(BF16), wo_ref[...], preferred_element_type=F32)
    out = x + mod_ref[0, GT1:GT1 + 1, :] * y
    out_ref[...] = out
    _route_block(out, mod_ref, g2_ref, wr_ref, eb_ref, tri_ref, hp_ref, eidx_ref, rank_ref, w_ref, cnt_ref)


def _outproj(x2, oa, ob, mod, g, w_gate, b_gate, w_pa, w_pb, w_o, g_ffn, wr_split, e_bias, seq, tm=512):
    n, d = x2.shape
    blocks_per_batch = seq // tm
    tri = jnp.asarray(np.triu(np.ones((tm, tm), np.float32)), BF16)
    row = lambda c: pl.BlockSpec((tm, c), lambda i: (i, 0))
    tok = lambda r: pl.BlockSpec((r, tm), lambda i: (0, i))
    full = lambda a: pl.BlockSpec(a.shape, lambda i: (0,) * a.ndim)
    return pl.pallas_call(
        _outproj_kernel,
        grid=(n // tm,),
        in_specs=[row(d), row(NA_WIDTH), row(DIL_OUT_WIDTH)] + [
            pl.BlockSpec((1, 6, d), lambda i: (i // blocks_per_batch, 0, 0)),
            full(g), full(w_gate), full(b_gate), full(w_pa), full(w_pb), full(w_o),
            full(g_ffn), full(wr_split), full(e_bias), full(tri),
        ],
        out_specs=[row(d), row(d // 2), tok(TOP_K), tok(TOP_K), row(LANES),
                   pl.BlockSpec((N_EXPERTS, LANES), lambda i: (0, 0))],
        out_shape=[
            jax.ShapeDtypeStruct((n, d), F32),
            jax.ShapeDtypeStruct((n, d // 2), jnp.int32),
            jax.ShapeDtypeStruct((TOP_K, n), jnp.int32),
            jax.ShapeDtypeStruct((TOP_K, n), jnp.int32),
            jax.ShapeDtypeStruct((n, LANES), F32),
            jax.ShapeDtypeStruct((N_EXPERTS, LANES), F32),
        ],
        compiler_params=_params(("arbitrary",)),
        name="out_proj",
    )(x2, oa, ob, mod, g, w_gate, b_gate, w_pa, w_pb, w_o, g_ffn, wr_split, e_bias, tri)


def _first_index_of_max(cur, idx, size):
    m = jnp.max(cur, axis=0, keepdims=True)
    first = jnp.min(jnp.where(cur == m, idx, size), axis=0, keepdims=True)
    return m, first


def _route_transposed(logits_t, e_bias):
    tokens = logits_t.shape[1]
    per_group = N_EXPERTS // N_EXPERT_GROUPS
    scores = jax.nn.sigmoid(logits_t)
    biased = scores + e_bias
    midx = lax.broadcasted_iota(jnp.int32, (per_group, tokens), 0)
    grp_scores = []
    for g in range(N_EXPERT_GROUPS):
        vals = biased[g * per_group:(g + 1) * per_group, :]
        m1, first = _first_index_of_max(vals, midx, per_group)
        m2 = jnp.max(jnp.where(midx == first, -jnp.inf, vals), axis=0, keepdims=True)
        grp_scores.append(m1 + m2)
    cur = jnp.concatenate(grp_scores, axis=0)
    gidx = lax.broadcasted_iota(jnp.int32, (N_EXPERT_GROUPS, tokens), 0)
    grp_sel = jnp.zeros((N_EXPERT_GROUPS, tokens), jnp.bool_)
    for _ in range(TOP_GROUPS):
        _, first = _first_index_of_max(cur, gidx, N_EXPERT_GROUPS)
        pick = gidx == first
        grp_sel = jnp.logical_or(grp_sel, pick)
        cur = jnp.where(pick, -jnp.inf, cur)
    rows = []
    for g in range(N_EXPERT_GROUPS):
        vals = biased[g * per_group:(g + 1) * per_group, :]
        rows.append(jnp.where(grp_sel[g:g + 1, :], vals, NEG_INF))
    cur = jnp.concatenate(rows, axis=0)
    eidx = lax.broadcasted_iota(jnp.int32, (N_EXPERTS, tokens), 0)
    firsts, picks, weights = [], [], []
    for _ in range(TOP_K):
        _, first = _first_index_of_max(cur, eidx, N_EXPERTS)
        pick = eidx == first
        firsts.append(first)
        picks.append(pick)
        weights.append(jnp.sum(jnp.where(pick, scores, 0.0), axis=0, keepdims=True))
        cur = jnp.where(pick, -jnp.inf, cur)
    total = functools.reduce(lambda a, b: a + b, weights)
    return firsts, picks, [w / total * ROUTED_SCALE for w in weights]


def _pack_halves(a):
    half = a.shape[1] // 2
    bits = lax.bitcast_convert_type(a.astype(BF16).astype(F32), jnp.int32)
    return lax.shift_right_logical(bits[:, :half], 16) | bits[:, half:]


def _unpack_halves(w):
    low = lax.bitcast_convert_type(lax.shift_left(w, 16), F32)
    high = lax.bitcast_convert_type(w & jnp.int32(-65536), F32)
    return low, high


def _route_block(x, mod_ref, g_ref, wr_ref, eb_ref, tri_ref, hp_ref, eidx_ref, rank_ref, w_ref, cnt_ref):
    tm = x.shape[0]

    @pl.when(pl.program_id(0) == 0)
    def _():
        cnt_ref[...] = jnp.zeros_like(cnt_ref)

    h = _modulated_norm(x, g_ref[...], mod_ref[0, SC2:SC2 + 1, :], mod_ref[0, SH2:SH2 + 1, :])
    hp_ref[...] = _pack_halves(h)
    h_hi = h.astype(BF16)
    h_lo = (h - h_hi.astype(F32)).astype(BF16)
    nt = (((1,), (1,)), ((), ()))
    logits_t = (lax.dot_general(wr_ref[0], h_hi, nt, preferred_element_type=F32)
                + lax.dot_general(wr_ref[0], h_lo, nt, preferred_element_type=F32)
                + lax.dot_general(wr_ref[1], h_hi, nt, preferred_element_type=F32))
    firsts, picks, weights = _route_transposed(logits_t, eb_ref[...])
    sel = functools.reduce(jnp.logical_or, picks)
    sel_f = jnp.where(sel, 1.0, 0.0)
    incl = jnp.dot(sel_f.astype(BF16), tri_ref[...], preferred_element_type=F32)
    before = cnt_ref[:, 0:1] + incl - sel_f
    eidx_ref[...] = jnp.concatenate(firsts, axis=0)
    rank_ref[...] = jnp.concatenate(
        [jnp.sum(jnp.where(p, before, 0.0), axis=0, keepdims=True) for p in picks], axis=0).astype(jnp.int32)
    pad = jnp.concatenate(weights + [jnp.zeros((LANES - TOP_K, tm), F32)], axis=0)
    w_ref[...] = pad.T
    cnt_ref[...] = cnt_ref[...] + incl[:, tm - 1:tm]


def _sc_worker_id():
    return lax.axis_index("subcore") * SC_CORES + lax.axis_index("core")


def _sc_scatter_rows(src, idx3, n_out):
    n, w = src.shape
    per_worker = n // SC_CHUNK // SC_WORKERS
    mesh = plsc.VectorSubcoreMesh(core_axis_name="core", subcore_axis_name="subcore")

    @functools.partial(
        pl.kernel, mesh=mesh, out_type=jax.ShapeDtypeStruct((n_out, w), src.dtype), name="moe_dispatch",
        scratch_types=[pltpu.VMEM((2, TOP_K, SC_CHUNK), jnp.int32), pltpu.VMEM((2, SC_CHUNK, w), src.dtype),
                       pltpu.SemaphoreType.DMA((2,)), pltpu.SemaphoreType.DMA((2,)), pltpu.SemaphoreType.DMA])
    def scatter(src_hbm, idx_hbm, out_hbm, idx_v, rows_v, idx_sem, row_sem, out_sem):
        first = _sc_worker_id() * per_worker

        def loads(chunk, slot):
            return (pltpu.make_async_copy(idx_hbm.at[chunk], idx_v.at[slot], idx_sem.at[slot]),
                    pltpu.make_async_copy(src_hbm.at[pl.ds(chunk * SC_CHUNK, SC_CHUNK)], rows_v.at[slot],
                                          row_sem.at[slot]))

        for cp in loads(first, 0):
            cp.start()

        @pl.loop(0, per_worker, step=2)
        def _(i):
            for slot in range(2):
                chunk = first + i + slot
                for cp in loads(chunk, slot):
                    cp.wait()

                @pl.when(i + slot + 1 < per_worker)
                def _():
                    for cp in loads(chunk + 1, 1 - slot):
                        cp.start()

                copies = [pltpu.make_async_copy(rows_v.at[slot], out_hbm.at[idx_v.at[slot, k]], out_sem)
                          for k in range(TOP_K)]
                for cp in copies:
                    cp.start()
                for cp in copies:
                    cp.wait()

    return scatter(src, idx3)


def _sc_gather_rows(src, idx3):
    _, w = src.shape
    chunks = idx3.shape[0]
    per_worker = chunks // SC_WORKERS
    mesh = plsc.VectorSubcoreMesh(core_axis_name="core", subcore_axis_name="subcore")

    @functools.partial(
        pl.kernel, mesh=mesh, out_type=jax.ShapeDtypeStruct((TOP_K, chunks * SC_CHUNK, w), src.dtype),
        name="moe_collect",
        scratch_types=[pltpu.VMEM((TOP_K, SC_CHUNK), jnp.int32), pltpu.VMEM((2, SC_CHUNK, w), src.dtype),
                       pltpu.SemaphoreType.DMA((2,)), pltpu.SemaphoreType.DMA((2,))])
    def gather(src_hbm, idx_hbm, out_hbm, idx_v, rows_v, in_sem, out_sem):
        first = _sc_worker_id() * per_worker

        @pl.loop(0, per_worker)
        def _(i):
            chunk = first + i
            pltpu.sync_copy(idx_hbm.at[chunk], idx_v)
            reads = [pltpu.make_async_copy(src_hbm.at[idx_v.at[k]], rows_v.at[k % 2], in_sem.at[k % 2])
                     for k in range(TOP_K)]
            writes = [pltpu.make_async_copy(rows_v.at[k % 2], out_hbm.at[k, pl.ds(chunk * SC_CHUNK, SC_CHUNK)],
                                            out_sem.at[k % 2]) for k in range(TOP_K)]
            reads[0].start()
            for k in range(TOP_K):
                if k + 1 < TOP_K:
                    if k >= 1:
                        writes[k - 1].wait()
                    reads[k + 1].start()
                reads[k].wait()
                writes[k].start()
            writes[TOP_K - 2].wait()
            writes[TOP_K - 1].wait()

    return gather(src, idx3)


def _swiglu(x, w_gate_up, w_down):
    gu = jnp.dot(x, w_gate_up, preferred_element_type=F32)
    gate = gu[:, :EXPERT_DIM]
    act = (gate * jax.nn.sigmoid(gate)) * gu[:, EXPERT_DIM:]
    return jnp.dot(act.astype(BF16), w_down, preferred_element_type=F32)


def _expert_ffn_kernel(be_ref, nv_ref, xs_ref, *refs, tb):
    slots = FFN_BLOCKS_PER_STEP
    w_refs, ys_ref, wgu_bf, wd_bf = refs[:3 * slots], refs[3 * slots], refs[3 * slots + 1], refs[3 * slots + 2]
    i = pl.program_id(0)
    sub = tb // FFN_SUB_BLOCKS
    row = lax.broadcasted_iota(jnp.int32, (sub, xs_ref.shape[1]), 0)

    for slot in range(slots):
        blk = i * slots + slot
        wg_ref, wu_ref, wd_ref = w_refs[3 * slot:3 * slot + 3]

        @pl.when(jnp.logical_or(i == 0, be_ref[blk] != be_ref[jnp.maximum(blk - slots, 0)]))
        def _():
            wgu_bf[slot, :, :EXPERT_DIM] = wg_ref[0, 0].astype(BF16)
            wgu_bf[slot, :, EXPERT_DIM:] = wu_ref[0, 0].astype(BF16)
            wd_bf[slot] = wd_ref[0, 0].astype(BF16)

    @pl.when(nv_ref[i * slots] > 0)
    def _():
        spans, xs = [], []
        for slot in range(slots):
            nvalid = nv_ref[i * slots + slot]
            for s in range(FFN_SUB_BLOCKS):
                span = pl.ds(slot * tb + s * sub, sub)
                packed = jnp.where(row < nvalid - s * sub, xs_ref[span, :], 0)
                low, high = _unpack_halves(packed)
                spans.append((slot, span))
                xs.append(jnp.concatenate([low, high], axis=1).astype(BF16))
        gus = [jnp.dot(x, wgu_bf[slot], preferred_element_type=F32) for (slot, _), x in zip(spans, xs)]
        acts = [((gu[:, :EXPERT_DIM] * jax.nn.sigmoid(gu[:, :EXPERT_DIM])) * gu[:, EXPERT_DIM:]).astype(BF16)
                for gu in gus]
        ys = [jnp.dot(a, wd_bf[slot], preferred_element_type=F32) for (slot, _), a in zip(spans, acts)]
        for (_, span), y in zip(spans, ys):
            ys_ref[span, :] = _pack_halves(y)

    @pl.when(nv_ref[i * slots] == 0)
    def _():
        ys_ref[...] = jnp.zeros_like(ys_ref)


def _expert_ffn(xs, block_expert, block_valid, w_gate, w_up, w_down, layer, tb):
    p, half = xs.shape
    d = 2 * half
    slots = FFN_BLOCKS_PER_STEP
    weight_specs = []
    for slot in range(slots):
        index = lambda i, be, nv, slot=slot: (layer, be[i * slots + slot], 0, 0)
        weight_specs += [pl.BlockSpec((1, 1, d, EXPERT_DIM), index), pl.BlockSpec((1, 1, d, EXPERT_DIM), index),
                         pl.BlockSpec((1, 1, EXPERT_DIM, d), index)]
    grid_spec = pltpu.PrefetchScalarGridSpec(
        num_scalar_prefetch=2,
        grid=(p // (tb * slots),),
        in_specs=[pl.BlockSpec((tb * slots, half), lambda i, be, nv: (i, 0))] + weight_specs,
        out_specs=pl.BlockSpec((tb * slots, half), lambda i, be, nv: (i, 0)),
        scratch_shapes=[pltpu.VMEM((slots, d, 2 * EXPERT_DIM), BF16), pltpu.VMEM((slots, EXPERT_DIM, d), BF16)],
    )
    return pl.pallas_call(
        functools.partial(_expert_ffn_kernel, tb=tb),
        grid_spec=grid_spec,
        out_shape=jax.ShapeDtypeStruct((p, half), jnp.int32),
        compiler_params=_params(("arbitrary",)),
        name="moe_expert_ffn",
    )(block_expert, block_valid, xs, *([w_gate, w_up, w_down] * slots))


def _combine_kernel(x_ref, hp_ref, yg_ref, w_ref, mod_ref, wgu_ref, wd_ref, gf_ref, o_ref, *, final_norm):
    low, high = _unpack_halves(hp_ref[...])
    h = jnp.concatenate([low, high], axis=1).astype(BF16)
    shared = _swiglu(h, wgu_ref[...], wd_ref[...])
    half = hp_ref.shape[1]
    acc_low, acc_high = shared[:, :half], shared[:, half:]
    w = w_ref[...]
    for k in range(TOP_K):
        low, high = _unpack_halves(yg_ref[k])
        wk = w[:, k:k + 1]
        acc_low = acc_low + wk * low
        acc_high = acc_high + wk * high
    y = jnp.concatenate([acc_low, acc_high], axis=1)
    out = x_ref[...] + mod_ref[0, GT2:GT2 + 1, :] * y
    if final_norm:
        out = out * lax.rsqrt(jnp.mean(out * out, axis=-1, keepdims=True) + EPS) * gf_ref[...]
    o_ref[...] = out


def _combine(x2, hp, yg, w, mod, ws_gate_up, ws_down, g_final, seq, final_norm, tm=512):
    n, d = x2.shape
    blocks_per_batch = seq // tm
    const = lambda a: pl.BlockSpec(a.shape, lambda i: (0,) * a.ndim)
    return pl.pallas_call(
        functools.partial(_combine_kernel, final_norm=final_norm),
        grid=(n // tm,),
        in_specs=[
            pl.BlockSpec((tm, d), lambda i: (i, 0)),
            pl.BlockSpec((tm, d // 2), lambda i: (i, 0)),
            pl.BlockSpec((TOP_K, tm, d // 2), lambda i: (0, i, 0)),
            pl.BlockSpec((tm, LANES), lambda i: (i, 0)),
            pl.BlockSpec((1, 6, d), lambda i: (i // blocks_per_batch, 0, 0)),
            const(ws_gate_up), const(ws_down), const(g_final),
        ],
        out_specs=pl.BlockSpec((tm, d), lambda i: (i, 0)),
        out_shape=jax.ShapeDtypeStruct((n, d), F32),
        compiler_params=_params(("arbitrary",)),
        name="moe_combine",
    )(x2, hp, yg, w, mod, ws_gate_up, ws_down, g_final)


def _token_mixer(x2, mod, g_mix, w_in, b_gate, rpb, w_pa, w_pb, w_o, g_ffn, w_router, e_bias, batch, seq):
    d = x2.shape[1]
    dil0 = 3 * NA_WIDTH
    group_cols = [[dil0 + part * DIL_WIDTH + grp * DIL_OUT_WIDTH for part in range(3)]
                  for grp in range(N_DIL_GROUPS)]
    order = sorted(range(N_DIL_GROUPS), key=lambda grp: DIL_GROUPS[grp][1] > 1)
    w_qkv = jnp.concatenate(
        [w_in[:, :dil0]] + [w_in[:, c:c + DIL_OUT_WIDTH] for grp in order for c in group_cols[grp]],
        axis=1).astype(BF16)
    tok, *residue = _inproj(x2, mod, g_mix.reshape(1, d), w_qkv, batch, seq)
    o_a = _neighbourhood_attention(tok, _na_bias_table(rpb), batch, seq)

    qkv_offsets = tuple(part * DIL_OUT_WIDTH // LANES for part in range(3))
    group_inputs = [None] * N_DIL_GROUPS
    residue = iter(residue)
    tok_offset = dil0 // LANES
    for grp in order:
        if DIL_GROUPS[grp][1] == 1:
            group_inputs[grp] = (tok.reshape(batch, 1, seq, tok.shape[1]),
                                 tuple(tok_offset + o for o in qkv_offsets))
            tok_offset += 3 * DIL_OUT_WIDTH // LANES
        else:
            group_inputs[grp] = (next(residue), qkv_offsets)
    o_b = _dilated_attention(group_inputs, batch, seq)

    wr_t = w_router.T
    wr_hi = wr_t.astype(BF16)
    wr_split = jnp.stack([wr_hi, (wr_t - wr_hi.astype(F32)).astype(BF16)])
    return _outproj(x2, o_a, o_b, mod, g_mix.reshape(1, d),
                    w_in[:, QKV_COLS:].astype(BF16), b_gate.reshape(1, -1),
                    w_pa.astype(BF16), w_pb.astype(BF16), w_o.astype(BF16),
                    g_ffn.reshape(1, d), wr_split, e_bias.reshape(-1, 1), seq)


def _dispatch_plan(eidx_t, rank_t, counts, tb):
    n = eidx_t.shape[1]
    n_blocks = -(-(n * TOP_K + N_EXPERTS * (tb - 1)) // tb)
    n_blocks = -(-n_blocks // FFN_BLOCKS_PER_STEP) * FFN_BLOCKS_PER_STEP
    padded = (counts + tb - 1) // tb * tb
    seg_end = jnp.cumsum(padded)
    seg_start = seg_end - padded
    experts = jnp.arange(N_EXPERTS, dtype=jnp.int32)

    def lookup(table, idx):
        sel = idx[None] == experts.reshape((N_EXPERTS,) + (1,) * idx.ndim)
        return jnp.sum(jnp.where(sel, table.reshape((N_EXPERTS,) + (1,) * idx.ndim), 0), axis=0)

    dest_t = lookup(seg_start, eidx_t) + rank_t
    idx3 = dest_t.reshape(TOP_K, n // SC_CHUNK, SC_CHUNK).transpose(1, 0, 2)
    block_start = jnp.arange(n_blocks, dtype=jnp.int32) * tb
    block_expert = jnp.sum((seg_end[:, None] <= block_start[None, :]).astype(jnp.int32), axis=0)
    block_expert = jnp.minimum(block_expert, N_EXPERTS - 1)
    block_valid = jnp.clip(lookup(counts, block_expert) - (block_start - lookup(seg_start, block_expert)), 0, tb)
    return idx3, block_expert, block_valid.astype(jnp.int32), n_blocks


def _moe_layer(x2, routed, mod, we_gate, we_up, we_down, layer, ws_gate, ws_up, ws_down, g_final, seq, final_norm,
               tb=EXPERT_ROW_BLOCK):
    d = x2.shape[1]
    hp, eidx_t, rank_t, w, cnt = routed
    counts = cnt[:, 0].astype(jnp.int32)
    idx3, block_expert, block_valid, n_blocks = _dispatch_plan(eidx_t, rank_t, counts, tb)
    xs = _sc_scatter_rows(hp, idx3, n_blocks * tb)
    ys = _expert_ffn(xs, block_expert, block_valid, we_gate, we_up, we_down, layer, tb)
    yg = _sc_gather_rows(ys, idx3)
    return _combine(x2, hp, yg, w, mod, jnp.concatenate([ws_gate, ws_up], axis=-1).astype(BF16),
                    ws_down.astype(BF16), g_final.reshape(1, d), seq, final_norm)


def kernel(x, c, w_ada, b_ada, g_mix, w_in, b_gate, rpb, w_pa, w_pb, w_o, g_ffn, w_router, e_bias,
           we_gate, we_up, we_down, ws_gate, ws_up, ws_down, g_final):
    batch, seq, d = x.shape
    depth = w_ada.shape[0]
    mods = _ada(c, w_ada, b_ada).reshape(depth, batch, 6, d)
    x2 = x.reshape(batch * seq, d)
    for l in range(depth):
        x2, *routed = _token_mixer(x2, mods[l], g_mix[l], w_in[l], b_gate[l], rpb[l], w_pa[l], w_pb[l], w_o[l],
                                   g_ffn[l], w_router[l], e_bias[l], batch, seq)
        x2 = _moe_layer(x2, routed, mods[l], we_gate, we_up, we_down, l,
                        ws_gate[l], ws_up[l], ws_down[l], g_final, seq, final_norm=(l == depth - 1))
    return x2.reshape(batch, seq, d)
```

```python
import functools

import numpy as np
import jax
import jax.numpy as jnp
from jax import lax
from jax.experimental import pallas as pl
from jax.experimental.pallas import tpu as pltpu
from jax.experimental.pallas import tpu_sc as plsc

HEAD_DIM = 64
GRID_W = 64
NA_HEADS = 8
NA_WIN_ROWS = 8
NA_WIN_COLS = 16
DIL_GROUPS = ((128, 1), (512, 4), (2048, 16))
DIL_HEADS_PER_GROUP = 4
N_DIL_GROUPS = len(DIL_GROUPS)
NA_WIDTH = NA_HEADS * HEAD_DIM
DIL_WIDTH = N_DIL_GROUPS * DIL_HEADS_PER_GROUP * HEAD_DIM
DIL_OUT_WIDTH = DIL_HEADS_PER_GROUP * HEAD_DIM
QKV_COLS = 3 * (NA_WIDTH + DIL_WIDTH)
N_EXPERTS = 64
TOP_K = 8
N_EXPERT_GROUPS = 8
TOP_GROUPS = 4
EXPERT_DIM = 256
ROUTED_SCALE = 2.5
ALIBI_MAX = 8.0
EPS = 1e-6
NEG_INF = -1e30

LANES = 128
HEADS_PER_LANE_TILE = LANES // HEAD_DIM
DIL_BLOCK = 64
VMEM_LIMIT_BYTES = 56 * 1024 * 1024

SC_CORES = 2
SC_SUBCORES = 16
SC_WORKERS = SC_CORES * SC_SUBCORES
SC_CHUNK = 64
EXPERT_ROW_BLOCK = 1024
FFN_SUB_BLOCKS = 4
FFN_BLOCKS_PER_STEP = 2

F32 = jnp.float32
BF16 = jnp.bfloat16
BF16_BITS = 16

SH1, SC1, GT1, SH2, SC2, GT2 = range(6)


def _params(sem):
    return pltpu.CompilerParams(dimension_semantics=sem, vmem_limit_bytes=VMEM_LIMIT_BYTES)


def _modulated_norm(x, g, scale, shift):
    r = lax.rsqrt(jnp.mean(x * x, axis=-1, keepdims=True) + EPS)
    return (x * r * g) * (1.0 + scale) + shift


def _ada_kernel(c_ref, w_ref, b_ref, o_ref):
    c = c_ref[...]
    act = c * jax.nn.sigmoid(c)
    o_ref[0] = jnp.dot(act, w_ref[0], preferred_element_type=F32,
                       precision=lax.Precision.HIGHEST) + b_ref[0]


def _ada(c, w_ada, b_ada):
    depth, d, six_d = w_ada.shape
    b = c.shape[0]
    tn = d
    return pl.pallas_call(
        _ada_kernel,
        grid=(depth, six_d // tn),
        in_specs=[
            pl.BlockSpec((b, d), lambda l, j: (0, 0)),
            pl.BlockSpec((1, d, tn), lambda l, j: (l, 0, j)),
            pl.BlockSpec((1, 1, tn), lambda l, j: (l, 0, j)),
        ],
        out_specs=pl.BlockSpec((1, b, tn), lambda l, j: (l, 0, j)),
        out_shape=jax.ShapeDtypeStruct((depth, b, six_d), F32),
        compiler_params=_params(("arbitrary", "arbitrary")),
        name="ada_mod",
    )(c, w_ada, b_ada.reshape(depth, 1, six_d))


def _inproj_kernel(x_ref, mod_ref, g_ref, w_ref, tok_ref, *rest, dilations):
    res_refs, acc_refs = rest[:len(dilations)], rest[len(dilations):]
    tm = x_ref.shape[0]
    tok_cols = tok_ref.shape[1]
    tn = (w_ref.shape[1] - tok_cols) // len(dilations)
    h = _modulated_norm(x_ref[...], g_ref[...], mod_ref[0, SC1:SC1 + 1, :], mod_ref[0, SH1:SH1 + 1, :]).astype(BF16)

    for c0 in range(0, tok_cols, tn):
        tok_ref[:, c0:c0 + tn] = jnp.dot(h, w_ref[:, c0:c0 + tn], preferred_element_type=F32).astype(BF16)

    for g, (res_ref, acc_ref, dilation) in enumerate(zip(res_refs, acc_refs, dilations)):
        c0 = tok_cols + g * tn
        res = jnp.dot(h, w_ref[:, c0:c0 + tn], preferred_element_type=F32)
        for c in range(acc_ref.shape[0]):
            acc_ref[c] = res[:, c * LANES:(c + 1) * LANES]
        for r in range(dilation):
            for c in range(acc_ref.shape[0]):
                res_ref[0, r, :, c * LANES:(c + 1) * LANES] = (
                    acc_ref[c, pl.ds(r, tm // dilation, stride=dilation), :].astype(BF16))


def _inproj(x2, mod, g, w_qkv, batch, seq, tm=512):
    n, d = x2.shape
    tn = 3 * DIL_OUT_WIDTH
    dilations = tuple(dil for _, dil in DIL_GROUPS if dil > 1)
    tok_cols = w_qkv.shape[1] - tn * len(dilations)
    blocks_per_batch = seq // tm
    res_specs = [pl.BlockSpec((1, dil, tm // dil, tn),
                              lambda i: (i // blocks_per_batch, 0, i % blocks_per_batch, 0)) for dil in dilations]
    res_shapes = [jax.ShapeDtypeStruct((batch, dil, seq // dil, tn), BF16) for dil in dilations]
    return pl.pallas_call(
        functools.partial(_inproj_kernel, dilations=dilations),
        grid=(n // tm,),
        in_specs=[
            pl.BlockSpec((tm, d), lambda i: (i, 0)),
            pl.BlockSpec((1, 6, d), lambda i: (i // blocks_per_batch, 0, 0)),
            pl.BlockSpec((1, d), lambda i: (0, 0)),
            pl.BlockSpec(w_qkv.shape, lambda i: (0, 0)),
        ],
        out_specs=[pl.BlockSpec((tm, tok_cols), lambda i: (i, 0))] + res_specs,
        out_shape=[jax.ShapeDtypeStruct((n, tok_cols), BF16)] + res_shapes,
        scratch_shapes=[pltpu.VMEM((tn // LANES, tm, LANES), F32) for _ in dilations],
        compiler_params=_params(("arbitrary",)),
        name="in_proj",
    )(x2, mod, g, w_qkv)


def _na_bias_table(rpb):
    heads = rpb.shape[0]
    cols = np.arange(GRID_W)
    col_start = np.clip(cols - NA_WIN_COLS // 2, 0, GRID_W - NA_WIN_COLS)
    col_mask = (cols[None, :] >= col_start[:, None]) & (cols[None, :] < col_start[:, None] + NA_WIN_COLS)
    edge = GRID_W - NA_WIN_COLS
    ext = jnp.concatenate([jnp.repeat(rpb[..., :1], edge, axis=-1), rpb, jnp.repeat(rpb[..., -1:], edge, axis=-1)],
                          axis=-1).astype(F32)
    rpb_cols = jnp.stack([ext[..., GRID_W - 1 - cq:2 * GRID_W - 1 - cq] for cq in range(GRID_W)],
                         axis=-2)
    t = jnp.stack([rpb_cols[:, NA_WIN_ROWS - 1 - off:2 * NA_WIN_ROWS - 1 - off] for off in range(NA_WIN_ROWS)],
                  axis=1)
    t = t.transpose(0, 1, 3, 2, 4)
    t = jnp.where(col_mask[:, None, :], t, NEG_INF)
    t = t.reshape(heads // HEADS_PER_LANE_TILE, HEADS_PER_LANE_TILE, NA_WIN_ROWS, GRID_W, NA_WIN_ROWS * GRID_W)
    return t.transpose(0, 2, 1, 3, 4).reshape(heads // HEADS_PER_LANE_TILE, NA_WIN_ROWS,
                                              HEADS_PER_LANE_TILE * GRID_W, NA_WIN_ROWS * GRID_W)


def _stack_heads(q, low):
    scaled = q * (HEAD_DIM ** -0.5)
    zero = jnp.zeros_like(scaled)
    return jnp.concatenate([jnp.where(low, scaled, zero), jnp.where(low, zero, scaled)], axis=0)


def _stacked_attention(items):
    scores = [lax.dot_general(q2, kw, (((1,), (1,)), ((), ())), preferred_element_type=F32) + bias
              for q2, kw, _, bias in items]
    probs = []
    for s in scores:
        m = jnp.max(s, axis=-1, keepdims=True)
        p = jnp.exp(s - m)
        probs.append((p.astype(BF16), m, jnp.sum(p, axis=-1, keepdims=True)))
    return [(jnp.dot(p, vw, preferred_element_type=F32) / z, m, z)
            for (p, m, z), (_, _, vw, _) in zip(probs, items)]


def _unstack_heads(a, low):
    half = a.shape[0] // HEADS_PER_LANE_TILE
    return jnp.where(low, a[:half], a[half:])


def _na_kernel(q_ref, k_ref, v_ref, bias_ref, o_ref, *, rows, rows_per_step):
    kr = NA_WIN_ROWS
    low = lax.broadcasted_iota(jnp.int32, (GRID_W, LANES), 1) < HEAD_DIM

    def body(i, carry):
        items, qrows = [], []
        for u in range(rows_per_step):
            r = i * rows_per_step + u
            rs = jnp.clip(r - kr // 2, 0, rows - kr)
            qrows.append(pl.ds(pl.multiple_of(r * GRID_W, GRID_W), GRID_W))
            wrows = pl.ds(pl.multiple_of(rs * GRID_W, GRID_W), kr * GRID_W)
            items.append((_stack_heads(q_ref[qrows[-1], :], low), k_ref[wrows, :], v_ref[wrows, :],
                          bias_ref[0, r - rs]))
        for rows_u, (o, _, _) in zip(qrows, _stacked_attention(items)):
            o_ref[rows_u, :] = _unstack_heads(o, low).astype(o_ref.dtype)
        return carry

    lax.fori_loop(0, rows // rows_per_step, body, 0)


def _neighbourhood_attention(qkv, bias, batch, seq, rows_per_step=16):
    n = qkv.shape[0]
    rows = seq // GRID_W
    pairs = NA_WIDTH // LANES
    return pl.pallas_call(
        functools.partial(_na_kernel, rows=rows, rows_per_step=rows_per_step),
        grid=(pairs, batch),
        in_specs=[
            pl.BlockSpec((seq, LANES), lambda p, b: (b, p)),
            pl.BlockSpec((seq, LANES), lambda p, b: (b, pairs + p)),
            pl.BlockSpec((seq, LANES), lambda p, b: (b, 2 * pairs + p)),
            pl.BlockSpec((1, NA_WIN_ROWS, HEADS_PER_LANE_TILE * GRID_W, NA_WIN_ROWS * GRID_W),
                         lambda p, b: (p, 0, 0, 0)),
        ],
        out_specs=pl.BlockSpec((seq, LANES), lambda p, b: (b, p)),
        out_shape=jax.ShapeDtypeStruct((n, NA_WIDTH), BF16),
        compiler_params=_params(("arbitrary", "arbitrary")),
        name="na_attn",
    )(qkv, qkv, qkv, bias)


def _alibi_slopes():
    n = N_DIL_GROUPS * DIL_HEADS_PER_GROUP
    s = np.exp2(-ALIBI_MAX * np.arange(1, n + 1, dtype=np.float64) / n).astype(np.float32)
    return s.reshape(N_DIL_GROUPS, DIL_HEADS_PER_GROUP)


def _dil_bias_table(group):
    blk = DIL_BLOCK
    dilation = DIL_GROUPS[group][1]
    slopes = _alibi_slopes()[group]
    qi = np.arange(blk)[:, None]
    kj = np.arange(3 * blk)[None, :]
    tables = []
    for shift in range(3):
        arel = np.abs(kj - qi - shift * blk)
        dist = (dilation * arel).astype(np.float32)
        per_head = [np.where(arel <= blk, -slopes[h] * dist, np.float32(NEG_INF)) for h in range(DIL_HEADS_PER_GROUP)]
        tables.append(np.stack(per_head))
    t = np.stack(tables, axis=1).astype(np.float32)
    pairs = DIL_HEADS_PER_GROUP // HEADS_PER_LANE_TILE
    t = t.reshape(pairs, HEADS_PER_LANE_TILE, 3, blk, 3 * blk).transpose(0, 2, 1, 3, 4)
    return t.reshape(pairs, 3, HEADS_PER_LANE_TILE * blk, 3 * blk)


def _dil_group(q_ref, k_ref, v_ref, bias_ref, o_ref, lse_ref, blocks_per_step):
    blk = DIL_BLOCK
    win = 3 * blk
    low = lax.broadcasted_iota(jnp.int32, (blk, LANES), 1) < HEAD_DIM
    dilation, length, _ = q_ref.shape
    nb = length // blk
    steps = dilation * nb

    def token_rows(sq, n):
        if dilation == 1:
            return pl.ds(pl.multiple_of(n * blk, blk), blk)
        return pl.ds(n * (blk * dilation) + sq, blk, stride=dilation)

    def body(i, carry):
        items, dst = [], []
        for u in range(blocks_per_step):
            t = i * blocks_per_step + u
            sq = t // nb
            n = t % nb
            wb = jnp.clip(n - 1, 0, nb - 3)
            qrows = pl.ds(pl.multiple_of(n * blk, blk), blk)
            wrows = pl.ds(pl.multiple_of(wb * blk, blk), win)
            dst.append(token_rows(sq, n))
            items.append((_stack_heads(q_ref[sq, qrows, :], low), k_ref[sq, wrows, :], v_ref[sq, wrows, :],
                          bias_ref[0, n - wb]))
        for rows, (o, m, z) in zip(dst, _stacked_attention(items)):
            o_ref[rows, :] = _unstack_heads(o, low)
            lse_ref[rows, :] = _unstack_heads(jnp.broadcast_to(m + jnp.log(z), o.shape), low)
        return carry

    lax.fori_loop(0, steps // blocks_per_step, body, 0)


def _dil_kernel(*refs, blocks_per_step, merge_rows):
    ng = N_DIL_GROUPS
    qkv_refs, bias_refs = refs[:3 * ng], refs[3 * ng:4 * ng]
    ob_ref, o_scr, lse_scr = refs[4 * ng:]
    for g in range(ng):
        q_ref, k_ref, v_ref = qkv_refs[3 * g:3 * g + 3]
        _dil_group(q_ref, k_ref, v_ref, bias_refs[g], o_scr.at[g], lse_scr.at[g], blocks_per_step)

    def merge(c, carry):
        rows = pl.ds(pl.multiple_of(c * merge_rows, merge_rows), merge_rows)
        lses = [lse_scr[g, rows, :] for g in range(ng)]
        top = functools.reduce(jnp.maximum, lses)
        es = [jnp.exp(l - top) for l in lses]
        num = functools.reduce(lambda a, b: a + b, [e * o_scr[g, rows, :] for g, e in enumerate(es)])
        ob_ref[rows, :] = (num / functools.reduce(lambda a, b: a + b, es)).astype(ob_ref.dtype)
        return carry

    lax.fori_loop(0, ob_ref.shape[0] // merge_rows, merge, 0)


def _dilated_attention(group_inputs, batch, seq, blocks_per_step=16, merge_rows=256):
    pairs = DIL_OUT_WIDTH // LANES
    operands, specs = [], []
    for qkv4, offsets in group_inputs:
        _, dilation, length, _ = qkv4.shape
        for off in offsets:
            operands.append(qkv4)
            specs.append(pl.BlockSpec((None, dilation, length, LANES),
                                      lambda p, b, off=off: (b, 0, 0, off + p)))
    for group in range(N_DIL_GROUPS):
        bias = jnp.asarray(_dil_bias_table(group))
        operands.append(bias)
        specs.append(pl.BlockSpec((1,) + bias.shape[1:], lambda p, b: (p, 0, 0, 0)))
    return pl.pallas_call(
        functools.partial(_dil_kernel, blocks_per_step=blocks_per_step, merge_rows=merge_rows),
        grid=(pairs, batch),
        in_specs=specs,
        out_specs=pl.BlockSpec((seq, LANES), lambda p, b: (b, p)),
        out_shape=jax.ShapeDtypeStruct((batch * seq, DIL_OUT_WIDTH), BF16),
        scratch_shapes=[pltpu.VMEM((N_DIL_GROUPS, seq, LANES), F32), pltpu.VMEM((N_DIL_GROUPS, seq, LANES), F32)],
        compiler_params=_params(("arbitrary", "arbitrary")),
        name="dil_attn",
    )(*operands)


def _outproj_kernel(x_ref, oa_ref, ob_ref, mod_ref, g_ref,
                    wg_ref, bg_ref, wpa_ref, wpb_ref, wo_ref, g2_ref, wr_ref, eb_ref, tri_ref,
                    out_ref, hp_ref, eidx_ref, rank_ref, w_ref, cnt_ref):
    d = x_ref.shape[1]
    x = x_ref[...]
    h = _modulated_norm(x, g_ref[...], mod_ref[0, SC1:SC1 + 1, :], mod_ref[0, SH1:SH1 + 1, :]).astype(BF16)

    ya = jnp.dot(oa_ref[...], wpa_ref[...], preferred_element_type=F32)
    yb = jnp.dot(ob_ref[...], wpb_ref[...], preferred_element_type=F32)
    ga = jax.nn.sigmoid(jnp.dot(h, wg_ref[:, :d], preferred_element_type=F32) + bg_ref[:, :d])
    mix = ga * ya
    gb = jax.nn.sigmoid(jnp.dot(h, wg_ref[:, d:], preferred_element_type=F32) + bg_ref[:, d:])
    mix = mix + gb * yb
    y = jnp.dot(mix.astype(BF16), wo_ref[...], preferred_element_type=F32)
    out = x + mod_ref[0, GT1:GT1 + 1, :] * y
    out_ref[...] = out
    _route_block(out, mod_ref, g2_ref, wr_ref, eb_ref, tri_ref, hp_ref, eidx_ref, rank_ref, w_ref, cnt_ref)


def _outproj(x2, oa, ob, mod, g, w_gate, b_gate, w_pa, w_pb, w_o, g_ffn, wr_split, e_bias, seq, tm=512):
    n, d = x2.shape
    assert n % tm == 0 and seq % tm == 0 and tm % SC_CHUNK == 0
    blocks_per_batch = seq // tm
    tri = jnp.asarray(np.triu(np.ones((tm, tm), np.float32)), BF16)
    row = lambda c: pl.BlockSpec((tm, c), lambda i: (i, 0))
    chunked = pl.BlockSpec((tm // SC_CHUNK, TOP_K, SC_CHUNK), lambda i: (i, 0, 0))
    full = lambda a: pl.BlockSpec(a.shape, lambda i: (0,) * a.ndim)
    return pl.pallas_call(
        _outproj_kernel,
        grid=(n // tm,),
        in_specs=[row(d), row(NA_WIDTH), row(DIL_OUT_WIDTH)] + [
            pl.BlockSpec((1, 6, d), lambda i: (i // blocks_per_batch, 0, 0)),
            full(g), full(w_gate), full(b_gate), full(w_pa), full(w_pb), full(w_o),
            full(g_ffn), full(wr_split), full(e_bias), full(tri),
        ],
        out_specs=[row(d), row(d // 2), chunked, chunked, row(LANES),
                   pl.BlockSpec((N_EXPERTS, LANES), lambda i: (0, 0))],
        out_shape=[
            jax.ShapeDtypeStruct((n, d), F32),
            jax.ShapeDtypeStruct((n, d // 2), jnp.int32),
            jax.ShapeDtypeStruct((n // SC_CHUNK, TOP_K, SC_CHUNK), jnp.int32),
            jax.ShapeDtypeStruct((n // SC_CHUNK, TOP_K, SC_CHUNK), jnp.int32),
            jax.ShapeDtypeStruct((n, LANES), F32),
            jax.ShapeDtypeStruct((N_EXPERTS, LANES), F32),
        ],
        compiler_params=_params(("arbitrary",)),
        name="out_proj",
    )(x2, oa, ob, mod, g, w_gate, b_gate, w_pa, w_pb, w_o, g_ffn, wr_split, e_bias, tri)


def _first_index_of_max(cur, idx, size):
    m = jnp.max(cur, axis=0, keepdims=True)
    first = jnp.min(jnp.where(cur == m, idx, size), axis=0, keepdims=True)
    return m, first


def _route_transposed(logits_t, e_bias):
    tokens = logits_t.shape[1]
    per_group = N_EXPERTS // N_EXPERT_GROUPS
    scores = jax.nn.sigmoid(logits_t)
    biased = scores + e_bias
    midx = lax.broadcasted_iota(jnp.int32, (per_group, tokens), 0)
    grp_scores = []
    for g in range(N_EXPERT_GROUPS):
        vals = biased[g * per_group:(g + 1) * per_group, :]
        m1, first = _first_index_of_max(vals, midx, per_group)
        m2 = jnp.max(jnp.where(midx == first, -jnp.inf, vals), axis=0, keepdims=True)
        grp_scores.append(m1 + m2)
    cur = jnp.concatenate(grp_scores, axis=0)
    gidx = lax.broadcasted_iota(jnp.int32, (N_EXPERT_GROUPS, tokens), 0)
    grp_sel = jnp.zeros((N_EXPERT_GROUPS, tokens), jnp.bool_)
    for _ in range(TOP_GROUPS):
        _, first = _first_index_of_max(cur, gidx, N_EXPERT_GROUPS)
        pick = gidx == first
        grp_sel = jnp.logical_or(grp_sel, pick)
        cur = jnp.where(pick, -jnp.inf, cur)
    rows = []
    for g in range(N_EXPERT_GROUPS):
        vals = biased[g * per_group:(g + 1) * per_group, :]
        rows.append(jnp.where(grp_sel[g:g + 1, :], vals, NEG_INF))
    cur = jnp.concatenate(rows, axis=0)
    eidx = lax.broadcasted_iota(jnp.int32, (N_EXPERTS, tokens), 0)
    firsts, picks, weights = [], [], []
    for _ in range(TOP_K):
        _, first = _first_index_of_max(cur, eidx, N_EXPERTS)
        pick = eidx == first
        firsts.append(first)
        picks.append(pick)
        weights.append(jnp.sum(jnp.where(pick, scores, 0.0), axis=0, keepdims=True))
        cur = jnp.where(pick, -jnp.inf, cur)
    total = functools.reduce(lambda a, b: a + b, weights)
    return firsts, picks, [w / total * ROUTED_SCALE for w in weights]


def _pack_halves(a):
    half = a.shape[1] // 2
    bits = lax.bitcast_convert_type(a.astype(BF16).astype(F32), jnp.int32)
    return lax.shift_right_logical(bits[:, :half], BF16_BITS) | bits[:, half:]


def _unpack_halves(w):
    low = lax.bitcast_convert_type(lax.shift_left(w, BF16_BITS), F32)
    high = lax.bitcast_convert_type(w & jnp.int32(-(1 << BF16_BITS)), F32)
    return low, high


def _route_block(x, mod_ref, g_ref, wr_ref, eb_ref, tri_ref, hp_ref, eidx_ref, rank_ref, w_ref, cnt_ref):
    tm = x.shape[0]

    @pl.when(pl.program_id(0) == 0)
    def _():
        cnt_ref[...] = jnp.zeros_like(cnt_ref)

    h = _modulated_norm(x, g_ref[...], mod_ref[0, SC2:SC2 + 1, :], mod_ref[0, SH2:SH2 + 1, :])
    hp_ref[...] = _pack_halves(h)
    h_hi = h.astype(BF16)
    h_lo = (h - h_hi.astype(F32)).astype(BF16)
    nt = (((1,), (1,)), ((), ()))
    logits_t = (lax.dot_general(wr_ref[0], h_hi, nt, preferred_element_type=F32)
                + lax.dot_general(wr_ref[0], h_lo, nt, preferred_element_type=F32)
                + lax.dot_general(wr_ref[1], h_hi, nt, preferred_element_type=F32))
    firsts, picks, weights = _route_transposed(logits_t, eb_ref[...])
    sel = functools.reduce(jnp.logical_or, picks)
    sel_f = jnp.where(sel, 1.0, 0.0)
    incl = jnp.dot(sel_f.astype(BF16), tri_ref[...], preferred_element_type=F32)
    before = cnt_ref[:, 0:1] + incl - sel_f
    eidx = jnp.concatenate(firsts, axis=0)
    rank = jnp.concatenate(
        [jnp.sum(jnp.where(p, before, 0.0), axis=0, keepdims=True) for p in picks], axis=0).astype(jnp.int32)
    for c in range(tm // SC_CHUNK):
        eidx_ref[c] = eidx[:, c * SC_CHUNK:(c + 1) * SC_CHUNK]
        rank_ref[c] = rank[:, c * SC_CHUNK:(c + 1) * SC_CHUNK]
    pad = jnp.concatenate(weights + [jnp.zeros((LANES - TOP_K, tm), F32)], axis=0)
    w_ref[...] = pad.T
    cnt_ref[...] = cnt_ref[...] + incl[:, tm - 1:tm]


def _sc_worker_id():
    return lax.axis_index("subcore") * SC_CORES + lax.axis_index("core")


def _sc_scatter_rows(src, idx3, n_out):
    n, w = src.shape
    assert n % (2 * SC_CHUNK * SC_WORKERS) == 0
    per_worker = n // SC_CHUNK // SC_WORKERS
    mesh = plsc.VectorSubcoreMesh(core_axis_name="core", subcore_axis_name="subcore")

    @functools.partial(
        pl.kernel, mesh=mesh, out_type=jax.ShapeDtypeStruct((n_out, w), src.dtype), name="moe_dispatch",
        scratch_types=[pltpu.VMEM((2, TOP_K, SC_CHUNK), jnp.int32), pltpu.VMEM((2, SC_CHUNK, w), src.dtype),
                       pltpu.SemaphoreType.DMA((2,)), pltpu.SemaphoreType.DMA((2,)), pltpu.SemaphoreType.DMA])
    def scatter(src_hbm, idx_hbm, out_hbm, idx_v, rows_v, idx_sem, row_sem, out_sem):
        first = _sc_worker_id() * per_worker

        def loads(chunk, slot):
            return (pltpu.make_async_copy(idx_hbm.at[chunk], idx_v.at[slot], idx_sem.at[slot]),
                    pltpu.make_async_copy(src_hbm.at[pl.ds(chunk * SC_CHUNK, SC_CHUNK)], rows_v.at[slot],
                                          row_sem.at[slot]))

        for cp in loads(first, 0):
            cp.start()

        @pl.loop(0, per_worker, step=2)
        def _(i):
            for slot in range(2):
                chunk = first + i + slot
                for cp in loads(chunk, slot):
                    cp.wait()

                @pl.when(i + slot + 1 < per_worker)
                def _():
                    for cp in loads(chunk + 1, 1 - slot):
                        cp.start()

                copies = [pltpu.make_async_copy(rows_v.at[slot], out_hbm.at[idx_v.at[slot, k]], out_sem)
                          for k in range(TOP_K)]
                for cp in copies:
                    cp.start()
                for cp in copies:
                    cp.wait()

    return scatter(src, idx3)


def _sc_gather_rows(src, idx3):
    _, w = src.shape
    chunks = idx3.shape[0]
    assert chunks % SC_WORKERS == 0
    per_worker = chunks // SC_WORKERS
    mesh = plsc.VectorSubcoreMesh(core_axis_name="core", subcore_axis_name="subcore")

    @functools.partial(
        pl.kernel, mesh=mesh, out_type=jax.ShapeDtypeStruct((TOP_K, chunks * SC_CHUNK, w), src.dtype),
        name="moe_collect",
        scratch_types=[pltpu.VMEM((TOP_K, SC_CHUNK), jnp.int32), pltpu.VMEM((2, SC_CHUNK, w), src.dtype),
                       pltpu.SemaphoreType.DMA((2,)), pltpu.SemaphoreType.DMA((2,))])
    def gather(src_hbm, idx_hbm, out_hbm, idx_v, rows_v, in_sem, out_sem):
        first = _sc_worker_id() * per_worker

        @pl.loop(0, per_worker)
        def _(i):
            chunk = first + i
            pltpu.sync_copy(idx_hbm.at[chunk], idx_v)
            reads = [pltpu.make_async_copy(src_hbm.at[idx_v.at[k]], rows_v.at[k % 2], in_sem.at[k % 2])
                     for k in range(TOP_K)]
            writes = [pltpu.make_async_copy(rows_v.at[k % 2], out_hbm.at[k, pl.ds(chunk * SC_CHUNK, SC_CHUNK)],
                                            out_sem.at[k % 2]) for k in range(TOP_K)]
            reads[0].start()
            for k in range(TOP_K):
                if k + 1 < TOP_K:
                    if k >= 1:
                        writes[k - 1].wait()
                    reads[k + 1].start()
                reads[k].wait()
                writes[k].start()
            writes[TOP_K - 2].wait()
            writes[TOP_K - 1].wait()

    return gather(src, idx3)


def _swiglu(x, w_gate_up, w_down):
    gu = jnp.dot(x, w_gate_up, preferred_element_type=F32)
    gate = gu[:, :EXPERT_DIM]
    act = (gate * jax.nn.sigmoid(gate)) * gu[:, EXPERT_DIM:]
    return jnp.dot(act.astype(BF16), w_down, preferred_element_type=F32)


def _expert_ffn_kernel(be_ref, nv_ref, xs_ref, *refs, tb):
    slots = FFN_BLOCKS_PER_STEP
    w_refs, ys_ref, wgu_bf, wd_bf = refs[:3 * slots], refs[3 * slots], refs[3 * slots + 1], refs[3 * slots + 2]
    i = pl.program_id(0)
    sub = tb // FFN_SUB_BLOCKS
    row = lax.broadcasted_iota(jnp.int32, (sub, xs_ref.shape[1]), 0)

    for slot in range(slots):
        blk = i * slots + slot
        wg_ref, wu_ref, wd_ref = w_refs[3 * slot:3 * slot + 3]

        @pl.when(jnp.logical_or(i == 0, be_ref[blk] != be_ref[jnp.maximum(blk - slots, 0)]))
        def _():
            wgu_bf[slot, :, :EXPERT_DIM] = wg_ref[0, 0].astype(BF16)
            wgu_bf[slot, :, EXPERT_DIM:] = wu_ref[0, 0].astype(BF16)
            wd_bf[slot] = wd_ref[0, 0].astype(BF16)

    @pl.when(nv_ref[i * slots] > 0)
    def _():
        spans, xs = [], []
        for slot in range(slots):
            nvalid = nv_ref[i * slots + slot]
            for s in range(FFN_SUB_BLOCKS):
                span = pl.ds(slot * tb + s * sub, sub)
                packed = jnp.where(row < nvalid - s * sub, xs_ref[span, :], 0)
                low, high = _unpack_halves(packed)
                spans.append((slot, span))
                xs.append(jnp.concatenate([low, high], axis=1).astype(BF16))
        gus = [jnp.dot(x, wgu_bf[slot], preferred_element_type=F32) for (slot, _), x in zip(spans, xs)]
        acts = [((gu[:, :EXPERT_DIM] * jax.nn.sigmoid(gu[:, :EXPERT_DIM])) * gu[:, EXPERT_DIM:]).astype(BF16)
                for gu in gus]
        ys = [jnp.dot(a, wd_bf[slot], preferred_element_type=F32) for (slot, _), a in zip(spans, acts)]
        for (_, span), y in zip(spans, ys):
            ys_ref[span, :] = _pack_halves(y)

    @pl.when(nv_ref[i * slots] == 0)
    def _():
        ys_ref[...] = jnp.zeros_like(ys_ref)


def _expert_ffn(xs, block_expert, block_valid, w_gate, w_up, w_down, layer, tb):
    p, half = xs.shape
    d = 2 * half
    slots = FFN_BLOCKS_PER_STEP
    weight_specs = []
    for slot in range(slots):
        index = lambda i, be, nv, slot=slot: (layer, be[i * slots + slot], 0, 0)
        weight_specs += [pl.BlockSpec((1, 1, d, EXPERT_DIM), index), pl.BlockSpec((1, 1, d, EXPERT_DIM), index),
                         pl.BlockSpec((1, 1, EXPERT_DIM, d), index)]
    grid_spec = pltpu.PrefetchScalarGridSpec(
        num_scalar_prefetch=2,
        grid=(p // (tb * slots),),
        in_specs=[pl.BlockSpec((tb * slots, half), lambda i, be, nv: (i, 0))] + weight_specs,
        out_specs=pl.BlockSpec((tb * slots, half), lambda i, be, nv: (i, 0)),
        scratch_shapes=[pltpu.VMEM((slots, d, 2 * EXPERT_DIM), BF16), pltpu.VMEM((slots, EXPERT_DIM, d), BF16)],
    )
    return pl.pallas_call(
        functools.partial(_expert_ffn_kernel, tb=tb),
        grid_spec=grid_spec,
        out_shape=jax.ShapeDtypeStruct((p, half), jnp.int32),
        compiler_params=_params(("arbitrary",)),
        name="moe_expert_ffn",
    )(block_expert, block_valid, xs, *([w_gate, w_up, w_down] * slots))


def _combine_kernel(x_ref, hp_ref, yg_ref, w_ref, mod_ref, wgu_ref, wd_ref, gf_ref, o_ref, *, final_norm):
    low, high = _unpack_halves(hp_ref[...])
    h = jnp.concatenate([low, high], axis=1).astype(BF16)
    shared = _swiglu(h, wgu_ref[...], wd_ref[...])
    half = hp_ref.shape[1]
    acc_low, acc_high = shared[:, :half], shared[:, half:]
    w = w_ref[...]
    for k in range(TOP_K):
        low, high = _unpack_halves(yg_ref[k])
        wk = w[:, k:k + 1]
        acc_low = acc_low + wk * low
        acc_high = acc_high + wk * high
    y = jnp.concatenate([acc_low, acc_high], axis=1)
    out = x_ref[...] + mod_ref[0, GT2:GT2 + 1, :] * y
    if final_norm:
        out = out * lax.rsqrt(jnp.mean(out * out, axis=-1, keepdims=True) + EPS) * gf_ref[...]
    o_ref[...] = out


def _combine(x2, hp, yg, w, mod, ws_gate_up, ws_down, g_final, seq, final_norm, tm=512):
    n, d = x2.shape
    blocks_per_batch = seq // tm
    const = lambda a: pl.BlockSpec(a.shape, lambda i: (0,) * a.ndim)
    return pl.pallas_call(
        functools.partial(_combine_kernel, final_norm=final_norm),
        grid=(n // tm,),
        in_specs=[
            pl.BlockSpec((tm, d), lambda i: (i, 0)),
            pl.BlockSpec((tm, d // 2), lambda i: (i, 0)),
            pl.BlockSpec((TOP_K, tm, d // 2), lambda i: (0, i, 0)),
            pl.BlockSpec((tm, LANES), lambda i: (i, 0)),
            pl.BlockSpec((1, 6, d), lambda i: (i // blocks_per_batch, 0, 0)),
            const(ws_gate_up), const(ws_down), const(g_final),
        ],
        out_specs=pl.BlockSpec((tm, d), lambda i: (i, 0)),
        out_shape=jax.ShapeDtypeStruct((n, d), F32),
        compiler_params=_params(("arbitrary",)),
        name="moe_combine",
    )(x2, hp, yg, w, mod, ws_gate_up, ws_down, g_final)


def _token_mixer(x2, mod, g_mix, w_in, b_gate, rpb, w_pa, w_pb, w_o, g_ffn, w_router, e_bias, batch, seq):
    d = x2.shape[1]
    dil0 = 3 * NA_WIDTH
    group_cols = [[dil0 + part * DIL_WIDTH + grp * DIL_OUT_WIDTH for part in range(3)]
                  for grp in range(N_DIL_GROUPS)]
    order = sorted(range(N_DIL_GROUPS), key=lambda grp: DIL_GROUPS[grp][1] > 1)
    w_qkv = jnp.concatenate(
        [w_in[:, :dil0]] + [w_in[:, c:c + DIL_OUT_WIDTH] for grp in order for c in group_cols[grp]],
        axis=1).astype(BF16)
    tok, *residue = _inproj(x2, mod, g_mix.reshape(1, d), w_qkv, batch, seq)
    o_a = _neighbourhood_attention(tok, _na_bias_table(rpb), batch, seq)

    qkv_offsets = tuple(part * DIL_OUT_WIDTH // LANES for part in range(3))
    group_inputs = [None] * N_DIL_GROUPS
    residue = iter(residue)
    tok_offset = dil0 // LANES
    for grp in order:
        if DIL_GROUPS[grp][1] == 1:
            group_inputs[grp] = (tok.reshape(batch, 1, seq, tok.shape[1]),
                                 tuple(tok_offset + o for o in qkv_offsets))
            tok_offset += 3 * DIL_OUT_WIDTH // LANES
        else:
            group_inputs[grp] = (next(residue), qkv_offsets)
    o_b = _dilated_attention(group_inputs, batch, seq)

    wr_t = w_router.T
    wr_hi = wr_t.astype(BF16)
    wr_split = jnp.stack([wr_hi, (wr_t - wr_hi.astype(F32)).astype(BF16)])
    return _outproj(x2, o_a, o_b, mod, g_mix.reshape(1, d),
                    w_in[:, QKV_COLS:].astype(BF16), b_gate.reshape(1, -1),
                    w_pa.astype(BF16), w_pb.astype(BF16), w_o.astype(BF16),
                    g_ffn.reshape(1, d), wr_split, e_bias.reshape(-1, 1), seq)


def _dispatch_plan(eidx3, rank3, counts, tb):
    n = eidx3.shape[0] * SC_CHUNK
    n_blocks = -(-(n * TOP_K + N_EXPERTS * (tb - 1)) // tb)
    n_blocks = -(-n_blocks // FFN_BLOCKS_PER_STEP) * FFN_BLOCKS_PER_STEP
    padded = (counts + tb - 1) // tb * tb
    seg_end = jnp.cumsum(padded)
    seg_start = seg_end - padded
    experts = jnp.arange(N_EXPERTS, dtype=jnp.int32)

    def lookup(table, idx):
        sel = idx[None] == experts.reshape((N_EXPERTS,) + (1,) * idx.ndim)
        return jnp.sum(jnp.where(sel, table.reshape((N_EXPERTS,) + (1,) * idx.ndim), 0), axis=0)

    idx3 = lookup(seg_start, eidx3) + rank3
    block_start = jnp.arange(n_blocks, dtype=jnp.int32) * tb
    block_expert = jnp.sum((seg_end[:, None] <= block_start[None, :]).astype(jnp.int32), axis=0)
    block_expert = jnp.minimum(block_expert, N_EXPERTS - 1)
    block_valid = jnp.clip(lookup(counts, block_expert) - (block_start - lookup(seg_start, block_expert)), 0, tb)
    return idx3, block_expert, block_valid.astype(jnp.int32), n_blocks


def _moe_layer(x2, routed, mod, we_gate, we_up, we_down, layer, ws_gate, ws_up, ws_down, g_final, seq, final_norm,
               tb=EXPERT_ROW_BLOCK):
    d = x2.shape[1]
    hp, eidx3, rank3, w, cnt = routed
    counts = cnt[:, 0].astype(jnp.int32)
    idx3, block_expert, block_valid, n_blocks = _dispatch_plan(eidx3, rank3, counts, tb)
    xs = _sc_scatter_rows(hp, idx3, n_blocks * tb)
    ys = _expert_ffn(xs, block_expert, block_valid, we_gate, we_up, we_down, layer, tb)
    yg = _sc_gather_rows(ys, idx3)
    return _combine(x2, hp, yg, w, mod, jnp.concatenate([ws_gate, ws_up], axis=-1).astype(BF16),
                    ws_down.astype(BF16), g_final.reshape(1, d), seq, final_norm)


def kernel(x, c, w_ada, b_ada, g_mix, w_in, b_gate, rpb, w_pa, w_pb, w_o, g_ffn, w_router, e_bias,
           we_gate, we_up, we_down, ws_gate, ws_up, ws_down, g_final):
    batch, seq, d = x.shape
    depth = w_ada.shape[0]
    mods = _ada(c, w_ada, b_ada).reshape(depth, batch, 6, d)
    x2 = x.reshape(batch * seq, d)
    for l in range(depth):
        x2, *routed = _token_mixer(x2, mods[l], g_mix[l], w_in[l], b_gate[l], rpb[l], w_pa[l], w_pb[l], w_o[l],
                                   g_ffn[l], w_router[l], e_bias[l], batch, seq)
        x2 = _moe_layer(x2, routed, mods[l], we_gate, we_up, we_down, l,
                        ws_gate[l], ws_up[l], ws_down[l], g_final, seq, final_norm=(l == depth - 1))
    return x2.reshape(batch, seq, d)
```

```python
import functools

import numpy as np
import jax
import jax.numpy as jnp
from jax import lax
from jax.experimental import pallas as pl
from jax.experimental.pallas import tpu as pltpu
from jax.experimental.pallas import tpu_sc as plsc

HEAD_DIM = 64
GRID_W = 64
NA_HEADS = 8
NA_WIN_ROWS = 8
NA_WIN_COLS = 16
DIL_GROUPS = ((128, 1), (512, 4), (2048, 16))
DIL_HEADS_PER_GROUP = 4
N_DIL_GROUPS = len(DIL_GROUPS)
NA_WIDTH = NA_HEADS * HEAD_DIM
DIL_WIDTH = N_DIL_GROUPS * DIL_HEADS_PER_GROUP * HEAD_DIM
DIL_OUT_WIDTH = DIL_HEADS_PER_GROUP * HEAD_DIM
QKV_COLS = 3 * (NA_WIDTH + DIL_WIDTH)
N_EXPERTS = 64
TOP_K = 8
N_EXPERT_GROUPS = 8
TOP_GROUPS = 4
EXPERT_DIM = 256
ROUTED_SCALE = 2.5
ALIBI_MAX = 8.0
EPS = 1e-6
NEG_INF = -1e30

LANES = 128
HEADS_PER_LANE_TILE = LANES // HEAD_DIM
DIL_BLOCK = 64
VMEM_LIMIT_BYTES = 56 * 1024 * 1024

SC_CORES = 2
SC_SUBCORES = 16
SC_WORKERS = SC_CORES * SC_SUBCORES
SC_CHUNK = 64
EXPERT_ROW_BLOCK = 1024
FFN_SUB_BLOCKS = 4
FFN_BLOCKS_PER_STEP = 2

F32 = jnp.float32
BF16 = jnp.bfloat16
BF16_BITS = 16

SH1, SC1, GT1, SH2, SC2, GT2 = range(6)


def _params(sem):
    return pltpu.CompilerParams(dimension_semantics=sem, vmem_limit_bytes=VMEM_LIMIT_BYTES)


def _modulated_norm(x, g, scale, shift):
    r = lax.rsqrt(jnp.mean(x * x, axis=-1, keepdims=True) + EPS)
    return (x * r) * (g * (1.0 + scale)) + shift


def _ada_kernel(c_ref, w_ref, b_ref, o_ref):
    c = c_ref[...]
    act = c * jax.nn.sigmoid(c)
    o_ref[0] = jnp.dot(act, w_ref[0], preferred_element_type=F32,
                       precision=lax.Precision.HIGHEST) + b_ref[0]


def _ada(c, w_ada, b_ada):
    depth, d, six_d = w_ada.shape
    b = c.shape[0]
    tn = d
    return pl.pallas_call(
        _ada_kernel,
        grid=(depth, six_d // tn),
        in_specs=[
            pl.BlockSpec((b, d), lambda l, j: (0, 0)),
            pl.BlockSpec((1, d, tn), lambda l, j: (l, 0, j)),
            pl.BlockSpec((1, 1, tn), lambda l, j: (l, 0, j)),
        ],
        out_specs=pl.BlockSpec((1, b, tn), lambda l, j: (l, 0, j)),
        out_shape=jax.ShapeDtypeStruct((depth, b, six_d), F32),
        compiler_params=_params(("arbitrary", "arbitrary")),
        name="ada_mod",
    )(c, w_ada, b_ada.reshape(depth, 1, six_d))


def _inproj_kernel(x_ref, mod_ref, g_ref, w_ref, tok_ref, *rest, dilations):
    res_refs, acc_refs = rest[:len(dilations)], rest[len(dilations):]
    tm = x_ref.shape[0]
    tok_cols = tok_ref.shape[1]
    tn = (w_ref.shape[1] - tok_cols) // len(dilations)
    h = _modulated_norm(x_ref[...], g_ref[...], mod_ref[0, SC1:SC1 + 1, :], mod_ref[0, SH1:SH1 + 1, :]).astype(BF16)

    for c0 in range(0, tok_cols, tn):
        tok_ref[:, c0:c0 + tn] = jnp.dot(h, w_ref[:, c0:c0 + tn], preferred_element_type=F32).astype(BF16)

    for g, (res_ref, acc_ref, dilation) in enumerate(zip(res_refs, acc_refs, dilations)):
        c0 = tok_cols + g * tn
        res = jnp.dot(h, w_ref[:, c0:c0 + tn], preferred_element_type=F32)
        for c in range(acc_ref.shape[0]):
            acc_ref[c] = res[:, c * LANES:(c + 1) * LANES]
        for r in range(dilation):
            for c in range(acc_ref.shape[0]):
                res_ref[0, r, :, c * LANES:(c + 1) * LANES] = (
                    acc_ref[c, pl.ds(r, tm // dilation, stride=dilation), :].astype(BF16))


def _inproj(x2, mod, g, w_qkv, batch, seq, tm=1024):
    n, d = x2.shape
    tn = 3 * DIL_OUT_WIDTH
    dilations = tuple(dil for _, dil in DIL_GROUPS if dil > 1)
    tok_cols = w_qkv.shape[1] - tn * len(dilations)
    blocks_per_batch = seq // tm
    res_specs = [pl.BlockSpec((1, dil, tm // dil, tn),
                              lambda i: (i // blocks_per_batch, 0, i % blocks_per_batch, 0)) for dil in dilations]
    res_shapes = [jax.ShapeDtypeStruct((batch, dil, seq // dil, tn), BF16) for dil in dilations]
    return pl.pallas_call(
        functools.partial(_inproj_kernel, dilations=dilations),
        grid=(n // tm,),
        in_specs=[
            pl.BlockSpec((tm, d), lambda i: (i, 0)),
            pl.BlockSpec((1, 6, d), lambda i: (i // blocks_per_batch, 0, 0)),
            pl.BlockSpec((1, d), lambda i: (0, 0)),
            pl.BlockSpec(w_qkv.shape, lambda i: (0, 0)),
        ],
        out_specs=[pl.BlockSpec((tm, tok_cols), lambda i: (i, 0))] + res_specs,
        out_shape=[jax.ShapeDtypeStruct((n, tok_cols), BF16)] + res_shapes,
        scratch_shapes=[pltpu.VMEM((tn // LANES, tm, LANES), F32) for _ in dilations],
        compiler_params=_params(("arbitrary",)),
        name="in_proj",
    )(x2, mod, g, w_qkv)


def _na_bias_table(rpb):
    heads = rpb.shape[0]
    cols = np.arange(GRID_W)
    col_start = np.clip(cols - NA_WIN_COLS // 2, 0, GRID_W - NA_WIN_COLS)
    col_mask = (cols[None, :] >= col_start[:, None]) & (cols[None, :] < col_start[:, None] + NA_WIN_COLS)
    edge = GRID_W - NA_WIN_COLS
    ext = jnp.concatenate([jnp.repeat(rpb[..., :1], edge, axis=-1), rpb, jnp.repeat(rpb[..., -1:], edge, axis=-1)],
                          axis=-1).astype(F32)
    rpb_cols = jnp.stack([ext[..., GRID_W - 1 - cq:2 * GRID_W - 1 - cq] for cq in range(GRID_W)],
                         axis=-2)
    t = jnp.stack([rpb_cols[:, NA_WIN_ROWS - 1 - off:2 * NA_WIN_ROWS - 1 - off] for off in range(NA_WIN_ROWS)],
                  axis=1)
    t = t.transpose(0, 1, 3, 2, 4)
    t = jnp.where(col_mask[:, None, :], t, NEG_INF)
    t = t.reshape(heads // HEADS_PER_LANE_TILE, HEADS_PER_LANE_TILE, NA_WIN_ROWS, GRID_W, NA_WIN_ROWS * GRID_W)
    return t.transpose(0, 2, 1, 3, 4).reshape(heads // HEADS_PER_LANE_TILE, NA_WIN_ROWS,
                                              HEADS_PER_LANE_TILE * GRID_W, NA_WIN_ROWS * GRID_W)


def _stack_heads(q, low):
    scaled = q * (HEAD_DIM ** -0.5)
    zero = jnp.zeros_like(scaled)
    return jnp.concatenate([jnp.where(low, scaled, zero), jnp.where(low, zero, scaled)], axis=0)


def _stacked_attention(items):
    scores = [lax.dot_general(q2, kw, (((1,), (1,)), ((), ())), preferred_element_type=F32) + bias
              for q2, kw, _, bias in items]
    probs = []
    for s in scores:
        m = jnp.max(s, axis=-1, keepdims=True)
        p = jnp.exp(s - m)
        probs.append((p.astype(BF16), m, jnp.sum(p, axis=-1, keepdims=True)))
    return [(jnp.dot(p, vw, preferred_element_type=F32) / z, m, z)
            for (p, m, z), (_, _, vw, _) in zip(probs, items)]


def _unstack_heads(a, low):
    half = a.shape[0] // HEADS_PER_LANE_TILE
    return jnp.where(low, a[:half], a[half:])


def _na_kernel(q_ref, k_ref, v_ref, bias_ref, o_ref, *, rows, rows_per_step):
    kr = NA_WIN_ROWS
    low = lax.broadcasted_iota(jnp.int32, (GRID_W, LANES), 1) < HEAD_DIM

    def body(i, carry):
        items, qrows = [], []
        for u in range(rows_per_step):
            r = i * rows_per_step + u
            rs = jnp.clip(r - kr // 2, 0, rows - kr)
            qrows.append(pl.ds(pl.multiple_of(r * GRID_W, GRID_W), GRID_W))
            wrows = pl.ds(pl.multiple_of(rs * GRID_W, GRID_W), kr * GRID_W)
            items.append((_stack_heads(q_ref[qrows[-1], :], low), k_ref[wrows, :], v_ref[wrows, :],
                          bias_ref[0, r - rs]))
        for rows_u, (o, _, _) in zip(qrows, _stacked_attention(items)):
            o_ref[rows_u, :] = _unstack_heads(o, low).astype(o_ref.dtype)
        return carry

    lax.fori_loop(0, rows // rows_per_step, body, 0)


def _neighbourhood_attention(qkv, bias, batch, seq, rows_per_step=16):
    n = qkv.shape[0]
    rows = seq // GRID_W
    pairs = NA_WIDTH // LANES
    return pl.pallas_call(
        functools.partial(_na_kernel, rows=rows, rows_per_step=rows_per_step),
        grid=(pairs, batch),
        in_specs=[
            pl.BlockSpec((seq, LANES), lambda p, b: (b, p)),
            pl.BlockSpec((seq, LANES), lambda p, b: (b, pairs + p)),
            pl.BlockSpec((seq, LANES), lambda p, b: (b, 2 * pairs + p)),
            pl.BlockSpec((1, NA_WIN_ROWS, HEADS_PER_LANE_TILE * GRID_W, NA_WIN_ROWS * GRID_W),
                         lambda p, b: (p, 0, 0, 0)),
        ],
        out_specs=pl.BlockSpec((seq, LANES), lambda p, b: (b, p)),
        out_shape=jax.ShapeDtypeStruct((n, NA_WIDTH), BF16),
        compiler_params=_params(("arbitrary", "arbitrary")),
        name="na_attn",
    )(qkv, qkv, qkv, bias)


def _alibi_slopes():
    n = N_DIL_GROUPS * DIL_HEADS_PER_GROUP
    s = np.exp2(-ALIBI_MAX * np.arange(1, n + 1, dtype=np.float64) / n).astype(np.float32)
    return s.reshape(N_DIL_GROUPS, DIL_HEADS_PER_GROUP)


def _dil_bias_table(group):
    blk = DIL_BLOCK
    dilation = DIL_GROUPS[group][1]
    slopes = _alibi_slopes()[group]
    qi = np.arange(blk)[:, None]
    kj = np.arange(3 * blk)[None, :]
    tables = []
    for shift in range(3):
        arel = np.abs(kj - qi - shift * blk)
        dist = (dilation * arel).astype(np.float32)
        per_head = [np.where(arel <= blk, -slopes[h] * dist, np.float32(NEG_INF)) for h in range(DIL_HEADS_PER_GROUP)]
        tables.append(np.stack(per_head))
    t = np.stack(tables, axis=1).astype(np.float32)
    pairs = DIL_HEADS_PER_GROUP // HEADS_PER_LANE_TILE
    t = t.reshape(pairs, HEADS_PER_LANE_TILE, 3, blk, 3 * blk).transpose(0, 2, 1, 3, 4)
    return t.reshape(pairs, 3, HEADS_PER_LANE_TILE * blk, 3 * blk)


def _dil_group(q_ref, k_ref, v_ref, bias_ref, o_ref, lse_ref, blocks_per_step):
    blk = DIL_BLOCK
    win = 3 * blk
    low = lax.broadcasted_iota(jnp.int32, (blk, LANES), 1) < HEAD_DIM
    dilation, length, _ = q_ref.shape
    nb = length // blk
    steps = dilation * nb

    def token_rows(sq, n):
        if dilation == 1:
            return pl.ds(pl.multiple_of(n * blk, blk), blk)
        return pl.ds(n * (blk * dilation) + sq, blk, stride=dilation)

    def body(i, carry):
        items, dst = [], []
        for u in range(blocks_per_step):
            t = i * blocks_per_step + u
            sq = t // nb
            n = t % nb
            wb = jnp.clip(n - 1, 0, nb - 3)
            qrows = pl.ds(pl.multiple_of(n * blk, blk), blk)
            wrows = pl.ds(pl.multiple_of(wb * blk, blk), win)
            dst.append(token_rows(sq, n))
            items.append((_stack_heads(q_ref[sq, qrows, :], low), k_ref[sq, wrows, :], v_ref[sq, wrows, :],
                          bias_ref[0, n - wb]))
        for rows, (o, m, z) in zip(dst, _stacked_attention(items)):
            o_ref[rows, :] = _unstack_heads(o, low)
            lse_ref[rows, :] = _unstack_heads(jnp.broadcast_to(m + jnp.log(z), o.shape), low)
        return carry

    lax.fori_loop(0, steps // blocks_per_step, body, 0)


def _dil_kernel(*refs, blocks_per_step, merge_rows):
    ng = N_DIL_GROUPS
    qkv_refs, bias_refs = refs[:3 * ng], refs[3 * ng:4 * ng]
    ob_ref, o_scr, lse_scr = refs[4 * ng:]
    for g in range(ng):
        q_ref, k_ref, v_ref = qkv_refs[3 * g:3 * g + 3]
        _dil_group(q_ref, k_ref, v_ref, bias_refs[g], o_scr.at[g], lse_scr.at[g], blocks_per_step)

    def merge(c, carry):
        rows = pl.ds(pl.multiple_of(c * merge_rows, merge_rows), merge_rows)
        lses = [lse_scr[g, rows, :] for g in range(ng)]
        top = functools.reduce(jnp.maximum, lses)
        es = [jnp.exp(l - top) for l in lses]
        num = functools.reduce(lambda a, b: a + b, [e * o_scr[g, rows, :] for g, e in enumerate(es)])
        ob_ref[rows, :] = (num / functools.reduce(lambda a, b: a + b, es)).astype(ob_ref.dtype)
        return carry

    lax.fori_loop(0, ob_ref.shape[0] // merge_rows, merge, 0)


def _dilated_attention(group_inputs, batch, seq, blocks_per_step=16, merge_rows=256):
    pairs = DIL_OUT_WIDTH // LANES
    operands, specs = [], []
    for qkv4, offsets in group_inputs:
        _, dilation, length, _ = qkv4.shape
        for off in offsets:
            operands.append(qkv4)
            specs.append(pl.BlockSpec((None, dilation, length, LANES),
                                      lambda p, b, off=off: (b, 0, 0, off + p)))
    for group in range(N_DIL_GROUPS):
        bias = jnp.asarray(_dil_bias_table(group))
        operands.append(bias)
        specs.append(pl.BlockSpec((1,) + bias.shape[1:], lambda p, b: (p, 0, 0, 0)))
    return pl.pallas_call(
        functools.partial(_dil_kernel, blocks_per_step=blocks_per_step, merge_rows=merge_rows),
        grid=(pairs, batch),
        in_specs=specs,
        out_specs=pl.BlockSpec((seq, LANES), lambda p, b: (b, p)),
        out_shape=jax.ShapeDtypeStruct((batch * seq, DIL_OUT_WIDTH), BF16),
        scratch_shapes=[pltpu.VMEM((N_DIL_GROUPS, seq, LANES), F32), pltpu.VMEM((N_DIL_GROUPS, seq, LANES), F32)],
        compiler_params=_params(("arbitrary", "arbitrary")),
        name="dil_attn",
    )(*operands)


def _outproj_kernel(x_ref, oa_ref, ob_ref, mod_ref, g_ref,
                    wg_ref, bg_ref, wpa_ref, wpb_ref, wo_ref, g2_ref, wr_ref, eb_ref, tri_ref,
                    out_ref, hp_ref, eidx_ref, rank_ref, w_ref, cnt_ref):
    d = x_ref.shape[1]
    x = x_ref[...]
    h = _modulated_norm(x, g_ref[...], mod_ref[0, SC1:SC1 + 1, :], mod_ref[0, SH1:SH1 + 1, :]).astype(BF16)

    ya = jnp.dot(oa_ref[...], wpa_ref[...], preferred_element_type=F32)
    yb = jnp.dot(ob_ref[...], wpb_ref[...], preferred_element_type=F32)
    ga = jax.nn.sigmoid(jnp.dot(h, wg_ref[:, :d], preferred_element_type=F32) + bg_ref[:, :d])
    mix = ga * ya
    gb = jax.nn.sigmoid(jnp.dot(h, wg_ref[:, d:], preferred_element_type=F32) + bg_ref[:, d:])
    mix = mix + gb * yb
    y = jnp.dot(mix.astype(BF16), wo_ref[...], preferred_element_type=F32)
    out = x + mod_ref[0, GT1:GT1 + 1, :] * y
    out_ref[...] = out
    _route_block(out, mod_ref, g2_ref, wr_ref, eb_ref, tri_ref, hp_ref, eidx_ref, rank_ref, w_ref, cnt_ref)


def _outproj(x2, oa, ob, mod, g, w_gate, b_gate, w_pa, w_pb, w_o, g_ffn, wr_split, e_bias, seq, tm=512):
    n, d = x2.shape
    assert n % tm == 0 and seq % tm == 0 and tm % SC_CHUNK == 0
    blocks_per_batch = seq // tm
    tri = jnp.asarray(np.triu(np.ones((tm, tm), np.float32)), BF16)
    row = lambda c: pl.BlockSpec((tm, c), lambda i: (i, 0))
    chunked = pl.BlockSpec((tm // SC_CHUNK, TOP_K, SC_CHUNK), lambda i: (i, 0, 0))
    full = lambda a: pl.BlockSpec(a.shape, lambda i: (0,) * a.ndim)
    return pl.pallas_call(
        _outproj_kernel,
        grid=(n // tm,),
        in_specs=[row(d), row(NA_WIDTH), row(DIL_OUT_WIDTH)] + [
            pl.BlockSpec((1, 6, d), lambda i: (i // blocks_per_batch, 0, 0)),
            full(g), full(w_gate), full(b_gate), full(w_pa), full(w_pb), full(w_o),
            full(g_ffn), full(wr_split), full(e_bias), full(tri),
        ],
        out_specs=[row(d), row(d // 2), chunked, chunked, row(LANES),
                   pl.BlockSpec((N_EXPERTS, LANES), lambda i: (0, 0))],
        out_shape=[
            jax.ShapeDtypeStruct((n, d), F32),
            jax.ShapeDtypeStruct((n, d // 2), jnp.int32),
            jax.ShapeDtypeStruct((n // SC_CHUNK, TOP_K, SC_CHUNK), jnp.int32),
            jax.ShapeDtypeStruct((n // SC_CHUNK, TOP_K, SC_CHUNK), jnp.int32),
            jax.ShapeDtypeStruct((n, LANES), F32),
            jax.ShapeDtypeStruct((N_EXPERTS, LANES), F32),
        ],
        compiler_params=_params(("arbitrary",)),
        name="out_proj",
    )(x2, oa, ob, mod, g, w_gate, b_gate, w_pa, w_pb, w_o, g_ffn, wr_split, e_bias, tri)


def _first_index_of_max(cur, idx, size):
    m = jnp.max(cur, axis=0, keepdims=True)
    first = jnp.min(jnp.where(cur == m, idx, size), axis=0, keepdims=True)
    return m, first


def _route_transposed(logits_t, e_bias):
    tokens = logits_t.shape[1]
    per_group = N_EXPERTS // N_EXPERT_GROUPS
    scores = jax.nn.sigmoid(logits_t)
    biased = scores + e_bias
    midx = lax.broadcasted_iota(jnp.int32, (per_group, tokens), 0)
    grp_scores = []
    for g in range(N_EXPERT_GROUPS):
        vals = biased[g * per_group:(g + 1) * per_group, :]
        m1, first = _first_index_of_max(vals, midx, per_group)
        m2 = jnp.max(jnp.where(midx == first, -jnp.inf, vals), axis=0, keepdims=True)
        grp_scores.append(m1 + m2)
    cur = jnp.concatenate(grp_scores, axis=0)
    gidx = lax.broadcasted_iota(jnp.int32, (N_EXPERT_GROUPS, tokens), 0)
    grp_sel = jnp.zeros((N_EXPERT_GROUPS, tokens), jnp.bool_)
    for _ in range(TOP_GROUPS):
        _, first = _first_index_of_max(cur, gidx, N_EXPERT_GROUPS)
        pick = gidx == first
        grp_sel = jnp.logical_or(grp_sel, pick)
        cur = jnp.where(pick, -jnp.inf, cur)
    rows = []
    for g in range(N_EXPERT_GROUPS):
        vals = biased[g * per_group:(g + 1) * per_group, :]
        rows.append(jnp.where(grp_sel[g:g + 1, :], vals, NEG_INF))
    cur = jnp.concatenate(rows, axis=0)
    eidx = lax.broadcasted_iota(jnp.int32, (N_EXPERTS, tokens), 0)
    firsts, picks, weights = [], [], []
    for _ in range(TOP_K):
        _, first = _first_index_of_max(cur, eidx, N_EXPERTS)
        pick = eidx == first
        firsts.append(first)
        picks.append(pick)
        weights.append(jnp.sum(jnp.where(pick, scores, 0.0), axis=0, keepdims=True))
        cur = jnp.where(pick, -jnp.inf, cur)
    total = functools.reduce(lambda a, b: a + b, weights)
    return firsts, picks, [w / total * ROUTED_SCALE for w in weights]


def _pack_halves(a):
    half = a.shape[1] // 2
    bits = lax.bitcast_convert_type(a.astype(BF16).astype(F32), jnp.int32)
    return lax.shift_right_logical(bits[:, :half], BF16_BITS) | bits[:, half:]


def _unpack_halves(w):
    low = lax.bitcast_convert_type(lax.shift_left(w, BF16_BITS), F32)
    high = lax.bitcast_convert_type(w & jnp.int32(-(1 << BF16_BITS)), F32)
    return low, high


def _route_block(x, mod_ref, g_ref, wr_ref, eb_ref, tri_ref, hp_ref, eidx_ref, rank_ref, w_ref, cnt_ref):
    tm = x.shape[0]

    @pl.when(pl.program_id(0) == 0)
    def _():
        cnt_ref[...] = jnp.zeros_like(cnt_ref)

    h = _modulated_norm(x, g_ref[...], mod_ref[0, SC2:SC2 + 1, :], mod_ref[0, SH2:SH2 + 1, :])
    hp_ref[...] = _pack_halves(h)
    h_hi = h.astype(BF16)
    h_lo = (h - h_hi.astype(F32)).astype(BF16)
    nt = (((1,), (1,)), ((), ()))
    logits_t = (lax.dot_general(wr_ref[0], h_hi, nt, preferred_element_type=F32)
                + lax.dot_general(wr_ref[0], h_lo, nt, preferred_element_type=F32)
                + lax.dot_general(wr_ref[1], h_hi, nt, preferred_element_type=F32))
    firsts, picks, weights = _route_transposed(logits_t, eb_ref[...])
    sel = functools.reduce(jnp.logical_or, picks)
    sel_f = jnp.where(sel, 1.0, 0.0)
    incl = jnp.dot(sel_f.astype(BF16), tri_ref[...], preferred_element_type=F32)
    before = cnt_ref[:, 0:1] + incl - sel_f
    eidx = jnp.concatenate(firsts, axis=0)
    rank = jnp.concatenate(
        [jnp.sum(jnp.where(p, before, 0.0), axis=0, keepdims=True) for p in picks], axis=0).astype(jnp.int32)
    for c in range(tm // SC_CHUNK):
        eidx_ref[c] = eidx[:, c * SC_CHUNK:(c + 1) * SC_CHUNK]
        rank_ref[c] = rank[:, c * SC_CHUNK:(c + 1) * SC_CHUNK]
    pad = jnp.concatenate(weights + [jnp.zeros((LANES - TOP_K, tm), F32)], axis=0)
    w_ref[...] = pad.T
    cnt_ref[...] = cnt_ref[...] + incl[:, tm - 1:tm]


def _sc_worker_id():
    return lax.axis_index("subcore") * SC_CORES + lax.axis_index("core")


def _sc_scatter_rows(src, idx3, n_out):
    n, w = src.shape
    assert n % (2 * SC_CHUNK * SC_WORKERS) == 0
    per_worker = n // SC_CHUNK // SC_WORKERS
    mesh = plsc.VectorSubcoreMesh(core_axis_name="core", subcore_axis_name="subcore")

    @functools.partial(
        pl.kernel, mesh=mesh, out_type=jax.ShapeDtypeStruct((n_out, w), src.dtype), name="moe_dispatch",
        scratch_types=[pltpu.VMEM((2, TOP_K, SC_CHUNK), jnp.int32), pltpu.VMEM((2, SC_CHUNK, w), src.dtype),
                       pltpu.SemaphoreType.DMA((2,)), pltpu.SemaphoreType.DMA((2,)), pltpu.SemaphoreType.DMA])
    def scatter(src_hbm, idx_hbm, out_hbm, idx_v, rows_v, idx_sem, row_sem, out_sem):
        first = _sc_worker_id() * per_worker

        def loads(chunk, slot):
            return (pltpu.make_async_copy(idx_hbm.at[chunk], idx_v.at[slot], idx_sem.at[slot]),
                    pltpu.make_async_copy(src_hbm.at[pl.ds(chunk * SC_CHUNK, SC_CHUNK)], rows_v.at[slot],
                                          row_sem.at[slot]))

        for cp in loads(first, 0):
            cp.start()

        @pl.loop(0, per_worker, step=2)
        def _(i):
            for slot in range(2):
                chunk = first + i + slot
                for cp in loads(chunk, slot):
                    cp.wait()

                @pl.when(i + slot + 1 < per_worker)
                def _():
                    for cp in loads(chunk + 1, 1 - slot):
                        cp.start()

                copies = [pltpu.make_async_copy(rows_v.at[slot], out_hbm.at[idx_v.at[slot, k]], out_sem)
                          for k in range(TOP_K)]
                for cp in copies:
                    cp.start()
                for cp in copies:
                    cp.wait()

    return scatter(src, idx3)


def _sc_gather_rows(src, idx3):
    _, w = src.shape
    chunks = idx3.shape[0]
    assert chunks % SC_WORKERS == 0
    per_worker = chunks // SC_WORKERS
    mesh = plsc.VectorSubcoreMesh(core_axis_name="core", subcore_axis_name="subcore")

    @functools.partial(
        pl.kernel, mesh=mesh, out_type=jax.ShapeDtypeStruct((TOP_K, chunks * SC_CHUNK, w), src.dtype),
        name="moe_collect",
        scratch_types=[pltpu.VMEM((TOP_K, SC_CHUNK), jnp.int32), pltpu.VMEM((2, SC_CHUNK, w), src.dtype),
                       pltpu.SemaphoreType.DMA((2,)), pltpu.SemaphoreType.DMA((2,))])
    def gather(src_hbm, idx_hbm, out_hbm, idx_v, rows_v, in_sem, out_sem):
        first = _sc_worker_id() * per_worker

        @pl.loop(0, per_worker)
        def _(i):
            chunk = first + i
            pltpu.sync_copy(idx_hbm.at[chunk], idx_v)
            reads = [pltpu.make_async_copy(src_hbm.at[idx_v.at[k]], rows_v.at[k % 2], in_sem.at[k % 2])
                     for k in range(TOP_K)]
            writes = [pltpu.make_async_copy(rows_v.at[k % 2], out_hbm.at[k, pl.ds(chunk * SC_CHUNK, SC_CHUNK)],
                                            out_sem.at[k % 2]) for k in range(TOP_K)]
            reads[0].start()
            for k in range(TOP_K):
                if k + 1 < TOP_K:
                    if k >= 1:
                        writes[k - 1].wait()
                    reads[k + 1].start()
                reads[k].wait()
                writes[k].start()
            writes[TOP_K - 2].wait()
            writes[TOP_K - 1].wait()

    return gather(src, idx3)


def _swiglu(x, w_gate_up, w_down):
    gu = jnp.dot(x, w_gate_up, preferred_element_type=F32)
    gate = gu[:, :EXPERT_DIM]
    act = (gate * jax.nn.sigmoid(gate)) * gu[:, EXPERT_DIM:]
    return jnp.dot(act.astype(BF16), w_down, preferred_element_type=F32)


def _expert_ffn_kernel(held_ref, use_ref, nv_ref, xs_ref, *refs, tb):
    slots = FFN_BLOCKS_PER_STEP
    w_refs, ys_ref, wgu_bf, wd_bf = refs[:3 * slots], refs[3 * slots], refs[3 * slots + 1], refs[3 * slots + 2]
    i = pl.program_id(0)
    sub = tb // FFN_SUB_BLOCKS
    row = lax.broadcasted_iota(jnp.int32, (sub, xs_ref.shape[1]), 0)

    for slot in range(slots):
        blk = i * slots + slot
        wg_ref, wu_ref, wd_ref = w_refs[3 * slot:3 * slot + 3]

        @pl.when(jnp.logical_or(i == 0, held_ref[blk] != held_ref[jnp.maximum(blk - slots, 0)]))
        def _():
            wgu_bf[slot, :, :EXPERT_DIM] = wg_ref[0, 0].astype(BF16)
            wgu_bf[slot, :, EXPERT_DIM:] = wu_ref[0, 0].astype(BF16)
            wd_bf[slot] = wd_ref[0, 0].astype(BF16)

    @pl.when(nv_ref[i * slots] > 0)
    def _():
        spans, xs = [], []
        for slot in range(slots):
            nvalid = nv_ref[i * slots + slot]
            source = 0 if slot == 0 else use_ref[i * slots + slot]
            for s in range(FFN_SUB_BLOCKS):
                span = pl.ds(slot * tb + s * sub, sub)
                packed = jnp.where(row < nvalid - s * sub, xs_ref[span, :], 0)
                low, high = _unpack_halves(packed)
                spans.append((source, span))
                xs.append(jnp.concatenate([low, high], axis=1).astype(BF16))
        gus = [jnp.dot(x, wgu_bf[source], preferred_element_type=F32) for (source, _), x in zip(spans, xs)]
        acts = [((gu[:, :EXPERT_DIM] * jax.nn.sigmoid(gu[:, :EXPERT_DIM])) * gu[:, EXPERT_DIM:]).astype(BF16)
                for gu in gus]
        ys = [jnp.dot(a, wd_bf[source], preferred_element_type=F32) for (source, _), a in zip(spans, acts)]
        for (_, span), y in zip(spans, ys):
            ys_ref[span, :] = _pack_halves(y)

    @pl.when(nv_ref[i * slots] == 0)
    def _():
        ys_ref[...] = jnp.zeros_like(ys_ref)


def _expert_ffn(xs, block_expert, block_valid, w_gate, w_up, w_down, layer, tb):
    p, half = xs.shape
    d = 2 * half
    slots = FFN_BLOCKS_PER_STEP
    assert slots == 2
    pairs = block_expert.reshape(-1, slots)
    fresh = pairs[:, 1] != pairs[:, 0]
    last_fresh = lax.cummax(jnp.where(fresh, jnp.arange(pairs.shape[0], dtype=jnp.int32), 0))
    held = jnp.stack([pairs[:, 0], pairs[last_fresh, 1]], axis=1).reshape(-1)
    use = jnp.stack([jnp.zeros_like(pairs[:, 0]), fresh.astype(jnp.int32)], axis=1).reshape(-1)
    weight_specs = []
    for slot in range(slots):
        index = lambda i, held, use, nv, slot=slot: (layer, held[i * slots + slot], 0, 0)
        weight_specs += [pl.BlockSpec((1, 1, d, EXPERT_DIM), index), pl.BlockSpec((1, 1, d, EXPERT_DIM), index),
                         pl.BlockSpec((1, 1, EXPERT_DIM, d), index)]
    grid_spec = pltpu.PrefetchScalarGridSpec(
        num_scalar_prefetch=3,
        grid=(p // (tb * slots),),
        in_specs=[pl.BlockSpec((tb * slots, half), lambda i, held, use, nv: (i, 0))] + weight_specs,
        out_specs=pl.BlockSpec((tb * slots, half), lambda i, held, use, nv: (i, 0)),
        scratch_shapes=[pltpu.VMEM((slots, d, 2 * EXPERT_DIM), BF16), pltpu.VMEM((slots, EXPERT_DIM, d), BF16)],
    )
    return pl.pallas_call(
        functools.partial(_expert_ffn_kernel, tb=tb),
        grid_spec=grid_spec,
        out_shape=jax.ShapeDtypeStruct((p, half), jnp.int32),
        compiler_params=_params(("arbitrary",)),
        name="moe_expert_ffn",
    )(held, use, block_valid, xs, *([w_gate, w_up, w_down] * slots))


def _combine_kernel(x_ref, hp_ref, yg_ref, w_ref, mod_ref, wgu_ref, wd_ref, gf_ref, o_ref, *, final_norm):
    low, high = _unpack_halves(hp_ref[...])
    h = jnp.concatenate([low, high], axis=1).astype(BF16)
    shared = _swiglu(h, wgu_ref[...], wd_ref[...])
    half = hp_ref.shape[1]
    acc_low, acc_high = shared[:, :half], shared[:, half:]
    w = w_ref[...]
    for k in range(TOP_K):
        low, high = _unpack_halves(yg_ref[k])
        wk = w[:, k:k + 1]
        acc_low = acc_low + wk * low
        acc_high = acc_high + wk * high
    y = jnp.concatenate([acc_low, acc_high], axis=1)
    out = x_ref[...] + mod_ref[0, GT2:GT2 + 1, :] * y
    if final_norm:
        out = out * lax.rsqrt(jnp.mean(out * out, axis=-1, keepdims=True) + EPS) * gf_ref[...]
    o_ref[...] = out


def _combine(x2, hp, yg, w, mod, ws_gate_up, ws_down, g_final, seq, final_norm, tm=512):
    n, d = x2.shape
    blocks_per_batch = seq // tm
    const = lambda a: pl.BlockSpec(a.shape, lambda i: (0,) * a.ndim)
    return pl.pallas_call(
        functools.partial(_combine_kernel, final_norm=final_norm),
        grid=(n // tm,),
        in_specs=[
            pl.BlockSpec((tm, d), lambda i: (i, 0)),
            pl.BlockSpec((tm, d // 2), lambda i: (i, 0)),
            pl.BlockSpec((TOP_K, tm, d // 2), lambda i: (0, i, 0)),
            pl.BlockSpec((tm, LANES), lambda i: (i, 0)),
            pl.BlockSpec((1, 6, d), lambda i: (i // blocks_per_batch, 0, 0)),
            const(ws_gate_up), const(ws_down), const(g_final),
        ],
        out_specs=pl.BlockSpec((tm, d), lambda i: (i, 0)),
        out_shape=jax.ShapeDtypeStruct((n, d), F32),
        compiler_params=_params(("arbitrary",)),
        name="moe_combine",
    )(x2, hp, yg, w, mod, ws_gate_up, ws_down, g_final)


def _token_mixer(x2, mod, g_mix, w_in, b_gate, rpb, w_pa, w_pb, w_o, g_ffn, w_router, e_bias, batch, seq):
    d = x2.shape[1]
    dil0 = 3 * NA_WIDTH
    group_cols = [[dil0 + part * DIL_WIDTH + grp * DIL_OUT_WIDTH for part in range(3)]
                  for grp in range(N_DIL_GROUPS)]
    order = sorted(range(N_DIL_GROUPS), key=lambda grp: DIL_GROUPS[grp][1] > 1)
    w_qkv = jnp.concatenate(
        [w_in[:, :dil0]] + [w_in[:, c:c + DIL_OUT_WIDTH] for grp in order for c in group_cols[grp]],
        axis=1).astype(BF16)
    tok, *residue = _inproj(x2, mod, g_mix.reshape(1, d), w_qkv, batch, seq)
    o_a = _neighbourhood_attention(tok, _na_bias_table(rpb), batch, seq)

    qkv_offsets = tuple(part * DIL_OUT_WIDTH // LANES for part in range(3))
    group_inputs = [None] * N_DIL_GROUPS
    residue = iter(residue)
    tok_offset = dil0 // LANES
    for grp in order:
        if DIL_GROUPS[grp][1] == 1:
            group_inputs[grp] = (tok.reshape(batch, 1, seq, tok.shape[1]),
                                 tuple(tok_offset + o for o in qkv_offsets))
            tok_offset += 3 * DIL_OUT_WIDTH // LANES
        else:
            group_inputs[grp] = (next(residue), qkv_offsets)
    o_b = _dilated_attention(group_inputs, batch, seq)

    wr_t = w_router.T
    wr_hi = wr_t.astype(BF16)
    wr_split = jnp.stack([wr_hi, (wr_t - wr_hi.astype(F32)).astype(BF16)])
    return _outproj(x2, o_a, o_b, mod, g_mix.reshape(1, d),
                    w_in[:, QKV_COLS:].astype(BF16), b_gate.reshape(1, -1),
                    w_pa.astype(BF16), w_pb.astype(BF16), w_o.astype(BF16),
                    g_ffn.reshape(1, d), wr_split, e_bias.reshape(-1, 1), seq)


def _dispatch_plan(eidx3, rank3, counts, tb):
    n = eidx3.shape[0] * SC_CHUNK
    n_blocks = -(-(n * TOP_K + N_EXPERTS * (tb - 1)) // tb)
    n_blocks = -(-n_blocks // FFN_BLOCKS_PER_STEP) * FFN_BLOCKS_PER_STEP
    padded = (counts + tb - 1) // tb * tb
    seg_end = jnp.cumsum(padded)
    seg_start = seg_end - padded
    experts = jnp.arange(N_EXPERTS, dtype=jnp.int32)

    def lookup(table, idx):
        sel = idx[None] == experts.reshape((N_EXPERTS,) + (1,) * idx.ndim)
        return jnp.sum(jnp.where(sel, table.reshape((N_EXPERTS,) + (1,) * idx.ndim), 0), axis=0)

    idx3 = lookup(seg_start, eidx3) + rank3
    block_start = jnp.arange(n_blocks, dtype=jnp.int32) * tb
    block_expert = jnp.sum((seg_end[:, None] <= block_start[None, :]).astype(jnp.int32), axis=0)
    block_expert = jnp.minimum(block_expert, N_EXPERTS - 1)
    block_valid = jnp.clip(lookup(counts, block_expert) - (block_start - lookup(seg_start, block_expert)), 0, tb)
    return idx3, block_expert, block_valid.astype(jnp.int32), n_blocks


def _moe_layer(x2, routed, mod, we_gate, we_up, we_down, layer, ws_gate, ws_up, ws_down, g_final, seq, final_norm,
               tb=EXPERT_ROW_BLOCK):
    d = x2.shape[1]
    hp, eidx3, rank3, w, cnt = routed
    counts = cnt[:, 0].astype(jnp.int32)
    idx3, block_expert, block_valid, n_blocks = _dispatch_plan(eidx3, rank3, counts, tb)
    xs = _sc_scatter_rows(hp, idx3, n_blocks * tb)
    ys = _expert_ffn(xs, block_expert, block_valid, we_gate, we_up, we_down, layer, tb)
    yg = _sc_gather_rows(ys, idx3)
    return _combine(x2, hp, yg, w, mod, jnp.concatenate([ws_gate, ws_up], axis=-1).astype(BF16),
                    ws_down.astype(BF16), g_final.reshape(1, d), seq, final_norm)


def kernel(x, c, w_ada, b_ada, g_mix, w_in, b_gate, rpb, w_pa, w_pb, w_o, g_ffn, w_router, e_bias,
           we_gate, we_up, we_down, ws_gate, ws_up, ws_down, g_final):
    batch, seq, d = x.shape
    depth = w_ada.shape[0]
    mods = _ada(c, w_ada, b_ada).reshape(depth, batch, 6, d)
    x2 = x.reshape(batch * seq, d)
    for l in range(depth):
        x2, *routed = _token_mixer(x2, mods[l], g_mix[l], w_in[l], b_gate[l], rpb[l], w_pa[l], w_pb[l], w_o[l],
                                   g_ffn[l], w_router[l], e_bias[l], batch, seq)
        x2 = _moe_layer(x2, routed, mods[l], we_gate, we_up, we_down, l,
                        ws_gate[l], ws_up[l], ws_down[l], g_final, seq, final_norm=(l == depth - 1))
    return x2.reshape(batch, seq, d)
```

```python
import functools

import numpy as np
import jax
import jax.numpy as jnp
from jax import lax
from jax.experimental import pallas as pl
from jax.experimental.pallas import tpu as pltpu
from jax.experimental.pallas import tpu_sc as plsc

HEAD_DIM = 64
GRID_W = 64
NA_HEADS = 8
NA_WIN_ROWS = 8
NA_WIN_COLS = 16
DIL_GROUPS = ((128, 1), (512, 4), (2048, 16))
DIL_HEADS_PER_GROUP = 4
N_DIL_GROUPS = len(DIL_GROUPS)
NA_WIDTH = NA_HEADS * HEAD_DIM
DIL_WIDTH = N_DIL_GROUPS * DIL_HEADS_PER_GROUP * HEAD_DIM
DIL_OUT_WIDTH = DIL_HEADS_PER_GROUP * HEAD_DIM
QKV_COLS = 3 * (NA_WIDTH + DIL_WIDTH)
N_EXPERTS = 64
TOP_K = 8
N_EXPERT_GROUPS = 8
TOP_GROUPS = 4
EXPERT_DIM = 256
ROUTED_SCALE = 2.5
ALIBI_MAX = 8.0
EPS = 1e-6
NEG_INF = -1e30

LANES = 128
HEADS_PER_LANE_TILE = LANES // HEAD_DIM
DIL_BLOCK = 64
VMEM_LIMIT_BYTES = 56 * 1024 * 1024

SC_CORES = 2
SC_SUBCORES = 16
SC_WORKERS = SC_CORES * SC_SUBCORES
SC_CHUNK = 64
EXPERT_ROW_BLOCK = 1024
FFN_SUB_BLOCKS = 4
FFN_BLOCKS_PER_STEP = 2

F32 = jnp.float32
BF16 = jnp.bfloat16
BF16_BITS = 16

SH1, SC1, GT1, SH2, SC2, GT2 = range(6)


def _params(sem):
    return pltpu.CompilerParams(dimension_semantics=sem, vmem_limit_bytes=VMEM_LIMIT_BYTES)


def _modulated_norm(x, g, scale, shift):
    r = lax.rsqrt(jnp.mean(x * x, axis=-1, keepdims=True) + EPS)
    return (x * r) * (g * (1.0 + scale)) + shift


def _ada_kernel(c_ref, w_ref, b_ref, o_ref):
    c = c_ref[...]
    act = c * jax.nn.sigmoid(c)
    o_ref[0] = jnp.dot(act, w_ref[0], preferred_element_type=F32,
                       precision=lax.Precision.HIGHEST) + b_ref[0]


def _ada(c, w_ada, b_ada):
    depth, d, six_d = w_ada.shape
    b = c.shape[0]
    tn = d
    return pl.pallas_call(
        _ada_kernel,
        grid=(depth, six_d // tn),
        in_specs=[
            pl.BlockSpec((b, d), lambda l, j: (0, 0)),
            pl.BlockSpec((1, d, tn), lambda l, j: (l, 0, j)),
            pl.BlockSpec((1, 1, tn), lambda l, j: (l, 0, j)),
        ],
        out_specs=pl.BlockSpec((1, b, tn), lambda l, j: (l, 0, j)),
        out_shape=jax.ShapeDtypeStruct((depth, b, six_d), F32),
        compiler_params=_params(("arbitrary", "arbitrary")),
        name="ada_mod",
    )(c, w_ada, b_ada.reshape(depth, 1, six_d))


def _inproj_kernel(x_ref, mod_ref, g_ref, w_ref, tok_ref, *rest, dilations):
    res_refs, acc_refs = rest[:len(dilations)], rest[len(dilations):]
    tm = x_ref.shape[0]
    tok_cols = tok_ref.shape[1]
    tn = (w_ref.shape[1] - tok_cols) // len(dilations)
    h = _modulated_norm(x_ref[...], g_ref[...], mod_ref[0, SC1:SC1 + 1, :], mod_ref[0, SH1:SH1 + 1, :]).astype(BF16)

    for c0 in range(0, tok_cols, tn):
        tok_ref[:, c0:c0 + tn] = jnp.dot(h, w_ref[:, c0:c0 + tn], preferred_element_type=F32).astype(BF16)

    for g, (res_ref, acc_ref, dilation) in enumerate(zip(res_refs, acc_refs, dilations)):
        c0 = tok_cols + g * tn
        res = jnp.dot(h, w_ref[:, c0:c0 + tn], preferred_element_type=F32)
        for c in range(acc_ref.shape[0]):
            acc_ref[c] = res[:, c * LANES:(c + 1) * LANES]
        for r in range(dilation):
            for c in range(acc_ref.shape[0]):
                res_ref[0, r, :, c * LANES:(c + 1) * LANES] = (
                    acc_ref[c, pl.ds(r, tm // dilation, stride=dilation), :].astype(BF16))


def _inproj(x2, mod, g, w_qkv, batch, seq, tm=1024):
    n, d = x2.shape
    tn = 3 * DIL_OUT_WIDTH
    dilations = tuple(dil for _, dil in DIL_GROUPS if dil > 1)
    tok_cols = w_qkv.shape[1] - tn * len(dilations)
    blocks_per_batch = seq // tm
    res_specs = [pl.BlockSpec((1, dil, tm // dil, tn),
                              lambda i: (i // blocks_per_batch, 0, i % blocks_per_batch, 0)) for dil in dilations]
    res_shapes = [jax.ShapeDtypeStruct((batch, dil, seq // dil, tn), BF16) for dil in dilations]
    return pl.pallas_call(
        functools.partial(_inproj_kernel, dilations=dilations),
        grid=(n // tm,),
        in_specs=[
            pl.BlockSpec((tm, d), lambda i: (i, 0)),
            pl.BlockSpec((1, 6, d), lambda i: (i // blocks_per_batch, 0, 0)),
            pl.BlockSpec((1, d), lambda i: (0, 0)),
            pl.BlockSpec(w_qkv.shape, lambda i: (0, 0)),
        ],
        out_specs=[pl.BlockSpec((tm, tok_cols), lambda i: (i, 0))] + res_specs,
        out_shape=[jax.ShapeDtypeStruct((n, tok_cols), BF16)] + res_shapes,
        scratch_shapes=[pltpu.VMEM((tn // LANES, tm, LANES), F32) for _ in dilations],
        compiler_params=_params(("arbitrary",)),
        name="in_proj",
    )(x2, mod, g, w_qkv)


def _na_bias_table(rpb):
    heads = rpb.shape[0]
    cols = np.arange(GRID_W)
    col_start = np.clip(cols - NA_WIN_COLS // 2, 0, GRID_W - NA_WIN_COLS)
    col_mask = (cols[None, :] >= col_start[:, None]) & (cols[None, :] < col_start[:, None] + NA_WIN_COLS)
    edge = GRID_W - NA_WIN_COLS
    ext = jnp.concatenate([jnp.repeat(rpb[..., :1], edge, axis=-1), rpb, jnp.repeat(rpb[..., -1:], edge, axis=-1)],
                          axis=-1).astype(F32)
    rpb_cols = jnp.stack([ext[..., GRID_W - 1 - cq:2 * GRID_W - 1 - cq] for cq in range(GRID_W)],
                         axis=-2)
    t = jnp.stack([rpb_cols[:, NA_WIN_ROWS - 1 - off:2 * NA_WIN_ROWS - 1 - off] for off in range(NA_WIN_ROWS)],
                  axis=1)
    t = t.transpose(0, 1, 3, 2, 4)
    t = jnp.where(col_mask[:, None, :], t, NEG_INF)
    t = t.reshape(heads // HEADS_PER_LANE_TILE, HEADS_PER_LANE_TILE, NA_WIN_ROWS, GRID_W, NA_WIN_ROWS * GRID_W)
    return t.transpose(0, 2, 1, 3, 4).reshape(heads // HEADS_PER_LANE_TILE, NA_WIN_ROWS,
                                              HEADS_PER_LANE_TILE * GRID_W, NA_WIN_ROWS * GRID_W)


def _stack_heads(q, low):
    scaled = q * (HEAD_DIM ** -0.5)
    zero = jnp.zeros_like(scaled)
    return jnp.concatenate([jnp.where(low, scaled, zero), jnp.where(low, zero, scaled)], axis=0)


def _stacked_attention(items):
    scores = [lax.dot_general(q2, kw, (((1,), (1,)), ((), ())), preferred_element_type=F32) + bias
              for q2, kw, _, bias in items]
    probs = []
    for s in scores:
        m = jnp.max(s, axis=-1, keepdims=True)
        p = jnp.exp(s - m)
        probs.append((p.astype(BF16), m, jnp.sum(p, axis=-1, keepdims=True)))
    return [(jnp.dot(p, vw, preferred_element_type=F32) / z, m, z)
            for (p, m, z), (_, _, vw, _) in zip(probs, items)]


def _unstack_heads(a, low):
    half = a.shape[0] // HEADS_PER_LANE_TILE
    return jnp.where(low, a[:half], a[half:])


def _na_kernel(q_ref, k_ref, v_ref, bias_ref, o_ref, *, rows, rows_per_step):
    kr = NA_WIN_ROWS
    low = lax.broadcasted_iota(jnp.int32, (GRID_W, LANES), 1) < HEAD_DIM

    def body(i, carry):
        items, qrows = [], []
        for u in range(rows_per_step):
            r = i * rows_per_step + u
            rs = jnp.clip(r - kr // 2, 0, rows - kr)
            qrows.append(pl.ds(pl.multiple_of(r * GRID_W, GRID_W), GRID_W))
            wrows = pl.ds(pl.multiple_of(rs * GRID_W, GRID_W), kr * GRID_W)
            items.append((_stack_heads(q_ref[qrows[-1], :], low), k_ref[wrows, :], v_ref[wrows, :],
                          bias_ref[0, r - rs]))
        for rows_u, (o, _, _) in zip(qrows, _stacked_attention(items)):
            o_ref[rows_u, :] = _unstack_heads(o, low).astype(o_ref.dtype)
        return carry

    lax.fori_loop(0, rows // rows_per_step, body, 0)


def _neighbourhood_attention(qkv, bias, batch, seq, rows_per_step=16):
    n = qkv.shape[0]
    rows = seq // GRID_W
    pairs = NA_WIDTH // LANES
    return pl.pallas_call(
        functools.partial(_na_kernel, rows=rows, rows_per_step=rows_per_step),
        grid=(pairs, batch),
        in_specs=[
            pl.BlockSpec((seq, LANES), lambda p, b: (b, p)),
            pl.BlockSpec((seq, LANES), lambda p, b: (b, pairs + p)),
            pl.BlockSpec((seq, LANES), lambda p, b: (b, 2 * pairs + p)),
            pl.BlockSpec((1, NA_WIN_ROWS, HEADS_PER_LANE_TILE * GRID_W, NA_WIN_ROWS * GRID_W),
                         lambda p, b: (p, 0, 0, 0)),
        ],
        out_specs=pl.BlockSpec((seq, LANES), lambda p, b: (b, p)),
        out_shape=jax.ShapeDtypeStruct((n, NA_WIDTH), BF16),
        compiler_params=_params(("arbitrary", "arbitrary")),
        name="na_attn",
    )(qkv, qkv, qkv, bias)


def _alibi_slopes():
    n = N_DIL_GROUPS * DIL_HEADS_PER_GROUP
    s = np.exp2(-ALIBI_MAX * np.arange(1, n + 1, dtype=np.float64) / n).astype(np.float32)
    return s.reshape(N_DIL_GROUPS, DIL_HEADS_PER_GROUP)


def _dil_bias_table(group):
    blk = DIL_BLOCK
    dilation = DIL_GROUPS[group][1]
    slopes = _alibi_slopes()[group]
    qi = np.arange(blk)[:, None]
    kj = np.arange(3 * blk)[None, :]
    tables = []
    for shift in range(3):
        arel = np.abs(kj - qi - shift * blk)
        dist = (dilation * arel).astype(np.float32)
        per_head = [np.where(arel <= blk, -slopes[h] * dist, np.float32(NEG_INF)) for h in range(DIL_HEADS_PER_GROUP)]
        tables.append(np.stack(per_head))
    t = np.stack(tables, axis=1).astype(np.float32)
    pairs = DIL_HEADS_PER_GROUP // HEADS_PER_LANE_TILE
    t = t.reshape(pairs, HEADS_PER_LANE_TILE, 3, blk, 3 * blk).transpose(0, 2, 1, 3, 4)
    return t.reshape(pairs, 3, HEADS_PER_LANE_TILE * blk, 3 * blk)


def _dil_group(q_ref, k_ref, v_ref, bias_ref, o_ref, lse_ref, blocks_per_step):
    blk = DIL_BLOCK
    win = 3 * blk
    low = lax.broadcasted_iota(jnp.int32, (blk, LANES), 1) < HEAD_DIM
    dilation, length, _ = q_ref.shape
    nb = length // blk
    steps = dilation * nb

    def token_rows(sq, n):
        if dilation == 1:
            return pl.ds(pl.multiple_of(n * blk, blk), blk)
        return pl.ds(n * (blk * dilation) + sq, blk, stride=dilation)

    def body(i, carry):
        items, dst = [], []
        for u in range(blocks_per_step):
            t = i * blocks_per_step + u
            sq = t // nb
            n = t % nb
            wb = jnp.clip(n - 1, 0, nb - 3)
            qrows = pl.ds(pl.multiple_of(n * blk, blk), blk)
            wrows = pl.ds(pl.multiple_of(wb * blk, blk), win)
            dst.append(token_rows(sq, n))
            items.append((_stack_heads(q_ref[sq, qrows, :], low), k_ref[sq, wrows, :], v_ref[sq, wrows, :],
                          bias_ref[0, n - wb]))
        for rows, (o, m, z) in zip(dst, _stacked_attention(items)):
            o_ref[rows, :] = _unstack_heads(o, low)
            lse_ref[rows, :] = _unstack_heads(jnp.broadcast_to(m + jnp.log(z), o.shape), low)
        return carry

    lax.fori_loop(0, steps // blocks_per_step, body, 0)


def _dil_kernel(*refs, blocks_per_step, merge_rows):
    ng = N_DIL_GROUPS
    qkv_refs, bias_refs = refs[:3 * ng], refs[3 * ng:4 * ng]
    ob_ref, o_scr, lse_scr = refs[4 * ng:]
    for g in range(ng):
        q_ref, k_ref, v_ref = qkv_refs[3 * g:3 * g + 3]
        _dil_group(q_ref, k_ref, v_ref, bias_refs[g], o_scr.at[g], lse_scr.at[g], blocks_per_step)

    def merge(c, carry):
        rows = pl.ds(pl.multiple_of(c * merge_rows, merge_rows), merge_rows)
        lses = [lse_scr[g, rows, :] for g in range(ng)]
        top = functools.reduce(jnp.maximum, lses)
        es = [jnp.exp(l - top) for l in lses]
        num = functools.reduce(lambda a, b: a + b, [e * o_scr[g, rows, :] for g, e in enumerate(es)])
        ob_ref[rows, :] = (num / functools.reduce(lambda a, b: a + b, es)).astype(ob_ref.dtype)
        return carry

    lax.fori_loop(0, ob_ref.shape[0] // merge_rows, merge, 0)


def _dilated_attention(group_inputs, batch, seq, blocks_per_step=16, merge_rows=256):
    pairs = DIL_OUT_WIDTH // LANES
    operands, specs = [], []
    for qkv4, offsets in group_inputs:
        _, dilation, length, _ = qkv4.shape
        for off in offsets:
            operands.append(qkv4)
            specs.append(pl.BlockSpec((None, dilation, length, LANES),
                                      lambda p, b, off=off: (b, 0, 0, off + p)))
    for group in range(N_DIL_GROUPS):
        bias = jnp.asarray(_dil_bias_table(group))
        operands.append(bias)
        specs.append(pl.BlockSpec((1,) + bias.shape[1:], lambda p, b: (p, 0, 0, 0)))
    return pl.pallas_call(
        functools.partial(_dil_kernel, blocks_per_step=blocks_per_step, merge_rows=merge_rows),
        grid=(pairs, batch),
        in_specs=specs,
        out_specs=pl.BlockSpec((seq, LANES), lambda p, b: (b, p)),
        out_shape=jax.ShapeDtypeStruct((batch * seq, DIL_OUT_WIDTH), BF16),
        scratch_shapes=[pltpu.VMEM((N_DIL_GROUPS, seq, LANES), F32), pltpu.VMEM((N_DIL_GROUPS, seq, LANES), F32)],
        compiler_params=_params(("arbitrary", "arbitrary")),
        name="dil_attn",
    )(*operands)


def _outproj_kernel(x_ref, oa_ref, ob_ref, mod_ref, g_ref,
                    wg_ref, bg_ref, wpa_ref, wpb_ref, wo_ref, g2_ref, wr_ref, eb_ref, tri_ref,
                    out_ref, hp_ref, eidx_ref, rank_ref, w_ref, cnt_ref):
    d = x_ref.shape[1]
    x = x_ref[...]
    h = _modulated_norm(x, g_ref[...], mod_ref[0, SC1:SC1 + 1, :], mod_ref[0, SH1:SH1 + 1, :]).astype(BF16)

    ya = jnp.dot(oa_ref[...], wpa_ref[...], preferred_element_type=F32)
    yb = jnp.dot(ob_ref[...], wpb_ref[...], preferred_element_type=F32)
    ga = jax.nn.sigmoid(jnp.dot(h, wg_ref[:, :d], preferred_element_type=F32) + bg_ref[:, :d])
    mix = ga * ya
    gb = jax.nn.sigmoid(jnp.dot(h, wg_ref[:, d:], preferred_element_type=F32) + bg_ref[:, d:])
    mix = mix + gb * yb
    y = jnp.dot(mix.astype(BF16), wo_ref[...], preferred_element_type=F32)
    out = x + mod_ref[0, GT1:GT1 + 1, :] * y
    out_ref[...] = out
    _route_block(out, mod_ref, g2_ref, wr_ref, eb_ref, tri_ref, hp_ref, eidx_ref, rank_ref, w_ref, cnt_ref)


def _outproj(x2, oa, ob, mod, g, w_gate, b_gate, w_pa, w_pb, w_o, g_ffn, wr_split, e_bias, seq, tm=512):
    n, d = x2.shape
    assert n % tm == 0 and seq % tm == 0 and tm % SC_CHUNK == 0
    blocks_per_batch = seq // tm
    tri = jnp.asarray(np.triu(np.ones((tm, tm), np.float32)), BF16)
    row = lambda c: pl.BlockSpec((tm, c), lambda i: (i, 0))
    chunked = pl.BlockSpec((tm // SC_CHUNK, TOP_K, SC_CHUNK), lambda i: (i, 0, 0))
    full = lambda a: pl.BlockSpec(a.shape, lambda i: (0,) * a.ndim)
    return pl.pallas_call(
        _outproj_kernel,
        grid=(n // tm,),
        in_specs=[row(d), row(NA_WIDTH), row(DIL_OUT_WIDTH)] + [
            pl.BlockSpec((1, 6, d), lambda i: (i // blocks_per_batch, 0, 0)),
            full(g), full(w_gate), full(b_gate), full(w_pa), full(w_pb), full(w_o),
            full(g_ffn), full(wr_split), full(e_bias), full(tri),
        ],
        out_specs=[row(d), row(d // 2), chunked, chunked, row(LANES),
                   pl.BlockSpec((N_EXPERTS, LANES), lambda i: (0, 0))],
        out_shape=[
            jax.ShapeDtypeStruct((n, d), F32),
            jax.ShapeDtypeStruct((n, d // 2), jnp.int32),
            jax.ShapeDtypeStruct((n // SC_CHUNK, TOP_K, SC_CHUNK), jnp.int32),
            jax.ShapeDtypeStruct((n // SC_CHUNK, TOP_K, SC_CHUNK), jnp.int32),
            jax.ShapeDtypeStruct((n, LANES), F32),
            jax.ShapeDtypeStruct((N_EXPERTS, LANES), F32),
        ],
        compiler_params=_params(("arbitrary",)),
        name="out_proj",
    )(x2, oa, ob, mod, g, w_gate, b_gate, w_pa, w_pb, w_o, g_ffn, wr_split, e_bias, tri)


def _first_index_of_max(cur, idx, size):
    m = jnp.max(cur, axis=0, keepdims=True)
    first = jnp.min(jnp.where(cur == m, idx, size), axis=0, keepdims=True)
    return m, first


def _route_transposed(logits_t, e_bias):
    tokens = logits_t.shape[1]
    per_group = N_EXPERTS // N_EXPERT_GROUPS
    scores = jax.nn.sigmoid(logits_t)
    biased = scores + e_bias
    midx = lax.broadcasted_iota(jnp.int32, (per_group, tokens), 0)
    grp_scores = []
    for g in range(N_EXPERT_GROUPS):
        vals = biased[g * per_group:(g + 1) * per_group, :]
        m1, first = _first_index_of_max(vals, midx, per_group)
        m2 = jnp.max(jnp.where(midx == first, -jnp.inf, vals), axis=0, keepdims=True)
        grp_scores.append(m1 + m2)
    cur = jnp.concatenate(grp_scores, axis=0)
    gidx = lax.broadcasted_iota(jnp.int32, (N_EXPERT_GROUPS, tokens), 0)
    grp_sel = jnp.zeros((N_EXPERT_GROUPS, tokens), jnp.bool_)
    for _ in range(TOP_GROUPS):
        _, first = _first_index_of_max(cur, gidx, N_EXPERT_GROUPS)
        pick = gidx == first
        grp_sel = jnp.logical_or(grp_sel, pick)
        cur = jnp.where(pick, -jnp.inf, cur)
    rows = []
    for g in range(N_EXPERT_GROUPS):
        vals = biased[g * per_group:(g + 1) * per_group, :]
        rows.append(jnp.where(grp_sel[g:g + 1, :], vals, NEG_INF))
    cur = jnp.concatenate(rows, axis=0)
    eidx = lax.broadcasted_iota(jnp.int32, (N_EXPERTS, tokens), 0)
    firsts, picks, weights = [], [], []
    for _ in range(TOP_K):
        _, first = _first_index_of_max(cur, eidx, N_EXPERTS)
        pick = eidx == first
        firsts.append(first)
        picks.append(pick)
        weights.append(jnp.sum(jnp.where(pick, scores, 0.0), axis=0, keepdims=True))
        cur = jnp.where(pick, -jnp.inf, cur)
    total = functools.reduce(lambda a, b: a + b, weights)
    return firsts, picks, [w / total * ROUTED_SCALE for w in weights]


def _pack_halves(a):
    half = a.shape[1] // 2
    bits = lax.bitcast_convert_type(a.astype(BF16).astype(F32), jnp.int32)
    return lax.shift_right_logical(bits[:, :half], BF16_BITS) | bits[:, half:]


def _unpack_halves(w):
    low = lax.bitcast_convert_type(lax.shift_left(w, BF16_BITS), F32)
    high = lax.bitcast_convert_type(w & jnp.int32(-(1 << BF16_BITS)), F32)
    return low, high


def _route_block(x, mod_ref, g_ref, wr_ref, eb_ref, tri_ref, hp_ref, eidx_ref, rank_ref, w_ref, cnt_ref):
    tm = x.shape[0]

    @pl.when(pl.program_id(0) == 0)
    def _():
        cnt_ref[...] = jnp.zeros_like(cnt_ref)

    h = _modulated_norm(x, g_ref[...], mod_ref[0, SC2:SC2 + 1, :], mod_ref[0, SH2:SH2 + 1, :])
    hp_ref[...] = _pack_halves(h)
    h_hi = h.astype(BF16)
    h_lo = (h - h_hi.astype(F32)).astype(BF16)
    nt = (((1,), (1,)), ((), ()))
    logits_t = (lax.dot_general(wr_ref[0], h_hi, nt, preferred_element_type=F32)
                + lax.dot_general(wr_ref[0], h_lo, nt, preferred_element_type=F32)
                + lax.dot_general(wr_ref[1], h_hi, nt, preferred_element_type=F32))
    firsts, picks, weights = _route_transposed(logits_t, eb_ref[...])
    sel = functools.reduce(jnp.logical_or, picks)
    sel_f = jnp.where(sel, 1.0, 0.0)
    incl = jnp.dot(sel_f.astype(BF16), tri_ref[...], preferred_element_type=F32)
    before = cnt_ref[:, 0:1] + incl - sel_f
    eidx = jnp.concatenate(firsts, axis=0)
    rank = jnp.concatenate(
        [jnp.sum(jnp.where(p, before, 0.0), axis=0, keepdims=True) for p in picks], axis=0).astype(jnp.int32)
    for c in range(tm // SC_CHUNK):
        eidx_ref[c] = eidx[:, c * SC_CHUNK:(c + 1) * SC_CHUNK]
        rank_ref[c] = rank[:, c * SC_CHUNK:(c + 1) * SC_CHUNK]
    pad = jnp.concatenate(weights + [jnp.zeros((LANES - TOP_K, tm), F32)], axis=0)
    w_ref[...] = pad.T
    cnt_ref[...] = cnt_ref[...] + incl[:, tm - 1:tm]


def _sc_worker_id():
    return lax.axis_index("subcore") * SC_CORES + lax.axis_index("core")


def _sc_scatter_rows(src, idx3, n_out):
    n, w = src.shape
    assert n % (2 * SC_CHUNK * SC_WORKERS) == 0
    per_worker = n // SC_CHUNK // SC_WORKERS
    mesh = plsc.VectorSubcoreMesh(core_axis_name="core", subcore_axis_name="subcore")

    @functools.partial(
        pl.kernel, mesh=mesh, out_type=jax.ShapeDtypeStruct((n_out, w), src.dtype), name="moe_dispatch",
        scratch_types=[pltpu.VMEM((2, TOP_K, SC_CHUNK), jnp.int32), pltpu.VMEM((2, SC_CHUNK, w), src.dtype),
                       pltpu.SemaphoreType.DMA((2,)), pltpu.SemaphoreType.DMA((2,)), pltpu.SemaphoreType.DMA])
    def scatter(src_hbm, idx_hbm, out_hbm, idx_v, rows_v, idx_sem, row_sem, out_sem):
        first = _sc_worker_id() * per_worker

        def loads(chunk, slot):
            return (pltpu.make_async_copy(idx_hbm.at[chunk], idx_v.at[slot], idx_sem.at[slot]),
                    pltpu.make_async_copy(src_hbm.at[pl.ds(chunk * SC_CHUNK, SC_CHUNK)], rows_v.at[slot],
                                          row_sem.at[slot]))

        for cp in loads(first, 0):
            cp.start()

        @pl.loop(0, per_worker, step=2)
        def _(i):
            for slot in range(2):
                chunk = first + i + slot
                for cp in loads(chunk, slot):
                    cp.wait()

                @pl.when(i + slot + 1 < per_worker)
                def _():
                    for cp in loads(chunk + 1, 1 - slot):
                        cp.start()

                copies = [pltpu.make_async_copy(rows_v.at[slot], out_hbm.at[idx_v.at[slot, k]], out_sem)
                          for k in range(TOP_K)]
                for cp in copies:
                    cp.start()
                for cp in copies:
                    cp.wait()

    return scatter(src, idx3)


def _sc_gather_rows(src, idx3):
    _, w = src.shape
    chunks = idx3.shape[0]
    assert chunks % SC_WORKERS == 0
    per_worker = chunks // SC_WORKERS
    mesh = plsc.VectorSubcoreMesh(core_axis_name="core", subcore_axis_name="subcore")

    @functools.partial(
        pl.kernel, mesh=mesh, out_type=jax.ShapeDtypeStruct((TOP_K, chunks * SC_CHUNK, w), src.dtype),
        name="moe_collect",
        scratch_types=[pltpu.VMEM((TOP_K, SC_CHUNK), jnp.int32), pltpu.VMEM((2, SC_CHUNK, w), src.dtype),
                       pltpu.SemaphoreType.DMA((2,)), pltpu.SemaphoreType.DMA((2,))])
    def gather(src_hbm, idx_hbm, out_hbm, idx_v, rows_v, in_sem, out_sem):
        first = _sc_worker_id() * per_worker

        @pl.loop(0, per_worker)
        def _(i):
            chunk = first + i
            pltpu.sync_copy(idx_hbm.at[chunk], idx_v)
            reads = [pltpu.make_async_copy(src_hbm.at[idx_v.at[k]], rows_v.at[k % 2], in_sem.at[k % 2])
                     for k in range(TOP_K)]
            writes = [pltpu.make_async_copy(rows_v.at[k % 2], out_hbm.at[k, pl.ds(chunk * SC_CHUNK, SC_CHUNK)],
                                            out_sem.at[k % 2]) for k in range(TOP_K)]
            reads[0].start()
            for k in range(TOP_K):
                if k + 1 < TOP_K:
                    if k >= 1:
                        writes[k - 1].wait()
                    reads[k + 1].start()
                reads[k].wait()
                writes[k].start()
            writes[TOP_K - 2].wait()
            writes[TOP_K - 1].wait()

    return gather(src, idx3)


def _swiglu(x, w_gate_up, w_down):
    gu = jnp.dot(x, w_gate_up, preferred_element_type=F32)
    gate = gu[:, :EXPERT_DIM]
    act = (gate * jax.nn.sigmoid(gate)) * gu[:, EXPERT_DIM:]
    return jnp.dot(act.astype(BF16), w_down, preferred_element_type=F32)


def _expert_ffn_kernel(held_ref, use_ref, nv_ref, xs_ref, *refs, tb):
    slots = FFN_BLOCKS_PER_STEP
    w_refs, ys_ref, wgu_bf, wd_bf = refs[:3 * slots], refs[3 * slots], refs[3 * slots + 1], refs[3 * slots + 2]
    i = pl.program_id(0)
    sub = tb // FFN_SUB_BLOCKS
    row = lax.broadcasted_iota(jnp.int32, (sub, xs_ref.shape[1]), 0)

    for slot in range(slots):
        blk = i * slots + slot
        wg_ref, wu_ref, wd_ref = w_refs[3 * slot:3 * slot + 3]

        @pl.when(jnp.logical_or(i == 0, held_ref[blk] != held_ref[jnp.maximum(blk - slots, 0)]))
        def _():
            wgu_bf[slot, :, :EXPERT_DIM] = wg_ref[0, 0].astype(BF16)
            wgu_bf[slot, :, EXPERT_DIM:] = wu_ref[0, 0].astype(BF16)
            wd_bf[slot] = wd_ref[0, 0].astype(BF16)

    @pl.when(nv_ref[i * slots] > 0)
    def _():
        spans, xs = [], []
        for slot in range(slots):
            nvalid = nv_ref[i * slots + slot]
            source = use_ref[i * slots + slot]
            for s in range(FFN_SUB_BLOCKS):
                span = pl.ds(slot * tb + s * sub, sub)
                packed = jnp.where(row < nvalid - s * sub, xs_ref[span, :], 0)
                low, high = _unpack_halves(packed)
                spans.append((source, span))
                xs.append(jnp.concatenate([low, high], axis=1).astype(BF16))
        gus = [jnp.dot(x, wgu_bf[source], preferred_element_type=F32) for (source, _), x in zip(spans, xs)]
        acts = [((gu[:, :EXPERT_DIM] * jax.nn.sigmoid(gu[:, :EXPERT_DIM])) * gu[:, EXPERT_DIM:]).astype(BF16)
                for gu in gus]
        ys = [jnp.dot(a, wd_bf[source], preferred_element_type=F32) for (source, _), a in zip(spans, acts)]
        for (_, span), y in zip(spans, ys):
            ys_ref[span, :] = _pack_halves(y)

    @pl.when(nv_ref[i * slots] == 0)
    def _():
        ys_ref[...] = jnp.zeros_like(ys_ref)


def _expert_ffn(xs, block_expert, block_valid, w_gate, w_up, w_down, layer, tb):
    p, half = xs.shape
    d = 2 * half
    slots = FFN_BLOCKS_PER_STEP
    assert slots == 2
    n_blocks = block_expert.shape[0]
    run = jnp.cumsum(jnp.concatenate([jnp.zeros((1,), jnp.int32),
                                      (block_expert[1:] != block_expert[:-1]).astype(jnp.int32)]))
    blocks = jnp.arange(n_blocks, dtype=jnp.int32)
    first_block = jnp.minimum(jnp.sum((run[None, :] < blocks[:, None]).astype(jnp.int32), axis=1), n_blocks - 1)
    run_expert = block_expert[first_block]
    newest = run.reshape(-1, slots)[:, slots - 1]
    held = []
    for slot in range(slots):
        r = newest - (newest - slot) % slots
        held.append(run_expert[jnp.where(r < 0, slot, r)])
    held = jnp.stack(held, axis=1).reshape(-1)
    use = run % slots
    weight_specs = []
    for slot in range(slots):
        index = lambda i, held, use, nv, slot=slot: (layer, held[i * slots + slot], 0, 0)
        weight_specs += [pl.BlockSpec((1, 1, d, EXPERT_DIM), index), pl.BlockSpec((1, 1, d, EXPERT_DIM), index),
                         pl.BlockSpec((1, 1, EXPERT_DIM, d), index)]
    grid_spec = pltpu.PrefetchScalarGridSpec(
        num_scalar_prefetch=3,
        grid=(p // (tb * slots),),
        in_specs=[pl.BlockSpec((tb * slots, half), lambda i, held, use, nv: (i, 0))] + weight_specs,
        out_specs=pl.BlockSpec((tb * slots, half), lambda i, held, use, nv: (i, 0)),
        scratch_shapes=[pltpu.VMEM((slots, d, 2 * EXPERT_DIM), BF16), pltpu.VMEM((slots, EXPERT_DIM, d), BF16)],
    )
    return pl.pallas_call(
        functools.partial(_expert_ffn_kernel, tb=tb),
        grid_spec=grid_spec,
        out_shape=jax.ShapeDtypeStruct((p, half), jnp.int32),
        compiler_params=_params(("arbitrary",)),
        name="moe_expert_ffn",
    )(held, use, block_valid, xs, *([w_gate, w_up, w_down] * slots))


def _combine_kernel(x_ref, hp_ref, yg_ref, w_ref, mod_ref, wgu_ref, wd_ref, gf_ref, o_ref, *, final_norm):
    low, high = _unpack_halves(hp_ref[...])
    h = jnp.concatenate([low, high], axis=1).astype(BF16)
    shared = _swiglu(h, wgu_ref[...], wd_ref[...])
    half = hp_ref.shape[1]
    acc_low, acc_high = shared[:, :half], shared[:, half:]
    w = w_ref[...]
    for k in range(TOP_K):
        low, high = _unpack_halves(yg_ref[k])
        wk = w[:, k:k + 1]
        acc_low = acc_low + wk * low
        acc_high = acc_high + wk * high
    y = jnp.concatenate([acc_low, acc_high], axis=1)
    out = x_ref[...] + mod_ref[0, GT2:GT2 + 1, :] * y
    if final_norm:
        out = out * lax.rsqrt(jnp.mean(out * out, axis=-1, keepdims=True) + EPS) * gf_ref[...]
    o_ref[...] = out


def _combine(x2, hp, yg, w, mod, ws_gate_up, ws_down, g_final, seq, final_norm, tm=512):
    n, d = x2.shape
    blocks_per_batch = seq // tm
    const = lambda a: pl.BlockSpec(a.shape, lambda i: (0,) * a.ndim)
    return pl.pallas_call(
        functools.partial(_combine_kernel, final_norm=final_norm),
        grid=(n // tm,),
        in_specs=[
            pl.BlockSpec((tm, d), lambda i: (i, 0)),
            pl.BlockSpec((tm, d // 2), lambda i: (i, 0)),
            pl.BlockSpec((TOP_K, tm, d // 2), lambda i: (0, i, 0)),
            pl.BlockSpec((tm, LANES), lambda i: (i, 0)),
            pl.BlockSpec((1, 6, d), lambda i: (i // blocks_per_batch, 0, 0)),
            const(ws_gate_up), const(ws_down), const(g_final),
        ],
        out_specs=pl.BlockSpec((tm, d), lambda i: (i, 0)),
        out_shape=jax.ShapeDtypeStruct((n, d), F32),
        compiler_params=_params(("arbitrary",)),
        name="moe_combine",
    )(x2, hp, yg, w, mod, ws_gate_up, ws_down, g_final)


def _token_mixer(x2, mod, g_mix, w_in, b_gate, rpb, w_pa, w_pb, w_o, g_ffn, w_router, e_bias, batch, seq):
    d = x2.shape[1]
    dil0 = 3 * NA_WIDTH
    group_cols = [[dil0 + part * DIL_WIDTH + grp * DIL_OUT_WIDTH for part in range(3)]
                  for grp in range(N_DIL_GROUPS)]
    order = sorted(range(N_DIL_GROUPS), key=lambda grp: DIL_GROUPS[grp][1] > 1)
    w_qkv = jnp.concatenate(
        [w_in[:, :dil0]] + [w_in[:, c:c + DIL_OUT_WIDTH] for grp in order for c in group_cols[grp]],
        axis=1).astype(BF16)
    tok, *residue = _inproj(x2, mod, g_mix.reshape(1, d), w_qkv, batch, seq)
    o_a = _neighbourhood_attention(tok, _na_bias_table(rpb), batch, seq)

    qkv_offsets = tuple(part * DIL_OUT_WIDTH // LANES for part in range(3))
    group_inputs = [None] * N_DIL_GROUPS
    residue = iter(residue)
    tok_offset = dil0 // LANES
    for grp in order:
        if DIL_GROUPS[grp][1] == 1:
            group_inputs[grp] = (tok.reshape(batch, 1, seq, tok.shape[1]),
                                 tuple(tok_offset + o for o in qkv_offsets))
            tok_offset += 3 * DIL_OUT_WIDTH // LANES
        else:
            group_inputs[grp] = (next(residue), qkv_offsets)
    o_b = _dilated_attention(group_inputs, batch, seq)

    wr_t = w_router.T
    wr_hi = wr_t.astype(BF16)
    wr_split = jnp.stack([wr_hi, (wr_t - wr_hi.astype(F32)).astype(BF16)])
    return _outproj(x2, o_a, o_b, mod, g_mix.reshape(1, d),
                    w_in[:, QKV_COLS:].astype(BF16), b_gate.reshape(1, -1),
                    w_pa.astype(BF16), w_pb.astype(BF16), w_o.astype(BF16),
                    g_ffn.reshape(1, d), wr_split, e_bias.reshape(-1, 1), seq)


def _dispatch_plan(eidx3, rank3, counts, tb):
    n = eidx3.shape[0] * SC_CHUNK
    n_blocks = -(-(n * TOP_K + N_EXPERTS * (tb - 1)) // tb)
    n_blocks = -(-n_blocks // FFN_BLOCKS_PER_STEP) * FFN_BLOCKS_PER_STEP
    padded = (counts + tb - 1) // tb * tb
    seg_end = jnp.cumsum(padded)
    seg_start = seg_end - padded
    experts = jnp.arange(N_EXPERTS, dtype=jnp.int32)

    def lookup(table, idx):
        sel = idx[None] == experts.reshape((N_EXPERTS,) + (1,) * idx.ndim)
        return jnp.sum(jnp.where(sel, table.reshape((N_EXPERTS,) + (1,) * idx.ndim), 0), axis=0)

    idx3 = lookup(seg_start, eidx3) + rank3
    block_start = jnp.arange(n_blocks, dtype=jnp.int32) * tb
    block_expert = jnp.sum((seg_end[:, None] <= block_start[None, :]).astype(jnp.int32), axis=0)
    block_expert = jnp.minimum(block_expert, N_EXPERTS - 1)
    block_valid = jnp.clip(lookup(counts, block_expert) - (block_start - lookup(seg_start, block_expert)), 0, tb)
    return idx3, block_expert, block_valid.astype(jnp.int32), n_blocks


def _moe_layer(x2, routed, mod, we_gate, we_up, we_down, layer, ws_gate, ws_up, ws_down, g_final, seq, final_norm,
               tb=EXPERT_ROW_BLOCK):
    d = x2.shape[1]
    hp, eidx3, rank3, w, cnt = routed
    counts = cnt[:, 0].astype(jnp.int32)
    idx3, block_expert, block_valid, n_blocks = _dispatch_plan(eidx3, rank3, counts, tb)
    xs = _sc_scatter_rows(hp, idx3, n_blocks * tb)
    ys = _expert_ffn(xs, block_expert, block_valid, we_gate, we_up, we_down, layer, tb)
    yg = _sc_gather_rows(ys, idx3)
    return _combine(x2, hp, yg, w, mod, jnp.concatenate([ws_gate, ws_up], axis=-1).astype(BF16),
                    ws_down.astype(BF16), g_final.reshape(1, d), seq, final_norm)


def kernel(x, c, w_ada, b_ada, g_mix, w_in, b_gate, rpb, w_pa, w_pb, w_o, g_ffn, w_router, e_bias,
           we_gate, we_up, we_down, ws_gate, ws_up, ws_down, g_final):
    batch, seq, d = x.shape
    depth = w_ada.shape[0]
    mods = _ada(c, w_ada, b_ada).reshape(depth, batch, 6, d)
    x2 = x.reshape(batch * seq, d)
    for l in range(depth):
        x2, *routed = _token_mixer(x2, mods[l], g_mix[l], w_in[l], b_gate[l], rpb[l], w_pa[l], w_pb[l], w_o[l],
                                   g_ffn[l], w_router[l], e_bias[l], batch, seq)
        x2 = _moe_layer(x2, routed, mods[l], we_gate, we_up, we_down, l,
                        ws_gate[l], ws_up[l], ws_down[l], g_final, seq, final_norm=(l == depth - 1))
    return x2.reshape(batch, seq, d)
```

```python
import functools

import numpy as np
import jax
import jax.numpy as jnp
from jax import lax
from jax.experimental import pallas as pl
from jax.experimental.pallas import tpu as pltpu
from jax.experimental.pallas import tpu_sc as plsc

HEAD_DIM = 64
GRID_W = 64
NA_HEADS = 8
NA_WIN_ROWS = 8
NA_WIN_COLS = 16
DIL_GROUPS = ((128, 1), (512, 4), (2048, 16))
DIL_HEADS_PER_GROUP = 4
N_DIL_GROUPS = len(DIL_GROUPS)
NA_WIDTH = NA_HEADS * HEAD_DIM
DIL_WIDTH = N_DIL_GROUPS * DIL_HEADS_PER_GROUP * HEAD_DIM
DIL_OUT_WIDTH = DIL_HEADS_PER_GROUP * HEAD_DIM
QKV_COLS = 3 * (NA_WIDTH + DIL_WIDTH)
N_EXPERTS = 64
TOP_K = 8
N_EXPERT_GROUPS = 8
TOP_GROUPS = 4
EXPERT_DIM = 256
ROUTED_SCALE = 2.5
ALIBI_MAX = 8.0
EPS = 1e-6
NEG_INF = -1e30

LANES = 128
HEADS_PER_LANE_TILE = LANES // HEAD_DIM
DIL_BLOCK = 64
VMEM_LIMIT_BYTES = 56 * 1024 * 1024

SC_CORES = 2
SC_SUBCORES = 16
SC_WORKERS = SC_CORES * SC_SUBCORES
SC_CHUNK = 64
EXPERT_ROW_BLOCK = 512
FFN_SUB_BLOCKS = 2
FFN_BLOCKS_PER_STEP = 4

F32 = jnp.float32
BF16 = jnp.bfloat16
BF16_BITS = 16

SH1, SC1, GT1, SH2, SC2, GT2 = range(6)


def _params(sem):
    return pltpu.CompilerParams(dimension_semantics=sem, vmem_limit_bytes=VMEM_LIMIT_BYTES)


def _modulated_norm(x, g, scale, shift):
    r = lax.rsqrt(jnp.mean(x * x, axis=-1, keepdims=True) + EPS)
    return (x * r) * (g * (1.0 + scale)) + shift


def _ada_kernel(c_ref, w_ref, b_ref, o_ref):
    c = c_ref[...]
    act = c * jax.nn.sigmoid(c)
    o_ref[0] = jnp.dot(act, w_ref[0], preferred_element_type=F32,
                       precision=lax.Precision.HIGHEST) + b_ref[0]


def _ada(c, w_ada, b_ada):
    depth, d, six_d = w_ada.shape
    b = c.shape[0]
    tn = d
    return pl.pallas_call(
        _ada_kernel,
        grid=(depth, six_d // tn),
        in_specs=[
            pl.BlockSpec((b, d), lambda l, j: (0, 0)),
            pl.BlockSpec((1, d, tn), lambda l, j: (l, 0, j)),
            pl.BlockSpec((1, 1, tn), lambda l, j: (l, 0, j)),
        ],
        out_specs=pl.BlockSpec((1, b, tn), lambda l, j: (l, 0, j)),
        out_shape=jax.ShapeDtypeStruct((depth, b, six_d), F32),
        compiler_params=_params(("arbitrary", "arbitrary")),
        name="ada_mod",
    )(c, w_ada, b_ada.reshape(depth, 1, six_d))


def _inproj_kernel(x_ref, mod_ref, g_ref, w_ref, tok_ref, *rest, dilations):
    res_refs, acc_refs = rest[:len(dilations)], rest[len(dilations):]
    tm = x_ref.shape[0]
    tok_cols = tok_ref.shape[1]
    tn = (w_ref.shape[1] - tok_cols) // len(dilations)
    h = _modulated_norm(x_ref[...], g_ref[...], mod_ref[0, SC1:SC1 + 1, :], mod_ref[0, SH1:SH1 + 1, :]).astype(BF16)

    for c0 in range(0, tok_cols, tn):
        tok_ref[:, c0:c0 + tn] = jnp.dot(h, w_ref[:, c0:c0 + tn], preferred_element_type=F32).astype(BF16)

    for g, (res_ref, acc_ref, dilation) in enumerate(zip(res_refs, acc_refs, dilations)):
        c0 = tok_cols + g * tn
        res = jnp.dot(h, w_ref[:, c0:c0 + tn], preferred_element_type=F32)
        for c in range(acc_ref.shape[0]):
            acc_ref[c] = res[:, c * LANES:(c + 1) * LANES]
        for r in range(dilation):
            for c in range(acc_ref.shape[0]):
                res_ref[0, r, :, c * LANES:(c + 1) * LANES] = (
                    acc_ref[c, pl.ds(r, tm // dilation, stride=dilation), :].astype(BF16))


def _inproj(x2, mod, g, w_qkv, batch, seq, tm=1024):
    n, d = x2.shape
    tn = 3 * DIL_OUT_WIDTH
    dilations = tuple(dil for _, dil in DIL_GROUPS if dil > 1)
    tok_cols = w_qkv.shape[1] - tn * len(dilations)
    blocks_per_batch = seq // tm
    res_specs = [pl.BlockSpec((1, dil, tm // dil, tn),
                              lambda i: (i // blocks_per_batch, 0, i % blocks_per_batch, 0)) for dil in dilations]
    res_shapes = [jax.ShapeDtypeStruct((batch, dil, seq // dil, tn), BF16) for dil in dilations]
    return pl.pallas_call(
        functools.partial(_inproj_kernel, dilations=dilations),
        grid=(n // tm,),
        in_specs=[
            pl.BlockSpec((tm, d), lambda i: (i, 0)),
            pl.BlockSpec((1, 6, d), lambda i: (i // blocks_per_batch, 0, 0)),
            pl.BlockSpec((1, d), lambda i: (0, 0)),
            pl.BlockSpec(w_qkv.shape, lambda i: (0, 0)),
        ],
        out_specs=[pl.BlockSpec((tm, tok_cols), lambda i: (i, 0))] + res_specs,
        out_shape=[jax.ShapeDtypeStruct((n, tok_cols), BF16)] + res_shapes,
        scratch_shapes=[pltpu.VMEM((tn // LANES, tm, LANES), F32) for _ in dilations],
        compiler_params=_params(("arbitrary",)),
        name="in_proj",
    )(x2, mod, g, w_qkv)


def _na_bias_table(rpb):
    heads = rpb.shape[0]
    cols = np.arange(GRID_W)
    col_start = np.clip(cols - NA_WIN_COLS // 2, 0, GRID_W - NA_WIN_COLS)
    col_mask = (cols[None, :] >= col_start[:, None]) & (cols[None, :] < col_start[:, None] + NA_WIN_COLS)
    edge = GRID_W - NA_WIN_COLS
    ext = jnp.concatenate([jnp.repeat(rpb[..., :1], edge, axis=-1), rpb, jnp.repeat(rpb[..., -1:], edge, axis=-1)],
                          axis=-1).astype(F32)
    rpb_cols = jnp.stack([ext[..., GRID_W - 1 - cq:2 * GRID_W - 1 - cq] for cq in range(GRID_W)],
                         axis=-2)
    t = jnp.stack([rpb_cols[:, NA_WIN_ROWS - 1 - off:2 * NA_WIN_ROWS - 1 - off] for off in range(NA_WIN_ROWS)],
                  axis=1)
    t = t.transpose(0, 1, 3, 2, 4)
    t = jnp.where(col_mask[:, None, :], t, NEG_INF)
    t = t.reshape(heads // HEADS_PER_LANE_TILE, HEADS_PER_LANE_TILE, NA_WIN_ROWS, GRID_W, NA_WIN_ROWS * GRID_W)
    return t.transpose(0, 2, 1, 3, 4).reshape(heads // HEADS_PER_LANE_TILE, NA_WIN_ROWS,
                                              HEADS_PER_LANE_TILE * GRID_W, NA_WIN_ROWS * GRID_W)


def _stack_heads(q, low):
    scaled = q * (HEAD_DIM ** -0.5)
    zero = jnp.zeros_like(scaled)
    return jnp.concatenate([jnp.where(low, scaled, zero), jnp.where(low, zero, scaled)], axis=0)


def _stacked_attention(items):
    scores = [lax.dot_general(q2, kw, (((1,), (1,)), ((), ())), preferred_element_type=F32) + bias
              for q2, kw, _, bias in items]
    probs = []
    for s in scores:
        m = jnp.max(s, axis=-1, keepdims=True)
        p = jnp.exp(s - m)
        probs.append((p.astype(BF16), m, jnp.sum(p, axis=-1, keepdims=True)))
    return [(jnp.dot(p, vw, preferred_element_type=F32) / z, m, z)
            for (p, m, z), (_, _, vw, _) in zip(probs, items)]


def _unstack_heads(a, low):
    half = a.shape[0] // HEADS_PER_LANE_TILE
    return jnp.where(low, a[:half], a[half:])


def _na_kernel(q_ref, k_ref, v_ref, bias_ref, o_ref, *, rows, rows_per_step):
    kr = NA_WIN_ROWS
    low = lax.broadcasted_iota(jnp.int32, (GRID_W, LANES), 1) < HEAD_DIM

    def body(i, carry):
        items, qrows = [], []
        for u in range(rows_per_step):
            r = i * rows_per_step + u
            rs = jnp.clip(r - kr // 2, 0, rows - kr)
            qrows.append(pl.ds(pl.multiple_of(r * GRID_W, GRID_W), GRID_W))
            wrows = pl.ds(pl.multiple_of(rs * GRID_W, GRID_W), kr * GRID_W)
            items.append((_stack_heads(q_ref[qrows[-1], :], low), k_ref[wrows, :], v_ref[wrows, :],
                          bias_ref[0, r - rs]))
        for rows_u, (o, _, _) in zip(qrows, _stacked_attention(items)):
            o_ref[rows_u, :] = _unstack_heads(o, low).astype(o_ref.dtype)
        return carry

    lax.fori_loop(0, rows // rows_per_step, body, 0)


def _neighbourhood_attention(qkv, bias, batch, seq, rows_per_step=16):
    n = qkv.shape[0]
    rows = seq // GRID_W
    pairs = NA_WIDTH // LANES
    return pl.pallas_call(
        functools.partial(_na_kernel, rows=rows, rows_per_step=rows_per_step),
        grid=(pairs, batch),
        in_specs=[
            pl.BlockSpec((seq, LANES), lambda p, b: (b, p)),
            pl.BlockSpec((seq, LANES), lambda p, b: (b, pairs + p)),
            pl.BlockSpec((seq, LANES), lambda p, b: (b, 2 * pairs + p)),
            pl.BlockSpec((1, NA_WIN_ROWS, HEADS_PER_LANE_TILE * GRID_W, NA_WIN_ROWS * GRID_W),
                         lambda p, b: (p, 0, 0, 0)),
        ],
        out_specs=pl.BlockSpec((seq, LANES), lambda p, b: (b, p)),
        out_shape=jax.ShapeDtypeStruct((n, NA_WIDTH), BF16),
        compiler_params=_params(("arbitrary", "arbitrary")),
        name="na_attn",
    )(qkv, qkv, qkv, bias)


def _alibi_slopes():
    n = N_DIL_GROUPS * DIL_HEADS_PER_GROUP
    s = np.exp2(-ALIBI_MAX * np.arange(1, n + 1, dtype=np.float64) / n).astype(np.float32)
    return s.reshape(N_DIL_GROUPS, DIL_HEADS_PER_GROUP)


def _dil_bias_table(group):
    blk = DIL_BLOCK
    dilation = DIL_GROUPS[group][1]
    slopes = _alibi_slopes()[group]
    qi = np.arange(blk)[:, None]
    kj = np.arange(3 * blk)[None, :]
    tables = []
    for shift in range(3):
        arel = np.abs(kj - qi - shift * blk)
        dist = (dilation * arel).astype(np.float32)
        per_head = [np.where(arel <= blk, -slopes[h] * dist, np.float32(NEG_INF)) for h in range(DIL_HEADS_PER_GROUP)]
        tables.append(np.stack(per_head))
    t = np.stack(tables, axis=1).astype(np.float32)
    pairs = DIL_HEADS_PER_GROUP // HEADS_PER_LANE_TILE
    t = t.reshape(pairs, HEADS_PER_LANE_TILE, 3, blk, 3 * blk).transpose(0, 2, 1, 3, 4)
    return t.reshape(pairs, 3, HEADS_PER_LANE_TILE * blk, 3 * blk)


def _dil_group(q_ref, k_ref, v_ref, bias_ref, o_ref, lse_ref, blocks_per_step):
    blk = DIL_BLOCK
    win = 3 * blk
    low = lax.broadcasted_iota(jnp.int32, (blk, LANES), 1) < HEAD_DIM
    dilation, length, _ = q_ref.shape
    nb = length // blk
    steps = dilation * nb

    def token_rows(sq, n):
        if dilation == 1:
            return pl.ds(pl.multiple_of(n * blk, blk), blk)
        return pl.ds(n * (blk * dilation) + sq, blk, stride=dilation)

    def body(i, carry):
        items, dst = [], []
        for u in range(blocks_per_step):
            t = i * blocks_per_step + u
            sq = t // nb
            n = t % nb
            wb = jnp.clip(n - 1, 0, nb - 3)
            qrows = pl.ds(pl.multiple_of(n * blk, blk), blk)
            wrows = pl.ds(pl.multiple_of(wb * blk, blk), win)
            dst.append(token_rows(sq, n))
            items.append((_stack_heads(q_ref[sq, qrows, :], low), k_ref[sq, wrows, :], v_ref[sq, wrows, :],
                          bias_ref[0, n - wb]))
        for rows, (o, m, z) in zip(dst, _stacked_attention(items)):
            o_ref[rows, :] = _unstack_heads(o, low)
            lse_ref[rows, :] = _unstack_heads(jnp.broadcast_to(m + jnp.log(z), o.shape), low)
        return carry

    lax.fori_loop(0, steps // blocks_per_step, body, 0)


def _dil_kernel(*refs, blocks_per_step, merge_rows):
    ng = N_DIL_GROUPS
    qkv_refs, bias_refs = refs[:3 * ng], refs[3 * ng:4 * ng]
    ob_ref, o_scr, lse_scr = refs[4 * ng:]
    for g in range(ng):
        q_ref, k_ref, v_ref = qkv_refs[3 * g:3 * g + 3]
        _dil_group(q_ref, k_ref, v_ref, bias_refs[g], o_scr.at[g], lse_scr.at[g], blocks_per_step)

    def merge(c, carry):
        rows = pl.ds(pl.multiple_of(c * merge_rows, merge_rows), merge_rows)
        lses = [lse_scr[g, rows, :] for g in range(ng)]
        top = functools.reduce(jnp.maximum, lses)
        es = [jnp.exp(l - top) for l in lses]
        num = functools.reduce(lambda a, b: a + b, [e * o_scr[g, rows, :] for g, e in enumerate(es)])
        ob_ref[rows, :] = (num / functools.reduce(lambda a, b: a + b, es)).astype(ob_ref.dtype)
        return carry

    lax.fori_loop(0, ob_ref.shape[0] // merge_rows, merge, 0)


def _dilated_attention(group_inputs, batch, seq, blocks_per_step=16, merge_rows=256):
    pairs = DIL_OUT_WIDTH // LANES
    operands, specs = [], []
    for qkv4, offsets in group_inputs:
        _, dilation, length, _ = qkv4.shape
        for off in offsets:
            operands.append(qkv4)
            specs.append(pl.BlockSpec((None, dilation, length, LANES),
                                      lambda p, b, off=off: (b, 0, 0, off + p)))
    for group in range(N_DIL_GROUPS):
        bias = jnp.asarray(_dil_bias_table(group))
        operands.append(bias)
        specs.append(pl.BlockSpec((1,) + bias.shape[1:], lambda p, b: (p, 0, 0, 0)))
    return pl.pallas_call(
        functools.partial(_dil_kernel, blocks_per_step=blocks_per_step, merge_rows=merge_rows),
        grid=(pairs, batch),
        in_specs=specs,
        out_specs=pl.BlockSpec((seq, LANES), lambda p, b: (b, p)),
        out_shape=jax.ShapeDtypeStruct((batch * seq, DIL_OUT_WIDTH), BF16),
        scratch_shapes=[pltpu.VMEM((N_DIL_GROUPS, seq, LANES), F32), pltpu.VMEM((N_DIL_GROUPS, seq, LANES), F32)],
        compiler_params=_params(("arbitrary", "arbitrary")),
        name="dil_attn",
    )(*operands)


def _outproj_kernel(x_ref, oa_ref, ob_ref, mod_ref, g_ref,
                    wg_ref, bg_ref, wpa_ref, wpb_ref, wo_ref, g2_ref, wr_ref, eb_ref, tri_ref,
                    out_ref, hp_ref, eidx_ref, rank_ref, w_ref, cnt_ref):
    d = x_ref.shape[1]
    x = x_ref[...]
    h = _modulated_norm(x, g_ref[...], mod_ref[0, SC1:SC1 + 1, :], mod_ref[0, SH1:SH1 + 1, :]).astype(BF16)

    ya = jnp.dot(oa_ref[...], wpa_ref[...], preferred_element_type=F32)
    yb = jnp.dot(ob_ref[...], wpb_ref[...], preferred_element_type=F32)
    ga = jax.nn.sigmoid(jnp.dot(h, wg_ref[:, :d], preferred_element_type=F32) + bg_ref[:, :d])
    mix = ga * ya
    gb = jax.nn.sigmoid(jnp.dot(h, wg_ref[:, d:], preferred_element_type=F32) + bg_ref[:, d:])
    mix = mix + gb * yb
    y = jnp.dot(mix.astype(BF16), wo_ref[...], preferred_element_type=F32)
    out = x + mod_ref[0, GT1:GT1 + 1, :] * y
    out_ref[...] = out
    _route_block(out, mod_ref, g2_ref, wr_ref, eb_ref, tri_ref, hp_ref, eidx_ref, rank_ref, w_ref, cnt_ref)


def _outproj(x2, oa, ob, mod, g, w_gate, b_gate, w_pa, w_pb, w_o, g_ffn, wr_split, e_bias, seq, tm=512):
    n, d = x2.shape
    assert n % tm == 0 and seq % tm == 0 and tm % SC_CHUNK == 0
    blocks_per_batch = seq // tm
    tri = jnp.asarray(np.triu(np.ones((tm, tm), np.float32)), BF16)
    row = lambda c: pl.BlockSpec((tm, c), lambda i: (i, 0))
    chunked = pl.BlockSpec((tm // SC_CHUNK, TOP_K, SC_CHUNK), lambda i: (i, 0, 0))
    full = lambda a: pl.BlockSpec(a.shape, lambda i: (0,) * a.ndim)
    return pl.pallas_call(
        _outproj_kernel,
        grid=(n // tm,),
        in_specs=[row(d), row(NA_WIDTH), row(DIL_OUT_WIDTH)] + [
            pl.BlockSpec((1, 6, d), lambda i: (i // blocks_per_batch, 0, 0)),
            full(g), full(w_gate), full(b_gate), full(w_pa), full(w_pb), full(w_o),
            full(g_ffn), full(wr_split), full(e_bias), full(tri),
        ],
        out_specs=[row(d), row(d // 2), chunked, chunked, row(LANES),
                   pl.BlockSpec((N_EXPERTS, LANES), lambda i: (0, 0))],
        out_shape=[
            jax.ShapeDtypeStruct((n, d), F32),
            jax.ShapeDtypeStruct((n, d // 2), jnp.int32),
            jax.ShapeDtypeStruct((n // SC_CHUNK, TOP_K, SC_CHUNK), jnp.int32),
            jax.ShapeDtypeStruct((n // SC_CHUNK, TOP_K, SC_CHUNK), jnp.int32),
            jax.ShapeDtypeStruct((n, LANES), F32),
            jax.ShapeDtypeStruct((N_EXPERTS, LANES), F32),
        ],
        compiler_params=_params(("arbitrary",)),
        name="out_proj",
    )(x2, oa, ob, mod, g, w_gate, b_gate, w_pa, w_pb, w_o, g_ffn, wr_split, e_bias, tri)


def _first_index_of_max(cur, idx, size):
    m = jnp.max(cur, axis=0, keepdims=True)
    first = jnp.min(jnp.where(cur == m, idx, size), axis=0, keepdims=True)
    return m, first


def _route_transposed(logits_t, e_bias):
    tokens = logits_t.shape[1]
    per_group = N_EXPERTS // N_EXPERT_GROUPS
    scores = jax.nn.sigmoid(logits_t)
    biased = scores + e_bias
    midx = lax.broadcasted_iota(jnp.int32, (per_group, tokens), 0)
    grp_scores = []
    for g in range(N_EXPERT_GROUPS):
        vals = biased[g * per_group:(g + 1) * per_group, :]
        m1, first = _first_index_of_max(vals, midx, per_group)
        m2 = jnp.max(jnp.where(midx == first, -jnp.inf, vals), axis=0, keepdims=True)
        grp_scores.append(m1 + m2)
    cur = jnp.concatenate(grp_scores, axis=0)
    gidx = lax.broadcasted_iota(jnp.int32, (N_EXPERT_GROUPS, tokens), 0)
    grp_sel = jnp.zeros((N_EXPERT_GROUPS, tokens), jnp.bool_)
    for _ in range(TOP_GROUPS):
        _, first = _first_index_of_max(cur, gidx, N_EXPERT_GROUPS)
        pick = gidx == first
        grp_sel = jnp.logical_or(grp_sel, pick)
        cur = jnp.where(pick, -jnp.inf, cur)
    rows = []
    for g in range(N_EXPERT_GROUPS):
        vals = biased[g * per_group:(g + 1) * per_group, :]
        rows.append(jnp.where(grp_sel[g:g + 1, :], vals, NEG_INF))
    cur = jnp.concatenate(rows, axis=0)
    eidx = lax.broadcasted_iota(jnp.int32, (N_EXPERTS, tokens), 0)
    firsts, picks, weights = [], [], []
    for _ in range(TOP_K):
        _, first = _first_index_of_max(cur, eidx, N_EXPERTS)
        pick = eidx == first
        firsts.append(first)
        picks.append(pick)
        weights.append(jnp.sum(jnp.where(pick, scores, 0.0), axis=0, keepdims=True))
        cur = jnp.where(pick, -jnp.inf, cur)
    total = functools.reduce(lambda a, b: a + b, weights)
    return firsts, picks, [w / total * ROUTED_SCALE for w in weights]


def _pack_halves(a):
    half = a.shape[1] // 2
    bits = lax.bitcast_convert_type(a.astype(BF16).astype(F32), jnp.int32)
    return lax.shift_right_logical(bits[:, :half], BF16_BITS) | bits[:, half:]


def _unpack_halves(w):
    low = lax.bitcast_convert_type(lax.shift_left(w, BF16_BITS), F32)
    high = lax.bitcast_convert_type(w & jnp.int32(-(1 << BF16_BITS)), F32)
    return low, high


def _route_block(x, mod_ref, g_ref, wr_ref, eb_ref, tri_ref, hp_ref, eidx_ref, rank_ref, w_ref, cnt_ref):
    tm = x.shape[0]

    @pl.when(pl.program_id(0) == 0)
    def _():
        cnt_ref[...] = jnp.zeros_like(cnt_ref)

    h = _modulated_norm(x, g_ref[...], mod_ref[0, SC2:SC2 + 1, :], mod_ref[0, SH2:SH2 + 1, :])
    hp_ref[...] = _pack_halves(h)
    h_hi = h.astype(BF16)
    h_lo = (h - h_hi.astype(F32)).astype(BF16)
    nt = (((1,), (1,)), ((), ()))
    logits_t = (lax.dot_general(wr_ref[0], h_hi, nt, preferred_element_type=F32)
                + lax.dot_general(wr_ref[0], h_lo, nt, preferred_element_type=F32)
                + lax.dot_general(wr_ref[1], h_hi, nt, preferred_element_type=F32))
    firsts, picks, weights = _route_transposed(logits_t, eb_ref[...])
    sel = functools.reduce(jnp.logical_or, picks)
    sel_f = jnp.where(sel, 1.0, 0.0)
    incl = jnp.dot(sel_f.astype(BF16), tri_ref[...], preferred_element_type=F32)
    before = cnt_ref[:, 0:1] + incl - sel_f
    eidx = jnp.concatenate(firsts, axis=0)
    rank = jnp.concatenate(
        [jnp.sum(jnp.where(p, before, 0.0), axis=0, keepdims=True) for p in picks], axis=0).astype(jnp.int32)
    for c in range(tm // SC_CHUNK):
        eidx_ref[c] = eidx[:, c * SC_CHUNK:(c + 1) * SC_CHUNK]
        rank_ref[c] = rank[:, c * SC_CHUNK:(c + 1) * SC_CHUNK]
    pad = jnp.concatenate(weights + [jnp.zeros((LANES - TOP_K, tm), F32)], axis=0)
    w_ref[...] = pad.T
    cnt_ref[...] = cnt_ref[...] + incl[:, tm - 1:tm]


def _sc_worker_id():
    return lax.axis_index("subcore") * SC_CORES + lax.axis_index("core")


def _sc_scatter_rows(src, idx3, n_out):
    n, w = src.shape
    assert n % (2 * SC_CHUNK * SC_WORKERS) == 0
    per_worker = n // SC_CHUNK // SC_WORKERS
    mesh = plsc.VectorSubcoreMesh(core_axis_name="core", subcore_axis_name="subcore")

    @functools.partial(
        pl.kernel, mesh=mesh, out_type=jax.ShapeDtypeStruct((n_out, w), src.dtype), name="moe_dispatch",
        scratch_types=[pltpu.VMEM((2, TOP_K, SC_CHUNK), jnp.int32), pltpu.VMEM((2, SC_CHUNK, w), src.dtype),
                       pltpu.SemaphoreType.DMA((2,)), pltpu.SemaphoreType.DMA((2,)), pltpu.SemaphoreType.DMA])
    def scatter(src_hbm, idx_hbm, out_hbm, idx_v, rows_v, idx_sem, row_sem, out_sem):
        first = _sc_worker_id() * per_worker

        def loads(chunk, slot):
            return (pltpu.make_async_copy(idx_hbm.at[chunk], idx_v.at[slot], idx_sem.at[slot]),
                    pltpu.make_async_copy(src_hbm.at[pl.ds(chunk * SC_CHUNK, SC_CHUNK)], rows_v.at[slot],
                                          row_sem.at[slot]))

        for cp in loads(first, 0):
            cp.start()

        @pl.loop(0, per_worker, step=2)
        def _(i):
            for slot in range(2):
                chunk = first + i + slot
                for cp in loads(chunk, slot):
                    cp.wait()

                @pl.when(i + slot + 1 < per_worker)
                def _():
                    for cp in loads(chunk + 1, 1 - slot):
                        cp.start()

                copies = [pltpu.make_async_copy(rows_v.at[slot], out_hbm.at[idx_v.at[slot, k]], out_sem)
                          for k in range(TOP_K)]
                for cp in copies:
                    cp.start()
                for cp in copies:
                    cp.wait()

    return scatter(src, idx3)


def _sc_gather_rows(src, idx3):
    _, w = src.shape
    chunks = idx3.shape[0]
    assert chunks % SC_WORKERS == 0
    per_worker = chunks // SC_WORKERS
    mesh = plsc.VectorSubcoreMesh(core_axis_name="core", subcore_axis_name="subcore")

    @functools.partial(
        pl.kernel, mesh=mesh, out_type=jax.ShapeDtypeStruct((TOP_K, chunks * SC_CHUNK, w), src.dtype),
        name="moe_collect",
        scratch_types=[pltpu.VMEM((TOP_K, SC_CHUNK), jnp.int32), pltpu.VMEM((2, SC_CHUNK, w), src.dtype),
                       pltpu.SemaphoreType.DMA((2,)), pltpu.SemaphoreType.DMA((2,))])
    def gather(src_hbm, idx_hbm, out_hbm, idx_v, rows_v, in_sem, out_sem):
        first = _sc_worker_id() * per_worker

        @pl.loop(0, per_worker)
        def _(i):
            chunk = first + i
            pltpu.sync_copy(idx_hbm.at[chunk], idx_v)
            reads = [pltpu.make_async_copy(src_hbm.at[idx_v.at[k]], rows_v.at[k % 2], in_sem.at[k % 2])
                     for k in range(TOP_K)]
            writes = [pltpu.make_async_copy(rows_v.at[k % 2], out_hbm.at[k, pl.ds(chunk * SC_CHUNK, SC_CHUNK)],
                                            out_sem.at[k % 2]) for k in range(TOP_K)]
            reads[0].start()
            for k in range(TOP_K):
                if k + 1 < TOP_K:
                    if k >= 1:
                        writes[k - 1].wait()
                    reads[k + 1].start()
                reads[k].wait()
                writes[k].start()
            writes[TOP_K - 2].wait()
            writes[TOP_K - 1].wait()

    return gather(src, idx3)


def _swiglu(x, w_gate_up, w_down):
    gu = jnp.dot(x, w_gate_up, preferred_element_type=F32)
    gate = gu[:, :EXPERT_DIM]
    act = (gate * jax.nn.sigmoid(gate)) * gu[:, EXPERT_DIM:]
    return jnp.dot(act.astype(BF16), w_down, preferred_element_type=F32)


def _expert_ffn_kernel(held_ref, use_ref, nv_ref, xs_ref, *refs, tb):
    slots = FFN_BLOCKS_PER_STEP
    w_refs, ys_ref, wgu_bf, wd_bf = refs[:3 * slots], refs[3 * slots], refs[3 * slots + 1], refs[3 * slots + 2]
    i = pl.program_id(0)
    sub = tb // FFN_SUB_BLOCKS
    row = lax.broadcasted_iota(jnp.int32, (sub, xs_ref.shape[1]), 0)

    for slot in range(slots):
        blk = i * slots + slot
        wg_ref, wu_ref, wd_ref = w_refs[3 * slot:3 * slot + 3]

        @pl.when(jnp.logical_or(i == 0, held_ref[blk] != held_ref[jnp.maximum(blk - slots, 0)]))
        def _():
            wgu_bf[slot, :, :EXPERT_DIM] = wg_ref[0, 0].astype(BF16)
            wgu_bf[slot, :, EXPERT_DIM:] = wu_ref[0, 0].astype(BF16)
            wd_bf[slot] = wd_ref[0, 0].astype(BF16)

    @pl.when(nv_ref[i * slots] > 0)
    def _():
        spans, xs = [], []
        for slot in range(slots):
            nvalid = nv_ref[i * slots + slot]
            source = use_ref[i * slots + slot]
            for s in range(FFN_SUB_BLOCKS):
                span = pl.ds(slot * tb + s * sub, sub)
                packed = jnp.where(row < nvalid - s * sub, xs_ref[span, :], 0)
                low, high = _unpack_halves(packed)
                spans.append((source, span))
                xs.append(jnp.concatenate([low, high], axis=1).astype(BF16))
        gus = [jnp.dot(x, wgu_bf[source], preferred_element_type=F32) for (source, _), x in zip(spans, xs)]
        acts = [((gu[:, :EXPERT_DIM] * jax.nn.sigmoid(gu[:, :EXPERT_DIM])) * gu[:, EXPERT_DIM:]).astype(BF16)
                for gu in gus]
        ys = [jnp.dot(a, wd_bf[source], preferred_element_type=F32) for (source, _), a in zip(spans, acts)]
        for (_, span), y in zip(spans, ys):
            ys_ref[span, :] = _pack_halves(y)

    @pl.when(nv_ref[i * slots] == 0)
    def _():
        ys_ref[...] = jnp.zeros_like(ys_ref)


def _expert_ffn(xs, block_expert, block_valid, w_gate, w_up, w_down, layer, tb):
    p, half = xs.shape
    d = 2 * half
    slots = FFN_BLOCKS_PER_STEP
    n_blocks = block_expert.shape[0]
    run = jnp.cumsum(jnp.concatenate([jnp.zeros((1,), jnp.int32),
                                      (block_expert[1:] != block_expert[:-1]).astype(jnp.int32)]))
    blocks = jnp.arange(n_blocks, dtype=jnp.int32)
    first_block = jnp.minimum(jnp.sum((run[None, :] < blocks[:, None]).astype(jnp.int32), axis=1), n_blocks - 1)
    run_expert = block_expert[first_block]
    newest = run.reshape(-1, slots)[:, slots - 1]
    held = []
    for slot in range(slots):
        r = newest - (newest - slot) % slots
        held.append(run_expert[jnp.where(r < 0, slot, r)])
    held = jnp.stack(held, axis=1).reshape(-1)
    use = run % slots
    weight_specs = []
    for slot in range(slots):
        index = lambda i, held, use, nv, slot=slot: (layer, held[i * slots + slot], 0, 0)
        weight_specs += [pl.BlockSpec((1, 1, d, EXPERT_DIM), index), pl.BlockSpec((1, 1, d, EXPERT_DIM), index),
                         pl.BlockSpec((1, 1, EXPERT_DIM, d), index)]
    grid_spec = pltpu.PrefetchScalarGridSpec(
        num_scalar_prefetch=3,
        grid=(p // (tb * slots),),
        in_specs=[pl.BlockSpec((tb * slots, half), lambda i, held, use, nv: (i, 0))] + weight_specs,
        out_specs=pl.BlockSpec((tb * slots, half), lambda i, held, use, nv: (i, 0)),
        scratch_shapes=[pltpu.VMEM((slots, d, 2 * EXPERT_DIM), BF16), pltpu.VMEM((slots, EXPERT_DIM, d), BF16)],
    )
    return pl.pallas_call(
        functools.partial(_expert_ffn_kernel, tb=tb),
        grid_spec=grid_spec,
        out_shape=jax.ShapeDtypeStruct((p, half), jnp.int32),
        compiler_params=_params(("arbitrary",)),
        name="moe_expert_ffn",
    )(held, use, block_valid, xs, *([w_gate, w_up, w_down] * slots))


def _combine_kernel(x_ref, hp_ref, yg_ref, w_ref, mod_ref, wgu_ref, wd_ref, gf_ref, o_ref, *, final_norm):
    low, high = _unpack_halves(hp_ref[...])
    h = jnp.concatenate([low, high], axis=1).astype(BF16)
    shared = _swiglu(h, wgu_ref[...], wd_ref[...])
    half = hp_ref.shape[1]
    acc_low, acc_high = shared[:, :half], shared[:, half:]
    w = w_ref[...]
    for k in range(TOP_K):
        low, high = _unpack_halves(yg_ref[k])
        wk = w[:, k:k + 1]
        acc_low = acc_low + wk * low
        acc_high = acc_high + wk * high
    y = jnp.concatenate([acc_low, acc_high], axis=1)
    out = x_ref[...] + mod_ref[0, GT2:GT2 + 1, :] * y
    if final_norm:
        out = out * lax.rsqrt(jnp.mean(out * out, axis=-1, keepdims=True) + EPS) * gf_ref[...]
    o_ref[...] = out


def _combine(x2, hp, yg, w, mod, ws_gate_up, ws_down, g_final, seq, final_norm, tm=512):
    n, d = x2.shape
    blocks_per_batch = seq // tm
    const = lambda a: pl.BlockSpec(a.shape, lambda i: (0,) * a.ndim)
    return pl.pallas_call(
        functools.partial(_combine_kernel, final_norm=final_norm),
        grid=(n // tm,),
        in_specs=[
            pl.BlockSpec((tm, d), lambda i: (i, 0)),
            pl.BlockSpec((tm, d // 2), lambda i: (i, 0)),
            pl.BlockSpec((TOP_K, tm, d // 2), lambda i: (0, i, 0)),
            pl.BlockSpec((tm, LANES), lambda i: (i, 0)),
            pl.BlockSpec((1, 6, d), lambda i: (i // blocks_per_batch, 0, 0)),
            const(ws_gate_up), const(ws_down), const(g_final),
        ],
        out_specs=pl.BlockSpec((tm, d), lambda i: (i, 0)),
        out_shape=jax.ShapeDtypeStruct((n, d), F32),
        compiler_params=_params(("arbitrary",)),
        name="moe_combine",
    )(x2, hp, yg, w, mod, ws_gate_up, ws_down, g_final)


def _token_mixer(x2, mod, g_mix, w_in, b_gate, rpb, w_pa, w_pb, w_o, g_ffn, w_router, e_bias, batch, seq):
    d = x2.shape[1]
    dil0 = 3 * NA_WIDTH
    group_cols = [[dil0 + part * DIL_WIDTH + grp * DIL_OUT_WIDTH for part in range(3)]
                  for grp in range(N_DIL_GROUPS)]
    order = sorted(range(N_DIL_GROUPS), key=lambda grp: DIL_GROUPS[grp][1] > 1)
    w_qkv = jnp.concatenate(
        [w_in[:, :dil0]] + [w_in[:, c:c + DIL_OUT_WIDTH] for grp in order for c in group_cols[grp]],
        axis=1).astype(BF16)
    tok, *residue = _inproj(x2, mod, g_mix.reshape(1, d), w_qkv, batch, seq)
    o_a = _neighbourhood_attention(tok, _na_bias_table(rpb), batch, seq)

    qkv_offsets = tuple(part * DIL_OUT_WIDTH // LANES for part in range(3))
    group_inputs = [None] * N_DIL_GROUPS
    residue = iter(residue)
    tok_offset = dil0 // LANES
    for grp in order:
        if DIL_GROUPS[grp][1] == 1:
            group_inputs[grp] = (tok.reshape(batch, 1, seq, tok.shape[1]),
                                 tuple(tok_offset + o for o in qkv_offsets))
            tok_offset += 3 * DIL_OUT_WIDTH // LANES
        else:
            group_inputs[grp] = (next(residue), qkv_offsets)
    o_b = _dilated_attention(group_inputs, batch, seq)

    wr_t = w_router.T
    wr_hi = wr_t.astype(BF16)
    wr_split = jnp.stack([wr_hi, (wr_t - wr_hi.astype(F32)).astype(BF16)])
    return _outproj(x2, o_a, o_b, mod, g_mix.reshape(1, d),
                    w_in[:, QKV_COLS:].astype(BF16), b_gate.reshape(1, -1),
                    w_pa.astype(BF16), w_pb.astype(BF16), w_o.astype(BF16),
                    g_ffn.reshape(1, d), wr_split, e_bias.reshape(-1, 1), seq)


def _dispatch_plan(eidx3, rank3, counts, tb):
    n = eidx3.shape[0] * SC_CHUNK
    n_blocks = -(-(n * TOP_K + N_EXPERTS * (tb - 1)) // tb)
    n_blocks = -(-n_blocks // FFN_BLOCKS_PER_STEP) * FFN_BLOCKS_PER_STEP
    padded = (counts + tb - 1) // tb * tb
    seg_end = jnp.cumsum(padded)
    seg_start = seg_end - padded
    experts = jnp.arange(N_EXPERTS, dtype=jnp.int32)

    def lookup(table, idx):
        sel = idx[None] == experts.reshape((N_EXPERTS,) + (1,) * idx.ndim)
        return jnp.sum(jnp.where(sel, table.reshape((N_EXPERTS,) + (1,) * idx.ndim), 0), axis=0)

    idx3 = lookup(seg_start, eidx3) + rank3
    block_start = jnp.arange(n_blocks, dtype=jnp.int32) * tb
    block_expert = jnp.sum((seg_end[:, None] <= block_start[None, :]).astype(jnp.int32), axis=0)
    block_expert = jnp.minimum(block_expert, N_EXPERTS - 1)
    block_valid = jnp.clip(lookup(counts, block_expert) - (block_start - lookup(seg_start, block_expert)), 0, tb)
    return idx3, block_expert, block_valid.astype(jnp.int32), n_blocks


def _moe_layer(x2, routed, mod, we_gate, we_up, we_down, layer, ws_gate, ws_up, ws_down, g_final, seq, final_norm,
               tb=EXPERT_ROW_BLOCK):
    d = x2.shape[1]
    hp, eidx3, rank3, w, cnt = routed
    counts = cnt[:, 0].astype(jnp.int32)
    idx3, block_expert, block_valid, n_blocks = _dispatch_plan(eidx3, rank3, counts, tb)
    xs = _sc_scatter_rows(hp, idx3, n_blocks * tb)
    ys = _expert_ffn(xs, block_expert, block_valid, we_gate, we_up, we_down, layer, tb)
    yg = _sc_gather_rows(ys, idx3)
    return _combine(x2, hp, yg, w, mod, jnp.concatenate([ws_gate, ws_up], axis=-1).astype(BF16),
                    ws_down.astype(BF16), g_final.reshape(1, d), seq, final_norm)


def kernel(x, c, w_ada, b_ada, g_mix, w_in, b_gate, rpb, w_pa, w_pb, w_o, g_ffn, w_router, e_bias,
           we_gate, we_up, we_down, ws_gate, ws_up, ws_down, g_final):
    batch, seq, d = x.shape
    depth = w_ada.shape[0]
    mods = _ada(c, w_ada, b_ada).reshape(depth, batch, 6, d)
    x2 = x.reshape(batch * seq, d)
    for l in range(depth):
        x2, *routed = _token_mixer(x2, mods[l], g_mix[l], w_in[l], b_gate[l], rpb[l], w_pa[l], w_pb[l], w_o[l],
                                   g_ffn[l], w_router[l], e_bias[l], batch, seq)
        x2 = _moe_layer(x2, routed, mods[l], we_gate, we_up, we_down, l,
                        ws_gate[l], ws_up[l], ws_down[l], g_final, seq, final_norm=(l == depth - 1))
    return x2.reshape(batch, seq, d)
```

```python
import functools

import numpy as np
import jax
import jax.numpy as jnp
from jax import lax
from jax.experimental import pallas as pl
from jax.experimental.pallas import tpu as pltpu
from jax.experimental.pallas import tpu_sc as plsc

HEAD_DIM = 64
GRID_W = 64
NA_HEADS = 8
NA_WIN_ROWS = 8
NA_WIN_COLS = 16
DIL_GROUPS = ((128, 1), (512, 4), (2048, 16))
DIL_HEADS_PER_GROUP = 4
N_DIL_GROUPS = len(DIL_GROUPS)
NA_WIDTH = NA_HEADS * HEAD_DIM
DIL_WIDTH = N_DIL_GROUPS * DIL_HEADS_PER_GROUP * HEAD_DIM
DIL_OUT_WIDTH = DIL_HEADS_PER_GROUP * HEAD_DIM
QKV_COLS = 3 * (NA_WIDTH + DIL_WIDTH)
N_EXPERTS = 64
TOP_K = 8
N_EXPERT_GROUPS = 8
TOP_GROUPS = 4
EXPERT_DIM = 256
ROUTED_SCALE = 2.5
ALIBI_MAX = 8.0
EPS = 1e-6
NEG_INF = -1e30

LANES = 128
HEADS_PER_LANE_TILE = LANES // HEAD_DIM
DIL_BLOCK = 64
VMEM_LIMIT_BYTES = 56 * 1024 * 1024

SC_CORES = 2
SC_SUBCORES = 16
SC_WORKERS = SC_CORES * SC_SUBCORES
SC_CHUNK = 64
EXPERT_ROW_BLOCK = 512
FFN_SUB_BLOCKS = 2
FFN_BLOCKS_PER_STEP = 4

F32 = jnp.float32
BF16 = jnp.bfloat16
BF16_BITS = 16

SH1, SC1, GT1, SH2, SC2, GT2 = range(6)


def _params(sem):
    return pltpu.CompilerParams(dimension_semantics=sem, vmem_limit_bytes=VMEM_LIMIT_BYTES)


def _modulated_norm(x, g, scale, shift):
    r = lax.rsqrt(jnp.mean(x * x, axis=-1, keepdims=True) + EPS)
    return (x * r) * (g * (1.0 + scale)) + shift


def _ada_kernel(c_ref, w_ref, b_ref, o_ref):
    c = c_ref[...]
    act = c * jax.nn.sigmoid(c)
    o_ref[0] = jnp.dot(act, w_ref[0], preferred_element_type=F32,
                       precision=lax.Precision.HIGHEST) + b_ref[0]


def _ada(c, w_ada, b_ada):
    depth, d, six_d = w_ada.shape
    b = c.shape[0]
    tn = d
    return pl.pallas_call(
        _ada_kernel,
        grid=(depth, six_d // tn),
        in_specs=[
            pl.BlockSpec((b, d), lambda l, j: (0, 0)),
            pl.BlockSpec((1, d, tn), lambda l, j: (l, 0, j)),
            pl.BlockSpec((1, 1, tn), lambda l, j: (l, 0, j)),
        ],
        out_specs=pl.BlockSpec((1, b, tn), lambda l, j: (l, 0, j)),
        out_shape=jax.ShapeDtypeStruct((depth, b, six_d), F32),
        compiler_params=_params(("arbitrary", "arbitrary")),
        name="ada_mod",
    )(c, w_ada, b_ada.reshape(depth, 1, six_d))


def _inproj_kernel(x_ref, mod_ref, g_ref, w_ref, tok_ref, *rest, dilations):
    res_refs, acc_refs = rest[:len(dilations)], rest[len(dilations):]
    tm = x_ref.shape[0]
    tok_cols = tok_ref.shape[1]
    tn = (w_ref.shape[1] - tok_cols) // len(dilations)
    h = _modulated_norm(x_ref[...], g_ref[...], mod_ref[0, SC1:SC1 + 1, :], mod_ref[0, SH1:SH1 + 1, :]).astype(BF16)

    for c0 in range(0, tok_cols, tn):
        tok_ref[:, c0:c0 + tn] = jnp.dot(h, w_ref[:, c0:c0 + tn], preferred_element_type=F32).astype(BF16)

    for g, (res_ref, acc_ref, dilation) in enumerate(zip(res_refs, acc_refs, dilations)):
        c0 = tok_cols + g * tn
        res = jnp.dot(h, w_ref[:, c0:c0 + tn], preferred_element_type=F32)
        for c in range(acc_ref.shape[0]):
            acc_ref[c] = res[:, c * LANES:(c + 1) * LANES]
        for r in range(dilation):
            for c in range(acc_ref.shape[0]):
                res_ref[0, r, :, c * LANES:(c + 1) * LANES] = (
                    acc_ref[c, pl.ds(r, tm // dilation, stride=dilation), :].astype(BF16))


def _inproj(x2, mod, g, w_qkv, batch, seq, tm=1024):
    n, d = x2.shape
    tn = 3 * DIL_OUT_WIDTH
    dilations = tuple(dil for _, dil in DIL_GROUPS if dil > 1)
    tok_cols = w_qkv.shape[1] - tn * len(dilations)
    blocks_per_batch = seq // tm
    res_specs = [pl.BlockSpec((1, dil, tm // dil, tn),
                              lambda i: (i // blocks_per_batch, 0, i % blocks_per_batch, 0)) for dil in dilations]
    res_shapes = [jax.ShapeDtypeStruct((batch, dil, seq // dil, tn), BF16) for dil in dilations]
    return pl.pallas_call(
        functools.partial(_inproj_kernel, dilations=dilations),
        grid=(n // tm,),
        in_specs=[
            pl.BlockSpec((tm, d), lambda i: (i, 0)),
            pl.BlockSpec((1, 6, d), lambda i: (i // blocks_per_batch, 0, 0)),
            pl.BlockSpec((1, d), lambda i: (0, 0)),
            pl.BlockSpec(w_qkv.shape, lambda i: (0, 0)),
        ],
        out_specs=[pl.BlockSpec((tm, tok_cols), lambda i: (i, 0))] + res_specs,
        out_shape=[jax.ShapeDtypeStruct((n, tok_cols), BF16)] + res_shapes,
        scratch_shapes=[pltpu.VMEM((tn // LANES, tm, LANES), F32) for _ in dilations],
        compiler_params=_params(("arbitrary",)),
        name="in_proj",
    )(x2, mod, g, w_qkv)


def _na_bias_table(rpb):
    heads = rpb.shape[0]
    cols = np.arange(GRID_W)
    col_start = np.clip(cols - NA_WIN_COLS // 2, 0, GRID_W - NA_WIN_COLS)
    col_mask = (cols[None, :] >= col_start[:, None]) & (cols[None, :] < col_start[:, None] + NA_WIN_COLS)
    edge = GRID_W - NA_WIN_COLS
    ext = jnp.concatenate([jnp.repeat(rpb[..., :1], edge, axis=-1), rpb, jnp.repeat(rpb[..., -1:], edge, axis=-1)],
                          axis=-1).astype(F32)
    rpb_cols = jnp.stack([ext[..., GRID_W - 1 - cq:2 * GRID_W - 1 - cq] for cq in range(GRID_W)],
                         axis=-2)
    t = jnp.stack([rpb_cols[:, NA_WIN_ROWS - 1 - off:2 * NA_WIN_ROWS - 1 - off] for off in range(NA_WIN_ROWS)],
                  axis=1)
    t = t.transpose(0, 1, 3, 2, 4)
    t = jnp.where(col_mask[:, None, :], t, NEG_INF)
    t = t.reshape(heads // HEADS_PER_LANE_TILE, HEADS_PER_LANE_TILE, NA_WIN_ROWS, GRID_W, NA_WIN_ROWS * GRID_W)
    return t.transpose(0, 2, 1, 3, 4).reshape(heads // HEADS_PER_LANE_TILE, NA_WIN_ROWS,
                                              HEADS_PER_LANE_TILE * GRID_W, NA_WIN_ROWS * GRID_W)


def _stack_heads(q, low):
    scaled = q * (HEAD_DIM ** -0.5)
    zero = jnp.zeros_like(scaled)
    return jnp.concatenate([jnp.where(low, scaled, zero), jnp.where(low, zero, scaled)], axis=0)


def _stacked_attention(items):
    scores = [lax.dot_general(q2, kw, (((1,), (1,)), ((), ())), preferred_element_type=F32) + bias
              for q2, kw, _, bias in items]
    probs = []
    for s in scores:
        m = jnp.max(s, axis=-1, keepdims=True)
        p = jnp.exp(s - m)
        probs.append((p.astype(BF16), m, jnp.sum(p, axis=-1, keepdims=True)))
    return [(jnp.dot(p, vw, preferred_element_type=F32) / z, m, z)
            for (p, m, z), (_, _, vw, _) in zip(probs, items)]


def _unstack_heads(a, low):
    half = a.shape[0] // HEADS_PER_LANE_TILE
    return jnp.where(low, a[:half], a[half:])


def _na_kernel(q_ref, k_ref, v_ref, bias_ref, o_ref, *, rows, rows_per_step):
    kr = NA_WIN_ROWS
    low = lax.broadcasted_iota(jnp.int32, (GRID_W, LANES), 1) < HEAD_DIM

    def body(i, carry):
        items, qrows = [], []
        for u in range(rows_per_step):
            r = i * rows_per_step + u
            rs = jnp.clip(r - kr // 2, 0, rows - kr)
            qrows.append(pl.ds(pl.multiple_of(r * GRID_W, GRID_W), GRID_W))
            wrows = pl.ds(pl.multiple_of(rs * GRID_W, GRID_W), kr * GRID_W)
            items.append((_stack_heads(q_ref[qrows[-1], :], low), k_ref[wrows, :], v_ref[wrows, :],
                          bias_ref[0, r - rs]))
        for rows_u, (o, _, _) in zip(qrows, _stacked_attention(items)):
            o_ref[rows_u, :] = _unstack_heads(o, low).astype(o_ref.dtype)
        return carry

    lax.fori_loop(0, rows // rows_per_step, body, 0)


def _neighbourhood_attention(qkv, bias, batch, seq, rows_per_step=16):
    n = qkv.shape[0]
    rows = seq // GRID_W
    pairs = NA_WIDTH // LANES
    return pl.pallas_call(
        functools.partial(_na_kernel, rows=rows, rows_per_step=rows_per_step),
        grid=(pairs, batch),
        in_specs=[
            pl.BlockSpec((seq, LANES), lambda p, b: (b, p)),
            pl.BlockSpec((seq, LANES), lambda p, b: (b, pairs + p)),
            pl.BlockSpec((seq, LANES), lambda p, b: (b, 2 * pairs + p)),
            pl.BlockSpec((1, NA_WIN_ROWS, HEADS_PER_LANE_TILE * GRID_W, NA_WIN_ROWS * GRID_W),
                         lambda p, b: (p, 0, 0, 0)),
        ],
        out_specs=pl.BlockSpec((seq, LANES), lambda p, b: (b, p)),
        out_shape=jax.ShapeDtypeStruct((n, NA_WIDTH), BF16),
        compiler_params=_params(("arbitrary", "arbitrary")),
        name="na_attn",
    )(qkv, qkv, qkv, bias)


def _alibi_slopes():
    n = N_DIL_GROUPS * DIL_HEADS_PER_GROUP
    s = np.exp2(-ALIBI_MAX * np.arange(1, n + 1, dtype=np.float64) / n).astype(np.float32)
    return s.reshape(N_DIL_GROUPS, DIL_HEADS_PER_GROUP)


def _dil_bias_table(group):
    blk = DIL_BLOCK
    dilation = DIL_GROUPS[group][1]
    slopes = _alibi_slopes()[group]
    qi = np.arange(blk)[:, None]
    kj = np.arange(3 * blk)[None, :]
    tables = []
    for shift in range(3):
        arel = np.abs(kj - qi - shift * blk)
        dist = (dilation * arel).astype(np.float32)
        per_head = [np.where(arel <= blk, -slopes[h] * dist, np.float32(NEG_INF)) for h in range(DIL_HEADS_PER_GROUP)]
        tables.append(np.stack(per_head))
    t = np.stack(tables, axis=1).astype(np.float32)
    pairs = DIL_HEADS_PER_GROUP // HEADS_PER_LANE_TILE
    t = t.reshape(pairs, HEADS_PER_LANE_TILE, 3, blk, 3 * blk).transpose(0, 2, 1, 3, 4)
    return t.reshape(pairs, 3, HEADS_PER_LANE_TILE * blk, 3 * blk)


def _dil_group(q_ref, k_ref, v_ref, bias_ref, o_ref, lse_ref, blocks_per_step):
    blk = DIL_BLOCK
    win = 3 * blk
    low = lax.broadcasted_iota(jnp.int32, (blk, LANES), 1) < HEAD_DIM
    dilation, length, _ = q_ref.shape
    nb = length // blk
    steps = dilation * nb

    def token_rows(sq, n):
        if dilation == 1:
            return pl.ds(pl.multiple_of(n * blk, blk), blk)
        return pl.ds(n * (blk * dilation) + sq, blk, stride=dilation)

    def body(i, carry):
        items, dst = [], []
        for u in range(blocks_per_step):
            t = i * blocks_per_step + u
            sq = t // nb
            n = t % nb
            wb = jnp.clip(n - 1, 0, nb - 3)
            qrows = pl.ds(pl.multiple_of(n * blk, blk), blk)
            wrows = pl.ds(pl.multiple_of(wb * blk, blk), win)
            dst.append(token_rows(sq, n))
            items.append((_stack_heads(q_ref[sq, qrows, :], low), k_ref[sq, wrows, :], v_ref[sq, wrows, :],
                          bias_ref[0, n - wb]))
        for rows, (o, m, z) in zip(dst, _stacked_attention(items)):
            o_ref[rows, :] = _unstack_heads(o, low)
            lse_ref[rows, :] = _unstack_heads(jnp.broadcast_to(m + jnp.log(z), o.shape), low)
        return carry

    lax.fori_loop(0, steps // blocks_per_step, body, 0)


def _dil_kernel(*refs, blocks_per_step, merge_rows):
    ng = N_DIL_GROUPS
    qkv_refs, bias_refs = refs[:3 * ng], refs[3 * ng:4 * ng]
    ob_ref, o_scr, lse_scr = refs[4 * ng:]
    for g in range(ng):
        q_ref, k_ref, v_ref = qkv_refs[3 * g:3 * g + 3]
        _dil_group(q_ref, k_ref, v_ref, bias_refs[g], o_scr.at[g], lse_scr.at[g], blocks_per_step)

    def merge(c, carry):
        rows = pl.ds(pl.multiple_of(c * merge_rows, merge_rows), merge_rows)
        lses = [lse_scr[g, rows, :] for g in range(ng)]
        top = functools.reduce(jnp.maximum, lses)
        es = [jnp.exp(l - top) for l in lses]
        num = functools.reduce(lambda a, b: a + b, [e * o_scr[g, rows, :] for g, e in enumerate(es)])
        ob_ref[rows, :] = (num / functools.reduce(lambda a, b: a + b, es)).astype(ob_ref.dtype)
        return carry

    lax.fori_loop(0, ob_ref.shape[0] // merge_rows, merge, 0)


def _dilated_attention(group_inputs, batch, seq, blocks_per_step=16, merge_rows=256):
    pairs = DIL_OUT_WIDTH // LANES
    operands, specs = [], []
    for qkv4, offsets in group_inputs:
        _, dilation, length, _ = qkv4.shape
        for off in offsets:
            operands.append(qkv4)
            specs.append(pl.BlockSpec((None, dilation, length, LANES),
                                      lambda p, b, off=off: (b, 0, 0, off + p)))
    for group in range(N_DIL_GROUPS):
        bias = jnp.asarray(_dil_bias_table(group))
        operands.append(bias)
        specs.append(pl.BlockSpec((1,) + bias.shape[1:], lambda p, b: (p, 0, 0, 0)))
    return pl.pallas_call(
        functools.partial(_dil_kernel, blocks_per_step=blocks_per_step, merge_rows=merge_rows),
        grid=(pairs, batch),
        in_specs=specs,
        out_specs=pl.BlockSpec((seq, LANES), lambda p, b: (b, p)),
        out_shape=jax.ShapeDtypeStruct((batch * seq, DIL_OUT_WIDTH), BF16),
        scratch_shapes=[pltpu.VMEM((N_DIL_GROUPS, seq, LANES), F32), pltpu.VMEM((N_DIL_GROUPS, seq, LANES), F32)],
        compiler_params=_params(("arbitrary", "arbitrary")),
        name="dil_attn",
    )(*operands)


def _outproj_kernel(x_ref, oa_ref, ob_ref, mod_ref, g_ref,
                    wg_ref, bg_ref, wpa_ref, wpb_ref, wo_ref, g2_ref, wr_ref, eb_ref, tri_ref,
                    out_ref, hp_ref, eidx_ref, rank_ref, w_ref, cnt_ref):
    d = x_ref.shape[1]
    x = x_ref[...]
    h = _modulated_norm(x, g_ref[...], mod_ref[0, SC1:SC1 + 1, :], mod_ref[0, SH1:SH1 + 1, :]).astype(BF16)

    ya = jnp.dot(oa_ref[...], wpa_ref[...], preferred_element_type=F32)
    yb = jnp.dot(ob_ref[...], wpb_ref[...], preferred_element_type=F32)
    ga = jax.nn.sigmoid(jnp.dot(h, wg_ref[:, :d], preferred_element_type=F32) + bg_ref[:, :d])
    mix = ga * ya
    gb = jax.nn.sigmoid(jnp.dot(h, wg_ref[:, d:], preferred_element_type=F32) + bg_ref[:, d:])
    mix = mix + gb * yb
    y = jnp.dot(mix.astype(BF16), wo_ref[...], preferred_element_type=F32)
    out = x + mod_ref[0, GT1:GT1 + 1, :] * y
    out_ref[...] = out
    _route_block(out, mod_ref, g2_ref, wr_ref, eb_ref, tri_ref, hp_ref, eidx_ref, rank_ref, w_ref, cnt_ref)


def _outproj(x2, oa, ob, mod, g, w_gate, b_gate, w_pa, w_pb, w_o, g_ffn, wr_split, e_bias, seq, tm=1024):
    n, d = x2.shape
    assert n % tm == 0 and seq % tm == 0 and tm % SC_CHUNK == 0
    blocks_per_batch = seq // tm
    tri = jnp.asarray(np.triu(np.ones((tm, tm), np.float32)), BF16)
    row = lambda c: pl.BlockSpec((tm, c), lambda i: (i, 0))
    chunked = pl.BlockSpec((tm // SC_CHUNK, TOP_K, SC_CHUNK), lambda i: (i, 0, 0))
    full = lambda a: pl.BlockSpec(a.shape, lambda i: (0,) * a.ndim)
    return pl.pallas_call(
        _outproj_kernel,
        grid=(n // tm,),
        in_specs=[row(d), row(NA_WIDTH), row(DIL_OUT_WIDTH)] + [
            pl.BlockSpec((1, 6, d), lambda i: (i // blocks_per_batch, 0, 0)),
            full(g), full(w_gate), full(b_gate), full(w_pa), full(w_pb), full(w_o),
            full(g_ffn), full(wr_split), full(e_bias), full(tri),
        ],
        out_specs=[row(d), row(d // 2), chunked, chunked, row(LANES),
                   pl.BlockSpec((N_EXPERTS, LANES), lambda i: (0, 0))],
        out_shape=[
            jax.ShapeDtypeStruct((n, d), F32),
            jax.ShapeDtypeStruct((n, d // 2), jnp.int32),
            jax.ShapeDtypeStruct((n // SC_CHUNK, TOP_K, SC_CHUNK), jnp.int32),
            jax.ShapeDtypeStruct((n // SC_CHUNK, TOP_K, SC_CHUNK), jnp.int32),
            jax.ShapeDtypeStruct((n, LANES), F32),
            jax.ShapeDtypeStruct((N_EXPERTS, LANES), F32),
        ],
        compiler_params=_params(("arbitrary",)),
        name="out_proj",
    )(x2, oa, ob, mod, g, w_gate, b_gate, w_pa, w_pb, w_o, g_ffn, wr_split, e_bias, tri)


def _first_index_of_max(cur, idx, size):
    m = jnp.max(cur, axis=0, keepdims=True)
    first = jnp.min(jnp.where(cur == m, idx, size), axis=0, keepdims=True)
    return m, first


def _route_transposed(logits_t, e_bias):
    tokens = logits_t.shape[1]
    per_group = N_EXPERTS // N_EXPERT_GROUPS
    scores = jax.nn.sigmoid(logits_t)
    biased = scores + e_bias
    midx = lax.broadcasted_iota(jnp.int32, (per_group, tokens), 0)
    grp_scores = []
    for g in range(N_EXPERT_GROUPS):
        vals = biased[g * per_group:(g + 1) * per_group, :]
        m1, first = _first_index_of_max(vals, midx, per_group)
        m2 = jnp.max(jnp.where(midx == first, -jnp.inf, vals), axis=0, keepdims=True)
        grp_scores.append(m1 + m2)
    cur = jnp.concatenate(grp_scores, axis=0)
    gidx = lax.broadcasted_iota(jnp.int32, (N_EXPERT_GROUPS, tokens), 0)
    grp_sel = jnp.zeros((N_EXPERT_GROUPS, tokens), jnp.bool_)
    for _ in range(TOP_GROUPS):
        _, first = _first_index_of_max(cur, gidx, N_EXPERT_GROUPS)
        pick = gidx == first
        grp_sel = jnp.logical_or(grp_sel, pick)
        cur = jnp.where(pick, -jnp.inf, cur)
    rows = []
    for g in range(N_EXPERT_GROUPS):
        vals = biased[g * per_group:(g + 1) * per_group, :]
        rows.append(jnp.where(grp_sel[g:g + 1, :], vals, NEG_INF))
    cur = jnp.concatenate(rows, axis=0)
    eidx = lax.broadcasted_iota(jnp.int32, (N_EXPERTS, tokens), 0)
    firsts, picks, weights = [], [], []
    for _ in range(TOP_K):
        _, first = _first_index_of_max(cur, eidx, N_EXPERTS)
        pick = eidx == first
        firsts.append(first)
        picks.append(pick)
        weights.append(jnp.sum(jnp.where(pick, scores, 0.0), axis=0, keepdims=True))
        cur = jnp.where(pick, -jnp.inf, cur)
    total = functools.reduce(lambda a, b: a + b, weights)
    return firsts, picks, [w / total * ROUTED_SCALE for w in weights]


def _pack_halves(a):
    half = a.shape[1] // 2
    bits = lax.bitcast_convert_type(a.astype(BF16).astype(F32), jnp.int32)
    return lax.shift_right_logical(bits[:, :half], BF16_BITS) | bits[:, half:]


def _unpack_halves(w):
    low = lax.bitcast_convert_type(lax.shift_left(w, BF16_BITS), F32)
    high = lax.bitcast_convert_type(w & jnp.int32(-(1 << BF16_BITS)), F32)
    return low, high


def _route_block(x, mod_ref, g_ref, wr_ref, eb_ref, tri_ref, hp_ref, eidx_ref, rank_ref, w_ref, cnt_ref):
    tm = x.shape[0]

    @pl.when(pl.program_id(0) == 0)
    def _():
        cnt_ref[...] = jnp.zeros_like(cnt_ref)

    h = _modulated_norm(x, g_ref[...], mod_ref[0, SC2:SC2 + 1, :], mod_ref[0, SH2:SH2 + 1, :])
    hp_ref[...] = _pack_halves(h)
    h_hi = h.astype(BF16)
    h_lo = (h - h_hi.astype(F32)).astype(BF16)
    nt = (((1,), (1,)), ((), ()))
    logits_t = (lax.dot_general(wr_ref[0], h_hi, nt, preferred_element_type=F32)
                + lax.dot_general(wr_ref[0], h_lo, nt, preferred_element_type=F32)
                + lax.dot_general(wr_ref[1], h_hi, nt, preferred_element_type=F32))
    firsts, picks, weights = _route_transposed(logits_t, eb_ref[...])
    sel = functools.reduce(jnp.logical_or, picks)
    sel_f = jnp.where(sel, 1.0, 0.0)
    incl = jnp.dot(sel_f.astype(BF16), tri_ref[...], preferred_element_type=F32)
    before = cnt_ref[:, 0:1] + incl - sel_f
    eidx = jnp.concatenate(firsts, axis=0)
    rank = jnp.concatenate(
        [jnp.sum(jnp.where(p, before, 0.0), axis=0, keepdims=True) for p in picks], axis=0).astype(jnp.int32)
    for c in range(tm // SC_CHUNK):
        eidx_ref[c] = eidx[:, c * SC_CHUNK:(c + 1) * SC_CHUNK]
        rank_ref[c] = rank[:, c * SC_CHUNK:(c + 1) * SC_CHUNK]
    pad = jnp.concatenate(weights + [jnp.zeros((LANES - TOP_K, tm), F32)], axis=0)
    w_ref[...] = pad.T
    cnt_ref[...] = cnt_ref[...] + incl[:, tm - 1:tm]


def _sc_worker_id():
    return lax.axis_index("subcore") * SC_CORES + lax.axis_index("core")


def _sc_scatter_rows(src, idx3, n_out):
    n, w = src.shape
    assert n % (2 * SC_CHUNK * SC_WORKERS) == 0
    per_worker = n // SC_CHUNK // SC_WORKERS
    mesh = plsc.VectorSubcoreMesh(core_axis_name="core", subcore_axis_name="subcore")

    @functools.partial(
        pl.kernel, mesh=mesh, out_type=jax.ShapeDtypeStruct((n_out, w), src.dtype), name="moe_dispatch",
        scratch_types=[pltpu.VMEM((2, TOP_K, SC_CHUNK), jnp.int32), pltpu.VMEM((2, SC_CHUNK, w), src.dtype),
                       pltpu.SemaphoreType.DMA((2,)), pltpu.SemaphoreType.DMA((2,)), pltpu.SemaphoreType.DMA])
    def scatter(src_hbm, idx_hbm, out_hbm, idx_v, rows_v, idx_sem, row_sem, out_sem):
        first = _sc_worker_id() * per_worker

        def loads(chunk, slot):
            return (pltpu.make_async_copy(idx_hbm.at[chunk], idx_v.at[slot], idx_sem.at[slot]),
                    pltpu.make_async_copy(src_hbm.at[pl.ds(chunk * SC_CHUNK, SC_CHUNK)], rows_v.at[slot],
                                          row_sem.at[slot]))

        for cp in loads(first, 0):
            cp.start()

        @pl.loop(0, per_worker, step=2)
        def _(i):
            for slot in range(2):
                chunk = first + i + slot
                for cp in loads(chunk, slot):
                    cp.wait()

                @pl.when(i + slot + 1 < per_worker)
                def _():
                    for cp in loads(chunk + 1, 1 - slot):
                        cp.start()

                copies = [pltpu.make_async_copy(rows_v.at[slot], out_hbm.at[idx_v.at[slot, k]], out_sem)
                          for k in range(TOP_K)]
                for cp in copies:
                    cp.start()
                for cp in copies:
                    cp.wait()

    return scatter(src, idx3)


def _sc_gather_rows(src, idx3):
    _, w = src.shape
    chunks = idx3.shape[0]
    assert chunks % SC_WORKERS == 0
    per_worker = chunks // SC_WORKERS
    mesh = plsc.VectorSubcoreMesh(core_axis_name="core", subcore_axis_name="subcore")

    @functools.partial(
        pl.kernel, mesh=mesh, out_type=jax.ShapeDtypeStruct((TOP_K, chunks * SC_CHUNK, w), src.dtype),
        name="moe_collect",
        scratch_types=[pltpu.VMEM((TOP_K, SC_CHUNK), jnp.int32), pltpu.VMEM((2, SC_CHUNK, w), src.dtype),
                       pltpu.SemaphoreType.DMA((2,)), pltpu.SemaphoreType.DMA((2,))])
    def gather(src_hbm, idx_hbm, out_hbm, idx_v, rows_v, in_sem, out_sem):
        first = _sc_worker_id() * per_worker

        @pl.loop(0, per_worker)
        def _(i):
            chunk = first + i
            pltpu.sync_copy(idx_hbm.at[chunk], idx_v)
            reads = [pltpu.make_async_copy(src_hbm.at[idx_v.at[k]], rows_v.at[k % 2], in_sem.at[k % 2])
                     for k in range(TOP_K)]
            writes = [pltpu.make_async_copy(rows_v.at[k % 2], out_hbm.at[k, pl.ds(chunk * SC_CHUNK, SC_CHUNK)],
                                            out_sem.at[k % 2]) for k in range(TOP_K)]
            reads[0].start()
            for k in range(TOP_K):
                if k + 1 < TOP_K:
                    if k >= 1:
                        writes[k - 1].wait()
                    reads[k + 1].start()
                reads[k].wait()
                writes[k].start()
            writes[TOP_K - 2].wait()
            writes[TOP_K - 1].wait()

    return gather(src, idx3)


def _swiglu(x, w_gate_up, w_down):
    gu = jnp.dot(x, w_gate_up, preferred_element_type=F32)
    gate = gu[:, :EXPERT_DIM]
    act = (gate * jax.nn.sigmoid(gate)) * gu[:, EXPERT_DIM:]
    return jnp.dot(act.astype(BF16), w_down, preferred_element_type=F32)


def _expert_ffn_kernel(held_ref, use_ref, nv_ref, xs_ref, *refs, tb):
    slots = FFN_BLOCKS_PER_STEP
    w_refs, ys_ref, wgu_bf, wd_bf = refs[:3 * slots], refs[3 * slots], refs[3 * slots + 1], refs[3 * slots + 2]
    i = pl.program_id(0)
    sub = tb // FFN_SUB_BLOCKS
    row = lax.broadcasted_iota(jnp.int32, (sub, xs_ref.shape[1]), 0)

    for slot in range(slots):
        blk = i * slots + slot
        wg_ref, wu_ref, wd_ref = w_refs[3 * slot:3 * slot + 3]

        @pl.when(jnp.logical_or(i == 0, held_ref[blk] != held_ref[jnp.maximum(blk - slots, 0)]))
        def _():
            wgu_bf[slot, :, :EXPERT_DIM] = wg_ref[0, 0].astype(BF16)
            wgu_bf[slot, :, EXPERT_DIM:] = wu_ref[0, 0].astype(BF16)
            wd_bf[slot] = wd_ref[0, 0].astype(BF16)

    @pl.when(nv_ref[i * slots] > 0)
    def _():
        spans, xs = [], []
        for slot in range(slots):
            nvalid = nv_ref[i * slots + slot]
            source = use_ref[i * slots + slot]
            for s in range(FFN_SUB_BLOCKS):
                span = pl.ds(slot * tb + s * sub, sub)
                packed = jnp.where(row < nvalid - s * sub, xs_ref[span, :], 0)
                low, high = _unpack_halves(packed)
                spans.append((source, span))
                xs.append(jnp.concatenate([low, high], axis=1).astype(BF16))
        gus = [jnp.dot(x, wgu_bf[source], preferred_element_type=F32) for (source, _), x in zip(spans, xs)]
        acts = [((gu[:, :EXPERT_DIM] * jax.nn.sigmoid(gu[:, :EXPERT_DIM])) * gu[:, EXPERT_DIM:]).astype(BF16)
                for gu in gus]
        ys = [jnp.dot(a, wd_bf[source], preferred_element_type=F32) for (source, _), a in zip(spans, acts)]
        for (_, span), y in zip(spans, ys):
            ys_ref[span, :] = _pack_halves(y)

    @pl.when(nv_ref[i * slots] == 0)
    def _():
        ys_ref[...] = jnp.zeros_like(ys_ref)


def _expert_ffn(xs, block_expert, block_valid, w_gate, w_up, w_down, layer, tb):
    p, half = xs.shape
    d = 2 * half
    slots = FFN_BLOCKS_PER_STEP
    n_blocks = block_expert.shape[0]
    run = jnp.cumsum(jnp.concatenate([jnp.zeros((1,), jnp.int32),
                                      (block_expert[1:] != block_expert[:-1]).astype(jnp.int32)]))
    blocks = jnp.arange(n_blocks, dtype=jnp.int32)
    first_block = jnp.minimum(jnp.sum((run[None, :] < blocks[:, None]).astype(jnp.int32), axis=1), n_blocks - 1)
    run_expert = block_expert[first_block]
    newest = run.reshape(-1, slots)[:, slots - 1]
    held = []
    for slot in range(slots):
        r = newest - (newest - slot) % slots
        held.append(run_expert[jnp.where(r < 0, slot, r)])
    held = jnp.stack(held, axis=1).reshape(-1)
    use = run % slots
    weight_specs = []
    for slot in range(slots):
        index = lambda i, held, use, nv, slot=slot: (layer, held[i * slots + slot], 0, 0)
        weight_specs += [pl.BlockSpec((1, 1, d, EXPERT_DIM), index), pl.BlockSpec((1, 1, d, EXPERT_DIM), index),
                         pl.BlockSpec((1, 1, EXPERT_DIM, d), index)]
    grid_spec = pltpu.PrefetchScalarGridSpec(
        num_scalar_prefetch=3,
        grid=(p // (tb * slots),),
        in_specs=[pl.BlockSpec((tb * slots, half), lambda i, held, use, nv: (i, 0))] + weight_specs,
        out_specs=pl.BlockSpec((tb * slots, half), lambda i, held, use, nv: (i, 0)),
        scratch_shapes=[pltpu.VMEM((slots, d, 2 * EXPERT_DIM), BF16), pltpu.VMEM((slots, EXPERT_DIM, d), BF16)],
    )
    return pl.pallas_call(
        functools.partial(_expert_ffn_kernel, tb=tb),
        grid_spec=grid_spec,
        out_shape=jax.ShapeDtypeStruct((p, half), jnp.int32),
        compiler_params=_params(("arbitrary",)),
        name="moe_expert_ffn",
    )(held, use, block_valid, xs, *([w_gate, w_up, w_down] * slots))


def _combine_kernel(x_ref, hp_ref, yg_ref, w_ref, mod_ref, wgu_ref, wd_ref, gf_ref, o_ref, *, final_norm):
    low, high = _unpack_halves(hp_ref[...])
    h = jnp.concatenate([low, high], axis=1).astype(BF16)
    shared = _swiglu(h, wgu_ref[...], wd_ref[...])
    half = hp_ref.shape[1]
    acc_low, acc_high = shared[:, :half], shared[:, half:]
    w = w_ref[...]
    for k in range(TOP_K):
        low, high = _unpack_halves(yg_ref[k])
        wk = w[:, k:k + 1]
        acc_low = acc_low + wk * low
        acc_high = acc_high + wk * high
    y = jnp.concatenate([acc_low, acc_high], axis=1)
    out = x_ref[...] + mod_ref[0, GT2:GT2 + 1, :] * y
    if final_norm:
        out = out * lax.rsqrt(jnp.mean(out * out, axis=-1, keepdims=True) + EPS) * gf_ref[...]
    o_ref[...] = out


def _combine(x2, hp, yg, w, mod, ws_gate_up, ws_down, g_final, seq, final_norm, tm=512):
    n, d = x2.shape
    blocks_per_batch = seq // tm
    const = lambda a: pl.BlockSpec(a.shape, lambda i: (0,) * a.ndim)
    return pl.pallas_call(
        functools.partial(_combine_kernel, final_norm=final_norm),
        grid=(n // tm,),
        in_specs=[
            pl.BlockSpec((tm, d), lambda i: (i, 0)),
            pl.BlockSpec((tm, d // 2), lambda i: (i, 0)),
            pl.BlockSpec((TOP_K, tm, d // 2), lambda i: (0, i, 0)),
            pl.BlockSpec((tm, LANES), lambda i: (i, 0)),
            pl.BlockSpec((1, 6, d), lambda i: (i // blocks_per_batch, 0, 0)),
            const(ws_gate_up), const(ws_down), const(g_final),
        ],
        out_specs=pl.BlockSpec((tm, d), lambda i: (i, 0)),
        out_shape=jax.ShapeDtypeStruct((n, d), F32),
        compiler_params=_params(("arbitrary",)),
        name="moe_combine",
    )(x2, hp, yg, w, mod, ws_gate_up, ws_down, g_final)


def _token_mixer(x2, mod, g_mix, w_in, b_gate, rpb, w_pa, w_pb, w_o, g_ffn, w_router, e_bias, batch, seq):
    d = x2.shape[1]
    dil0 = 3 * NA_WIDTH
    group_cols = [[dil0 + part * DIL_WIDTH + grp * DIL_OUT_WIDTH for part in range(3)]
                  for grp in range(N_DIL_GROUPS)]
    order = sorted(range(N_DIL_GROUPS), key=lambda grp: DIL_GROUPS[grp][1] > 1)
    w_qkv = jnp.concatenate(
        [w_in[:, :dil0]] + [w_in[:, c:c + DIL_OUT_WIDTH] for grp in order for c in group_cols[grp]],
        axis=1).astype(BF16)
    tok, *residue = _inproj(x2, mod, g_mix.reshape(1, d), w_qkv, batch, seq)
    o_a = _neighbourhood_attention(tok, _na_bias_table(rpb), batch, seq)

    qkv_offsets = tuple(part * DIL_OUT_WIDTH // LANES for part in range(3))
    group_inputs = [None] * N_DIL_GROUPS
    residue = iter(residue)
    tok_offset = dil0 // LANES
    for grp in order:
        if DIL_GROUPS[grp][1] == 1:
            group_inputs[grp] = (tok.reshape(batch, 1, seq, tok.shape[1]),
                                 tuple(tok_offset + o for o in qkv_offsets))
            tok_offset += 3 * DIL_OUT_WIDTH // LANES
        else:
            group_inputs[grp] = (next(residue), qkv_offsets)
    o_b = _dilated_attention(group_inputs, batch, seq)

    wr_t = w_router.T
    wr_hi = wr_t.astype(BF16)
    wr_split = jnp.stack([wr_hi, (wr_t - wr_hi.astype(F32)).astype(BF16)])
    return _outproj(x2, o_a, o_b, mod, g_mix.reshape(1, d),
                    w_in[:, QKV_COLS:].astype(BF16), b_gate.reshape(1, -1),
                    w_pa.astype(BF16), w_pb.astype(BF16), w_o.astype(BF16),
                    g_ffn.reshape(1, d), wr_split, e_bias.reshape(-1, 1), seq)


def _dispatch_plan(eidx3, rank3, counts, tb):
    n = eidx3.shape[0] * SC_CHUNK
    n_blocks = -(-(n * TOP_K + N_EXPERTS * (tb - 1)) // tb)
    n_blocks = -(-n_blocks // FFN_BLOCKS_PER_STEP) * FFN_BLOCKS_PER_STEP
    padded = (counts + tb - 1) // tb * tb
    seg_end = jnp.cumsum(padded)
    seg_start = seg_end - padded
    experts = jnp.arange(N_EXPERTS, dtype=jnp.int32)

    def lookup(table, idx):
        sel = idx[None] == experts.reshape((N_EXPERTS,) + (1,) * idx.ndim)
        return jnp.sum(jnp.where(sel, table.reshape((N_EXPERTS,) + (1,) * idx.ndim), 0), axis=0)

    idx3 = lookup(seg_start, eidx3) + rank3
    block_start = jnp.arange(n_blocks, dtype=jnp.int32) * tb
    block_expert = jnp.sum((seg_end[:, None] <= block_start[None, :]).astype(jnp.int32), axis=0)
    block_expert = jnp.minimum(block_expert, N_EXPERTS - 1)
    block_valid = jnp.clip(lookup(counts, block_expert) - (block_start - lookup(seg_start, block_expert)), 0, tb)
    return idx3, block_expert, block_valid.astype(jnp.int32), n_blocks


def _moe_layer(x2, routed, mod, we_gate, we_up, we_down, layer, ws_gate, ws_up, ws_down, g_final, seq, final_norm,
               tb=EXPERT_ROW_BLOCK):
    d = x2.shape[1]
    hp, eidx3, rank3, w, cnt = routed
    counts = cnt[:, 0].astype(jnp.int32)
    idx3, block_expert, block_valid, n_blocks = _dispatch_plan(eidx3, rank3, counts, tb)
    xs = _sc_scatter_rows(hp, idx3, n_blocks * tb)
    ys = _expert_ffn(xs, block_expert, block_valid, we_gate, we_up, we_down, layer, tb)
    yg = _sc_gather_rows(ys, idx3)
    return _combine(x2, hp, yg, w, mod, jnp.concatenate([ws_gate, ws_up], axis=-1).astype(BF16),
                    ws_down.astype(BF16), g_final.reshape(1, d), seq, final_norm)


def kernel(x, c, w_ada, b_ada, g_mix, w_in, b_gate, rpb, w_pa, w_pb, w_o, g_ffn, w_router, e_bias,
           we_gate, we_up, we_down, ws_gate, ws_up, ws_down, g_final):
    batch, seq, d = x.shape
    depth = w_ada.shape[0]
    mods = _ada(c, w_ada, b_ada).reshape(depth, batch, 6, d)
    x2 = x.reshape(batch * seq, d)
    for l in range(depth):
        x2, *routed = _token_mixer(x2, mods[l], g_mix[l], w_in[l], b_gate[l], rpb[l], w_pa[l], w_pb[l], w_o[l],
                                   g_ffn[l], w_router[l], e_bias[l], batch, seq)
        x2 = _moe_layer(x2, routed, mods[l], we_gate, we_up, we_down, l,
                        ws_gate[l], ws_up[l], ws_down[l], g_final, seq, final_norm=(l == depth - 1))
    return x2.reshape(batch, seq, d)
```

```python
import functools

import numpy as np
import jax
import jax.numpy as jnp
from jax import lax
from jax.experimental import pallas as pl
from jax.experimental.pallas import tpu as pltpu
from jax.experimental.pallas import tpu_sc as plsc

HEAD_DIM = 64
GRID_W = 64
NA_HEADS = 8
NA_WIN_ROWS = 8
NA_WIN_COLS = 16
DIL_GROUPS = ((128, 1), (512, 4), (2048, 16))
DIL_HEADS_PER_GROUP = 4
N_DIL_GROUPS = len(DIL_GROUPS)
NA_WIDTH = NA_HEADS * HEAD_DIM
DIL_WIDTH = N_DIL_GROUPS * DIL_HEADS_PER_GROUP * HEAD_DIM
DIL_OUT_WIDTH = DIL_HEADS_PER_GROUP * HEAD_DIM
QKV_COLS = 3 * (NA_WIDTH + DIL_WIDTH)
N_EXPERTS = 64
TOP_K = 8
N_EXPERT_GROUPS = 8
TOP_GROUPS = 4
EXPERT_DIM = 256
ROUTED_SCALE = 2.5
ALIBI_MAX = 8.0
EPS = 1e-6
NEG_INF = -1e30

LANES = 128
HEADS_PER_LANE_TILE = LANES // HEAD_DIM
DIL_BLOCK = 64
VMEM_LIMIT_BYTES = 56 * 1024 * 1024

SC_CORES = 2
SC_SUBCORES = 16
SC_WORKERS = SC_CORES * SC_SUBCORES
SC_CHUNK = 64
EXPERT_ROW_BLOCK = 512
FFN_SUB_BLOCKS = 2
FFN_BLOCKS_PER_STEP = 4

F32 = jnp.float32
BF16 = jnp.bfloat16
BF16_BITS = 16

SH1, SC1, GT1, SH2, SC2, GT2 = range(6)


def _params(sem):
    return pltpu.CompilerParams(dimension_semantics=sem, vmem_limit_bytes=VMEM_LIMIT_BYTES)


def _modulated_norm(x, g, scale, shift):
    r = lax.rsqrt(jnp.mean(x * x, axis=-1, keepdims=True) + EPS)
    return (x * r) * (g * (1.0 + scale)) + shift


def _ada_kernel(c_ref, w_ref, b_ref, o_ref):
    c = c_ref[...]
    act = c * jax.nn.sigmoid(c)
    o_ref[0] = jnp.dot(act, w_ref[0], preferred_element_type=F32,
                       precision=lax.Precision.HIGHEST) + b_ref[0]


def _ada(c, w_ada, b_ada):
    depth, d, six_d = w_ada.shape
    b = c.shape[0]
    tn = d
    return pl.pallas_call(
        _ada_kernel,
        grid=(depth, six_d // tn),
        in_specs=[
            pl.BlockSpec((b, d), lambda l, j: (0, 0)),
            pl.BlockSpec((1, d, tn), lambda l, j: (l, 0, j)),
            pl.BlockSpec((1, 1, tn), lambda l, j: (l, 0, j)),
        ],
        out_specs=pl.BlockSpec((1, b, tn), lambda l, j: (l, 0, j)),
        out_shape=jax.ShapeDtypeStruct((depth, b, six_d), F32),
        compiler_params=_params(("arbitrary", "arbitrary")),
        name="ada_mod",
    )(c, w_ada, b_ada.reshape(depth, 1, six_d))


def _inproj_kernel(x_ref, mod_ref, g_ref, w_ref, tok_ref, *rest, dilations):
    res_refs, acc_refs = rest[:len(dilations)], rest[len(dilations):]
    tm = x_ref.shape[0]
    tok_cols = tok_ref.shape[1]
    tn = (w_ref.shape[1] - tok_cols) // len(dilations)
    h = _modulated_norm(x_ref[...], g_ref[...], mod_ref[0, SC1:SC1 + 1, :], mod_ref[0, SH1:SH1 + 1, :]).astype(BF16)

    for c0 in range(0, tok_cols, tn):
        tok_ref[:, c0:c0 + tn] = jnp.dot(h, w_ref[:, c0:c0 + tn], preferred_element_type=F32).astype(BF16)

    for g, (res_ref, acc_ref, dilation) in enumerate(zip(res_refs, acc_refs, dilations)):
        c0 = tok_cols + g * tn
        res = jnp.dot(h, w_ref[:, c0:c0 + tn], preferred_element_type=F32)
        for c in range(acc_ref.shape[0]):
            acc_ref[c] = res[:, c * LANES:(c + 1) * LANES]
        for r in range(dilation):
            for c in range(acc_ref.shape[0]):
                res_ref[0, r, :, c * LANES:(c + 1) * LANES] = (
                    acc_ref[c, pl.ds(r, tm // dilation, stride=dilation), :].astype(BF16))


def _inproj(x2, mod, g, w_qkv, batch, seq, tm=1024):
    n, d = x2.shape
    tn = 3 * DIL_OUT_WIDTH
    dilations = tuple(dil for _, dil in DIL_GROUPS if dil > 1)
    tok_cols = w_qkv.shape[1] - tn * len(dilations)
    blocks_per_batch = seq // tm
    res_specs = [pl.BlockSpec((1, dil, tm // dil, tn),
                              lambda i: (i // blocks_per_batch, 0, i % blocks_per_batch, 0)) for dil in dilations]
    res_shapes = [jax.ShapeDtypeStruct((batch, dil, seq // dil, tn), BF16) for dil in dilations]
    return pl.pallas_call(
        functools.partial(_inproj_kernel, dilations=dilations),
        grid=(n // tm,),
        in_specs=[
            pl.BlockSpec((tm, d), lambda i: (i, 0)),
            pl.BlockSpec((1, 6, d), lambda i: (i // blocks_per_batch, 0, 0)),
            pl.BlockSpec((1, d), lambda i: (0, 0)),
            pl.BlockSpec(w_qkv.shape, lambda i: (0, 0)),
        ],
        out_specs=[pl.BlockSpec((tm, tok_cols), lambda i: (i, 0))] + res_specs,
        out_shape=[jax.ShapeDtypeStruct((n, tok_cols), BF16)] + res_shapes,
        scratch_shapes=[pltpu.VMEM((tn // LANES, tm, LANES), F32) for _ in dilations],
        compiler_params=_params(("arbitrary",)),
        name="in_proj",
    )(x2, mod, g, w_qkv)


def _na_bias_table(rpb):
    heads = rpb.shape[0]
    cols = np.arange(GRID_W)
    col_start = np.clip(cols - NA_WIN_COLS // 2, 0, GRID_W - NA_WIN_COLS)
    col_mask = (cols[None, :] >= col_start[:, None]) & (cols[None, :] < col_start[:, None] + NA_WIN_COLS)
    edge = GRID_W - NA_WIN_COLS
    ext = jnp.concatenate([jnp.repeat(rpb[..., :1], edge, axis=-1), rpb, jnp.repeat(rpb[..., -1:], edge, axis=-1)],
                          axis=-1).astype(F32)
    rpb_cols = jnp.stack([ext[..., GRID_W - 1 - cq:2 * GRID_W - 1 - cq] for cq in range(GRID_W)],
                         axis=-2)
    t = jnp.stack([rpb_cols[:, NA_WIN_ROWS - 1 - off:2 * NA_WIN_ROWS - 1 - off] for off in range(NA_WIN_ROWS)],
                  axis=1)
    t = t.transpose(0, 1, 3, 2, 4)
    t = jnp.where(col_mask[:, None, :], t, NEG_INF)
    t = t.reshape(heads // HEADS_PER_LANE_TILE, HEADS_PER_LANE_TILE, NA_WIN_ROWS, GRID_W, NA_WIN_ROWS * GRID_W)
    return t.transpose(0, 2, 1, 3, 4).reshape(heads // HEADS_PER_LANE_TILE, NA_WIN_ROWS,
                                              HEADS_PER_LANE_TILE * GRID_W, NA_WIN_ROWS * GRID_W)


def _stack_heads(q, low):
    scaled = q * (HEAD_DIM ** -0.5)
    zero = jnp.zeros_like(scaled)
    return jnp.concatenate([jnp.where(low, scaled, zero), jnp.where(low, zero, scaled)], axis=0)


def _stacked_attention(items):
    scores = [lax.dot_general(q2, kw, (((1,), (1,)), ((), ())), preferred_element_type=F32) + bias
              for q2, kw, _, bias in items]
    probs = []
    for s in scores:
        m = jnp.max(s, axis=-1, keepdims=True)
        p = jnp.exp(s - m)
        probs.append((p.astype(BF16), m, jnp.sum(p, axis=-1, keepdims=True)))
    return [(jnp.dot(p, vw, preferred_element_type=F32) / z, m, z)
            for (p, m, z), (_, _, vw, _) in zip(probs, items)]


def _unstack_heads(a, low):
    half = a.shape[0] // HEADS_PER_LANE_TILE
    return jnp.where(low, a[:half], a[half:])


def _na_kernel(q_ref, k_ref, v_ref, bias_ref, o_ref, *, rows, rows_per_step):
    kr = NA_WIN_ROWS
    low = lax.broadcasted_iota(jnp.int32, (GRID_W, LANES), 1) < HEAD_DIM

    def body(i, carry):
        items, qrows = [], []
        for u in range(rows_per_step):
            r = i * rows_per_step + u
            rs = jnp.clip(r - kr // 2, 0, rows - kr)
            qrows.append(pl.ds(pl.multiple_of(r * GRID_W, GRID_W), GRID_W))
            wrows = pl.ds(pl.multiple_of(rs * GRID_W, GRID_W), kr * GRID_W)
            items.append((_stack_heads(q_ref[qrows[-1], :], low), k_ref[wrows, :], v_ref[wrows, :],
                          bias_ref[0, r - rs]))
        for rows_u, (o, _, _) in zip(qrows, _stacked_attention(items)):
            o_ref[rows_u, :] = _unstack_heads(o, low).astype(o_ref.dtype)
        return carry

    lax.fori_loop(0, rows // rows_per_step, body, 0)


def _neighbourhood_attention(qkv, bias, batch, seq, rows_per_step=16):
    n = qkv.shape[0]
    rows = seq // GRID_W
    pairs = NA_WIDTH // LANES
    return pl.pallas_call(
        functools.partial(_na_kernel, rows=rows, rows_per_step=rows_per_step),
        grid=(pairs, batch),
        in_specs=[
            pl.BlockSpec((seq, LANES), lambda p, b: (b, p)),
            pl.BlockSpec((seq, LANES), lambda p, b: (b, pairs + p)),
            pl.BlockSpec((seq, LANES), lambda p, b: (b, 2 * pairs + p)),
            pl.BlockSpec((1, NA_WIN_ROWS, HEADS_PER_LANE_TILE * GRID_W, NA_WIN_ROWS * GRID_W),
                         lambda p, b: (p, 0, 0, 0)),
        ],
        out_specs=pl.BlockSpec((seq, LANES), lambda p, b: (b, p)),
        out_shape=jax.ShapeDtypeStruct((n, NA_WIDTH), BF16),
        compiler_params=_params(("arbitrary", "arbitrary")),
        name="na_attn",
    )(qkv, qkv, qkv, bias)


def _alibi_slopes():
    n = N_DIL_GROUPS * DIL_HEADS_PER_GROUP
    s = np.exp2(-ALIBI_MAX * np.arange(1, n + 1, dtype=np.float64) / n).astype(np.float32)
    return s.reshape(N_DIL_GROUPS, DIL_HEADS_PER_GROUP)


def _dil_bias_table(group):
    blk = DIL_BLOCK
    dilation = DIL_GROUPS[group][1]
    slopes = _alibi_slopes()[group]
    qi = np.arange(blk)[:, None]
    kj = np.arange(3 * blk)[None, :]
    tables = []
    for shift in range(3):
        arel = np.abs(kj - qi - shift * blk)
        dist = (dilation * arel).astype(np.float32)
        per_head = [np.where(arel <= blk, -slopes[h] * dist, np.float32(NEG_INF)) for h in range(DIL_HEADS_PER_GROUP)]
        tables.append(np.stack(per_head))
    t = np.stack(tables, axis=1).astype(np.float32)
    pairs = DIL_HEADS_PER_GROUP // HEADS_PER_LANE_TILE
    t = t.reshape(pairs, HEADS_PER_LANE_TILE, 3, blk, 3 * blk).transpose(0, 2, 1, 3, 4)
    return t.reshape(pairs, 3, HEADS_PER_LANE_TILE * blk, 3 * blk)


def _dil_group(q_ref, k_ref, v_ref, bias_ref, o_ref, lse_ref, blocks_per_step):
    blk = DIL_BLOCK
    win = 3 * blk
    low = lax.broadcasted_iota(jnp.int32, (blk, LANES), 1) < HEAD_DIM
    dilation, length, _ = q_ref.shape
    nb = length // blk
    steps = dilation * nb

    def token_rows(sq, n):
        if dilation == 1:
            return pl.ds(pl.multiple_of(n * blk, blk), blk)
        return pl.ds(n * (blk * dilation) + sq, blk, stride=dilation)

    def body(i, carry):
        items, dst = [], []
        for u in range(blocks_per_step):
            t = i * blocks_per_step + u
            sq = t // nb
            n = t % nb
            wb = jnp.clip(n - 1, 0, nb - 3)
            qrows = pl.ds(pl.multiple_of(n * blk, blk), blk)
            wrows = pl.ds(pl.multiple_of(wb * blk, blk), win)
            dst.append(token_rows(sq, n))
            items.append((_stack_heads(q_ref[sq, qrows, :], low), k_ref[sq, wrows, :], v_ref[sq, wrows, :],
                          bias_ref[0, n - wb]))
        for rows, (o, m, z) in zip(dst, _stacked_attention(items)):
            o_ref[rows, :] = _unstack_heads(o, low)
            lse_ref[rows, :] = _unstack_heads(jnp.broadcast_to(m + jnp.log(z), o.shape), low)
        return carry

    lax.fori_loop(0, steps // blocks_per_step, body, 0)


def _dil_kernel(*refs, blocks_per_step, merge_rows):
    ng = N_DIL_GROUPS
    qkv_refs, bias_refs = refs[:3 * ng], refs[3 * ng:4 * ng]
    ob_ref, o_scr, lse_scr = refs[4 * ng:]
    for g in range(ng):
        q_ref, k_ref, v_ref = qkv_refs[3 * g:3 * g + 3]
        _dil_group(q_ref, k_ref, v_ref, bias_refs[g], o_scr.at[g], lse_scr.at[g], blocks_per_step)

    def merge(c, carry):
        rows = pl.ds(pl.multiple_of(c * merge_rows, merge_rows), merge_rows)
        lses = [lse_scr[g, rows, :] for g in range(ng)]
        top = functools.reduce(jnp.maximum, lses)
        es = [jnp.exp(l - top) for l in lses]
        num = functools.reduce(lambda a, b: a + b, [e * o_scr[g, rows, :] for g, e in enumerate(es)])
        ob_ref[rows, :] = (num / functools.reduce(lambda a, b: a + b, es)).astype(ob_ref.dtype)
        return carry

    lax.fori_loop(0, ob_ref.shape[0] // merge_rows, merge, 0)


def _dilated_attention(group_inputs, batch, seq, blocks_per_step=16, merge_rows=256):
    pairs = DIL_OUT_WIDTH // LANES
    operands, specs = [], []
    for qkv4, offsets in group_inputs:
        _, dilation, length, _ = qkv4.shape
        for off in offsets:
            operands.append(qkv4)
            specs.append(pl.BlockSpec((None, dilation, length, LANES),
                                      lambda p, b, off=off: (b, 0, 0, off + p)))
    for group in range(N_DIL_GROUPS):
        bias = jnp.asarray(_dil_bias_table(group))
        operands.append(bias)
        specs.append(pl.BlockSpec((1,) + bias.shape[1:], lambda p, b: (p, 0, 0, 0)))
    return pl.pallas_call(
        functools.partial(_dil_kernel, blocks_per_step=blocks_per_step, merge_rows=merge_rows),
        grid=(pairs, batch),
        in_specs=specs,
        out_specs=pl.BlockSpec((seq, LANES), lambda p, b: (b, p)),
        out_shape=jax.ShapeDtypeStruct((batch * seq, DIL_OUT_WIDTH), BF16),
        scratch_shapes=[pltpu.VMEM((N_DIL_GROUPS, seq, LANES), F32), pltpu.VMEM((N_DIL_GROUPS, seq, LANES), F32)],
        compiler_params=_params(("arbitrary", "arbitrary")),
        name="dil_attn",
    )(*operands)


def _outproj_kernel(x_ref, oa_ref, ob_ref, mod_ref, g_ref,
                    wg_ref, bg_ref, wpa_ref, wpb_ref, wo_ref, g2_ref, wr_ref, eb_ref, tri_ref,
                    out_ref, hp_ref, eidx_ref, rank_ref, w_ref, cnt_ref):
    d = x_ref.shape[1]
    x = x_ref[...]
    h = _modulated_norm(x, g_ref[...], mod_ref[0, SC1:SC1 + 1, :], mod_ref[0, SH1:SH1 + 1, :]).astype(BF16)

    ya = jnp.dot(oa_ref[...], wpa_ref[...], preferred_element_type=F32)
    yb = jnp.dot(ob_ref[...], wpb_ref[...], preferred_element_type=F32)
    ga = jax.nn.sigmoid(jnp.dot(h, wg_ref[:, :d], preferred_element_type=F32) + bg_ref[:, :d])
    mix = ga * ya
    gb = jax.nn.sigmoid(jnp.dot(h, wg_ref[:, d:], preferred_element_type=F32) + bg_ref[:, d:])
    mix = mix + gb * yb
    y = jnp.dot(mix.astype(BF16), wo_ref[...], preferred_element_type=F32)
    out = x + mod_ref[0, GT1:GT1 + 1, :] * y
    out_ref[...] = out
    _route_block(out, mod_ref, g2_ref, wr_ref, eb_ref, tri_ref, hp_ref, eidx_ref, rank_ref, w_ref, cnt_ref)


def _outproj(x2, oa, ob, mod, g, w_gate, b_gate, w_pa, w_pb, w_o, g_ffn, wr_split, e_bias, seq, tm=1024):
    n, d = x2.shape
    assert n % tm == 0 and seq % tm == 0 and tm % SC_CHUNK == 0
    blocks_per_batch = seq // tm
    tri = jnp.asarray(np.triu(np.ones((tm, tm), np.float32)), BF16)
    row = lambda c: pl.BlockSpec((tm, c), lambda i: (i, 0))
    chunked = pl.BlockSpec((tm // SC_CHUNK, TOP_K, SC_CHUNK), lambda i: (i, 0, 0))
    full = lambda a: pl.BlockSpec(a.shape, lambda i: (0,) * a.ndim)
    return pl.pallas_call(
        _outproj_kernel,
        grid=(n // tm,),
        in_specs=[row(d), row(NA_WIDTH), row(DIL_OUT_WIDTH)] + [
            pl.BlockSpec((1, 6, d), lambda i: (i // blocks_per_batch, 0, 0)),
            full(g), full(w_gate), full(b_gate), full(w_pa), full(w_pb), full(w_o),
            full(g_ffn), full(wr_split), full(e_bias), full(tri),
        ],
        out_specs=[row(d), row(d // 2), chunked, chunked, row(LANES),
                   pl.BlockSpec((N_EXPERTS, LANES), lambda i: (0, 0))],
        out_shape=[
            jax.ShapeDtypeStruct((n, d), F32),
            jax.ShapeDtypeStruct((n, d // 2), jnp.int32),
            jax.ShapeDtypeStruct((n // SC_CHUNK, TOP_K, SC_CHUNK), jnp.int32),
            jax.ShapeDtypeStruct((n // SC_CHUNK, TOP_K, SC_CHUNK), jnp.int32),
            jax.ShapeDtypeStruct((n, LANES), F32),
            jax.ShapeDtypeStruct((N_EXPERTS, LANES), F32),
        ],
        compiler_params=_params(("arbitrary",)),
        name="out_proj",
    )(x2, oa, ob, mod, g, w_gate, b_gate, w_pa, w_pb, w_o, g_ffn, wr_split, e_bias, tri)


def _first_index_of_max(cur, idx, size):
    m = jnp.max(cur, axis=0, keepdims=True)
    first = jnp.min(jnp.where(cur == m, idx, size), axis=0, keepdims=True)
    return m, first


def _route_transposed(logits_t, e_bias):
    tokens = logits_t.shape[1]
    per_group = N_EXPERTS // N_EXPERT_GROUPS
    scores = jax.nn.sigmoid(logits_t)
    biased = scores + e_bias
    midx = lax.broadcasted_iota(jnp.int32, (per_group, tokens), 0)
    grp_scores = []
    for g in range(N_EXPERT_GROUPS):
        vals = biased[g * per_group:(g + 1) * per_group, :]
        m1, first = _first_index_of_max(vals, midx, per_group)
        m2 = jnp.max(jnp.where(midx == first, -jnp.inf, vals), axis=0, keepdims=True)
        grp_scores.append(m1 + m2)
    cur = jnp.concatenate(grp_scores, axis=0)
    gidx = lax.broadcasted_iota(jnp.int32, (N_EXPERT_GROUPS, tokens), 0)
    grp_sel = jnp.zeros((N_EXPERT_GROUPS, tokens), jnp.bool_)
    for _ in range(TOP_GROUPS):
        _, first = _first_index_of_max(cur, gidx, N_EXPERT_GROUPS)
        pick = gidx == first
        grp_sel = jnp.logical_or(grp_sel, pick)
        cur = jnp.where(pick, -jnp.inf, cur)
    rows = []
    for g in range(N_EXPERT_GROUPS):
        vals = biased[g * per_group:(g + 1) * per_group, :]
        rows.append(jnp.where(grp_sel[g:g + 1, :], vals, NEG_INF))
    cur = jnp.concatenate(rows, axis=0)
    eidx = lax.broadcasted_iota(jnp.int32, (N_EXPERTS, tokens), 0)
    firsts, picks, weights = [], [], []
    for _ in range(TOP_K):
        _, first = _first_index_of_max(cur, eidx, N_EXPERTS)
        pick = eidx == first
        firsts.append(first)
        picks.append(pick)
        weights.append(jnp.sum(jnp.where(pick, scores, 0.0), axis=0, keepdims=True))
        cur = jnp.where(pick, -jnp.inf, cur)
    total = functools.reduce(lambda a, b: a + b, weights)
    return firsts, picks, [w / total * ROUTED_SCALE for w in weights]


def _pack_halves(a):
    half = a.shape[1] // 2
    bits = lax.bitcast_convert_type(a.astype(BF16).astype(F32), jnp.int32)
    return lax.shift_right_logical(bits[:, :half], BF16_BITS) | bits[:, half:]


def _unpack_halves(w):
    low = lax.bitcast_convert_type(lax.shift_left(w, BF16_BITS), F32)
    high = lax.bitcast_convert_type(w & jnp.int32(-(1 << BF16_BITS)), F32)
    return low, high


def _route_block(x, mod_ref, g_ref, wr_ref, eb_ref, tri_ref, hp_ref, eidx_ref, rank_ref, w_ref, cnt_ref):
    tm = x.shape[0]

    @pl.when(pl.program_id(0) == 0)
    def _():
        cnt_ref[...] = jnp.zeros_like(cnt_ref)

    h = _modulated_norm(x, g_ref[...], mod_ref[0, SC2:SC2 + 1, :], mod_ref[0, SH2:SH2 + 1, :])
    hp_ref[...] = _pack_halves(h)
    h_hi = h.astype(BF16)
    h_lo = (h - h_hi.astype(F32)).astype(BF16)
    nt = (((1,), (1,)), ((), ()))
    logits_t = (lax.dot_general(wr_ref[0], h_hi, nt, preferred_element_type=F32)
                + lax.dot_general(wr_ref[0], h_lo, nt, preferred_element_type=F32)
                + lax.dot_general(wr_ref[1], h_hi, nt, preferred_element_type=F32))
    firsts, picks, weights = _route_transposed(logits_t, eb_ref[...])
    sel = functools.reduce(jnp.logical_or, picks)
    sel_f = jnp.where(sel, 1.0, 0.0)
    incl = jnp.dot(sel_f.astype(BF16), tri_ref[...], preferred_element_type=F32)
    before = cnt_ref[:, 0:1] + incl - sel_f
    eidx = jnp.concatenate(firsts, axis=0)
    rank = jnp.concatenate(
        [jnp.sum(jnp.where(p, before, 0.0), axis=0, keepdims=True) for p in picks], axis=0).astype(jnp.int32)
    for c in range(tm // SC_CHUNK):
        eidx_ref[c] = eidx[:, c * SC_CHUNK:(c + 1) * SC_CHUNK]
        rank_ref[c] = rank[:, c * SC_CHUNK:(c + 1) * SC_CHUNK]
    pad = jnp.concatenate(weights + [jnp.zeros((LANES - TOP_K, tm), F32)], axis=0)
    w_ref[...] = pad.T
    cnt_ref[...] = cnt_ref[...] + incl[:, tm - 1:tm]


def _sc_worker_id():
    return lax.axis_index("subcore") * SC_CORES + lax.axis_index("core")


def _sc_scatter_rows(src, idx3, n_out):
    n, w = src.shape
    assert n % (2 * SC_CHUNK * SC_WORKERS) == 0
    per_worker = n // SC_CHUNK // SC_WORKERS
    mesh = plsc.VectorSubcoreMesh(core_axis_name="core", subcore_axis_name="subcore")

    @functools.partial(
        pl.kernel, mesh=mesh, out_type=jax.ShapeDtypeStruct((n_out, w), src.dtype), name="moe_dispatch",
        scratch_types=[pltpu.VMEM((2, TOP_K, SC_CHUNK), jnp.int32), pltpu.VMEM((2, SC_CHUNK, w), src.dtype),
                       pltpu.SemaphoreType.DMA((2,)), pltpu.SemaphoreType.DMA((2,)), pltpu.SemaphoreType.DMA])
    def scatter(src_hbm, idx_hbm, out_hbm, idx_v, rows_v, idx_sem, row_sem, out_sem):
        first = _sc_worker_id() * per_worker

        def loads(chunk, slot):
            return (pltpu.make_async_copy(idx_hbm.at[chunk], idx_v.at[slot], idx_sem.at[slot]),
                    pltpu.make_async_copy(src_hbm.at[pl.ds(chunk * SC_CHUNK, SC_CHUNK)], rows_v.at[slot],
                                          row_sem.at[slot]))

        for cp in loads(first, 0):
            cp.start()

        @pl.loop(0, per_worker, step=2)
        def _(i):
            for slot in range(2):
                chunk = first + i + slot
                for cp in loads(chunk, slot):
                    cp.wait()

                @pl.when(i + slot + 1 < per_worker)
                def _():
                    for cp in loads(chunk + 1, 1 - slot):
                        cp.start()

                copies = [pltpu.make_async_copy(rows_v.at[slot], out_hbm.at[idx_v.at[slot, k]], out_sem)
                          for k in range(TOP_K)]
                for cp in copies:
                    cp.start()
                for cp in copies:
                    cp.wait()

    return scatter(src, idx3)


def _sc_gather_rows(src, idx3):
    _, w = src.shape
    chunks = idx3.shape[0]
    assert chunks % SC_WORKERS == 0
    per_worker = chunks // SC_WORKERS
    mesh = plsc.VectorSubcoreMesh(core_axis_name="core", subcore_axis_name="subcore")

    @functools.partial(
        pl.kernel, mesh=mesh, out_type=jax.ShapeDtypeStruct((TOP_K, chunks * SC_CHUNK, w), src.dtype),
        name="moe_collect",
        scratch_types=[pltpu.VMEM((per_worker, TOP_K, SC_CHUNK), jnp.int32),
                       pltpu.VMEM((2, SC_CHUNK, w), src.dtype),
                       pltpu.SemaphoreType.DMA((2,)), pltpu.SemaphoreType.DMA((2,))])
    def gather(src_hbm, idx_hbm, out_hbm, idx_v, rows_v, in_sem, out_sem):
        first = _sc_worker_id() * per_worker
        pltpu.sync_copy(idx_hbm.at[pl.ds(first, per_worker)], idx_v)

        @pl.loop(0, per_worker)
        def _(i):
            chunk = first + i
            reads = [pltpu.make_async_copy(src_hbm.at[idx_v.at[i, k]], rows_v.at[k % 2], in_sem.at[k % 2])
                     for k in range(TOP_K)]
            writes = [pltpu.make_async_copy(rows_v.at[k % 2], out_hbm.at[k, pl.ds(chunk * SC_CHUNK, SC_CHUNK)],
                                            out_sem.at[k % 2]) for k in range(TOP_K)]
            reads[0].start()
            for k in range(TOP_K):
                if k + 1 < TOP_K:
                    if k >= 1:
                        writes[k - 1].wait()
                    reads[k + 1].start()
                reads[k].wait()
                writes[k].start()
            writes[TOP_K - 2].wait()
            writes[TOP_K - 1].wait()

    return gather(src, idx3)


def _swiglu(x, w_gate_up, w_down):
    gu = jnp.dot(x, w_gate_up, preferred_element_type=F32)
    gate = gu[:, :EXPERT_DIM]
    act = (gate * jax.nn.sigmoid(gate)) * gu[:, EXPERT_DIM:]
    return jnp.dot(act.astype(BF16), w_down, preferred_element_type=F32)


def _expert_ffn_kernel(held_ref, use_ref, nv_ref, xs_ref, *refs, tb):
    slots = FFN_BLOCKS_PER_STEP
    w_refs, ys_ref, wgu_bf, wd_bf = refs[:3 * slots], refs[3 * slots], refs[3 * slots + 1], refs[3 * slots + 2]
    i = pl.program_id(0)
    sub = tb // FFN_SUB_BLOCKS
    row = lax.broadcasted_iota(jnp.int32, (sub, xs_ref.shape[1]), 0)

    for slot in range(slots):
        blk = i * slots + slot
        wg_ref, wu_ref, wd_ref = w_refs[3 * slot:3 * slot + 3]

        @pl.when(jnp.logical_or(i == 0, held_ref[blk] != held_ref[jnp.maximum(blk - slots, 0)]))
        def _():
            wgu_bf[slot, :, :EXPERT_DIM] = wg_ref[0, 0].astype(BF16)
            wgu_bf[slot, :, EXPERT_DIM:] = wu_ref[0, 0].astype(BF16)
            wd_bf[slot] = wd_ref[0, 0].astype(BF16)

    @pl.when(nv_ref[i * slots] > 0)
    def _():
        spans, xs = [], []
        for slot in range(slots):
            nvalid = nv_ref[i * slots + slot]
            source = use_ref[i * slots + slot]
            for s in range(FFN_SUB_BLOCKS):
                span = pl.ds(slot * tb + s * sub, sub)
                packed = jnp.where(row < nvalid - s * sub, xs_ref[span, :], 0)
                low, high = _unpack_halves(packed)
                spans.append((source, span))
                xs.append(jnp.concatenate([low, high], axis=1).astype(BF16))
        gus = [jnp.dot(x, wgu_bf[source], preferred_element_type=F32) for (source, _), x in zip(spans, xs)]
        acts = [((gu[:, :EXPERT_DIM] * jax.nn.sigmoid(gu[:, :EXPERT_DIM])) * gu[:, EXPERT_DIM:]).astype(BF16)
                for gu in gus]
        ys = [jnp.dot(a, wd_bf[source], preferred_element_type=F32) for (source, _), a in zip(spans, acts)]
        for (_, span), y in zip(spans, ys):
            ys_ref[span, :] = _pack_halves(y)

    @pl.when(nv_ref[i * slots] == 0)
    def _():
        ys_ref[...] = jnp.zeros_like(ys_ref)


def _expert_ffn(xs, block_expert, block_valid, w_gate, w_up, w_down, layer, tb):
    p, half = xs.shape
    d = 2 * half
    slots = FFN_BLOCKS_PER_STEP
    n_blocks = block_expert.shape[0]
    run = jnp.cumsum(jnp.concatenate([jnp.zeros((1,), jnp.int32),
                                      (block_expert[1:] != block_expert[:-1]).astype(jnp.int32)]))
    blocks = jnp.arange(n_blocks, dtype=jnp.int32)
    first_block = jnp.minimum(jnp.sum((run[None, :] < blocks[:, None]).astype(jnp.int32), axis=1), n_blocks - 1)
    run_expert = block_expert[first_block]
    newest = run.reshape(-1, slots)[:, slots - 1]
    held = []
    for slot in range(slots):
        r = newest - (newest - slot) % slots
        held.append(run_expert[jnp.where(r < 0, slot, r)])
    held = jnp.stack(held, axis=1).reshape(-1)
    use = run % slots
    weight_specs = []
    for slot in range(slots):
        index = lambda i, held, use, nv, slot=slot: (layer, held[i * slots + slot], 0, 0)
        weight_specs += [pl.BlockSpec((1, 1, d, EXPERT_DIM), index), pl.BlockSpec((1, 1, d, EXPERT_DIM), index),
                         pl.BlockSpec((1, 1, EXPERT_DIM, d), index)]
    grid_spec = pltpu.PrefetchScalarGridSpec(
        num_scalar_prefetch=3,
        grid=(p // (tb * slots),),
        in_specs=[pl.BlockSpec((tb * slots, half), lambda i, held, use, nv: (i, 0))] + weight_specs,
        out_specs=pl.BlockSpec((tb * slots, half), lambda i, held, use, nv: (i, 0)),
        scratch_shapes=[pltpu.VMEM((slots, d, 2 * EXPERT_DIM), BF16), pltpu.VMEM((slots, EXPERT_DIM, d), BF16)],
    )
    return pl.pallas_call(
        functools.partial(_expert_ffn_kernel, tb=tb),
        grid_spec=grid_spec,
        out_shape=jax.ShapeDtypeStruct((p, half), jnp.int32),
        compiler_params=_params(("arbitrary",)),
        name="moe_expert_ffn",
    )(held, use, block_valid, xs, *([w_gate, w_up, w_down] * slots))


def _combine_kernel(x_ref, hp_ref, yg_ref, w_ref, mod_ref, wgu_ref, wd_ref, gf_ref, o_ref, *, final_norm):
    low, high = _unpack_halves(hp_ref[...])
    h = jnp.concatenate([low, high], axis=1).astype(BF16)
    shared = _swiglu(h, wgu_ref[...], wd_ref[...])
    half = hp_ref.shape[1]
    acc_low, acc_high = shared[:, :half], shared[:, half:]
    w = w_ref[...]
    for k in range(TOP_K):
        low, high = _unpack_halves(yg_ref[k])
        wk = w[:, k:k + 1]
        acc_low = acc_low + wk * low
        acc_high = acc_high + wk * high
    y = jnp.concatenate([acc_low, acc_high], axis=1)
    out = x_ref[...] + mod_ref[0, GT2:GT2 + 1, :] * y
    if final_norm:
        out = out * lax.rsqrt(jnp.mean(out * out, axis=-1, keepdims=True) + EPS) * gf_ref[...]
    o_ref[...] = out


def _combine(x2, hp, yg, w, mod, ws_gate_up, ws_down, g_final, seq, final_norm, tm=512):
    n, d = x2.shape
    blocks_per_batch = seq // tm
    const = lambda a: pl.BlockSpec(a.shape, lambda i: (0,) * a.ndim)
    return pl.pallas_call(
        functools.partial(_combine_kernel, final_norm=final_norm),
        grid=(n // tm,),
        in_specs=[
            pl.BlockSpec((tm, d), lambda i: (i, 0)),
            pl.BlockSpec((tm, d // 2), lambda i: (i, 0)),
            pl.BlockSpec((TOP_K, tm, d // 2), lambda i: (0, i, 0)),
            pl.BlockSpec((tm, LANES), lambda i: (i, 0)),
            pl.BlockSpec((1, 6, d), lambda i: (i // blocks_per_batch, 0, 0)),
            const(ws_gate_up), const(ws_down), const(g_final),
        ],
        out_specs=pl.BlockSpec((tm, d), lambda i: (i, 0)),
        out_shape=jax.ShapeDtypeStruct((n, d), F32),
        compiler_params=_params(("arbitrary",)),
        name="moe_combine",
    )(x2, hp, yg, w, mod, ws_gate_up, ws_down, g_final)


def _token_mixer(x2, mod, g_mix, w_in, b_gate, rpb, w_pa, w_pb, w_o, g_ffn, w_router, e_bias, batch, seq):
    d = x2.shape[1]
    dil0 = 3 * NA_WIDTH
    group_cols = [[dil0 + part * DIL_WIDTH + grp * DIL_OUT_WIDTH for part in range(3)]
                  for grp in range(N_DIL_GROUPS)]
    order = sorted(range(N_DIL_GROUPS), key=lambda grp: DIL_GROUPS[grp][1] > 1)
    w_qkv = jnp.concatenate(
        [w_in[:, :dil0]] + [w_in[:, c:c + DIL_OUT_WIDTH] for grp in order for c in group_cols[grp]],
        axis=1).astype(BF16)
    tok, *residue = _inproj(x2, mod, g_mix.reshape(1, d), w_qkv, batch, seq)
    o_a = _neighbourhood_attention(tok, _na_bias_table(rpb), batch, seq)

    qkv_offsets = tuple(part * DIL_OUT_WIDTH // LANES for part in range(3))
    group_inputs = [None] * N_DIL_GROUPS
    residue = iter(residue)
    tok_offset = dil0 // LANES
    for grp in order:
        if DIL_GROUPS[grp][1] == 1:
            group_inputs[grp] = (tok.reshape(batch, 1, seq, tok.shape[1]),
                                 tuple(tok_offset + o for o in qkv_offsets))
            tok_offset += 3 * DIL_OUT_WIDTH // LANES
        else:
            group_inputs[grp] = (next(residue), qkv_offsets)
    o_b = _dilated_attention(group_inputs, batch, seq)

    wr_t = w_router.T
    wr_hi = wr_t.astype(BF16)
    wr_split = jnp.stack([wr_hi, (wr_t - wr_hi.astype(F32)).astype(BF16)])
    return _outproj(x2, o_a, o_b, mod, g_mix.reshape(1, d),
                    w_in[:, QKV_COLS:].astype(BF16), b_gate.reshape(1, -1),
                    w_pa.astype(BF16), w_pb.astype(BF16), w_o.astype(BF16),
                    g_ffn.reshape(1, d), wr_split, e_bias.reshape(-1, 1), seq)


def _dispatch_plan(eidx3, rank3, counts, tb):
    n = eidx3.shape[0] * SC_CHUNK
    n_blocks = -(-(n * TOP_K + N_EXPERTS * (tb - 1)) // tb)
    n_blocks = -(-n_blocks // FFN_BLOCKS_PER_STEP) * FFN_BLOCKS_PER_STEP
    padded = (counts + tb - 1) // tb * tb
    seg_end = jnp.cumsum(padded)
    seg_start = seg_end - padded
    experts = jnp.arange(N_EXPERTS, dtype=jnp.int32)

    def lookup(table, idx):
        sel = idx[None] == experts.reshape((N_EXPERTS,) + (1,) * idx.ndim)
        return jnp.sum(jnp.where(sel, table.reshape((N_EXPERTS,) + (1,) * idx.ndim), 0), axis=0)

    idx3 = lookup(seg_start, eidx3) + rank3
    block_start = jnp.arange(n_blocks, dtype=jnp.int32) * tb
    block_expert = jnp.sum((seg_end[:, None] <= block_start[None, :]).astype(jnp.int32), axis=0)
    block_expert = jnp.minimum(block_expert, N_EXPERTS - 1)
    block_valid = jnp.clip(lookup(counts, block_expert) - (block_start - lookup(seg_start, block_expert)), 0, tb)
    return idx3, block_expert, block_valid.astype(jnp.int32), n_blocks


def _moe_layer(x2, routed, mod, we_gate, we_up, we_down, layer, ws_gate, ws_up, ws_down, g_final, seq, final_norm,
               tb=EXPERT_ROW_BLOCK):
    d = x2.shape[1]
    hp, eidx3, rank3, w, cnt = routed
    counts = cnt[:, 0].astype(jnp.int32)
    idx3, block_expert, block_valid, n_blocks = _dispatch_plan(eidx3, rank3, counts, tb)
    xs = _sc_scatter_rows(hp, idx3, n_blocks * tb)
    ys = _expert_ffn(xs, block_expert, block_valid, we_gate, we_up, we_down, layer, tb)
    yg = _sc_gather_rows(ys, idx3)
    return _combine(x2, hp, yg, w, mod, jnp.concatenate([ws_gate, ws_up], axis=-1).astype(BF16),
                    ws_down.astype(BF16), g_final.reshape(1, d), seq, final_norm)


def kernel(x, c, w_ada, b_ada, g_mix, w_in, b_gate, rpb, w_pa, w_pb, w_o, g_ffn, w_router, e_bias,
           we_gate, we_up, we_down, ws_gate, ws_up, ws_down, g_final):
    batch, seq, d = x.shape
    depth = w_ada.shape[0]
    mods = _ada(c, w_ada, b_ada).reshape(depth, batch, 6, d)
    x2 = x.reshape(batch * seq, d)
    for l in range(depth):
        x2, *routed = _token_mixer(x2, mods[l], g_mix[l], w_in[l], b_gate[l], rpb[l], w_pa[l], w_pb[l], w_o[l],
                                   g_ffn[l], w_router[l], e_bias[l], batch, seq)
        x2 = _moe_layer(x2, routed, mods[l], we_gate, we_up, we_down, l,
                        ws_gate[l], ws_up[l], ws_down[l], g_final, seq, final_norm=(l == depth - 1))
    return x2.reshape(batch, seq, d)
```

```python
import functools

import numpy as np
import jax
import jax.numpy as jnp
from jax import lax
from jax.experimental import pallas as pl
from jax.experimental.pallas import tpu as pltpu
from jax.experimental.pallas import tpu_sc as plsc

HEAD_DIM = 64
GRID_W = 64
NA_HEADS = 8
NA_WIN_ROWS = 8
NA_WIN_COLS = 16
DIL_GROUPS = ((128, 1), (512, 4), (2048, 16))
DIL_HEADS_PER_GROUP = 4
N_DIL_GROUPS = len(DIL_GROUPS)
NA_WIDTH = NA_HEADS * HEAD_DIM
DIL_WIDTH = N_DIL_GROUPS * DIL_HEADS_PER_GROUP * HEAD_DIM
DIL_OUT_WIDTH = DIL_HEADS_PER_GROUP * HEAD_DIM
QKV_COLS = 3 * (NA_WIDTH + DIL_WIDTH)
N_EXPERTS = 64
TOP_K = 8
N_EXPERT_GROUPS = 8
TOP_GROUPS = 4
EXPERT_DIM = 256
ROUTED_SCALE = 2.5
ALIBI_MAX = 8.0
EPS = 1e-6
NEG_INF = -1e30

LANES = 128
HEADS_PER_LANE_TILE = LANES // HEAD_DIM
DIL_BLOCK = 64
VMEM_LIMIT_BYTES = 56 * 1024 * 1024

SC_CORES = 2
SC_SUBCORES = 16
SC_WORKERS = SC_CORES * SC_SUBCORES
SC_CHUNK = 64
EXPERT_ROW_BLOCK = 512
FFN_SUB_BLOCKS = 2
FFN_BLOCKS_PER_STEP = 4

F32 = jnp.float32
BF16 = jnp.bfloat16
BF16_BITS = 16

SH1, SC1, GT1, SH2, SC2, GT2 = range(6)


def _params(sem):
    return pltpu.CompilerParams(dimension_semantics=sem, vmem_limit_bytes=VMEM_LIMIT_BYTES)


def _modulated_norm(x, g, scale, shift):
    r = lax.rsqrt(jnp.mean(x * x, axis=-1, keepdims=True) + EPS)
    return (x * r) * (g * (1.0 + scale)) + shift


def _ada_kernel(c_ref, w_ref, b_ref, o_ref):
    c = c_ref[...]
    act = c * jax.nn.sigmoid(c)
    o_ref[0] = jnp.dot(act, w_ref[0], preferred_element_type=F32,
                       precision=lax.Precision.HIGHEST) + b_ref[0]


def _ada(c, w_ada, b_ada):
    depth, d, six_d = w_ada.shape
    b = c.shape[0]
    tn = d
    return pl.pallas_call(
        _ada_kernel,
        grid=(depth, six_d // tn),
        in_specs=[
            pl.BlockSpec((b, d), lambda l, j: (0, 0)),
            pl.BlockSpec((1, d, tn), lambda l, j: (l, 0, j)),
            pl.BlockSpec((1, 1, tn), lambda l, j: (l, 0, j)),
        ],
        out_specs=pl.BlockSpec((1, b, tn), lambda l, j: (l, 0, j)),
        out_shape=jax.ShapeDtypeStruct((depth, b, six_d), F32),
        compiler_params=_params(("arbitrary", "arbitrary")),
        name="ada_mod",
    )(c, w_ada, b_ada.reshape(depth, 1, six_d))


def _inproj_kernel(x_ref, mod_ref, g_ref, w_ref, tok_ref, *rest, dilations):
    res_refs, acc_refs = rest[:len(dilations)], rest[len(dilations):]
    tm = x_ref.shape[0]
    tok_cols = tok_ref.shape[1]
    tn = (w_ref.shape[1] - tok_cols) // len(dilations)
    h = _modulated_norm(x_ref[...], g_ref[...], mod_ref[0, SC1:SC1 + 1, :], mod_ref[0, SH1:SH1 + 1, :]).astype(BF16)

    for c0 in range(0, tok_cols, tn):
        tok_ref[:, c0:c0 + tn] = jnp.dot(h, w_ref[:, c0:c0 + tn], preferred_element_type=F32).astype(BF16)

    for g, (res_ref, acc_ref, dilation) in enumerate(zip(res_refs, acc_refs, dilations)):
        c0 = tok_cols + g * tn
        res = jnp.dot(h, w_ref[:, c0:c0 + tn], preferred_element_type=F32)
        for c in range(acc_ref.shape[0]):
            acc_ref[c] = res[:, c * LANES:(c + 1) * LANES]
        for r in range(dilation):
            for c in range(acc_ref.shape[0]):
                res_ref[0, r, :, c * LANES:(c + 1) * LANES] = (
                    acc_ref[c, pl.ds(r, tm // dilation, stride=dilation), :].astype(BF16))


def _inproj(x2, mod, g, w_qkv, batch, seq, tm=1024):
    n, d = x2.shape
    tn = 3 * DIL_OUT_WIDTH
    dilations = tuple(dil for _, dil in DIL_GROUPS if dil > 1)
    tok_cols = w_qkv.shape[1] - tn * len(dilations)
    blocks_per_batch = seq // tm
    res_specs = [pl.BlockSpec((1, dil, tm // dil, tn),
                              lambda i: (i // blocks_per_batch, 0, i % blocks_per_batch, 0)) for dil in dilations]
    res_shapes = [jax.ShapeDtypeStruct((batch, dil, seq // dil, tn), BF16) for dil in dilations]
    return pl.pallas_call(
        functools.partial(_inproj_kernel, dilations=dilations),
        grid=(n // tm,),
        in_specs=[
            pl.BlockSpec((tm, d), lambda i: (i, 0)),
            pl.BlockSpec((1, 6, d), lambda i: (i // blocks_per_batch, 0, 0)),
            pl.BlockSpec((1, d), lambda i: (0, 0)),
            pl.BlockSpec(w_qkv.shape, lambda i: (0, 0)),
        ],
        out_specs=[pl.BlockSpec((tm, tok_cols), lambda i: (i, 0))] + res_specs,
        out_shape=[jax.ShapeDtypeStruct((n, tok_cols), BF16)] + res_shapes,
        scratch_shapes=[pltpu.VMEM((tn // LANES, tm, LANES), F32) for _ in dilations],
        compiler_params=_params(("arbitrary",)),
        name="in_proj",
    )(x2, mod, g, w_qkv)


def _na_bias_table(rpb):
    heads = rpb.shape[0]
    cols = np.arange(GRID_W)
    col_start = np.clip(cols - NA_WIN_COLS // 2, 0, GRID_W - NA_WIN_COLS)
    col_mask = (cols[None, :] >= col_start[:, None]) & (cols[None, :] < col_start[:, None] + NA_WIN_COLS)
    edge = GRID_W - NA_WIN_COLS
    ext = jnp.concatenate([jnp.repeat(rpb[..., :1], edge, axis=-1), rpb, jnp.repeat(rpb[..., -1:], edge, axis=-1)],
                          axis=-1).astype(F32)
    rpb_cols = jnp.stack([ext[..., GRID_W - 1 - cq:2 * GRID_W - 1 - cq] for cq in range(GRID_W)],
                         axis=1)
    pairs = heads // HEADS_PER_LANE_TILE
    by_pair = rpb_cols.reshape(pairs, HEADS_PER_LANE_TILE, GRID_W, 2 * NA_WIN_ROWS - 1, GRID_W)
    t = jnp.stack([by_pair[:, :, :, NA_WIN_ROWS - 1 - off:2 * NA_WIN_ROWS - 1 - off] for off in range(NA_WIN_ROWS)],
                  axis=1)
    t = jnp.where(col_mask[:, None, :], t, NEG_INF)
    return t.reshape(pairs, NA_WIN_ROWS, HEADS_PER_LANE_TILE * GRID_W, NA_WIN_ROWS * GRID_W)


def _stack_heads(q, low):
    scaled = q * (HEAD_DIM ** -0.5)
    zero = jnp.zeros_like(scaled)
    return jnp.concatenate([jnp.where(low, scaled, zero), jnp.where(low, zero, scaled)], axis=0)


def _stacked_attention(items):
    scores = [lax.dot_general(q2, kw, (((1,), (1,)), ((), ())), preferred_element_type=F32) + bias
              for q2, kw, _, bias in items]
    probs = []
    for s in scores:
        m = jnp.max(s, axis=-1, keepdims=True)
        p = jnp.exp(s - m)
        probs.append((p.astype(BF16), m, jnp.sum(p, axis=-1, keepdims=True)))
    return [(jnp.dot(p, vw, preferred_element_type=F32) / z, m, z)
            for (p, m, z), (_, _, vw, _) in zip(probs, items)]


def _unstack_heads(a, low):
    half = a.shape[0] // HEADS_PER_LANE_TILE
    return jnp.where(low, a[:half], a[half:])


def _na_kernel(q_ref, k_ref, v_ref, bias_ref, o_ref, *, rows, rows_per_step):
    kr = NA_WIN_ROWS
    low = lax.broadcasted_iota(jnp.int32, (GRID_W, LANES), 1) < HEAD_DIM

    def body(i, carry):
        items, qrows = [], []
        for u in range(rows_per_step):
            r = i * rows_per_step + u
            rs = jnp.clip(r - kr // 2, 0, rows - kr)
            qrows.append(pl.ds(pl.multiple_of(r * GRID_W, GRID_W), GRID_W))
            wrows = pl.ds(pl.multiple_of(rs * GRID_W, GRID_W), kr * GRID_W)
            items.append((_stack_heads(q_ref[qrows[-1], :], low), k_ref[wrows, :], v_ref[wrows, :],
                          bias_ref[0, r - rs]))
        for rows_u, (o, _, _) in zip(qrows, _stacked_attention(items)):
            o_ref[rows_u, :] = _unstack_heads(o, low).astype(o_ref.dtype)
        return carry

    lax.fori_loop(0, rows // rows_per_step, body, 0)


def _neighbourhood_attention(qkv, bias, batch, seq, rows_per_step=16):
    n = qkv.shape[0]
    rows = seq // GRID_W
    pairs = NA_WIDTH // LANES
    return pl.pallas_call(
        functools.partial(_na_kernel, rows=rows, rows_per_step=rows_per_step),
        grid=(pairs, batch),
        in_specs=[
            pl.BlockSpec((seq, LANES), lambda p, b: (b, p)),
            pl.BlockSpec((seq, LANES), lambda p, b: (b, pairs + p)),
            pl.BlockSpec((seq, LANES), lambda p, b: (b, 2 * pairs + p)),
            pl.BlockSpec((1, NA_WIN_ROWS, HEADS_PER_LANE_TILE * GRID_W, NA_WIN_ROWS * GRID_W),
                         lambda p, b: (p, 0, 0, 0)),
        ],
        out_specs=pl.BlockSpec((seq, LANES), lambda p, b: (b, p)),
        out_shape=jax.ShapeDtypeStruct((n, NA_WIDTH), BF16),
        compiler_params=_params(("arbitrary", "arbitrary")),
        name="na_attn",
    )(qkv, qkv, qkv, bias)


def _alibi_slopes():
    n = N_DIL_GROUPS * DIL_HEADS_PER_GROUP
    s = np.exp2(-ALIBI_MAX * np.arange(1, n + 1, dtype=np.float64) / n).astype(np.float32)
    return s.reshape(N_DIL_GROUPS, DIL_HEADS_PER_GROUP)


def _dil_bias_table(group):
    blk = DIL_BLOCK
    dilation = DIL_GROUPS[group][1]
    slopes = _alibi_slopes()[group]
    qi = np.arange(blk)[:, None]
    kj = np.arange(3 * blk)[None, :]
    tables = []
    for shift in range(3):
        arel = np.abs(kj - qi - shift * blk)
        dist = (dilation * arel).astype(np.float32)
        per_head = [np.where(arel <= blk, -slopes[h] * dist, np.float32(NEG_INF)) for h in range(DIL_HEADS_PER_GROUP)]
        tables.append(np.stack(per_head))
    t = np.stack(tables, axis=1).astype(np.float32)
    pairs = DIL_HEADS_PER_GROUP // HEADS_PER_LANE_TILE
    t = t.reshape(pairs, HEADS_PER_LANE_TILE, 3, blk, 3 * blk).transpose(0, 2, 1, 3, 4)
    return t.reshape(pairs, 3, HEADS_PER_LANE_TILE * blk, 3 * blk)


def _dil_group(q_ref, k_ref, v_ref, bias_ref, o_ref, lse_ref, blocks_per_step):
    blk = DIL_BLOCK
    win = 3 * blk
    low = lax.broadcasted_iota(jnp.int32, (blk, LANES), 1) < HEAD_DIM
    dilation, length, _ = q_ref.shape
    nb = length // blk
    steps = dilation * nb

    def token_rows(sq, n):
        if dilation == 1:
            return pl.ds(pl.multiple_of(n * blk, blk), blk)
        return pl.ds(n * (blk * dilation) + sq, blk, stride=dilation)

    def body(i, carry):
        items, dst = [], []
        for u in range(blocks_per_step):
            t = i * blocks_per_step + u
            sq = t // nb
            n = t % nb
            wb = jnp.clip(n - 1, 0, nb - 3)
            qrows = pl.ds(pl.multiple_of(n * blk, blk), blk)
            wrows = pl.ds(pl.multiple_of(wb * blk, blk), win)
            dst.append(token_rows(sq, n))
            items.append((_stack_heads(q_ref[sq, qrows, :], low), k_ref[sq, wrows, :], v_ref[sq, wrows, :],
                          bias_ref[0, n - wb]))
        for rows, (o, m, z) in zip(dst, _stacked_attention(items)):
            o_ref[rows, :] = _unstack_heads(o, low)
            lse_ref[rows, :] = _unstack_heads(jnp.broadcast_to(m + jnp.log(z), o.shape), low)
        return carry

    lax.fori_loop(0, steps // blocks_per_step, body, 0)


def _dil_kernel(*refs, blocks_per_step, merge_rows):
    ng = N_DIL_GROUPS
    qkv_refs, bias_refs = refs[:3 * ng], refs[3 * ng:4 * ng]
    ob_ref, o_scr, lse_scr = refs[4 * ng:]
    for g in range(ng):
        q_ref, k_ref, v_ref = qkv_refs[3 * g:3 * g + 3]
        _dil_group(q_ref, k_ref, v_ref, bias_refs[g], o_scr.at[g], lse_scr.at[g], blocks_per_step)

    def merge(c, carry):
        rows = pl.ds(pl.multiple_of(c * merge_rows, merge_rows), merge_rows)
        lses = [lse_scr[g, rows, :] for g in range(ng)]
        top = functools.reduce(jnp.maximum, lses)
        es = [jnp.exp(l - top) for l in lses]
        num = functools.reduce(lambda a, b: a + b, [e * o_scr[g, rows, :] for g, e in enumerate(es)])
        ob_ref[rows, :] = (num / functools.reduce(lambda a, b: a + b, es)).astype(ob_ref.dtype)
        return carry

    lax.fori_loop(0, ob_ref.shape[0] // merge_rows, merge, 0)


def _dilated_attention(group_inputs, batch, seq, blocks_per_step=16, merge_rows=256):
    pairs = DIL_OUT_WIDTH // LANES
    operands, specs = [], []
    for qkv4, offsets in group_inputs:
        _, dilation, length, _ = qkv4.shape
        for off in offsets:
            operands.append(qkv4)
            specs.append(pl.BlockSpec((None, dilation, length, LANES),
                                      lambda p, b, off=off: (b, 0, 0, off + p)))
    for group in range(N_DIL_GROUPS):
        bias = jnp.asarray(_dil_bias_table(group))
        operands.append(bias)
        specs.append(pl.BlockSpec((1,) + bias.shape[1:], lambda p, b: (p, 0, 0, 0)))
    return pl.pallas_call(
        functools.partial(_dil_kernel, blocks_per_step=blocks_per_step, merge_rows=merge_rows),
        grid=(pairs, batch),
        in_specs=specs,
        out_specs=pl.BlockSpec((seq, LANES), lambda p, b: (b, p)),
        out_shape=jax.ShapeDtypeStruct((batch * seq, DIL_OUT_WIDTH), BF16),
        scratch_shapes=[pltpu.VMEM((N_DIL_GROUPS, seq, LANES), F32), pltpu.VMEM((N_DIL_GROUPS, seq, LANES), F32)],
        compiler_params=_params(("arbitrary", "arbitrary")),
        name="dil_attn",
    )(*operands)


def _outproj_kernel(x_ref, oa_ref, ob_ref, mod_ref, g_ref,
                    wg_ref, bg_ref, wpa_ref, wpb_ref, wo_ref, g2_ref, wr_ref, eb_ref, tri_ref,
                    out_ref, hp_ref, eidx_ref, rank_ref, w_ref, cnt_ref):
    d = x_ref.shape[1]
    x = x_ref[...]
    h = _modulated_norm(x, g_ref[...], mod_ref[0, SC1:SC1 + 1, :], mod_ref[0, SH1:SH1 + 1, :]).astype(BF16)

    ya = jnp.dot(oa_ref[...], wpa_ref[...], preferred_element_type=F32)
    yb = jnp.dot(ob_ref[...], wpb_ref[...], preferred_element_type=F32)
    ga = jax.nn.sigmoid(jnp.dot(h, wg_ref[:, :d], preferred_element_type=F32) + bg_ref[:, :d])
    mix = ga * ya
    gb = jax.nn.sigmoid(jnp.dot(h, wg_ref[:, d:], preferred_element_type=F32) + bg_ref[:, d:])
    mix = mix + gb * yb
    y = jnp.dot(mix.astype(BF16), wo_ref[...], preferred_element_type=F32)
    out = x + mod_ref[0, GT1:GT1 + 1, :] * y
    out_ref[...] = out
    _route_block(out, mod_ref, g2_ref, wr_ref, eb_ref, tri_ref, hp_ref, eidx_ref, rank_ref, w_ref, cnt_ref)


def _outproj(x2, oa, ob, mod, g, w_gate, b_gate, w_pa, w_pb, w_o, g_ffn, wr_split, e_bias, seq, tm=1024):
    n, d = x2.shape
    assert n % tm == 0 and seq % tm == 0 and tm % SC_CHUNK == 0
    blocks_per_batch = seq // tm
    tri = jnp.asarray(np.triu(np.ones((tm, tm), np.float32)), BF16)
    row = lambda c: pl.BlockSpec((tm, c), lambda i: (i, 0))
    chunked = pl.BlockSpec((tm // SC_CHUNK, TOP_K, SC_CHUNK), lambda i: (i, 0, 0))
    full = lambda a: pl.BlockSpec(a.shape, lambda i: (0,) * a.ndim)
    return pl.pallas_call(
        _outproj_kernel,
        grid=(n // tm,),
        in_specs=[row(d), row(NA_WIDTH), row(DIL_OUT_WIDTH)] + [
            pl.BlockSpec((1, 6, d), lambda i: (i // blocks_per_batch, 0, 0)),
            full(g), full(w_gate), full(b_gate), full(w_pa), full(w_pb), full(w_o),
            full(g_ffn), full(wr_split), full(e_bias), full(tri),
        ],
        out_specs=[row(d), row(d // 2), chunked, chunked, row(LANES),
                   pl.BlockSpec((N_EXPERTS, LANES), lambda i: (0, 0))],
        out_shape=[
            jax.ShapeDtypeStruct((n, d), F32),
            jax.ShapeDtypeStruct((n, d // 2), jnp.int32),
            jax.ShapeDtypeStruct((n // SC_CHUNK, TOP_K, SC_CHUNK), jnp.int32),
            jax.ShapeDtypeStruct((n // SC_CHUNK, TOP_K, SC_CHUNK), jnp.int32),
            jax.ShapeDtypeStruct((n, LANES), F32),
            jax.ShapeDtypeStruct((N_EXPERTS, LANES), F32),
        ],
        compiler_params=_params(("arbitrary",)),
        name="out_proj",
    )(x2, oa, ob, mod, g, w_gate, b_gate, w_pa, w_pb, w_o, g_ffn, wr_split, e_bias, tri)


def _first_index_of_max(cur, idx, size):
    m = jnp.max(cur, axis=0, keepdims=True)
    first = jnp.min(jnp.where(cur == m, idx, size), axis=0, keepdims=True)
    return m, first


def _route_transposed(logits_t, e_bias):
    tokens = logits_t.shape[1]
    per_group = N_EXPERTS // N_EXPERT_GROUPS
    scores = jax.nn.sigmoid(logits_t)
    biased = scores + e_bias
    midx = lax.broadcasted_iota(jnp.int32, (per_group, tokens), 0)
    grp_scores = []
    for g in range(N_EXPERT_GROUPS):
        vals = biased[g * per_group:(g + 1) * per_group, :]
        m1, first = _first_index_of_max(vals, midx, per_group)
        m2 = jnp.max(jnp.where(midx == first, -jnp.inf, vals), axis=0, keepdims=True)
        grp_scores.append(m1 + m2)
    cur = jnp.concatenate(grp_scores, axis=0)
    gidx = lax.broadcasted_iota(jnp.int32, (N_EXPERT_GROUPS, tokens), 0)
    grp_sel = jnp.zeros((N_EXPERT_GROUPS, tokens), jnp.bool_)
    for _ in range(TOP_GROUPS):
        _, first = _first_index_of_max(cur, gidx, N_EXPERT_GROUPS)
        pick = gidx == first
        grp_sel = jnp.logical_or(grp_sel, pick)
        cur = jnp.where(pick, -jnp.inf, cur)
    rows = []
    for g in range(N_EXPERT_GROUPS):
        vals = biased[g * per_group:(g + 1) * per_group, :]
        rows.append(jnp.where(grp_sel[g:g + 1, :], vals, NEG_INF))
    cur = jnp.concatenate(rows, axis=0)
    eidx = lax.broadcasted_iota(jnp.int32, (N_EXPERTS, tokens), 0)
    firsts, picks, weights = [], [], []
    for _ in range(TOP_K):
        _, first = _first_index_of_max(cur, eidx, N_EXPERTS)
        pick = eidx == first
        firsts.append(first)
        picks.append(pick)
        weights.append(jnp.sum(jnp.where(pick, scores, 0.0), axis=0, keepdims=True))
        cur = jnp.where(pick, -jnp.inf, cur)
    total = functools.reduce(lambda a, b: a + b, weights)
    return firsts, picks, [w / total * ROUTED_SCALE for w in weights]


def _pack_halves(a):
    half = a.shape[1] // 2
    bits = lax.bitcast_convert_type(a.astype(BF16).astype(F32), jnp.int32)
    return lax.shift_right_logical(bits[:, :half], BF16_BITS) | bits[:, half:]


def _unpack_halves(w):
    low = lax.bitcast_convert_type(lax.shift_left(w, BF16_BITS), F32)
    high = lax.bitcast_convert_type(w & jnp.int32(-(1 << BF16_BITS)), F32)
    return low, high


def _route_block(x, mod_ref, g_ref, wr_ref, eb_ref, tri_ref, hp_ref, eidx_ref, rank_ref, w_ref, cnt_ref):
    tm = x.shape[0]

    @pl.when(pl.program_id(0) == 0)
    def _():
        cnt_ref[...] = jnp.zeros_like(cnt_ref)

    h = _modulated_norm(x, g_ref[...], mod_ref[0, SC2:SC2 + 1, :], mod_ref[0, SH2:SH2 + 1, :])
    hp_ref[...] = _pack_halves(h)
    h_hi = h.astype(BF16)
    h_lo = (h - h_hi.astype(F32)).astype(BF16)
    nt = (((1,), (1,)), ((), ()))
    logits_t = (lax.dot_general(wr_ref[0], h_hi, nt, preferred_element_type=F32)
                + lax.dot_general(wr_ref[0], h_lo, nt, preferred_element_type=F32)
                + lax.dot_general(wr_ref[1], h_hi, nt, preferred_element_type=F32))
    firsts, picks, weights = _route_transposed(logits_t, eb_ref[...])
    sel = functools.reduce(jnp.logical_or, picks)
    sel_f = jnp.where(sel, 1.0, 0.0)
    incl = jnp.dot(sel_f.astype(BF16), tri_ref[...], preferred_element_type=F32)
    before = cnt_ref[:, 0:1] + incl - sel_f
    eidx = jnp.concatenate(firsts, axis=0)
    rank = jnp.concatenate(
        [jnp.sum(jnp.where(p, before, 0.0), axis=0, keepdims=True) for p in picks], axis=0).astype(jnp.int32)
    for c in range(tm // SC_CHUNK):
        eidx_ref[c] = eidx[:, c * SC_CHUNK:(c + 1) * SC_CHUNK]
        rank_ref[c] = rank[:, c * SC_CHUNK:(c + 1) * SC_CHUNK]
    pad = jnp.concatenate(weights + [jnp.zeros((LANES - TOP_K, tm), F32)], axis=0)
    w_ref[...] = pad.T
    cnt_ref[...] = cnt_ref[...] + incl[:, tm - 1:tm]


def _sc_worker_id():
    return lax.axis_index("subcore") * SC_CORES + lax.axis_index("core")


def _sc_scatter_rows(src, idx3, n_out):
    n, w = src.shape
    assert n % (2 * SC_CHUNK * SC_WORKERS) == 0
    per_worker = n // SC_CHUNK // SC_WORKERS
    mesh = plsc.VectorSubcoreMesh(core_axis_name="core", subcore_axis_name="subcore")

    @functools.partial(
        pl.kernel, mesh=mesh, out_type=jax.ShapeDtypeStruct((n_out, w), src.dtype), name="moe_dispatch",
        scratch_types=[pltpu.VMEM((2, TOP_K, SC_CHUNK), jnp.int32), pltpu.VMEM((2, SC_CHUNK, w), src.dtype),
                       pltpu.SemaphoreType.DMA((2,)), pltpu.SemaphoreType.DMA((2,)), pltpu.SemaphoreType.DMA])
    def scatter(src_hbm, idx_hbm, out_hbm, idx_v, rows_v, idx_sem, row_sem, out_sem):
        first = _sc_worker_id() * per_worker

        def loads(chunk, slot):
            return (pltpu.make_async_copy(idx_hbm.at[chunk], idx_v.at[slot], idx_sem.at[slot]),
                    pltpu.make_async_copy(src_hbm.at[pl.ds(chunk * SC_CHUNK, SC_CHUNK)], rows_v.at[slot],
                                          row_sem.at[slot]))

        for cp in loads(first, 0):
            cp.start()

        @pl.loop(0, per_worker, step=2)
        def _(i):
            for slot in range(2):
                chunk = first + i + slot
                for cp in loads(chunk, slot):
                    cp.wait()

                @pl.when(i + slot + 1 < per_worker)
                def _():
                    for cp in loads(chunk + 1, 1 - slot):
                        cp.start()

                copies = [pltpu.make_async_copy(rows_v.at[slot], out_hbm.at[idx_v.at[slot, k]], out_sem)
                          for k in range(TOP_K)]
                for cp in copies:
                    cp.start()
                for cp in copies:
                    cp.wait()

    return scatter(src, idx3)


def _sc_gather_rows(src, idx3):
    _, w = src.shape
    chunks = idx3.shape[0]
    assert chunks % SC_WORKERS == 0
    per_worker = chunks // SC_WORKERS
    mesh = plsc.VectorSubcoreMesh(core_axis_name="core", subcore_axis_name="subcore")

    @functools.partial(
        pl.kernel, mesh=mesh, out_type=jax.ShapeDtypeStruct((TOP_K, chunks * SC_CHUNK, w), src.dtype),
        name="moe_collect",
        scratch_types=[pltpu.VMEM((TOP_K, SC_CHUNK), jnp.int32), pltpu.VMEM((2, SC_CHUNK, w), src.dtype),
                       pltpu.SemaphoreType.DMA((2,)), pltpu.SemaphoreType.DMA((2,))])
    def gather(src_hbm, idx_hbm, out_hbm, idx_v, rows_v, in_sem, out_sem):
        first = _sc_worker_id() * per_worker

        @pl.loop(0, per_worker)
        def _(i):
            chunk = first + i
            pltpu.sync_copy(idx_hbm.at[chunk], idx_v)
            reads = [pltpu.make_async_copy(src_hbm.at[idx_v.at[k]], rows_v.at[k % 2], in_sem.at[k % 2])
                     for k in range(TOP_K)]
            writes = [pltpu.make_async_copy(rows_v.at[k % 2], out_hbm.at[k, pl.ds(chunk * SC_CHUNK, SC_CHUNK)],
                                            out_sem.at[k % 2]) for k in range(TOP_K)]
            reads[0].start()
            for k in range(TOP_K):
                if k + 1 < TOP_K:
                    if k >= 1:
                        writes[k - 1].wait()
                    reads[k + 1].start()
                reads[k].wait()
                writes[k].start()
            writes[TOP_K - 2].wait()
            writes[TOP_K - 1].wait()

    return gather(src, idx3)


def _swiglu(x, w_gate_up, w_down):
    gu = jnp.dot(x, w_gate_up, preferred_element_type=F32)
    gate = gu[:, :EXPERT_DIM]
    act = (gate * jax.nn.sigmoid(gate)) * gu[:, EXPERT_DIM:]
    return jnp.dot(act.astype(BF16), w_down, preferred_element_type=F32)


def _expert_ffn_kernel(held_ref, use_ref, nv_ref, xs_ref, *refs, tb):
    slots = FFN_BLOCKS_PER_STEP
    w_refs, ys_ref, wgu_bf, wd_bf = refs[:3 * slots], refs[3 * slots], refs[3 * slots + 1], refs[3 * slots + 2]
    i = pl.program_id(0)
    sub = tb // FFN_SUB_BLOCKS
    row = lax.broadcasted_iota(jnp.int32, (sub, xs_ref.shape[1]), 0)

    for slot in range(slots):
        blk = i * slots + slot
        wg_ref, wu_ref, wd_ref = w_refs[3 * slot:3 * slot + 3]

        @pl.when(jnp.logical_or(i == 0, held_ref[blk] != held_ref[jnp.maximum(blk - slots, 0)]))
        def _():
            wgu_bf[slot, :, :EXPERT_DIM] = wg_ref[0, 0].astype(BF16)
            wgu_bf[slot, :, EXPERT_DIM:] = wu_ref[0, 0].astype(BF16)
            wd_bf[slot] = wd_ref[0, 0].astype(BF16)

    @pl.when(nv_ref[i * slots] > 0)
    def _():
        spans, xs = [], []
        for slot in range(slots):
            nvalid = nv_ref[i * slots + slot]
            source = use_ref[i * slots + slot]
            for s in range(FFN_SUB_BLOCKS):
                span = pl.ds(slot * tb + s * sub, sub)
                packed = jnp.where(row < nvalid - s * sub, xs_ref[span, :], 0)
                low, high = _unpack_halves(packed)
                spans.append((source, span))
                xs.append(jnp.concatenate([low, high], axis=1).astype(BF16))
        gus = [jnp.dot(x, wgu_bf[source], preferred_element_type=F32) for (source, _), x in zip(spans, xs)]
        acts = [((gu[:, :EXPERT_DIM] * jax.nn.sigmoid(gu[:, :EXPERT_DIM])) * gu[:, EXPERT_DIM:]).astype(BF16)
                for gu in gus]
        ys = [jnp.dot(a, wd_bf[source], preferred_element_type=F32) for (source, _), a in zip(spans, acts)]
        for (_, span), y in zip(spans, ys):
            ys_ref[span, :] = _pack_halves(y)

    @pl.when(nv_ref[i * slots] == 0)
    def _():
        ys_ref[...] = jnp.zeros_like(ys_ref)


def _expert_ffn(xs, block_expert, block_valid, w_gate, w_up, w_down, layer, tb):
    p, half = xs.shape
    d = 2 * half
    slots = FFN_BLOCKS_PER_STEP
    n_blocks = block_expert.shape[0]
    run = jnp.cumsum(jnp.concatenate([jnp.zeros((1,), jnp.int32),
                                      (block_expert[1:] != block_expert[:-1]).astype(jnp.int32)]))
    blocks = jnp.arange(n_blocks, dtype=jnp.int32)
    first_block = jnp.minimum(jnp.sum((run[None, :] < blocks[:, None]).astype(jnp.int32), axis=1), n_blocks - 1)
    run_expert = block_expert[first_block]
    newest = run.reshape(-1, slots)[:, slots - 1]
    held = []
    for slot in range(slots):
        r = newest - (newest - slot) % slots
        held.append(run_expert[jnp.where(r < 0, slot, r)])
    held = jnp.stack(held, axis=1).reshape(-1)
    use = run % slots
    weight_specs = []
    for slot in range(slots):
        index = lambda i, held, use, nv, slot=slot: (layer, held[i * slots + slot], 0, 0)
        weight_specs += [pl.BlockSpec((1, 1, d, EXPERT_DIM), index), pl.BlockSpec((1, 1, d, EXPERT_DIM), index),
                         pl.BlockSpec((1, 1, EXPERT_DIM, d), index)]
    grid_spec = pltpu.PrefetchScalarGridSpec(
        num_scalar_prefetch=3,
        grid=(p // (tb * slots),),
        in_specs=[pl.BlockSpec((tb * slots, half), lambda i, held, use, nv: (i, 0))] + weight_specs,
        out_specs=pl.BlockSpec((tb * slots, half), lambda i, held, use, nv: (i, 0)),
        scratch_shapes=[pltpu.VMEM((slots, d, 2 * EXPERT_DIM), BF16), pltpu.VMEM((slots, EXPERT_DIM, d), BF16)],
    )
    return pl.pallas_call(
        functools.partial(_expert_ffn_kernel, tb=tb),
        grid_spec=grid_spec,
        out_shape=jax.ShapeDtypeStruct((p, half), jnp.int32),
        compiler_params=_params(("arbitrary",)),
        name="moe_expert_ffn",
    )(held, use, block_valid, xs, *([w_gate, w_up, w_down] * slots))


def _combine_kernel(x_ref, hp_ref, yg_ref, w_ref, mod_ref, wgu_ref, wd_ref, gf_ref, o_ref, *, final_norm):
    low, high = _unpack_halves(hp_ref[...])
    h = jnp.concatenate([low, high], axis=1).astype(BF16)
    shared = _swiglu(h, wgu_ref[...], wd_ref[...])
    half = hp_ref.shape[1]
    acc_low, acc_high = shared[:, :half], shared[:, half:]
    w = w_ref[...]
    for k in range(TOP_K):
        low, high = _unpack_halves(yg_ref[k])
        wk = w[:, k:k + 1]
        acc_low = acc_low + wk * low
        acc_high = acc_high + wk * high
    y = jnp.concatenate([acc_low, acc_high], axis=1)
    out = x_ref[...] + mod_ref[0, GT2:GT2 + 1, :] * y
    if final_norm:
        out = out * lax.rsqrt(jnp.mean(out * out, axis=-1, keepdims=True) + EPS) * gf_ref[...]
    o_ref[...] = out


def _combine(x2, hp, yg, w, mod, ws_gate_up, ws_down, g_final, seq, final_norm, tm=512):
    n, d = x2.shape
    blocks_per_batch = seq // tm
    const = lambda a: pl.BlockSpec(a.shape, lambda i: (0,) * a.ndim)
    return pl.pallas_call(
        functools.partial(_combine_kernel, final_norm=final_norm),
        grid=(n // tm,),
        in_specs=[
            pl.BlockSpec((tm, d), lambda i: (i, 0)),
            pl.BlockSpec((tm, d // 2), lambda i: (i, 0)),
            pl.BlockSpec((TOP_K, tm, d // 2), lambda i: (0, i, 0)),
            pl.BlockSpec((tm, LANES), lambda i: (i, 0)),
            pl.BlockSpec((1, 6, d), lambda i: (i // blocks_per_batch, 0, 0)),
            const(ws_gate_up), const(ws_down), const(g_final),
        ],
        out_specs=pl.BlockSpec((tm, d), lambda i: (i, 0)),
        out_shape=jax.ShapeDtypeStruct((n, d), F32),
        compiler_params=_params(("arbitrary",)),
        name="moe_combine",
    )(x2, hp, yg, w, mod, ws_gate_up, ws_down, g_final)


def _token_mixer(x2, mod, g_mix, w_in, b_gate, rpb, w_pa, w_pb, w_o, g_ffn, w_router, e_bias, batch, seq):
    d = x2.shape[1]
    dil0 = 3 * NA_WIDTH
    group_cols = [[dil0 + part * DIL_WIDTH + grp * DIL_OUT_WIDTH for part in range(3)]
                  for grp in range(N_DIL_GROUPS)]
    order = sorted(range(N_DIL_GROUPS), key=lambda grp: DIL_GROUPS[grp][1] > 1)
    w_qkv = jnp.concatenate(
        [w_in[:, :dil0]] + [w_in[:, c:c + DIL_OUT_WIDTH] for grp in order for c in group_cols[grp]],
        axis=1).astype(BF16)
    tok, *residue = _inproj(x2, mod, g_mix.reshape(1, d), w_qkv, batch, seq)
    o_a = _neighbourhood_attention(tok, _na_bias_table(rpb), batch, seq)

    qkv_offsets = tuple(part * DIL_OUT_WIDTH // LANES for part in range(3))
    group_inputs = [None] * N_DIL_GROUPS
    residue = iter(residue)
    tok_offset = dil0 // LANES
    for grp in order:
        if DIL_GROUPS[grp][1] == 1:
            group_inputs[grp] = (tok.reshape(batch, 1, seq, tok.shape[1]),
                                 tuple(tok_offset + o for o in qkv_offsets))
            tok_offset += 3 * DIL_OUT_WIDTH // LANES
        else:
            group_inputs[grp] = (next(residue), qkv_offsets)
    o_b = _dilated_attention(group_inputs, batch, seq)

    wr_t = w_router.T
    wr_hi = wr_t.astype(BF16)
    wr_split = jnp.stack([wr_hi, (wr_t - wr_hi.astype(F32)).astype(BF16)])
    return _outproj(x2, o_a, o_b, mod, g_mix.reshape(1, d),
                    w_in[:, QKV_COLS:].astype(BF16), b_gate.reshape(1, -1),
                    w_pa.astype(BF16), w_pb.astype(BF16), w_o.astype(BF16),
                    g_ffn.reshape(1, d), wr_split, e_bias.reshape(-1, 1), seq)


def _dispatch_plan(eidx3, rank3, counts, tb):
    n = eidx3.shape[0] * SC_CHUNK
    n_blocks = -(-(n * TOP_K + N_EXPERTS * (tb - 1)) // tb)
    n_blocks = -(-n_blocks // FFN_BLOCKS_PER_STEP) * FFN_BLOCKS_PER_STEP
    padded = (counts + tb - 1) // tb * tb
    seg_end = jnp.cumsum(padded)
    seg_start = seg_end - padded
    experts = jnp.arange(N_EXPERTS, dtype=jnp.int32)

    def lookup(table, idx):
        sel = idx[None] == experts.reshape((N_EXPERTS,) + (1,) * idx.ndim)
        return jnp.sum(jnp.where(sel, table.reshape((N_EXPERTS,) + (1,) * idx.ndim), 0), axis=0)

    idx3 = lookup(seg_start, eidx3) + rank3
    block_start = jnp.arange(n_blocks, dtype=jnp.int32) * tb
    block_expert = jnp.sum((seg_end[:, None] <= block_start[None, :]).astype(jnp.int32), axis=0)
    block_expert = jnp.minimum(block_expert, N_EXPERTS - 1)
    block_valid = jnp.clip(lookup(counts, block_expert) - (block_start - lookup(seg_start, block_expert)), 0, tb)
    return idx3, block_expert, block_valid.astype(jnp.int32), n_blocks


def _moe_layer(x2, routed, mod, we_gate, we_up, we_down, layer, ws_gate, ws_up, ws_down, g_final, seq, final_norm,
               tb=EXPERT_ROW_BLOCK):
    d = x2.shape[1]
    hp, eidx3, rank3, w, cnt = routed
    counts = cnt[:, 0].astype(jnp.int32)
    idx3, block_expert, block_valid, n_blocks = _dispatch_plan(eidx3, rank3, counts, tb)
    xs = _sc_scatter_rows(hp, idx3, n_blocks * tb)
    ys = _expert_ffn(xs, block_expert, block_valid, we_gate, we_up, we_down, layer, tb)
    yg = _sc_gather_rows(ys, idx3)
    return _combine(x2, hp, yg, w, mod, jnp.concatenate([ws_gate, ws_up], axis=-1).astype(BF16),
                    ws_down.astype(BF16), g_final.reshape(1, d), seq, final_norm)


def kernel(x, c, w_ada, b_ada, g_mix, w_in, b_gate, rpb, w_pa, w_pb, w_o, g_ffn, w_router, e_bias,
           we_gate, we_up, we_down, ws_gate, ws_up, ws_down, g_final):
    batch, seq, d = x.shape
    depth = w_ada.shape[0]
    mods = _ada(c, w_ada, b_ada).reshape(depth, batch, 6, d)
    x2 = x.reshape(batch * seq, d)
    for l in range(depth):
        x2, *routed = _token_mixer(x2, mods[l], g_mix[l], w_in[l], b_gate[l], rpb[l], w_pa[l], w_pb[l], w_o[l],
                                   g_ffn[l], w_router[l], e_bias[l], batch, seq)
        x2 = _moe_layer(x2, routed, mods[l], we_gate, we_up, we_down, l,
                        ws_gate[l], ws_up[l], ws_down[l], g_final, seq, final_norm=(l == depth - 1))
    return x2.reshape(batch, seq, d)
```
